```python
import jax, jax.numpy as jnp
from jax import lax
import numpy as np


D_MODEL = 1024
BATCH = 2
SEQ = 16384
DEPTH = 2

GRID_W = 64
CTX_LEN = 256
MIX_WIDTH = D_MODEL
FOURIER_WIDTH = D_MODEL // 4
FOURIER_HEADS = 4
GLA_VALUE_WIDTH = D_MODEL // 2
GLA_HEADS = 4
GLA_KEY_WIDTH = GLA_VALUE_WIDTH // 2
GLA_DK = GLA_KEY_WIDTH // GLA_HEADS
GLA_DV = GLA_VALUE_WIDTH // GLA_HEADS
GLA_GATE_RANK = 16
GLA_GATE_NORMALIZER = 16.0
GLA_CHUNK = 64
CONV_WIDTH = MIX_WIDTH - FOURIER_WIDTH - GLA_VALUE_WIDTH
CONV_KERNEL = 31
N_EXPERTS = 32
TOP_K = 4
D_FF = D_MODEL
SWIGLU_LIMIT = 7.0
SWIGLU_ALPHA = 1.702
MOE_BLOCK = 256
NORM_EPS = 1e-6

PROJ_WIDTHS = (FOURIER_WIDTH, GLA_KEY_WIDTH, GLA_KEY_WIDTH, GLA_VALUE_WIDTH, GLA_VALUE_WIDTH,
               GLA_GATE_RANK, GLA_GATE_RANK, 2 * CONV_WIDTH)
PROJ_SPLITS = tuple(int(s) for s in np.cumsum(PROJ_WIDTHS)[:-1])
IN_WIDTH = int(sum(PROJ_WIDTHS))

kernel_name = 'hybrid_fourier_gla_conformer_moe_dit'


def rmsnorm(x, g):
    xf = x.astype(jnp.float32)
    y = xf * lax.rsqrt(jnp.mean(xf * xf, axis=-1, keepdims=True) + NORM_EPS)
    return (y * g.astype(jnp.float32)).astype(x.dtype)


def layernorm(x, g, b):
    xf = x.astype(jnp.float32)
    mu = jnp.mean(xf, axis=-1, keepdims=True)
    xc = xf - mu
    y = xc * lax.rsqrt(jnp.mean(xc * xc, axis=-1, keepdims=True) + NORM_EPS)
    return (y * g.astype(jnp.float32) + b.astype(jnp.float32)).astype(x.dtype)


def modulate(h, shift, scale):
    return h * (1.0 + scale) + shift


def fourier_mix(u):
    bsz, length, _ = u.shape
    uf = u.astype(jnp.float32).reshape(bsz, length, FOURIER_HEADS, FOURIER_WIDTH // FOURIER_HEADS)
    y = jnp.real(jnp.fft.fftn(uf, axes=(1, 3), norm='ortho'))
    return y.reshape(bsz, length, FOURIER_WIDTH).astype(u.dtype)


def split_heads(t, d):
    bsz, length, _ = t.shape
    return t.reshape(bsz, length, GLA_HEADS, d).transpose(0, 2, 1, 3)


def gla_chunked(q, k, v, g, s0):
    bsz, heads, length, _ = q.shape
    dv = v.shape[-1]
    n = length // GLA_CHUNK

    def blocks(t):
        return t.astype(jnp.float32).reshape(bsz, heads, n, GLA_CHUNK, t.shape[-1])

    q, k, v, g = blocks(q), blocks(k), blocks(v), blocks(g)
    b = jnp.cumsum(g, axis=3)
    b_last = b[:, :, :, -1:, :]
    qe = q * jnp.exp(b)
    ke = k * jnp.exp(-b)
    past_mask = jnp.tril(jnp.ones((GLA_CHUNK, GLA_CHUNK), dtype=bool))
    scores = jnp.where(past_mask, jnp.einsum('bhnid,bhnjd->bhnij', qe, ke), 0.0)
    o_intra = jnp.einsum('bhnij,bhnje->bhnie', scores, v)
    ds = jnp.einsum('bhnjd,bhnje->bhnde', k * jnp.exp(b_last - b), v)
    decay = jnp.exp(b_last[:, :, :, 0, :])

    def step(s, inp):
        d, dsn = inp
        return d[..., None] * s + dsn, s

    s_final, s_start = lax.scan(step, s0.astype(jnp.float32),
                                (jnp.moveaxis(decay, 2, 0), jnp.moveaxis(ds, 2, 0)))
    s_start = jnp.moveaxis(s_start, 0, 2)
    o = o_intra + jnp.einsum('bhnid,bhnde->bhnie', qe, s_start)
    return o.reshape(bsz, heads, length, dv), s_final


def gla_two_streams(ctx_in, lat_in):
    qc, kc, vc, gfc, gbc = ctx_in
    ql, kl, vl, gfl, gbl = lat_in
    bsz = qc.shape[0]
    zeros = jnp.zeros((bsz, GLA_HEADS, GLA_DK, GLA_DV), jnp.float32)

    def rev(t):
        return t[:, :, ::-1]

    ocf, scf = gla_chunked(qc, kc, vc, gfc, zeros)
    olf, _ = gla_chunked(ql, kl, vl, gfl, scf)
    ocb, scb = gla_chunked(rev(qc), rev(kc), rev(vc), rev(gbc), zeros)
    olb, _ = gla_chunked(rev(ql), rev(kl), rev(vl), rev(gbl), scb)
    return ocf + rev(ocb), olf + rev(olb)


def gla_prep(p, wg2_f, bg_f, wg2_b, bg_b):
    q = split_heads(p[1], GLA_DK) * (GLA_DK ** -0.5)
    k = split_heads(p[2], GLA_DK)
    v = split_heads(p[3], GLA_DV)
    gf = jax.nn.log_sigmoid((p[5] @ wg2_f + bg_f).astype(jnp.float32)) / GLA_GATE_NORMALIZER
    gb = jax.nn.log_sigmoid((p[6] @ wg2_b + bg_b).astype(jnp.float32)) / GLA_GATE_NORMALIZER
    return (q, k, v, split_heads(gf, GLA_DK), split_heads(gb, GLA_DK))


def gla_output(o, r, norm_g):
    bsz, heads, length, dv = o.shape
    o = o.transpose(0, 2, 1, 3)
    o = o * lax.rsqrt(jnp.mean(o * o, axis=-1, keepdims=True) + NORM_EPS) * norm_g.astype(jnp.float32)
    return (o.reshape(bsz, length, heads * dv) * jax.nn.silu(r.astype(jnp.float32))).astype(r.dtype)


def conformer_conv(u, n_seg, conv_w, conv_b, ln_g, ln_b):
    bsz, length, _ = u.shape
    a, gate = jnp.split(u, 2, axis=-1)
    y = (a * jax.nn.sigmoid(gate)).reshape(bsz * n_seg, length // n_seg, CONV_WIDTH)
    y = lax.conv_general_dilated(
        y, conv_w.reshape(CONV_KERNEL, 1, CONV_WIDTH).astype(y.dtype),
        window_strides=(1,), padding=[(CONV_KERNEL // 2, CONV_KERNEL // 2)],
        dimension_numbers=('NWC', 'WIO', 'NWC'), feature_group_count=CONV_WIDTH)
    y = y.reshape(bsz, length, CONV_WIDTH) + conv_b
    return jax.nn.silu(layernorm(y, ln_g, ln_b))


def moe_ffn(h, router_w, router_b, w_gu, b_gu, w_dn, b_dn):
    n_tok = h.shape[0]
    logits = (h @ router_w + router_b).astype(jnp.float32)
    top_logits, top_idx = lax.top_k(logits, TOP_K)
    probs = jax.nn.softmax(top_logits, axis=-1).astype(h.dtype)
    n_assign = n_tok * TOP_K
    flat_e = top_idx.reshape(-1).astype(jnp.int32)
    order = jnp.argsort(flat_e).astype(jnp.int32)
    sorted_e = flat_e[order]
    counts = jnp.bincount(flat_e, length=N_EXPERTS).astype(jnp.int32)
    padded = (counts + MOE_BLOCK - 1) // MOE_BLOCK * MOE_BLOCK
    pad_end = jnp.cumsum(padded)
    pad_start = pad_end - padded
    sort_start = jnp.cumsum(counts) - counts
    slot = jnp.arange(n_assign, dtype=jnp.int32) - sort_start[sorted_e] + pad_start[sorted_e]
    n_blocks = -(-n_assign // MOE_BLOCK) + N_EXPERTS
    slot_tok = jnp.zeros((n_blocks * MOE_BLOCK,), jnp.int32).at[slot].set(order // TOP_K)
    block_e = jnp.minimum(
        jnp.searchsorted(pad_end, jnp.arange(n_blocks, dtype=jnp.int32) * MOE_BLOCK, side='right'),
        N_EXPERTS - 1)
    xb = h[slot_tok].reshape(n_blocks, MOE_BLOCK, h.shape[-1])

    def expert_block(args):
        xblk, e = args
        gu = xblk @ w_gu[e] + b_gu[e]
        gate = jnp.minimum(gu[:, :D_FF], SWIGLU_LIMIT)
        up = jnp.clip(gu[:, D_FF:], -SWIGLU_LIMIT, SWIGLU_LIMIT)
        act = gate * jax.nn.sigmoid(SWIGLU_ALPHA * gate) * (up + 1.0)
        return act @ w_dn[e] + b_dn[e]

    yb = lax.map(expert_block, (xb, block_e)).reshape(n_blocks * MOE_BLOCK, h.shape[-1])
    assign_slot = jnp.zeros((n_assign,), jnp.int32).at[order].set(slot)
    y = yb[assign_slot].reshape(n_tok, TOP_K, h.shape[-1])
    return jnp.einsum('nkd,nk->nd', y, probs)


def setup_inputs(seed: int = 0) -> dict:
    key = jax.random.key(seed)
    ks = jax.random.split(key, 26)

    def nrm(k, shape, scale):
        return jax.random.normal(k, shape, jnp.float32) * scale

    L = DEPTH
    return {
        'x': nrm(ks[0], (BATCH, SEQ, D_MODEL), 1.0),
        'c': nrm(ks[1], (BATCH, D_MODEL), 1.0),
        'ctx': nrm(ks[2], (BATCH, CTX_LEN, D_MODEL), 1.0),
        'c_ctx': nrm(ks[3], (D_MODEL,), 1.0),
        'norm1_g': 1.0 + nrm(ks[4], (L, D_MODEL), 0.01),
        'norm2_g': 1.0 + nrm(ks[5], (L, D_MODEL), 0.01),
        'w_mod': nrm(ks[6], (L, D_MODEL, 6 * D_MODEL), 0.5 * D_MODEL ** -0.5),
        'b_mod': nrm(ks[7], (L, 6 * D_MODEL), 0.02),
        'w_in': nrm(ks[8], (L, D_MODEL, IN_WIDTH), D_MODEL ** -0.5),
        'gla_wg2_f': nrm(ks[9], (L, GLA_GATE_RANK, GLA_KEY_WIDTH), GLA_GATE_RANK ** -0.5),
        'gla_bg_f': nrm(ks[10], (L, GLA_KEY_WIDTH), 0.1),
        'gla_wg2_b': nrm(ks[11], (L, GLA_GATE_RANK, GLA_KEY_WIDTH), GLA_GATE_RANK ** -0.5),
        'gla_bg_b': nrm(ks[12], (L, GLA_KEY_WIDTH), 0.1),
        'gla_norm_g': 1.0 + nrm(ks[13], (L, GLA_DV), 0.01),
        'conv_w': nrm(ks[14], (L, CONV_KERNEL, CONV_WIDTH), CONV_KERNEL ** -0.5),
        'conv_b': nrm(ks[15], (L, CONV_WIDTH), 0.02),
        'conv_ln_g': 1.0 + nrm(ks[16], (L, CONV_WIDTH), 0.01),
        'conv_ln_b': nrm(ks[17], (L, CONV_WIDTH), 0.02),
        'w_out': nrm(ks[18], (L, MIX_WIDTH, D_MODEL), MIX_WIDTH ** -0.5),
        'router_w': nrm(ks[19], (L, D_MODEL, N_EXPERTS), D_MODEL ** -0.5),
        'router_b': nrm(ks[20], (L, N_EXPERTS), 0.01),
        'exp_w_gu': nrm(ks[21], (L, N_EXPERTS, D_MODEL, 2 * D_FF), D_MODEL ** -0.5),
        'exp_b_gu': nrm(ks[22], (L, N_EXPERTS, 2 * D_FF), 0.01),
        'exp_w_dn': nrm(ks[23], (L, N_EXPERTS, D_FF, D_MODEL), D_FF ** -0.5),
        'exp_b_dn': nrm(ks[24], (L, N_EXPERTS, D_MODEL), 0.01),
        'final_norm_g': 1.0 + nrm(ks[25], (D_MODEL,), 0.01),
    }


def reference(x, c, ctx, c_ctx, norm1_g, norm2_g, w_mod, b_mod, w_in, gla_wg2_f, gla_bg_f,
              gla_wg2_b, gla_bg_b, gla_norm_g, conv_w, conv_b, conv_ln_g, conv_ln_b, w_out,
              router_w, router_b, exp_w_gu, exp_b_gu, exp_w_dn, exp_b_dn, final_norm_g):
    bsz, seq_len, d = x.shape
    ctx_len = ctx.shape[1]
    rows = seq_len // GRID_W
    n_ctx_tok = bsz * ctx_len
    x_lat, x_ctx = x, ctx
    silu_c = jax.nn.silu(c)
    silu_cc = jax.nn.silu(c_ctx)

    for layer in range(DEPTH):
        last = layer == DEPTH - 1
        mod_lat = jnp.split((silu_c @ w_mod[layer] + b_mod[layer])[:, None, :], 6, axis=-1)
        mod_ctx = jnp.split(silu_cc @ w_mod[layer] + b_mod[layer], 6, axis=-1)

        h_lat = modulate(rmsnorm(x_lat, norm1_g[layer]), mod_lat[0], mod_lat[1])
        h_ctx = modulate(rmsnorm(x_ctx, norm1_g[layer]), mod_ctx[0], mod_ctx[1])
        p_lat = jnp.split(h_lat @ w_in[layer], PROJ_SPLITS, axis=-1)
        p_ctx = jnp.split(h_ctx @ w_in[layer], PROJ_SPLITS, axis=-1)

        o_ctx, o_lat = gla_two_streams(
            gla_prep(p_ctx, gla_wg2_f[layer], gla_bg_f[layer], gla_wg2_b[layer], gla_bg_b[layer]),
            gla_prep(p_lat, gla_wg2_f[layer], gla_bg_f[layer], gla_wg2_b[layer], gla_bg_b[layer]))

        y_lat = jnp.concatenate([
            fourier_mix(p_lat[0]),
            gla_output(o_lat, p_lat[4], gla_norm_g[layer]),
            conformer_conv(p_lat[7], rows, conv_w[layer], conv_b[layer], conv_ln_g[layer], conv_ln_b[layer]),
        ], axis=-1) @ w_out[layer]
        x_lat = x_lat + mod_lat[2] * y_lat
        if not last:
            y_ctx = jnp.concatenate([
                fourier_mix(p_ctx[0]),
                gla_output(o_ctx, p_ctx[4], gla_norm_g[layer]),
                conformer_conv(p_ctx[7], 1, conv_w[layer], conv_b[layer], conv_ln_g[layer], conv_ln_b[layer]),
            ], axis=-1) @ w_out[layer]
            x_ctx = x_ctx + mod_ctx[2] * y_ctx

        h2_lat = modulate(rmsnorm(x_lat, norm2_g[layer]), mod_lat[3], mod_lat[4]).reshape(-1, d)
        moe_args = (router_w[layer], router_b[layer], exp_w_gu[layer], exp_b_gu[layer],
                    exp_w_dn[layer], exp_b_dn[layer])
        if last:
            f_lat = moe_ffn(h2_lat, *moe_args)
        else:
            h2_ctx = modulate(rmsnorm(x_ctx, norm2_g[layer]), mod_ctx[3], mod_ctx[4]).reshape(-1, d)
            f_all = moe_ffn(jnp.concatenate([h2_ctx, h2_lat], axis=0), *moe_args)
            x_ctx = x_ctx + mod_ctx[5] * f_all[:n_ctx_tok].reshape(bsz, ctx_len, d)
            f_lat = f_all[n_ctx_tok:]
        x_lat = x_lat + mod_lat[5] * f_lat.reshape(bsz, seq_len, d)

    return rmsnorm(x_lat, final_norm_g)
```

```python
import functools

import jax
import jax.numpy as jnp
from jax import lax
from jax.experimental import pallas as pl
from jax.experimental.pallas import tpu as pltpu

F32 = jnp.float32
BF16 = jnp.bfloat16

D_MODEL = 1024
DEPTH = 2
GRID_W = 64
FOURIER_WIDTH = 256
FOURIER_HEADS = 4
FOURIER_HEAD_DIM = FOURIER_WIDTH // FOURIER_HEADS
GLA_HEADS = 4
GLA_KEY_WIDTH = 256
GLA_VALUE_WIDTH = 512
GLA_DK = GLA_KEY_WIDTH // GLA_HEADS
GLA_DV = GLA_VALUE_WIDTH // GLA_HEADS
GLA_GATE_RANK = 16
GLA_GATE_NORMALIZER = 16.0
CONV_WIDTH = 256
CONV_KERNEL = 31
N_EXPERTS = 32
TOP_K = 4
D_FF = D_MODEL
SWIGLU_LIMIT = 7.0
SWIGLU_ALPHA = 1.702
NORM_EPS = 1e-6

LANES = 128
VMEM_LIMIT = 48 * 1024 * 1024

COL_U = 0
COL_Q = COL_U + FOURIER_WIDTH
COL_K = COL_Q + GLA_KEY_WIDTH
COL_V = COL_K + GLA_KEY_WIDTH
COL_R = COL_V + GLA_VALUE_WIDTH
COL_CA = COL_R + GLA_VALUE_WIDTH
COL_CG = COL_CA + CONV_WIDTH
COL_GL = COL_CG + CONV_WIDTH
GLR_PAD = LANES
IN_PAD = COL_GL + GLR_PAD

GLA_CHUNK = 128
TM_ROWS = 512
TM_EXPERT = 256
FFT_N2 = 128
FFT_KB = 8
CONV_HALO = 16


def _cparams(*sem):
    return pltpu.CompilerParams(dimension_semantics=sem, vmem_limit_bytes=VMEM_LIMIT)


def _dot(a, b):
    return jnp.dot(a, b, preferred_element_type=F32)


def _dot_nt(a, b):
    return lax.dot_general(a, b, (((1,), (1,)), ((), ())), preferred_element_type=F32)


def _dot_tn(a, b):
    return lax.dot_general(a, b, (((0,), (0,)), ((), ())), preferred_element_type=F32)


def _split(a):
    hi = a.astype(BF16)
    lo = (a - hi.astype(F32)).astype(BF16)
    return hi, lo


def _dot3(a, b):
    ah, al = _split(a)
    bh, bl = _split(b)
    return _dot(ah, bh) + _dot(ah, bl) + _dot(al, bh)


def _sigmoid(x):
    return 1.0 / (1.0 + jnp.exp(-x))


def _rms(x, g):
    ms = jnp.mean(x * x, axis=-1, keepdims=True)
    return x * lax.rsqrt(ms + NORM_EPS) * g


def _mod_kernel(cv_ref, w_ref, b_ref, o_ref):
    cv = cv_ref[...]
    a = cv * _sigmoid(cv)
    o_ref[...] = _dot3(a, w_ref[...]) + b_ref[...]


def _mod_call(cvec, w_mod, b_mod):
    depth, d, n = w_mod.shape
    rows = cvec.shape[0]
    tn = 1536
    return pl.pallas_call(
        _mod_kernel,
        grid=(depth, n // tn),
        in_specs=[
            pl.BlockSpec((rows, d), lambda l, j: (0, 0)),
            pl.BlockSpec((None, d, tn), lambda l, j: (l, 0, j)),
            pl.BlockSpec((None, 1, tn), lambda l, j: (l, 0, j)),
        ],
        out_specs=pl.BlockSpec((None, rows, tn), lambda l, j: (l, 0, j)),
        out_shape=jax.ShapeDtypeStruct((depth, rows, n), F32),
        compiler_params=_cparams("arbitrary", "arbitrary"),
    )(cvec, w_mod, b_mod.reshape(depth, 1, n))


def _inproj_kernel(x_ref, g_ref, sh_ref, sc_ref, w_ref, *rest):
    u_ref, r_ref, glu_ref, q_ref, k_ref, v_ref, gl_ref = rest[-7:]
    x = x_ref[...]
    h = _rms(x, g_ref[...]) * (1.0 + sc_ref[...]) + sh_ref[...]
    p = _dot(h.astype(BF16), w_ref[...])
    u_ref[...] = p[:, COL_U:COL_Q].astype(u_ref.dtype)
    q_ref[...] = (p[:, COL_Q:COL_K] * (GLA_DK ** -0.5)).astype(q_ref.dtype)
    k_ref[...] = p[:, COL_K:COL_V].astype(k_ref.dtype)
    v_ref[...] = p[:, COL_V:COL_R].astype(v_ref.dtype)
    r_ref[...] = p[:, COL_R:COL_CA].astype(r_ref.dtype)
    glu_ref[...] = (p[:, COL_CA:COL_CG] * _sigmoid(p[:, COL_CG:COL_GL])).astype(glu_ref.dtype)
    gl_ref[...] = p[:, COL_GL:IN_PAD]


def _inproj_call(x, mods, group_of_batch, norm_g, w_in_p, lt, row0, combined=None):
    bsz, n, d = x.shape
    tm = min(TM_ROWS, n)
    assert n % tm == 0 and row0 % tm == 0
    blk0 = row0 // tm
    widths = (GLA_KEY_WIDTH, GLA_KEY_WIDTH, GLA_VALUE_WIDTH, GLR_PAD)
    dtypes = (BF16, BF16, BF16, F32)
    row_spec = lambda w: pl.BlockSpec((None, tm, w), lambda b, i: (b, i, 0))
    comb_spec = lambda w: pl.BlockSpec((None, tm, w), lambda b, i: (b, blk0 + i, 0))
    mod_spec = lambda which: pl.BlockSpec(
        (None, None, 1, d), lambda b, i: (group_of_batch(b), which, 0, 0))
    in_specs = [
        row_spec(d),
        pl.BlockSpec((1, d), lambda b, i: (0, 0)),
        mod_spec(0), mod_spec(1),
        pl.BlockSpec((d, IN_PAD), lambda b, i: (0, 0)),
    ]
    args = [x, norm_g.reshape(1, d), mods, mods, w_in_p]
    aliases = {}
    if combined is not None:
        for t, arr in enumerate(combined):
            in_specs.append(pl.BlockSpec(memory_space=pl.ANY))
            aliases[len(args)] = 3 + t
            args.append(arr)
    out_shape = [
        jax.ShapeDtypeStruct((bsz, n, FOURIER_WIDTH), BF16),
        jax.ShapeDtypeStruct((bsz, n, GLA_VALUE_WIDTH), BF16),
        jax.ShapeDtypeStruct((bsz, n, CONV_WIDTH), BF16),
    ] + [jax.ShapeDtypeStruct((bsz, lt, w), dt) for w, dt in zip(widths, dtypes)]
    out_specs = [row_spec(FOURIER_WIDTH), row_spec(GLA_VALUE_WIDTH), row_spec(CONV_WIDTH)] + [
        comb_spec(w) for w in widths]
    outs = pl.pallas_call(
        _inproj_kernel,
        grid=(bsz, n // tm),
        in_specs=in_specs,
        out_specs=out_specs,
        out_shape=out_shape,
        input_output_aliases=aliases,
        compiler_params=_cparams("arbitrary", "arbitrary"),
    )(*args)
    return outs[:3], outs[3:]


def _dft_tables(length):
    n2 = FFT_N2
    n1 = length // n2
    two_pi = 2.0 * jnp.pi

    def cs(num, den):
        ang = (num % den).astype(F32) * (two_pi / den)
        return jnp.cos(ang), jnp.sin(ang)

    k1 = jnp.arange(n1, dtype=jnp.int32)
    c1, s1 = cs(k1[:, None] * k1[None, :], n1)
    stage1 = (jnp.concatenate([c1, -s1], axis=0) * (n1 ** -0.5)).astype(BF16)
    k2 = jnp.arange(n2, dtype=jnp.int32)
    freq = k1[:, None, None] + n1 * k2[None, :, None]
    c2, s2 = cs(freq * k2[None, None, :], length)
    mr, mi = c2 * (n2 ** -0.5), -s2 * (n2 ** -0.5)
    stage2 = jnp.concatenate([jnp.concatenate([mr, -mi], axis=2),
                              jnp.concatenate([mi, mr], axis=2)], axis=1).astype(BF16)
    return stage1, stage2


def _channel_tables():
    hd = FOURIER_HEAD_DIM
    c = jnp.arange(FOURIER_WIDTH, dtype=jnp.int32)
    same_head = (c[:, None] // hd) == (c[None, :] // hd)
    ang = (((c[:, None] % hd) * (c[None, :] % hd)) % hd).astype(F32) * (2.0 * jnp.pi / hd)
    scale = hd ** -0.5
    bdc = jnp.where(same_head, jnp.cos(ang) * scale, 0.0).astype(BF16)
    bds = jnp.where(same_head, jnp.sin(ang) * scale, 0.0).astype(BF16)
    return bdc, bds


def _fft1_kernel(x_ref, cs_ref, zr_ref, zi_ref):
    n1 = x_ref.shape[0]
    z = _dot(cs_ref[...], x_ref[...])
    zr_ref[...] = z[:n1].astype(zr_ref.dtype)
    zi_ref[...] = z[n1:].astype(zi_ref.dtype)


def _fft2_kernel(zr_ref, zi_ref, m_ref, bdc_ref, bds_ref, o_ref):
    kb, n2, w = zr_ref.shape
    for j in range(kb):
        z = jnp.concatenate([zr_ref[j], zi_ref[j]], axis=0)
        a = _dot(m_ref[j], z)
        y = _dot(a[:n2].astype(BF16), bdc_ref[...]) + _dot(a[n2:].astype(BF16), bds_ref[...])
        o_ref[:, j * w:(j + 1) * w] = y.astype(o_ref.dtype)


def _fourier_long(u, tables, chan):
    bsz, length, w = u.shape
    stage1, stage2 = tables
    bdc, bds = chan
    n2 = FFT_N2
    n1 = length // n2
    tn = 4096
    cols = n2 * w
    zr, zi = pl.pallas_call(
        _fft1_kernel,
        grid=(bsz, cols // tn),
        in_specs=[pl.BlockSpec((None, n1, tn), lambda b, j: (b, 0, j)),
                  pl.BlockSpec((2 * n1, n1), lambda b, j: (0, 0))],
        out_specs=[pl.BlockSpec((None, n1, tn), lambda b, j: (b, 0, j))] * 2,
        out_shape=[jax.ShapeDtypeStruct((bsz, n1, cols), BF16)] * 2,
        compiler_params=_cparams("arbitrary", "arbitrary"),
    )(u.reshape(bsz, n1, cols), stage1)
    kb = FFT_KB
    z_spec = pl.BlockSpec((None, kb, n2, w), lambda b, j: (b, j, 0, 0))
    y = pl.pallas_call(
        _fft2_kernel,
        grid=(bsz, n1 // kb),
        in_specs=[z_spec, z_spec,
                  pl.BlockSpec((kb, 2 * n2, 2 * n2), lambda b, j: (j, 0, 0)),
                  pl.BlockSpec((w, w), lambda b, j: (0, 0)),
                  pl.BlockSpec((w, w), lambda b, j: (0, 0))],
        out_specs=pl.BlockSpec((None, n2, kb * w), lambda b, j: (b, 0, j)),
        out_shape=jax.ShapeDtypeStruct((bsz, n2, n1 * w), BF16),
        compiler_params=_cparams("arbitrary", "arbitrary"),
    )(zr.reshape(bsz, n1, n2, w), zi.reshape(bsz, n1, n2, w), stage2, bdc, bds)
    return y.reshape(bsz, length, w)


def _dft_short_kernel(u_ref, c_ref, s_ref, bdc_ref, bds_ref, o_ref):
    u = u_ref[...]
    p = _dot(u, bdc_ref[...]).astype(BF16)
    q = _dot(u, bds_ref[...]).astype(BF16)
    o_ref[...] = (_dot(c_ref[...], p) - _dot(s_ref[...], q)).astype(o_ref.dtype)


def _fourier_short(u, chan):
    bsz, length, w = u.shape
    bdc, bds = chan
    k = jnp.arange(length, dtype=jnp.int32)
    ang = ((k[:, None] * k[None, :]) % length).astype(F32) * (2.0 * jnp.pi / length)
    c = (jnp.cos(ang) * length ** -0.5).astype(BF16)
    s = (jnp.sin(ang) * length ** -0.5).astype(BF16)
    full = lambda n: pl.BlockSpec((n, n), lambda b: (0, 0))
    return pl.pallas_call(
        _dft_short_kernel,
        grid=(bsz,),
        in_specs=[pl.BlockSpec((None, length, w), lambda b: (b, 0, 0)),
                  full(length), full(length), full(w), full(w)],
        out_specs=pl.BlockSpec((None, length, w), lambda b: (b, 0, 0)),
        out_shape=jax.ShapeDtypeStruct((bsz, length, w), BF16),
        compiler_params=_cparams("arbitrary"),
    )(u, c, s, bdc, bds)


def _conv_kernel(x_ref, w_ref, cb_ref, lg_ref, lb_ref, o_ref, pad_ref, *, seg):
    nseg = x_ref.shape[0] // seg
    width = x_ref.shape[1]
    halo = jnp.zeros((CONV_HALO, width), F32)
    for s in range(nseg):
        pad_ref[s, 0:CONV_HALO, :] = halo
        pad_ref[s, CONV_HALO:CONV_HALO + seg, :] = x_ref[s * seg:(s + 1) * seg, :].astype(F32)
        pad_ref[s, CONV_HALO + seg:2 * CONV_HALO + seg, :] = halo
    first = CONV_HALO - CONV_KERNEL // 2
    for s in range(nseg):
        acc = jnp.zeros((seg, width), F32)
        for j in range(CONV_KERNEL):
            acc = acc + pad_ref[s, first + j:first + j + seg, :] * w_ref[j:j + 1, :]
        y = acc + cb_ref[...]
        mu = jnp.mean(y, axis=-1, keepdims=True)
        yc = y - mu
        var = jnp.mean(yc * yc, axis=-1, keepdims=True)
        z = yc * lax.rsqrt(var + NORM_EPS) * lg_ref[...] + lb_ref[...]
        o_ref[s * seg:(s + 1) * seg, :] = (z * _sigmoid(z)).astype(o_ref.dtype)


def _conv_call(glu, seg, conv_w, conv_b, ln_g, ln_b):
    bsz, n, w = glu.shape
    t = max(seg, min(TM_ROWS, n))
    assert n % t == 0 and t % seg == 0
    vec = lambda: pl.BlockSpec((1, w), lambda b, i: (0, 0))
    return pl.pallas_call(
        functools.partial(_conv_kernel, seg=seg),
        grid=(bsz, n // t),
        in_specs=[pl.BlockSpec((None, t, w), lambda b, i: (b, i, 0)),
                  pl.BlockSpec((CONV_KERNEL, w), lambda b, i: (0, 0)),
                  vec(), vec(), vec()],
        out_specs=pl.BlockSpec((None, t, w), lambda b, i: (b, i, 0)),
        out_shape=jax.ShapeDtypeStruct((bsz, n, w), BF16),
        scratch_shapes=[pltpu.VMEM((t // seg, seg + 2 * CONV_HALO, w), F32)],
        compiler_params=_cparams("arbitrary", "arbitrary"),
    )(glu, conv_w, conv_b.reshape(1, w), ln_g.reshape(1, w), ln_b.reshape(1, w))


def _gla_direction(q_ref, k_ref, v_ref, gl_ref, wg_ref, bg_ref, o_ref, st_ref, reverse):
    c = q_ref.shape[0]
    kw = GLA_KEY_WIDTH
    col0 = kw if reverse else 0
    pre = _dot3(gl_ref[...], wg_ref[:, col0:col0 + kw]) + bg_ref[:, col0:col0 + kw]
    g = (jnp.minimum(pre, 0.0) - jnp.log(1.0 + jnp.exp(-jnp.abs(pre)))) * (1.0 / GLA_GATE_NORMALIZER)
    row = lax.broadcasted_iota(jnp.int32, (c, c), 0)
    col = lax.broadcasted_iota(jnp.int32, (c, c), 1)
    seen = (col >= row) if reverse else (col <= row)
    tri = jnp.where(seen, 1.0, 0.0).astype(BF16)
    gh, gl = _split(g)
    b = _dot(tri, gh) + _dot(tri, gl)
    mid = c // 2 if reverse else c // 2 - 1
    last = 0 if reverse else c - 1
    b_mid = b[mid:mid + 1, :]
    b_last = b[last:last + 1, :]
    q = q_ref[...].astype(F32)
    k = k_ref[...].astype(F32)
    qe = q * jnp.exp(b - b_mid)
    ke = k * jnp.exp(b_mid - b)
    kd = k * jnp.exp(b_last - b)
    st = st_ref[...]
    head_of_lane = lax.broadcasted_iota(jnp.int32, (1, kw), 1) // GLA_DK
    q_heads = jnp.concatenate(
        [jnp.where(head_of_lane == h, qe, 0.0) for h in range(GLA_HEADS)], axis=0).astype(BF16)
    rhs = jnp.concatenate([ke, st * jnp.exp(b_mid)], axis=0).astype(BF16)
    res = _dot_nt(q_heads, rhs)
    v = v_ref[...]
    outs = []
    for h in range(GLA_HEADS):
        blk = res[h * c:(h + 1) * c, :]
        scores = jnp.where(seen, blk[:, :c], 0.0).astype(BF16)
        outs.append(_dot(scores, v[:, h * GLA_DV:(h + 1) * GLA_DV]) + blk[:, c:])
    o_ref[...] = jnp.concatenate(outs, axis=1).astype(o_ref.dtype)
    kv = _dot_tn(v, kd.astype(BF16))
    ds = jnp.zeros_like(st)
    for h in range(GLA_HEADS):
        ds = ds + jnp.where(head_of_lane == h, kv[h * GLA_DV:(h + 1) * GLA_DV, :], 0.0)
    st_ref[...] = st * jnp.exp(b_last) + ds


def _gla_kernel(qf, kf, vf, gf, qb, kb, vb, gb, wg_ref, bg_ref, of_ref, ob_ref, sf_ref, sb_ref):
    @pl.when(pl.program_id(1) == 0)
    def _():
        sf_ref[...] = jnp.zeros_like(sf_ref)
        sb_ref[...] = jnp.zeros_like(sb_ref)

    _gla_direction(qf, kf, vf, gf, wg_ref, bg_ref, of_ref, sf_ref, False)
    _gla_direction(qb, kb, vb, gb, wg_ref, bg_ref, ob_ref, sb_ref, True)


def _gla_call(q, k, v, gl, wg_pad, bg_cat, n_lat, n_ctx):
    bsz, lt, _ = q.shape
    c = GLA_CHUNK
    assert n_lat % c == 0 and n_ctx % c == 0
    cl, cc = n_lat // c, n_ctx // c

    def fwd_blk(j):
        return jnp.where(j < cc, cl + j, j - cc)

    def bwd_blk(j):
        return jnp.where(j < cc, cl + cc - 1 - j, cl - 1 - (j - cc))

    def spec(w, blk):
        return pl.BlockSpec((None, c, w), lambda b, j: (b, blk(j), 0))

    widths = (GLA_KEY_WIDTH, GLA_KEY_WIDTH, GLA_VALUE_WIDTH, GLR_PAD)
    in_specs = [spec(w, fwd_blk) for w in widths] + [spec(w, bwd_blk) for w in widths] + [
        pl.BlockSpec(wg_pad.shape, lambda b, j: (0, 0)),
        pl.BlockSpec(bg_cat.shape, lambda b, j: (0, 0))]
    return pl.pallas_call(
        _gla_kernel,
        grid=(bsz, cl + cc),
        in_specs=in_specs,
        out_specs=[spec(GLA_VALUE_WIDTH, fwd_blk), spec(GLA_VALUE_WIDTH, bwd_blk)],
        out_shape=[jax.ShapeDtypeStruct((bsz, lt, GLA_VALUE_WIDTH), F32)] * 2,
        scratch_shapes=[pltpu.VMEM((GLA_DV, GLA_KEY_WIDTH), F32)] * 2,
        compiler_params=_cparams("arbitrary", "arbitrary"),
    )(q, k, v, gl, q, k, v, gl, wg_pad, bg_cat)


def _outproj_kernel(yf_ref, of_ref, ob_ref, r_ref, cv_ref, x_ref, gate_ref, sh_ref, sc_ref,
                    gn_ref, wo_ref, n2_ref, rwh_ref, rwl_ref, rb_ref, *rest):
    xo_ref, h2_ref, idx_ref, prob_ref = rest[-4:]
    o = of_ref[...] + ob_ref[...]
    heads = []
    for h in range(GLA_HEADS):
        oh = o[:, h * GLA_DV:(h + 1) * GLA_DV]
        heads.append(oh * lax.rsqrt(jnp.mean(oh * oh, axis=-1, keepdims=True) + NORM_EPS))
    r = r_ref[...].astype(F32)
    gla = jnp.concatenate(heads, axis=1) * gn_ref[...] * (r * _sigmoid(r))
    c0, c1 = FOURIER_WIDTH, FOURIER_WIDTH + GLA_VALUE_WIDTH
    y = (_dot(yf_ref[...], wo_ref[0:c0, :]) + _dot(gla.astype(BF16), wo_ref[c0:c1, :])
         + _dot(cv_ref[...], wo_ref[c1:, :]))
    xn = x_ref[...] + gate_ref[...] * y
    xo_ref[...] = xn
    h2 = _rms(xn, n2_ref[...]) * (1.0 + sc_ref[...]) + sh_ref[...]
    hh, hl = _split(h2)
    h2_ref[...] = hh
    logits = (_dot_nt(rwh_ref[...], hh) + _dot_nt(rwh_ref[...], hl) + _dot_nt(rwl_ref[...], hh)
              + rb_ref[...])
    expert = lax.broadcasted_iota(jnp.int32, logits.shape, 0)
    vals, idxs = [], []
    cur = logits
    for _ in range(TOP_K):
        m = jnp.max(cur, axis=0, keepdims=True)
        ix = jnp.min(jnp.where(cur == m, expert, N_EXPERTS), axis=0, keepdims=True)
        vals.append(m)
        idxs.append(ix)
        cur = jnp.where(expert == ix, -jnp.inf, cur)
    es = [jnp.exp(vv - vals[0]) for vv in vals]
    inv = 1.0 / functools.reduce(lambda a, b: a + b, es)
    idx_ref[...] = jnp.concatenate(idxs, axis=0)
    prob_ref[...] = jnp.concatenate([e * inv for e in es], axis=0)


def _outproj_call(yf, o_f, o_b, o_row0, r, cv, x, mods, group_of_batch, gn_tiled, w_out, norm2_g,
                  rw_hi, rw_lo, rb, n_tok, tok0, carried=None):
    bsz, n, d = x.shape
    tm = min(TM_ROWS, n)
    assert n % tm == 0 and o_row0 % tm == 0 and tok0 % tm == 0
    nb = n // tm
    row = lambda w: pl.BlockSpec((None, tm, w), lambda b, i: (b, i, 0))
    orow = pl.BlockSpec((None, tm, GLA_VALUE_WIDTH), lambda b, i: (b, o_row0 // tm + i, 0))
    mod = lambda which: pl.BlockSpec((None, None, 1, d), lambda b, i: (group_of_batch(b), which, 0, 0))
    const = lambda a: pl.BlockSpec(a.shape, lambda b, i: (0,) * a.ndim)
    consts = [gn_tiled, w_out, norm2_g.reshape(1, d), rw_hi, rw_lo, rb]
    in_specs = [row(FOURIER_WIDTH), orow, orow, row(GLA_VALUE_WIDTH), row(CONV_WIDTH), row(d),
                mod(2), mod(3), mod(4)] + [const(a) for a in consts]
    args = [yf, o_f, o_b, r, cv, x, mods, mods, mods] + consts
    aliases = {}
    if carried is not None:
        for t, arr in enumerate(carried):
            in_specs.append(pl.BlockSpec(memory_space=pl.ANY))
            aliases[len(args)] = 1 + t
            args.append(arr)
    tokblk = lambda b, i: tok0 // tm + b * nb + i
    out_specs = [row(d),
                 pl.BlockSpec((tm, d), lambda b, i: (tokblk(b, i), 0)),
                 pl.BlockSpec((TOP_K, tm), lambda b, i: (0, tokblk(b, i))),
                 pl.BlockSpec((TOP_K, tm), lambda b, i: (0, tokblk(b, i)))]
    out_shape = [jax.ShapeDtypeStruct((bsz, n, d), F32),
                 jax.ShapeDtypeStruct((n_tok, d), BF16),
                 jax.ShapeDtypeStruct((TOP_K, n_tok), jnp.int32),
                 jax.ShapeDtypeStruct((TOP_K, n_tok), F32)]
    outs = pl.pallas_call(
        _outproj_kernel,
        grid=(bsz, nb),
        in_specs=in_specs,
        out_specs=out_specs,
        out_shape=out_shape,
        input_output_aliases=aliases,
        compiler_params=_cparams("arbitrary", "arbitrary"),
    )(*args)
    return outs[0], outs[1:]


def _expert_kernel(be_ref, nu_ref, x_ref, wgu_ref, bgu_ref, wdn_ref, bdn_ref, o_ref, wgu_s, wdn_s):
    i = pl.program_id(0)

    @pl.when(i < nu_ref[0])
    def _():
        changed = jnp.logical_or(i == 0, be_ref[i] != be_ref[jnp.maximum(i - 1, 0)])

        @pl.when(changed)
        def _():
            rows = 128
            for s in range(wgu_ref.shape[0] // rows):
                wgu_s[s * rows:(s + 1) * rows, :] = wgu_ref[s * rows:(s + 1) * rows, :].astype(BF16)
                wdn_s[s * rows:(s + 1) * rows, :] = wdn_ref[s * rows:(s + 1) * rows, :].astype(BF16)

        gu = _dot(x_ref[...], wgu_s[...]) + bgu_ref[...]
        gate = jnp.minimum(gu[:, :D_FF], SWIGLU_LIMIT)
        up = jnp.clip(gu[:, D_FF:], -SWIGLU_LIMIT, SWIGLU_LIMIT)
        act = gate * _sigmoid(SWIGLU_ALPHA * gate) * (up + 1.0)
        o_ref[...] = (_dot(act.astype(BF16), wdn_s[...]) + bdn_ref[...]).astype(o_ref.dtype)


def _expert_call(block_e, n_used, xs, w_gu, b_gu, w_dn, b_dn):
    rows, d = xs.shape
    tm = TM_EXPERT
    nblk = rows // tm
    e, _, f2 = w_gu.shape
    live = lambda i, nu: jnp.minimum(i, nu[0] - 1)
    grid_spec = pltpu.PrefetchScalarGridSpec(
        num_scalar_prefetch=2,
        grid=(nblk,),
        in_specs=[
            pl.BlockSpec((tm, d), lambda i, be, nu: (live(i, nu), 0)),
            pl.BlockSpec((None, d, f2), lambda i, be, nu: (be[live(i, nu)], 0, 0)),
            pl.BlockSpec((None, 1, f2), lambda i, be, nu: (be[live(i, nu)], 0, 0)),
            pl.BlockSpec((None, f2 // 2, d), lambda i, be, nu: (be[live(i, nu)], 0, 0)),
            pl.BlockSpec((None, 1, d), lambda i, be, nu: (be[live(i, nu)], 0, 0)),
        ],
        out_specs=pl.BlockSpec((tm, d), lambda i, be, nu: (live(i, nu), 0)),
        scratch_shapes=[pltpu.VMEM((d, f2), BF16), pltpu.VMEM((f2 // 2, d), BF16)],
    )
    return pl.pallas_call(
        _expert_kernel,
        grid_spec=grid_spec,
        out_shape=jax.ShapeDtypeStruct((rows, d), BF16),
        compiler_params=_cparams("arbitrary"),
    )(block_e, n_used, xs, w_gu, b_gu.reshape(e, 1, f2), w_dn, b_dn.reshape(e, 1, d))


def _routing_tables(idx_t, n_tok):
    tm = TM_EXPERT
    n_assign = TOP_K * n_tok
    flat_e = idx_t.reshape(-1)
    order = jnp.argsort(flat_e).astype(jnp.int32)
    sorted_e = flat_e[order]
    counts = jnp.bincount(flat_e, length=N_EXPERTS).astype(jnp.int32)
    padded = (counts + tm - 1) // tm * tm
    pad_end = jnp.cumsum(padded)
    pad_start = pad_end - padded
    sort_start = jnp.cumsum(counts) - counts
    slot = jnp.arange(n_assign, dtype=jnp.int32) - sort_start[sorted_e] + pad_start[sorted_e]
    n_blocks = -(-n_assign // tm) + N_EXPERTS
    slot_tok = jnp.zeros((n_blocks * tm,), jnp.int32).at[slot].set(order % n_tok)
    block_e = jnp.minimum(
        jnp.searchsorted(pad_end, jnp.arange(n_blocks, dtype=jnp.int32) * tm, side='right'),
        N_EXPERTS - 1).astype(jnp.int32)
    n_used = (pad_end[-1:] // tm).astype(jnp.int32)
    assign_slot = jnp.zeros((n_assign,), jnp.int32).at[order].set(slot)
    return slot_tok, block_e, n_used, assign_slot.reshape(TOP_K, n_tok)


def _combine_kernel(y_ref, p_ref, x_ref, gate_ref, fg_ref, o_ref, *, final_norm):
    f = jnp.zeros(x_ref.shape, F32)
    for kk in range(TOP_K):
        f = f + y_ref[kk].astype(F32) * p_ref[:, kk:kk + 1]
    xn = x_ref[...] + gate_ref[...] * f
    if final_norm:
        xn = _rms(xn, fg_ref[...])
    o_ref[...] = xn


def _combine_call(yg, probs, x, mods, group_of_batch, final_g, tok0, final_norm):
    bsz, n, d = x.shape
    tm = min(TM_ROWS, n)
    nb = n // tm
    tokblk = lambda b, i: tok0 // tm + b * nb + i
    return pl.pallas_call(
        functools.partial(_combine_kernel, final_norm=final_norm),
        grid=(bsz, nb),
        in_specs=[pl.BlockSpec((TOP_K, tm, d), lambda b, i: (0, tokblk(b, i), 0)),
                  pl.BlockSpec((tm, TOP_K), lambda b, i: (tokblk(b, i), 0)),
                  pl.BlockSpec((None, tm, d), lambda b, i: (b, i, 0)),
                  pl.BlockSpec((None, None, 1, d), lambda b, i: (group_of_batch(b), 5, 0, 0)),
                  pl.BlockSpec((1, d), lambda b, i: (0, 0))],
        out_specs=pl.BlockSpec((None, tm, d), lambda b, i: (b, i, 0)),
        out_shape=jax.ShapeDtypeStruct((bsz, n, d), F32),
        compiler_params=_cparams("arbitrary", "arbitrary"),
    )(yg, probs, x, mods, final_g.reshape(1, d))


def kernel(x, c, ctx, c_ctx, norm1_g, norm2_g, w_mod, b_mod, w_in, gla_wg2_f, gla_bg_f, gla_wg2_b,
           gla_bg_b, gla_norm_g, conv_w, conv_b, conv_ln_g, conv_ln_b, w_out, router_w, router_b,
           exp_w_gu, exp_b_gu, exp_w_dn, exp_b_dn, final_norm_g):
    bsz, seq, d = x.shape
    n_ctx = ctx.shape[1]
    depth = w_mod.shape[0]
    assert d == D_MODEL and seq % (FFT_N2 * FFT_KB) == 0 and seq % n_ctx == 0
    lt = seq + n_ctx
    ctx_group = bsz

    rows = 8
    cvec = jnp.concatenate([c, c_ctx[None, :], jnp.zeros((rows - bsz - 1, d), F32)], axis=0)
    mods_all = _mod_call(cvec, w_mod, b_mod).reshape(depth, rows, 6, 1, d)

    def pack_w_in(w):
        o = [0, 256, 512, 768, 1280, 1792, 1808, 1824, 2336]
        parts = [w[:, o[0]:o[1]], w[:, o[1]:o[2]], w[:, o[2]:o[3]], w[:, o[3]:o[4]], w[:, o[4]:o[5]],
                 w[:, o[7]:o[7] + CONV_WIDTH], w[:, o[7] + CONV_WIDTH:o[8]], w[:, o[5]:o[7]],
                 jnp.zeros((d, GLR_PAD - 2 * GLA_GATE_RANK), w.dtype)]
        return jnp.concatenate(parts, axis=1).astype(BF16)

    long_tables = _dft_tables(seq)
    chan = _channel_tables()
    x_lat, x_ctx = x, ctx
    lat_group = lambda b: b
    ctx_group_fn = lambda b: ctx_group

    for layer in range(depth):
        last = layer == depth - 1
        mods = mods_all[layer]
        w_in_p = pack_w_in(w_in[layer])
        wg_pad = jnp.zeros((GLR_PAD, 2 * GLA_KEY_WIDTH), F32)
        wg_pad = wg_pad.at[:GLA_GATE_RANK, :GLA_KEY_WIDTH].set(gla_wg2_f[layer])
        wg_pad = wg_pad.at[GLA_GATE_RANK:2 * GLA_GATE_RANK, GLA_KEY_WIDTH:].set(gla_wg2_b[layer])
        bg_cat = jnp.concatenate([gla_bg_f[layer], gla_bg_b[layer]])[None, :]
        w_out_b = w_out[layer].astype(BF16)
        gn_tiled = jnp.tile(gla_norm_g[layer], GLA_HEADS)[None, :]
        rw_t = router_w[layer].T
        rw_hi = rw_t.astype(BF16)
        rw_lo = (rw_t - rw_hi.astype(F32)).astype(BF16)
        rb = router_b[layer][:, None]

        (u_l, r_l, glu_l), comb = _inproj_call(x_lat, mods, lat_group, norm1_g[layer], w_in_p, lt, 0)
        (u_c, r_c, glu_c), comb = _inproj_call(x_ctx, mods, ctx_group_fn, norm1_g[layer], w_in_p, lt,
                                               seq, combined=comb)
        o_f, o_b = _gla_call(*comb, wg_pad, bg_cat, seq, n_ctx)
        yf_l = _fourier_long(u_l, long_tables, chan)
        cv_l = _conv_call(glu_l, seq // (seq // GRID_W), conv_w[layer], conv_b[layer],
                          conv_ln_g[layer], conv_ln_b[layer])
        n_tok = bsz * seq + (0 if last else bsz * n_ctx)
        x_lat, routed = _outproj_call(yf_l, o_f, o_b, 0, r_l, cv_l, x_lat, mods, lat_group, gn_tiled,
                                      w_out_b, norm2_g[layer], rw_hi, rw_lo, rb, n_tok, 0)
        if not last:
            yf_c = _fourier_short(u_c, chan)
            cv_c = _conv_call(glu_c, n_ctx, conv_w[layer], conv_b[layer], conv_ln_g[layer],
                              conv_ln_b[layer])
            x_ctx, routed = _outproj_call(yf_c, o_f, o_b, seq, r_c, cv_c, x_ctx, mods, ctx_group_fn,
                                          gn_tiled, w_out_b, norm2_g[layer], rw_hi, rw_lo, rb, n_tok,
                                          bsz * seq, carried=routed)

        h2, idx_t, prob_t = routed
        slot_tok, block_e, n_used, assign_slot = _routing_tables(idx_t, n_tok)
        xs = jnp.take(h2, slot_tok, axis=0)
        yb = _expert_call(block_e, n_used, xs, exp_w_gu[layer], exp_b_gu[layer], exp_w_dn[layer],
                          exp_b_dn[layer])
        yg = jnp.take(yb, assign_slot, axis=0)
        probs = prob_t.T
        x_lat = _combine_call(yg, probs, x_lat, mods, lat_group, final_norm_g, 0, last)
        if not last:
            x_ctx = _combine_call(yg, probs, x_ctx, mods, ctx_group_fn, final_norm_g, bsz * seq, False)

    return x_lat
```

```python
import functools

import jax
import jax.numpy as jnp
from jax import lax
from jax.experimental import pallas as pl
from jax.experimental.pallas import tpu as pltpu

F32 = jnp.float32
BF16 = jnp.bfloat16

D_MODEL = 1024
DEPTH = 2
GRID_W = 64
FOURIER_WIDTH = 256
FOURIER_HEADS = 4
FOURIER_HEAD_DIM = FOURIER_WIDTH // FOURIER_HEADS
GLA_HEADS = 4
GLA_KEY_WIDTH = 256
GLA_VALUE_WIDTH = 512
GLA_DK = GLA_KEY_WIDTH // GLA_HEADS
GLA_DV = GLA_VALUE_WIDTH // GLA_HEADS
GLA_GATE_RANK = 16
GLA_GATE_NORMALIZER = 16.0
CONV_WIDTH = 256
CONV_KERNEL = 31
N_EXPERTS = 32
TOP_K = 4
D_FF = D_MODEL
SWIGLU_LIMIT = 7.0
SWIGLU_ALPHA = 1.702
NORM_EPS = 1e-6

LANES = 128
VMEM_LIMIT = 48 * 1024 * 1024

COL_U = 0
COL_Q = COL_U + FOURIER_WIDTH
COL_K = COL_Q + GLA_KEY_WIDTH
COL_V = COL_K + GLA_KEY_WIDTH
COL_R = COL_V + GLA_VALUE_WIDTH
COL_CA = COL_R + GLA_VALUE_WIDTH
COL_CG = COL_CA + CONV_WIDTH
COL_GL = COL_CG + CONV_WIDTH
GLR_PAD = LANES
IN_PAD = COL_GL + GLR_PAD

GLA_CHUNK = 128
TM_ROWS = 512
TM_EXPERT = 256
FFT_N2 = 128
FFT_KB = 8
CONV_HALO = 16


def _cparams(*sem):
    return pltpu.CompilerParams(dimension_semantics=sem, vmem_limit_bytes=VMEM_LIMIT)


def _dot(a, b):
    return jnp.dot(a, b, preferred_element_type=F32)


def _dot_nt(a, b):
    return lax.dot_general(a, b, (((1,), (1,)), ((), ())), preferred_element_type=F32)


def _dot_tn(a, b):
    return lax.dot_general(a, b, (((0,), (0,)), ((), ())), preferred_element_type=F32)


def _split(a):
    hi = a.astype(BF16)
    lo = (a - hi.astype(F32)).astype(BF16)
    return hi, lo


def _dot3(a, b):
    ah, al = _split(a)
    bh, bl = _split(b)
    return _dot(ah, bh) + _dot(ah, bl) + _dot(al, bh)


def _sigmoid(x):
    return 1.0 / (1.0 + jnp.exp(-x))


def _rms(x, g):
    ms = jnp.mean(x * x, axis=-1, keepdims=True)
    return x * lax.rsqrt(ms + NORM_EPS) * g


def _mod_kernel(cv_ref, w_ref, b_ref, o_ref):
    cv = cv_ref[...]
    a = cv * _sigmoid(cv)
    o_ref[...] = _dot3(a, w_ref[...]) + b_ref[...]


def _mod_call(cvec, w_mod, b_mod):
    depth, d, n = w_mod.shape
    rows = cvec.shape[0]
    tn = 1536
    return pl.pallas_call(
        _mod_kernel,
        grid=(depth, n // tn),
        in_specs=[
            pl.BlockSpec((rows, d), lambda l, j: (0, 0)),
            pl.BlockSpec((None, d, tn), lambda l, j: (l, 0, j)),
            pl.BlockSpec((None, 1, tn), lambda l, j: (l, 0, j)),
        ],
        out_specs=pl.BlockSpec((None, rows, tn), lambda l, j: (l, 0, j)),
        out_shape=jax.ShapeDtypeStruct((depth, rows, n), F32),
        compiler_params=_cparams("arbitrary", "arbitrary"),
    )(cvec, w_mod, b_mod.reshape(depth, 1, n))


def _inproj_kernel(x_ref, g_ref, sh_ref, sc_ref, w_ref, *rest):
    u_ref, r_ref, glu_ref, q_ref, k_ref, v_ref, gl_ref = rest[-7:]
    x = x_ref[...]
    h = _rms(x, g_ref[...]) * (1.0 + sc_ref[...]) + sh_ref[...]
    p = _dot(h.astype(BF16), w_ref[...])
    u_ref[...] = p[:, COL_U:COL_Q].astype(u_ref.dtype)
    q_ref[...] = (p[:, COL_Q:COL_K] * (GLA_DK ** -0.5)).astype(q_ref.dtype)
    k_ref[...] = p[:, COL_K:COL_V].astype(k_ref.dtype)
    v_ref[...] = p[:, COL_V:COL_R].astype(v_ref.dtype)
    r_ref[...] = p[:, COL_R:COL_CA].astype(r_ref.dtype)
    glu_ref[...] = (p[:, COL_CA:COL_CG] * _sigmoid(p[:, COL_CG:COL_GL])).astype(glu_ref.dtype)
    gl_ref[...] = p[:, COL_GL:IN_PAD]


def _inproj_call(x, mods, group_of_batch, norm_g, w_in_p, lt, row0, combined=None):
    bsz, n, d = x.shape
    tm = min(TM_ROWS, n)
    assert n % tm == 0 and row0 % tm == 0
    blk0 = row0 // tm
    widths = (GLA_KEY_WIDTH, GLA_KEY_WIDTH, GLA_VALUE_WIDTH, GLR_PAD)
    dtypes = (BF16, BF16, BF16, F32)
    row_spec = lambda w: pl.BlockSpec((None, tm, w), lambda b, i: (b, i, 0))
    comb_spec = lambda w: pl.BlockSpec((None, tm, w), lambda b, i: (b, blk0 + i, 0))
    mod_spec = lambda which: pl.BlockSpec(
        (None, None, 1, d), lambda b, i: (group_of_batch(b), which, 0, 0))
    in_specs = [
        row_spec(d),
        pl.BlockSpec((1, d), lambda b, i: (0, 0)),
        mod_spec(0), mod_spec(1),
        pl.BlockSpec((d, IN_PAD), lambda b, i: (0, 0)),
    ]
    args = [x, norm_g.reshape(1, d), mods, mods, w_in_p]
    aliases = {}
    if combined is not None:
        for t, arr in enumerate(combined):
            in_specs.append(pl.BlockSpec(memory_space=pl.ANY))
            aliases[len(args)] = 3 + t
            args.append(arr)
    out_shape = [
        jax.ShapeDtypeStruct((bsz, n, FOURIER_WIDTH), BF16),
        jax.ShapeDtypeStruct((bsz, n, GLA_VALUE_WIDTH), BF16),
        jax.ShapeDtypeStruct((bsz, n, CONV_WIDTH), BF16),
    ] + [jax.ShapeDtypeStruct((bsz, lt, w), dt) for w, dt in zip(widths, dtypes)]
    out_specs = [row_spec(FOURIER_WIDTH), row_spec(GLA_VALUE_WIDTH), row_spec(CONV_WIDTH)] + [
        comb_spec(w) for w in widths]
    outs = pl.pallas_call(
        _inproj_kernel,
        grid=(bsz, n // tm),
        in_specs=in_specs,
        out_specs=out_specs,
        out_shape=out_shape,
        input_output_aliases=aliases,
        compiler_params=_cparams("arbitrary", "arbitrary"),
    )(*args)
    return outs[:3], outs[3:]


def _dft_tables(length):
    n2 = FFT_N2
    n1 = length // n2
    two_pi = 2.0 * jnp.pi

    def cs(num, den):
        ang = (num % den).astype(F32) * (two_pi / den)
        return jnp.cos(ang), jnp.sin(ang)

    k1 = jnp.arange(n1, dtype=jnp.int32)
    c1, s1 = cs(k1[:, None] * k1[None, :], n1)
    stage1 = (jnp.concatenate([c1, -s1], axis=0) * (n1 ** -0.5)).astype(BF16)
    k2 = jnp.arange(n2, dtype=jnp.int32)
    ct, st = cs(k1[:, None] * k2[None, :], length)
    cf, sf = cs(k2[:, None] * k2[None, :], n2)
    scale = n2 ** -0.5
    mr = (ct[:, None, :] * cf[None] - st[:, None, :] * sf[None]) * scale
    mi = -(st[:, None, :] * cf[None] + ct[:, None, :] * sf[None]) * scale
    stage2 = jnp.concatenate([jnp.concatenate([mr, -mi], axis=2),
                              jnp.concatenate([mi, mr], axis=2)], axis=1).astype(BF16)
    return stage1, stage2


def _channel_tables():
    hd = FOURIER_HEAD_DIM
    c = jnp.arange(FOURIER_WIDTH, dtype=jnp.int32)
    same_head = (c[:, None] // hd) == (c[None, :] // hd)
    ang = (((c[:, None] % hd) * (c[None, :] % hd)) % hd).astype(F32) * (2.0 * jnp.pi / hd)
    scale = hd ** -0.5
    bdc = jnp.where(same_head, jnp.cos(ang) * scale, 0.0).astype(BF16)
    bds = jnp.where(same_head, jnp.sin(ang) * scale, 0.0).astype(BF16)
    return bdc, bds


def _fft1_kernel(x_ref, cs_ref, zr_ref, zi_ref):
    n1 = x_ref.shape[0]
    z = _dot(cs_ref[...], x_ref[...])
    zr_ref[...] = z[:n1].astype(zr_ref.dtype)
    zi_ref[...] = z[n1:].astype(zi_ref.dtype)


def _fft2_kernel(zr_ref, zi_ref, m_ref, bdc_ref, bds_ref, o_ref):
    kb, n2, w = zr_ref.shape
    for j in range(kb):
        z = jnp.concatenate([zr_ref[j], zi_ref[j]], axis=0)
        a = _dot(m_ref[j], z)
        y = _dot(a[:n2].astype(BF16), bdc_ref[...]) + _dot(a[n2:].astype(BF16), bds_ref[...])
        o_ref[:, j * w:(j + 1) * w] = y.astype(o_ref.dtype)


def _fourier_long(u, tables, chan):
    bsz, length, w = u.shape
    stage1, stage2 = tables
    bdc, bds = chan
    n2 = FFT_N2
    n1 = length // n2
    tn = 4096
    cols = n2 * w
    zr, zi = pl.pallas_call(
        _fft1_kernel,
        grid=(bsz, cols // tn),
        in_specs=[pl.BlockSpec((None, n1, tn), lambda b, j: (b, 0, j)),
                  pl.BlockSpec((2 * n1, n1), lambda b, j: (0, 0))],
        out_specs=[pl.BlockSpec((None, n1, tn), lambda b, j: (b, 0, j))] * 2,
        out_shape=[jax.ShapeDtypeStruct((bsz, n1, cols), BF16)] * 2,
        compiler_params=_cparams("arbitrary", "arbitrary"),
    )(u.reshape(bsz, n1, cols), stage1)
    kb = FFT_KB
    z_spec = pl.BlockSpec((None, kb, n2, w), lambda b, j: (b, j, 0, 0))
    y = pl.pallas_call(
        _fft2_kernel,
        grid=(bsz, n1 // kb),
        in_specs=[z_spec, z_spec,
                  pl.BlockSpec((kb, 2 * n2, 2 * n2), lambda b, j: (j, 0, 0)),
                  pl.BlockSpec((w, w), lambda b, j: (0, 0)),
                  pl.BlockSpec((w, w), lambda b, j: (0, 0))],
        out_specs=pl.BlockSpec((None, n2, kb * w), lambda b, j: (b, 0, j)),
        out_shape=jax.ShapeDtypeStruct((bsz, n2, n1 * w), BF16),
        compiler_params=_cparams("arbitrary", "arbitrary"),
    )(zr.reshape(bsz, n1, n2, w), zi.reshape(bsz, n1, n2, w), stage2, bdc, bds)
    return y.reshape(bsz, length, w)


def _dft_short_kernel(u_ref, c_ref, s_ref, bdc_ref, bds_ref, o_ref):
    u = u_ref[...]
    p = _dot(u, bdc_ref[...]).astype(BF16)
    q = _dot(u, bds_ref[...]).astype(BF16)
    o_ref[...] = (_dot(c_ref[...], p) - _dot(s_ref[...], q)).astype(o_ref.dtype)


def _fourier_short(u, chan):
    bsz, length, w = u.shape
    bdc, bds = chan
    k = jnp.arange(length, dtype=jnp.int32)
    ang = ((k[:, None] * k[None, :]) % length).astype(F32) * (2.0 * jnp.pi / length)
    c = (jnp.cos(ang) * length ** -0.5).astype(BF16)
    s = (jnp.sin(ang) * length ** -0.5).astype(BF16)
    full = lambda n: pl.BlockSpec((n, n), lambda b: (0, 0))
    return pl.pallas_call(
        _dft_short_kernel,
        grid=(bsz,),
        in_specs=[pl.BlockSpec((None, length, w), lambda b: (b, 0, 0)),
                  full(length), full(length), full(w), full(w)],
        out_specs=pl.BlockSpec((None, length, w), lambda b: (b, 0, 0)),
        out_shape=jax.ShapeDtypeStruct((bsz, length, w), BF16),
        compiler_params=_cparams("arbitrary"),
    )(u, c, s, bdc, bds)


def _conv_kernel(x_ref, w_ref, cb_ref, lg_ref, lb_ref, o_ref, pad_ref, *, seg):
    nseg = x_ref.shape[0] // seg
    width = x_ref.shape[1]
    halo = jnp.zeros((CONV_HALO, width), F32)
    for s in range(nseg):
        pad_ref[s, 0:CONV_HALO, :] = halo
        pad_ref[s, CONV_HALO:CONV_HALO + seg, :] = x_ref[s * seg:(s + 1) * seg, :].astype(F32)
        pad_ref[s, CONV_HALO + seg:2 * CONV_HALO + seg, :] = halo
    first = CONV_HALO - CONV_KERNEL // 2
    for s in range(nseg):
        acc = jnp.zeros((seg, width), F32)
        for j in range(CONV_KERNEL):
            acc = acc + pad_ref[s, first + j:first + j + seg, :] * w_ref[j:j + 1, :]
        y = acc + cb_ref[...]
        mu = jnp.mean(y, axis=-1, keepdims=True)
        yc = y - mu
        var = jnp.mean(yc * yc, axis=-1, keepdims=True)
        z = yc * lax.rsqrt(var + NORM_EPS) * lg_ref[...] + lb_ref[...]
        o_ref[s * seg:(s + 1) * seg, :] = (z * _sigmoid(z)).astype(o_ref.dtype)


def _conv_call(glu, seg, conv_w, conv_b, ln_g, ln_b):
    bsz, n, w = glu.shape
    t = max(seg, min(TM_ROWS, n))
    assert n % t == 0 and t % seg == 0
    vec = lambda: pl.BlockSpec((1, w), lambda b, i: (0, 0))
    return pl.pallas_call(
        functools.partial(_conv_kernel, seg=seg),
        grid=(bsz, n // t),
        in_specs=[pl.BlockSpec((None, t, w), lambda b, i: (b, i, 0)),
                  pl.BlockSpec((CONV_KERNEL, w), lambda b, i: (0, 0)),
                  vec(), vec(), vec()],
        out_specs=pl.BlockSpec((None, t, w), lambda b, i: (b, i, 0)),
        out_shape=jax.ShapeDtypeStruct((bsz, n, w), BF16),
        scratch_shapes=[pltpu.VMEM((t // seg, seg + 2 * CONV_HALO, w), F32)],
        compiler_params=_cparams("arbitrary", "arbitrary"),
    )(glu, conv_w, conv_b.reshape(1, w), ln_g.reshape(1, w), ln_b.reshape(1, w))


def _gla_direction(q_ref, k_ref, v_ref, gl_ref, wg_ref, bg_ref, o_ref, st_ref, reverse):
    c = q_ref.shape[0]
    kw = GLA_KEY_WIDTH
    col0 = kw if reverse else 0
    pre = _dot3(gl_ref[...], wg_ref[:, col0:col0 + kw]) + bg_ref[:, col0:col0 + kw]
    g = (jnp.minimum(pre, 0.0) - jnp.log(1.0 + jnp.exp(-jnp.abs(pre)))) * (1.0 / GLA_GATE_NORMALIZER)
    row = lax.broadcasted_iota(jnp.int32, (c, c), 0)
    col = lax.broadcasted_iota(jnp.int32, (c, c), 1)
    seen = (col >= row) if reverse else (col <= row)
    tri = jnp.where(seen, 1.0, 0.0).astype(BF16)
    gh, gl = _split(g)
    b = _dot(tri, gh) + _dot(tri, gl)
    mid = c // 2 if reverse else c // 2 - 1
    last = 0 if reverse else c - 1
    b_mid = b[mid:mid + 1, :]
    b_last = b[last:last + 1, :]
    q = q_ref[...].astype(F32)
    k = k_ref[...].astype(F32)
    qe = q * jnp.exp(b - b_mid)
    ke = k * jnp.exp(b_mid - b)
    kd = k * jnp.exp(b_last - b)
    st = st_ref[...]
    head_of_lane = lax.broadcasted_iota(jnp.int32, (1, kw), 1) // GLA_DK
    q_heads = jnp.concatenate(
        [jnp.where(head_of_lane == h, qe, 0.0) for h in range(GLA_HEADS)], axis=0).astype(BF16)
    rhs = jnp.concatenate([ke, st * jnp.exp(b_mid)], axis=0).astype(BF16)
    res = _dot_nt(q_heads, rhs)
    v = v_ref[...]
    outs = []
    for h in range(GLA_HEADS):
        blk = res[h * c:(h + 1) * c, :]
        scores = jnp.where(seen, blk[:, :c], 0.0).astype(BF16)
        outs.append(_dot(scores, v[:, h * GLA_DV:(h + 1) * GLA_DV]) + blk[:, c:])
    o_ref[...] = jnp.concatenate(outs, axis=1).astype(o_ref.dtype)
    kv = _dot_tn(v, kd.astype(BF16))
    ds = jnp.zeros_like(st)
    for h in range(GLA_HEADS):
        ds = ds + jnp.where(head_of_lane == h, kv[h * GLA_DV:(h + 1) * GLA_DV, :], 0.0)
    st_ref[...] = st * jnp.exp(b_last) + ds


def _gla_kernel(qf, kf, vf, gf, qb, kb, vb, gb, wg_ref, bg_ref, of_ref, ob_ref, sf_ref, sb_ref):
    @pl.when(pl.program_id(1) == 0)
    def _():
        sf_ref[...] = jnp.zeros_like(sf_ref)
        sb_ref[...] = jnp.zeros_like(sb_ref)

    _gla_direction(qf, kf, vf, gf, wg_ref, bg_ref, of_ref, sf_ref, False)
    _gla_direction(qb, kb, vb, gb, wg_ref, bg_ref, ob_ref, sb_ref, True)


def _gla_call(q, k, v, gl, wg_pad, bg_cat, n_lat, n_ctx):
    bsz, lt, _ = q.shape
    c = GLA_CHUNK
    assert n_lat % c == 0 and n_ctx % c == 0
    cl, cc = n_lat // c, n_ctx // c

    def fwd_blk(j):
        return jnp.where(j < cc, cl + j, j - cc)

    def bwd_blk(j):
        return jnp.where(j < cc, cl + cc - 1 - j, cl - 1 - (j - cc))

    def spec(w, blk):
        return pl.BlockSpec((None, c, w), lambda b, j: (b, blk(j), 0))

    widths = (GLA_KEY_WIDTH, GLA_KEY_WIDTH, GLA_VALUE_WIDTH, GLR_PAD)
    in_specs = [spec(w, fwd_blk) for w in widths] + [spec(w, bwd_blk) for w in widths] + [
        pl.BlockSpec(wg_pad.shape, lambda b, j: (0, 0)),
        pl.BlockSpec(bg_cat.shape, lambda b, j: (0, 0))]
    return pl.pallas_call(
        _gla_kernel,
        grid=(bsz, cl + cc),
        in_specs=in_specs,
        out_specs=[spec(GLA_VALUE_WIDTH, fwd_blk), spec(GLA_VALUE_WIDTH, bwd_blk)],
        out_shape=[jax.ShapeDtypeStruct((bsz, lt, GLA_VALUE_WIDTH), F32)] * 2,
        scratch_shapes=[pltpu.VMEM((GLA_DV, GLA_KEY_WIDTH), F32)] * 2,
        compiler_params=_cparams("arbitrary", "arbitrary"),
    )(q, k, v, gl, q, k, v, gl, wg_pad, bg_cat)


def _outproj_kernel(yf_ref, of_ref, ob_ref, r_ref, cv_ref, x_ref, gate_ref, sh_ref, sc_ref,
                    gn_ref, wo_ref, n2_ref, rwh_ref, rwl_ref, rb_ref, *rest):
    xo_ref, h2_ref, idx_ref, prob_ref = rest[-4:]
    o = of_ref[...] + ob_ref[...]
    heads = []
    for h in range(GLA_HEADS):
        oh = o[:, h * GLA_DV:(h + 1) * GLA_DV]
        heads.append(oh * lax.rsqrt(jnp.mean(oh * oh, axis=-1, keepdims=True) + NORM_EPS))
    r = r_ref[...].astype(F32)
    gla = jnp.concatenate(heads, axis=1) * gn_ref[...] * (r * _sigmoid(r))
    c0, c1 = FOURIER_WIDTH, FOURIER_WIDTH + GLA_VALUE_WIDTH
    y = (_dot(yf_ref[...], wo_ref[0:c0, :]) + _dot(gla.astype(BF16), wo_ref[c0:c1, :])
         + _dot(cv_ref[...], wo_ref[c1:, :]))
    xn = x_ref[...] + gate_ref[...] * y
    xo_ref[...] = xn
    h2 = _rms(xn, n2_ref[...]) * (1.0 + sc_ref[...]) + sh_ref[...]
    hh, hl = _split(h2)
    h2_ref[...] = hh
    logits = (_dot_nt(rwh_ref[...], hh) + _dot_nt(rwh_ref[...], hl) + _dot_nt(rwl_ref[...], hh)
              + rb_ref[...])
    expert = lax.broadcasted_iota(jnp.int32, logits.shape, 0)
    vals, idxs = [], []
    cur = logits
    for _ in range(TOP_K):
        m = jnp.max(cur, axis=0, keepdims=True)
        ix = jnp.min(jnp.where(cur == m, expert, N_EXPERTS), axis=0, keepdims=True)
        vals.append(m)
        idxs.append(ix)
        cur = jnp.where(expert == ix, -jnp.inf, cur)
    es = [jnp.exp(vv - vals[0]) for vv in vals]
    inv = 1.0 / functools.reduce(lambda a, b: a + b, es)
    idx_ref[...] = jnp.concatenate(idxs, axis=0)
    prob_ref[...] = jnp.concatenate([e * inv for e in es], axis=0)


def _outproj_call(yf, o_f, o_b, o_row0, r, cv, x, mods, group_of_batch, gn_tiled, w_out, norm2_g,
                  rw_hi, rw_lo, rb, n_tok, tok0, carried=None):
    bsz, n, d = x.shape
    tm = min(TM_ROWS, n)
    assert n % tm == 0 and o_row0 % tm == 0 and tok0 % tm == 0
    nb = n // tm
    row = lambda w: pl.BlockSpec((None, tm, w), lambda b, i: (b, i, 0))
    orow = pl.BlockSpec((None, tm, GLA_VALUE_WIDTH), lambda b, i: (b, o_row0 // tm + i, 0))
    mod = lambda which: pl.BlockSpec((None, None, 1, d), lambda b, i: (group_of_batch(b), which, 0, 0))
    const = lambda a: pl.BlockSpec(a.shape, lambda b, i: (0,) * a.ndim)
    consts = [gn_tiled, w_out, norm2_g.reshape(1, d), rw_hi, rw_lo, rb]
    in_specs = [row(FOURIER_WIDTH), orow, orow, row(GLA_VALUE_WIDTH), row(CONV_WIDTH), row(d),
                mod(2), mod(3), mod(4)] + [const(a) for a in consts]
    args = [yf, o_f, o_b, r, cv, x, mods, mods, mods] + consts
    aliases = {}
    if carried is not None:
        for t, arr in enumerate(carried):
            in_specs.append(pl.BlockSpec(memory_space=pl.ANY))
            aliases[len(args)] = 1 + t
            args.append(arr)
    tokblk = lambda b, i: tok0 // tm + b * nb + i
    out_specs = [row(d),
                 pl.BlockSpec((tm, d), lambda b, i: (tokblk(b, i), 0)),
                 pl.BlockSpec((TOP_K, tm), lambda b, i: (0, tokblk(b, i))),
                 pl.BlockSpec((TOP_K, tm), lambda b, i: (0, tokblk(b, i)))]
    out_shape = [jax.ShapeDtypeStruct((bsz, n, d), F32),
                 jax.ShapeDtypeStruct((n_tok, d), BF16),
                 jax.ShapeDtypeStruct((TOP_K, n_tok), jnp.int32),
                 jax.ShapeDtypeStruct((TOP_K, n_tok), F32)]
    outs = pl.pallas_call(
        _outproj_kernel,
        grid=(bsz, nb),
        in_specs=in_specs,
        out_specs=out_specs,
        out_shape=out_shape,
        input_output_aliases=aliases,
        compiler_params=_cparams("arbitrary", "arbitrary"),
    )(*args)
    return outs[0], outs[1:]


def _expert_kernel(be_ref, nu_ref, x_ref, wgu_ref, bgu_ref, wdn_ref, bdn_ref, o_ref, wgu_s, wdn_s):
    i = pl.program_id(0)

    @pl.when(i < nu_ref[0])
    def _():
        changed = jnp.logical_or(i == 0, be_ref[i] != be_ref[jnp.maximum(i - 1, 0)])

        @pl.when(changed)
        def _():
            rows = 128
            for s in range(wgu_ref.shape[0] // rows):
                wgu_s[s * rows:(s + 1) * rows, :] = wgu_ref[s * rows:(s + 1) * rows, :].astype(BF16)
                wdn_s[s * rows:(s + 1) * rows, :] = wdn_ref[s * rows:(s + 1) * rows, :].astype(BF16)

        gu = _dot(x_ref[...], wgu_s[...]) + bgu_ref[...]
        gate = jnp.minimum(gu[:, :D_FF], SWIGLU_LIMIT)
        up = jnp.clip(gu[:, D_FF:], -SWIGLU_LIMIT, SWIGLU_LIMIT)
        act = gate * _sigmoid(SWIGLU_ALPHA * gate) * (up + 1.0)
        o_ref[...] = (_dot(act.astype(BF16), wdn_s[...]) + bdn_ref[...]).astype(o_ref.dtype)


def _expert_call(block_e, n_used, xs, layer, w_gu, b_gu, w_dn, b_dn):
    rows, d = xs.shape
    tm = TM_EXPERT
    nblk = rows // tm
    depth, e, _, f2 = w_gu.shape
    live = lambda i, nu: jnp.minimum(i, nu[0] - 1)
    wmap = lambda i, be, nu: (layer, be[live(i, nu)], 0, 0)
    grid_spec = pltpu.PrefetchScalarGridSpec(
        num_scalar_prefetch=2,
        grid=(nblk,),
        in_specs=[
            pl.BlockSpec((tm, d), lambda i, be, nu: (live(i, nu), 0)),
            pl.BlockSpec((None, None, d, f2), wmap),
            pl.BlockSpec((None, None, 1, f2), wmap),
            pl.BlockSpec((None, None, f2 // 2, d), wmap),
            pl.BlockSpec((None, None, 1, d), wmap),
        ],
        out_specs=pl.BlockSpec((tm, d), lambda i, be, nu: (live(i, nu), 0)),
        scratch_shapes=[pltpu.VMEM((d, f2), BF16), pltpu.VMEM((f2 // 2, d), BF16)],
    )
    return pl.pallas_call(
        _expert_kernel,
        grid_spec=grid_spec,
        out_shape=jax.ShapeDtypeStruct((rows, d), BF16),
        compiler_params=_cparams("arbitrary"),
    )(block_e, n_used, xs, w_gu, b_gu.reshape(depth, e, 1, f2), w_dn, b_dn.reshape(depth, e, 1, d))


def _rank_kernel(idx_ref, rank_ref, cnt_ref, base_ref):
    @pl.when(pl.program_id(0) == 0)
    def _():
        base_ref[...] = jnp.zeros_like(base_ref)

    tb = idx_ref.shape[1]
    row = lax.broadcasted_iota(jnp.int32, (tb, tb), 0)
    col = lax.broadcasted_iota(jnp.int32, (tb, tb), 1)
    earlier = jnp.where(row < col, 1.0, 0.0).astype(BF16)
    expert = lax.broadcasted_iota(jnp.int32, (N_EXPERTS, tb), 0)
    base = base_ref[...]
    ranks = []
    for kk in range(TOP_K):
        hit = expert == idx_ref[kk:kk + 1, :]
        onehot = hit.astype(F32)
        before = _dot(onehot.astype(BF16), earlier) + base
        ranks.append(jnp.sum(before * onehot, axis=0, keepdims=True))
        base = base + jnp.sum(onehot, axis=1, keepdims=True)
    rank_ref[...] = jnp.concatenate(ranks, axis=0).astype(jnp.int32)
    base_ref[...] = base
    cnt_ref[...] = base.astype(jnp.int32)


def _slot_kernel(idx_ref, rank_ref, start_ref, slot_ref):
    expert = lax.broadcasted_iota(jnp.int32, (N_EXPERTS, idx_ref.shape[1]), 0)
    rows = []
    for kk in range(TOP_K):
        start = jnp.sum(jnp.where(expert == idx_ref[kk:kk + 1, :], start_ref[...], 0), axis=0,
                        keepdims=True)
        rows.append(rank_ref[kk:kk + 1, :] + start)
    slot_ref[...] = jnp.concatenate(rows, axis=0)


def _routing_tables(idx_t, n_tok):
    tm = TM_EXPERT
    tb = TM_ROWS
    assert n_tok % tb == 0
    blk = pl.BlockSpec((TOP_K, tb), lambda i: (0, i))
    rank_t, counts = pl.pallas_call(
        _rank_kernel,
        grid=(n_tok // tb,),
        in_specs=[blk],
        out_specs=[blk, pl.BlockSpec((N_EXPERTS, 1), lambda i: (0, 0))],
        out_shape=[jax.ShapeDtypeStruct((TOP_K, n_tok), jnp.int32),
                   jax.ShapeDtypeStruct((N_EXPERTS, 1), jnp.int32)],
        scratch_shapes=[pltpu.VMEM((N_EXPERTS, 1), F32)],
        compiler_params=_cparams("arbitrary"),
    )(idx_t)
    counts = counts[:, 0]
    padded = (counts + tm - 1) // tm * tm
    pad_end = jnp.cumsum(padded)
    pad_start = pad_end - padded
    slot_t = pl.pallas_call(
        _slot_kernel,
        grid=(n_tok // tb,),
        in_specs=[blk, blk, pl.BlockSpec((N_EXPERTS, 1), lambda i: (0, 0))],
        out_specs=blk,
        out_shape=jax.ShapeDtypeStruct((TOP_K, n_tok), jnp.int32),
        compiler_params=_cparams("arbitrary"),
    )(idx_t, rank_t, pad_start[:, None])
    n_blocks = -(-TOP_K * n_tok // tm) + N_EXPERTS
    block_e = jnp.minimum(
        jnp.searchsorted(pad_end, jnp.arange(n_blocks, dtype=jnp.int32) * tm, side='right'),
        N_EXPERTS - 1).astype(jnp.int32)
    n_used = (pad_end[-1:] // tm).astype(jnp.int32)
    tok = jnp.tile(jnp.arange(n_tok, dtype=jnp.int32), TOP_K)
    slot_tok = jnp.zeros((n_blocks * tm,), jnp.int32).at[slot_t.reshape(-1)].set(
        tok, unique_indices=True)
    return slot_tok, block_e, n_used, slot_t


def _combine_kernel(y_ref, p_ref, x_ref, gate_ref, fg_ref, o_ref, *, final_norm):
    f = jnp.zeros(x_ref.shape, F32)
    for kk in range(TOP_K):
        f = f + y_ref[kk].astype(F32) * p_ref[:, kk:kk + 1]
    xn = x_ref[...] + gate_ref[...] * f
    if final_norm:
        xn = _rms(xn, fg_ref[...])
    o_ref[...] = xn


def _combine_call(yg, probs, x, mods, group_of_batch, final_g, tok0, final_norm):
    bsz, n, d = x.shape
    tm = min(TM_ROWS, n)
    nb = n // tm
    tokblk = lambda b, i: tok0 // tm + b * nb + i
    return pl.pallas_call(
        functools.partial(_combine_kernel, final_norm=final_norm),
        grid=(bsz, nb),
        in_specs=[pl.BlockSpec((TOP_K, tm, d), lambda b, i: (0, tokblk(b, i), 0)),
                  pl.BlockSpec((tm, TOP_K), lambda b, i: (tokblk(b, i), 0)),
                  pl.BlockSpec((None, tm, d), lambda b, i: (b, i, 0)),
                  pl.BlockSpec((None, None, 1, d), lambda b, i: (group_of_batch(b), 5, 0, 0)),
                  pl.BlockSpec((1, d), lambda b, i: (0, 0))],
        out_specs=pl.BlockSpec((None, tm, d), lambda b, i: (b, i, 0)),
        out_shape=jax.ShapeDtypeStruct((bsz, n, d), F32),
        compiler_params=_cparams("arbitrary", "arbitrary"),
    )(yg, probs, x, mods, final_g.reshape(1, d))


def kernel(x, c, ctx, c_ctx, norm1_g, norm2_g, w_mod, b_mod, w_in, gla_wg2_f, gla_bg_f, gla_wg2_b,
           gla_bg_b, gla_norm_g, conv_w, conv_b, conv_ln_g, conv_ln_b, w_out, router_w, router_b,
           exp_w_gu, exp_b_gu, exp_w_dn, exp_b_dn, final_norm_g):
    bsz, seq, d = x.shape
    n_ctx = ctx.shape[1]
    depth = w_mod.shape[0]
    assert d == D_MODEL and seq % (FFT_N2 * FFT_KB) == 0 and seq % n_ctx == 0
    lt = seq + n_ctx
    ctx_group = bsz

    rows = 8
    cvec = jnp.concatenate([c, c_ctx[None, :], jnp.zeros((rows - bsz - 1, d), F32)], axis=0)
    mods_all = _mod_call(cvec, w_mod, b_mod).reshape(depth, rows, 6, 1, d)

    def pack_w_in(w):
        o = [0, 256, 512, 768, 1280, 1792, 1808, 1824, 2336]
        parts = [w[:, o[0]:o[1]], w[:, o[1]:o[2]], w[:, o[2]:o[3]], w[:, o[3]:o[4]], w[:, o[4]:o[5]],
                 w[:, o[7]:o[7] + CONV_WIDTH], w[:, o[7] + CONV_WIDTH:o[8]], w[:, o[5]:o[7]],
                 jnp.zeros((d, GLR_PAD - 2 * GLA_GATE_RANK), w.dtype)]
        return jnp.concatenate(parts, axis=1).astype(BF16)

    long_tables = _dft_tables(seq)
    chan = _channel_tables()
    x_lat, x_ctx = x, ctx
    lat_group = lambda b: b
    ctx_group_fn = lambda b: ctx_group

    for layer in range(depth):
        last = layer == depth - 1
        mods = mods_all[layer]
        w_in_p = pack_w_in(w_in[layer])
        wg_pad = jnp.zeros((GLR_PAD, 2 * GLA_KEY_WIDTH), F32)
        wg_pad = wg_pad.at[:GLA_GATE_RANK, :GLA_KEY_WIDTH].set(gla_wg2_f[layer])
        wg_pad = wg_pad.at[GLA_GATE_RANK:2 * GLA_GATE_RANK, GLA_KEY_WIDTH:].set(gla_wg2_b[layer])
        bg_cat = jnp.concatenate([gla_bg_f[layer], gla_bg_b[layer]])[None, :]
        w_out_b = w_out[layer].astype(BF16)
        gn_tiled = jnp.tile(gla_norm_g[layer], GLA_HEADS)[None, :]
        rw_t = router_w[layer].T
        rw_hi = rw_t.astype(BF16)
        rw_lo = (rw_t - rw_hi.astype(F32)).astype(BF16)
        rb = router_b[layer][:, None]

        (u_l, r_l, glu_l), comb = _inproj_call(x_lat, mods, lat_group, norm1_g[layer], w_in_p, lt, 0)
        (u_c, r_c, glu_c), comb = _inproj_call(x_ctx, mods, ctx_group_fn, norm1_g[layer], w_in_p, lt,
                                               seq, combined=comb)
        o_f, o_b = _gla_call(*comb, wg_pad, bg_cat, seq, n_ctx)
        yf_l = _fourier_long(u_l, long_tables, chan)
        cv_l = _conv_call(glu_l, seq // (seq // GRID_W), conv_w[layer], conv_b[layer],
                          conv_ln_g[layer], conv_ln_b[layer])
        n_tok = bsz * seq + (0 if last else bsz * n_ctx)
        x_lat, routed = _outproj_call(yf_l, o_f, o_b, 0, r_l, cv_l, x_lat, mods, lat_group, gn_tiled,
                                      w_out_b, norm2_g[layer], rw_hi, rw_lo, rb, n_tok, 0)
        if not last:
            yf_c = _fourier_short(u_c, chan)
            cv_c = _conv_call(glu_c, n_ctx, conv_w[layer], conv_b[layer], conv_ln_g[layer],
                              conv_ln_b[layer])
            x_ctx, routed = _outproj_call(yf_c, o_f, o_b, seq, r_c, cv_c, x_ctx, mods, ctx_group_fn,
                                          gn_tiled, w_out_b, norm2_g[layer], rw_hi, rw_lo, rb, n_tok,
                                          bsz * seq, carried=routed)

        h2, idx_t, prob_t = routed
        slot_tok, block_e, n_used, assign_slot = _routing_tables(idx_t, n_tok)
        xs = jnp.take(h2, slot_tok, axis=0)
        yb = _expert_call(block_e, n_used, xs, layer, exp_w_gu, exp_b_gu, exp_w_dn, exp_b_dn)
        yg = jnp.take(yb, assign_slot, axis=0)
        probs = prob_t.T
        x_lat = _combine_call(yg, probs, x_lat, mods, lat_group, final_norm_g, 0, last)
        if not last:
            x_ctx = _combine_call(yg, probs, x_ctx, mods, ctx_group_fn, final_norm_g, bsz * seq, False)

    return x_lat
```

```python
import functools

import jax
import jax.numpy as jnp
from jax import lax
from jax.experimental import pallas as pl
from jax.experimental.pallas import tpu as pltpu

F32 = jnp.float32
BF16 = jnp.bfloat16

D_MODEL = 1024
DEPTH = 2
GRID_W = 64
FOURIER_WIDTH = 256
FOURIER_HEADS = 4
FOURIER_HEAD_DIM = FOURIER_WIDTH // FOURIER_HEADS
GLA_HEADS = 4
GLA_KEY_WIDTH = 256
GLA_VALUE_WIDTH = 512
GLA_DK = GLA_KEY_WIDTH // GLA_HEADS
GLA_DV = GLA_VALUE_WIDTH // GLA_HEADS
GLA_GATE_RANK = 16
GLA_GATE_NORMALIZER = 16.0
CONV_WIDTH = 256
CONV_KERNEL = 31
N_EXPERTS = 32
TOP_K = 4
D_FF = D_MODEL
SWIGLU_LIMIT = 7.0
SWIGLU_ALPHA = 1.702
NORM_EPS = 1e-6

LANES = 128
VMEM_LIMIT = 48 * 1024 * 1024

COL_U = 0
COL_Q = COL_U + FOURIER_WIDTH
COL_K = COL_Q + GLA_KEY_WIDTH
COL_V = COL_K + GLA_KEY_WIDTH
COL_R = COL_V + GLA_VALUE_WIDTH
COL_CA = COL_R + GLA_VALUE_WIDTH
COL_CG = COL_CA + CONV_WIDTH
COL_GL = COL_CG + CONV_WIDTH
GLR_PAD = LANES
IN_PAD = COL_GL + GLR_PAD

GLA_CHUNK = 128
TM_ROWS = 512
TM_EXPERT = 256
FFT_N2 = 128
FFT_KB = 8
CONV_HALO = 16


def _cparams(*sem):
    return pltpu.CompilerParams(dimension_semantics=sem, vmem_limit_bytes=VMEM_LIMIT)


def _dot(a, b):
    return jnp.dot(a, b, preferred_element_type=F32)


def _dot_nt(a, b):
    return lax.dot_general(a, b, (((1,), (1,)), ((), ())), preferred_element_type=F32)


def _dot_tn(a, b):
    return lax.dot_general(a, b, (((0,), (0,)), ((), ())), preferred_element_type=F32)


def _split(a):
    hi = a.astype(BF16)
    lo = (a - hi.astype(F32)).astype(BF16)
    return hi, lo


def _dot3(a, b):
    ah, al = _split(a)
    bh, bl = _split(b)
    return _dot(ah, bh) + _dot(ah, bl) + _dot(al, bh)


def _sigmoid(x):
    return 1.0 / (1.0 + jnp.exp(-x))


def _rms(x, g):
    ms = jnp.mean(x * x, axis=-1, keepdims=True)
    return x * lax.rsqrt(ms + NORM_EPS) * g


def _mod_kernel(cv_ref, w_ref, b_ref, o_ref):
    cv = cv_ref[...]
    a = cv * _sigmoid(cv)
    o_ref[...] = _dot3(a, w_ref[...]) + b_ref[...]


def _mod_call(cvec, w_mod, b_mod):
    depth, d, n = w_mod.shape
    rows = cvec.shape[0]
    tn = 1536
    return pl.pallas_call(
        _mod_kernel,
        grid=(depth, n // tn),
        in_specs=[
            pl.BlockSpec((rows, d), lambda l, j: (0, 0)),
            pl.BlockSpec((None, d, tn), lambda l, j: (l, 0, j)),
            pl.BlockSpec((None, 1, tn), lambda l, j: (l, 0, j)),
        ],
        out_specs=pl.BlockSpec((None, rows, tn), lambda l, j: (l, 0, j)),
        out_shape=jax.ShapeDtypeStruct((depth, rows, n), F32),
        compiler_params=_cparams("arbitrary", "arbitrary"),
    )(cvec, w_mod, b_mod.reshape(depth, 1, n))


def _inproj_kernel(x_ref, g_ref, sh_ref, sc_ref, w_ref, *rest):
    u_ref, r_ref, glu_ref, q_ref, k_ref, v_ref, gl_ref = rest[-7:]
    x = x_ref[...]
    h = _rms(x, g_ref[...]) * (1.0 + sc_ref[...]) + sh_ref[...]
    p = _dot(h.astype(BF16), w_ref[...])
    u_ref[...] = p[:, COL_U:COL_Q].astype(u_ref.dtype)
    q_ref[...] = (p[:, COL_Q:COL_K] * (GLA_DK ** -0.5)).astype(q_ref.dtype)
    k_ref[...] = p[:, COL_K:COL_V].astype(k_ref.dtype)
    v_ref[...] = p[:, COL_V:COL_R].astype(v_ref.dtype)
    r_ref[...] = p[:, COL_R:COL_CA].astype(r_ref.dtype)
    glu_ref[...] = (p[:, COL_CA:COL_CG] * _sigmoid(p[:, COL_CG:COL_GL])).astype(glu_ref.dtype)
    gl_ref[...] = p[:, COL_GL:IN_PAD]


def _inproj_call(x, mods, group_of_batch, norm_g, w_in_p, lt, row0, combined=None):
    bsz, n, d = x.shape
    tm = min(TM_ROWS, n)
    assert n % tm == 0 and row0 % tm == 0
    blk0 = row0 // tm
    widths = (GLA_KEY_WIDTH, GLA_KEY_WIDTH, GLA_VALUE_WIDTH, GLR_PAD)
    dtypes = (BF16, BF16, BF16, F32)
    row_spec = lambda w: pl.BlockSpec((None, tm, w), lambda b, i: (b, i, 0))
    comb_spec = lambda w: pl.BlockSpec((None, tm, w), lambda b, i: (b, blk0 + i, 0))
    mod_spec = lambda which: pl.BlockSpec(
        (None, None, 1, d), lambda b, i: (group_of_batch(b), which, 0, 0))
    in_specs = [
        row_spec(d),
        pl.BlockSpec((1, d), lambda b, i: (0, 0)),
        mod_spec(0), mod_spec(1),
        pl.BlockSpec((d, IN_PAD), lambda b, i: (0, 0)),
    ]
    args = [x, norm_g.reshape(1, d), mods, mods, w_in_p]
    aliases = {}
    if combined is not None:
        for t, arr in enumerate(combined):
            in_specs.append(pl.BlockSpec(memory_space=pl.ANY))
            aliases[len(args)] = 3 + t
            args.append(arr)
    out_shape = [
        jax.ShapeDtypeStruct((bsz, n, FOURIER_WIDTH), BF16),
        jax.ShapeDtypeStruct((bsz, n, GLA_VALUE_WIDTH), BF16),
        jax.ShapeDtypeStruct((bsz, n, CONV_WIDTH), BF16),
    ] + [jax.ShapeDtypeStruct((bsz, lt, w), dt) for w, dt in zip(widths, dtypes)]
    out_specs = [row_spec(FOURIER_WIDTH), row_spec(GLA_VALUE_WIDTH), row_spec(CONV_WIDTH)] + [
        comb_spec(w) for w in widths]
    outs = pl.pallas_call(
        _inproj_kernel,
        grid=(bsz, n // tm),
        in_specs=in_specs,
        out_specs=out_specs,
        out_shape=out_shape,
        input_output_aliases=aliases,
        compiler_params=_cparams("arbitrary", "arbitrary"),
    )(*args)
    return outs[:3], outs[3:]


def _dft_tables(length):
    n2 = FFT_N2
    n1 = length // n2
    two_pi = 2.0 * jnp.pi

    def cs(num, den):
        ang = (num % den).astype(F32) * (two_pi / den)
        return jnp.cos(ang), jnp.sin(ang)

    k1 = jnp.arange(n1, dtype=jnp.int32)
    c1, s1 = cs(k1[:, None] * k1[None, :], n1)
    stage1 = (jnp.concatenate([c1, -s1], axis=0) * (n1 ** -0.5)).astype(BF16)
    k2 = jnp.arange(n2, dtype=jnp.int32)
    ct, st = cs(k1[:, None] * k2[None, :], length)
    cf, sf = cs(k2[:, None] * k2[None, :], n2)
    scale = n2 ** -0.5
    mr = (ct[:, None, :] * cf[None] - st[:, None, :] * sf[None]) * scale
    mi = -(st[:, None, :] * cf[None] + ct[:, None, :] * sf[None]) * scale
    stage2 = jnp.concatenate([jnp.concatenate([mr, -mi], axis=2),
                              jnp.concatenate([mi, mr], axis=2)], axis=1).astype(BF16)
    return stage1, stage2


def _channel_tables():
    hd = FOURIER_HEAD_DIM
    c = jnp.arange(FOURIER_WIDTH, dtype=jnp.int32)
    same_head = (c[:, None] // hd) == (c[None, :] // hd)
    ang = (((c[:, None] % hd) * (c[None, :] % hd)) % hd).astype(F32) * (2.0 * jnp.pi / hd)
    scale = hd ** -0.5
    bdc = jnp.where(same_head, jnp.cos(ang) * scale, 0.0).astype(BF16)
    bds = jnp.where(same_head, jnp.sin(ang) * scale, 0.0).astype(BF16)
    return bdc, bds


def _fft1_kernel(x_ref, cs_ref, zr_ref, zi_ref):
    n1 = x_ref.shape[0]
    z = _dot(cs_ref[...], x_ref[...])
    zr_ref[...] = z[:n1].astype(zr_ref.dtype)
    zi_ref[...] = z[n1:].astype(zi_ref.dtype)


def _fft2_kernel(zr_ref, zi_ref, m_ref, bdc_ref, bds_ref, o_ref):
    kb, n2, w = zr_ref.shape
    for j in range(kb):
        z = jnp.concatenate([zr_ref[j], zi_ref[j]], axis=0)
        a = _dot(m_ref[j], z)
        y = _dot(a[:n2].astype(BF16), bdc_ref[...]) + _dot(a[n2:].astype(BF16), bds_ref[...])
        o_ref[:, j * w:(j + 1) * w] = y.astype(o_ref.dtype)


def _fourier_long(u, tables, chan):
    bsz, length, w = u.shape
    stage1, stage2 = tables
    bdc, bds = chan
    n2 = FFT_N2
    n1 = length // n2
    tn = 4096
    cols = n2 * w
    zr, zi = pl.pallas_call(
        _fft1_kernel,
        grid=(bsz, cols // tn),
        in_specs=[pl.BlockSpec((None, n1, tn), lambda b, j: (b, 0, j)),
                  pl.BlockSpec((2 * n1, n1), lambda b, j: (0, 0))],
        out_specs=[pl.BlockSpec((None, n1, tn), lambda b, j: (b, 0, j))] * 2,
        out_shape=[jax.ShapeDtypeStruct((bsz, n1, cols), BF16)] * 2,
        compiler_params=_cparams("arbitrary", "arbitrary"),
    )(u.reshape(bsz, n1, cols), stage1)
    kb = FFT_KB
    z_spec = pl.BlockSpec((None, kb, n2, w), lambda b, j: (b, j, 0, 0))
    y = pl.pallas_call(
        _fft2_kernel,
        grid=(bsz, n1 // kb),
        in_specs=[z_spec, z_spec,
                  pl.BlockSpec((kb, 2 * n2, 2 * n2), lambda b, j: (j, 0, 0)),
                  pl.BlockSpec((w, w), lambda b, j: (0, 0)),
                  pl.BlockSpec((w, w), lambda b, j: (0, 0))],
        out_specs=pl.BlockSpec((None, n2, kb * w), lambda b, j: (b, 0, j)),
        out_shape=jax.ShapeDtypeStruct((bsz, n2, n1 * w), BF16),
        compiler_params=_cparams("arbitrary", "arbitrary"),
    )(zr.reshape(bsz, n1, n2, w), zi.reshape(bsz, n1, n2, w), stage2, bdc, bds)
    return y.reshape(bsz, length, w)


def _dft_short_kernel(u_ref, c_ref, s_ref, bdc_ref, bds_ref, o_ref):
    u = u_ref[...]
    p = _dot(u, bdc_ref[...]).astype(BF16)
    q = _dot(u, bds_ref[...]).astype(BF16)
    o_ref[...] = (_dot(c_ref[...], p) - _dot(s_ref[...], q)).astype(o_ref.dtype)


def _fourier_short(u, chan):
    bsz, length, w = u.shape
    bdc, bds = chan
    k = jnp.arange(length, dtype=jnp.int32)
    ang = ((k[:, None] * k[None, :]) % length).astype(F32) * (2.0 * jnp.pi / length)
    c = (jnp.cos(ang) * length ** -0.5).astype(BF16)
    s = (jnp.sin(ang) * length ** -0.5).astype(BF16)
    full = lambda n: pl.BlockSpec((n, n), lambda b: (0, 0))
    return pl.pallas_call(
        _dft_short_kernel,
        grid=(bsz,),
        in_specs=[pl.BlockSpec((None, length, w), lambda b: (b, 0, 0)),
                  full(length), full(length), full(w), full(w)],
        out_specs=pl.BlockSpec((None, length, w), lambda b: (b, 0, 0)),
        out_shape=jax.ShapeDtypeStruct((bsz, length, w), BF16),
        compiler_params=_cparams("arbitrary"),
    )(u, c, s, bdc, bds)


def _conv_kernel(x_ref, w_ref, cb_ref, lg_ref, lb_ref, o_ref, pad_ref, *, seg):
    nseg = x_ref.shape[0] // seg
    width = x_ref.shape[1]
    halo = jnp.zeros((CONV_HALO, width), F32)
    for s in range(nseg):
        pad_ref[s, 0:CONV_HALO, :] = halo
        pad_ref[s, CONV_HALO:CONV_HALO + seg, :] = x_ref[s * seg:(s + 1) * seg, :].astype(F32)
        pad_ref[s, CONV_HALO + seg:2 * CONV_HALO + seg, :] = halo
    first = CONV_HALO - CONV_KERNEL // 2
    for s in range(nseg):
        acc = jnp.zeros((seg, width), F32)
        for j in range(CONV_KERNEL):
            acc = acc + pad_ref[s, first + j:first + j + seg, :] * w_ref[j:j + 1, :]
        y = acc + cb_ref[...]
        mu = jnp.mean(y, axis=-1, keepdims=True)
        yc = y - mu
        var = jnp.mean(yc * yc, axis=-1, keepdims=True)
        z = yc * lax.rsqrt(var + NORM_EPS) * lg_ref[...] + lb_ref[...]
        o_ref[s * seg:(s + 1) * seg, :] = (z * _sigmoid(z)).astype(o_ref.dtype)


def _conv_call(glu, seg, conv_w, conv_b, ln_g, ln_b):
    bsz, n, w = glu.shape
    t = max(seg, min(TM_ROWS, n))
    assert n % t == 0 and t % seg == 0
    vec = lambda: pl.BlockSpec((1, w), lambda b, i: (0, 0))
    return pl.pallas_call(
        functools.partial(_conv_kernel, seg=seg),
        grid=(bsz, n // t),
        in_specs=[pl.BlockSpec((None, t, w), lambda b, i: (b, i, 0)),
                  pl.BlockSpec((CONV_KERNEL, w), lambda b, i: (0, 0)),
                  vec(), vec(), vec()],
        out_specs=pl.BlockSpec((None, t, w), lambda b, i: (b, i, 0)),
        out_shape=jax.ShapeDtypeStruct((bsz, n, w), BF16),
        scratch_shapes=[pltpu.VMEM((t // seg, seg + 2 * CONV_HALO, w), F32)],
        compiler_params=_cparams("arbitrary", "arbitrary"),
    )(glu, conv_w, conv_b.reshape(1, w), ln_g.reshape(1, w), ln_b.reshape(1, w))


def _gla_direction(q_ref, k_ref, v_ref, gl_ref, wg_ref, bg_ref, o_ref, st_ref, reverse):
    c = q_ref.shape[0]
    kw = GLA_KEY_WIDTH
    col0 = kw if reverse else 0
    pre = _dot3(gl_ref[...], wg_ref[:, col0:col0 + kw]) + bg_ref[:, col0:col0 + kw]
    g = (jnp.minimum(pre, 0.0) - jnp.log(1.0 + jnp.exp(-jnp.abs(pre)))) * (1.0 / GLA_GATE_NORMALIZER)
    row = lax.broadcasted_iota(jnp.int32, (c, c), 0)
    col = lax.broadcasted_iota(jnp.int32, (c, c), 1)
    seen = (col >= row) if reverse else (col <= row)
    tri = jnp.where(seen, 1.0, 0.0).astype(BF16)
    gh, gl = _split(g)
    b = _dot(tri, gh) + _dot(tri, gl)
    mid = c // 2 if reverse else c // 2 - 1
    last = 0 if reverse else c - 1
    b_mid = b[mid:mid + 1, :]
    b_last = b[last:last + 1, :]
    q = q_ref[...].astype(F32)
    k = k_ref[...].astype(F32)
    qe = q * jnp.exp(b - b_mid)
    ke = k * jnp.exp(b_mid - b)
    kd = k * jnp.exp(b_last - b)
    st = st_ref[...]
    head_of_lane = lax.broadcasted_iota(jnp.int32, (1, kw), 1) // GLA_DK
    q_heads = jnp.concatenate(
        [jnp.where(head_of_lane == h, qe, 0.0) for h in range(GLA_HEADS)], axis=0).astype(BF16)
    rhs = jnp.concatenate([ke, st * jnp.exp(b_mid)], axis=0).astype(BF16)
    res = _dot_nt(q_heads, rhs)
    v = v_ref[...]
    outs = []
    for h in range(GLA_HEADS):
        blk = res[h * c:(h + 1) * c, :]
        scores = jnp.where(seen, blk[:, :c], 0.0).astype(BF16)
        outs.append(_dot(scores, v[:, h * GLA_DV:(h + 1) * GLA_DV]) + blk[:, c:])
    o_ref[...] = jnp.concatenate(outs, axis=1).astype(o_ref.dtype)
    kv = _dot_tn(v, kd.astype(BF16))
    ds = jnp.zeros_like(st)
    for h in range(GLA_HEADS):
        ds = ds + jnp.where(head_of_lane == h, kv[h * GLA_DV:(h + 1) * GLA_DV, :], 0.0)
    st_ref[...] = st * jnp.exp(b_last) + ds


def _gla_kernel(qf, kf, vf, gf, qb, kb, vb, gb, wg_ref, bg_ref, of_ref, ob_ref, sf_ref, sb_ref):
    @pl.when(pl.program_id(1) == 0)
    def _():
        sf_ref[...] = jnp.zeros_like(sf_ref)
        sb_ref[...] = jnp.zeros_like(sb_ref)

    _gla_direction(qf, kf, vf, gf, wg_ref, bg_ref, of_ref, sf_ref, False)
    _gla_direction(qb, kb, vb, gb, wg_ref, bg_ref, ob_ref, sb_ref, True)


def _gla_call(q, k, v, gl, wg_pad, bg_cat, n_lat, n_ctx):
    bsz, lt, _ = q.shape
    c = GLA_CHUNK
    assert n_lat % c == 0 and n_ctx % c == 0
    cl, cc = n_lat // c, n_ctx // c

    def fwd_blk(j):
        return jnp.where(j < cc, cl + j, j - cc)

    def bwd_blk(j):
        return jnp.where(j < cc, cl + cc - 1 - j, cl - 1 - (j - cc))

    def spec(w, blk):
        return pl.BlockSpec((None, c, w), lambda b, j: (b, blk(j), 0))

    widths = (GLA_KEY_WIDTH, GLA_KEY_WIDTH, GLA_VALUE_WIDTH, GLR_PAD)
    in_specs = [spec(w, fwd_blk) for w in widths] + [spec(w, bwd_blk) for w in widths] + [
        pl.BlockSpec(wg_pad.shape, lambda b, j: (0, 0)),
        pl.BlockSpec(bg_cat.shape, lambda b, j: (0, 0))]
    return pl.pallas_call(
        _gla_kernel,
        grid=(bsz, cl + cc),
        in_specs=in_specs,
        out_specs=[spec(GLA_VALUE_WIDTH, fwd_blk), spec(GLA_VALUE_WIDTH, bwd_blk)],
        out_shape=[jax.ShapeDtypeStruct((bsz, lt, GLA_VALUE_WIDTH), F32)] * 2,
        scratch_shapes=[pltpu.VMEM((GLA_DV, GLA_KEY_WIDTH), F32)] * 2,
        compiler_params=_cparams("arbitrary", "arbitrary"),
    )(q, k, v, gl, q, k, v, gl, wg_pad, bg_cat)


def _outproj_kernel(yf_ref, of_ref, ob_ref, r_ref, cv_ref, x_ref, gate_ref, sh_ref, sc_ref,
                    gn_ref, wo_ref, n2_ref, rwh_ref, rwl_ref, rb_ref, *rest):
    xo_ref, h2_ref, idx_ref, prob_ref = rest[-4:]
    o = of_ref[...] + ob_ref[...]
    heads = []
    for h in range(GLA_HEADS):
        oh = o[:, h * GLA_DV:(h + 1) * GLA_DV]
        heads.append(oh * lax.rsqrt(jnp.mean(oh * oh, axis=-1, keepdims=True) + NORM_EPS))
    r = r_ref[...].astype(F32)
    gla = jnp.concatenate(heads, axis=1) * gn_ref[...] * (r * _sigmoid(r))
    c0, c1 = FOURIER_WIDTH, FOURIER_WIDTH + GLA_VALUE_WIDTH
    y = (_dot(yf_ref[...], wo_ref[0:c0, :]) + _dot(gla.astype(BF16), wo_ref[c0:c1, :])
         + _dot(cv_ref[...], wo_ref[c1:, :]))
    xn = x_ref[...] + gate_ref[...] * y
    xo_ref[...] = xn
    h2 = _rms(xn, n2_ref[...]) * (1.0 + sc_ref[...]) + sh_ref[...]
    hh, hl = _split(h2)
    h2_ref[...] = h2
    logits = (_dot_nt(rwh_ref[...], hh) + _dot_nt(rwh_ref[...], hl) + _dot_nt(rwl_ref[...], hh)
              + rb_ref[...])
    expert = lax.broadcasted_iota(jnp.int32, logits.shape, 0)
    vals, idxs = [], []
    cur = logits
    for _ in range(TOP_K):
        m = jnp.max(cur, axis=0, keepdims=True)
        ix = jnp.min(jnp.where(cur == m, expert, N_EXPERTS), axis=0, keepdims=True)
        vals.append(m)
        idxs.append(ix)
        cur = jnp.where(expert == ix, -jnp.inf, cur)
    es = [jnp.exp(vv - vals[0]) for vv in vals]
    inv = 1.0 / functools.reduce(lambda a, b: a + b, es)
    idx_ref[...] = jnp.concatenate(idxs, axis=0)
    prob_ref[...] = jnp.concatenate([e * inv for e in es], axis=0)


def _outproj_call(yf, o_f, o_b, o_row0, r, cv, x, mods, group_of_batch, gn_tiled, w_out, norm2_g,
                  rw_hi, rw_lo, rb, n_tok, tok0, carried=None):
    bsz, n, d = x.shape
    tm = min(TM_ROWS, n)
    assert n % tm == 0 and o_row0 % tm == 0 and tok0 % tm == 0
    nb = n // tm
    row = lambda w: pl.BlockSpec((None, tm, w), lambda b, i: (b, i, 0))
    orow = pl.BlockSpec((None, tm, GLA_VALUE_WIDTH), lambda b, i: (b, o_row0 // tm + i, 0))
    mod = lambda which: pl.BlockSpec((None, None, 1, d), lambda b, i: (group_of_batch(b), which, 0, 0))
    const = lambda a: pl.BlockSpec(a.shape, lambda b, i: (0,) * a.ndim)
    consts = [gn_tiled, w_out, norm2_g.reshape(1, d), rw_hi, rw_lo, rb]
    in_specs = [row(FOURIER_WIDTH), orow, orow, row(GLA_VALUE_WIDTH), row(CONV_WIDTH), row(d),
                mod(2), mod(3), mod(4)] + [const(a) for a in consts]
    args = [yf, o_f, o_b, r, cv, x, mods, mods, mods] + consts
    aliases = {}
    if carried is not None:
        for t, arr in enumerate(carried):
            in_specs.append(pl.BlockSpec(memory_space=pl.ANY))
            aliases[len(args)] = 1 + t
            args.append(arr)
    tokblk = lambda b, i: tok0 // tm + b * nb + i
    out_specs = [row(d),
                 pl.BlockSpec((tm, d), lambda b, i: (tokblk(b, i), 0)),
                 pl.BlockSpec((TOP_K, tm), lambda b, i: (0, tokblk(b, i))),
                 pl.BlockSpec((TOP_K, tm), lambda b, i: (0, tokblk(b, i)))]
    out_shape = [jax.ShapeDtypeStruct((bsz, n, d), F32),
                 jax.ShapeDtypeStruct((n_tok, d), F32),
                 jax.ShapeDtypeStruct((TOP_K, n_tok), jnp.int32),
                 jax.ShapeDtypeStruct((TOP_K, n_tok), F32)]
    outs = pl.pallas_call(
        _outproj_kernel,
        grid=(bsz, nb),
        in_specs=in_specs,
        out_specs=out_specs,
        out_shape=out_shape,
        input_output_aliases=aliases,
        compiler_params=_cparams("arbitrary", "arbitrary"),
    )(*args)
    return outs[0], outs[1:]


def _expert_kernel(be_ref, nu_ref, x_ref, wgu_ref, bgu_ref, wdn_ref, bdn_ref, o_ref, wgu_s, wdn_s):
    i = pl.program_id(0)

    @pl.when(i >= nu_ref[0])
    def _():
        o_ref[...] = jnp.zeros_like(o_ref)

    @pl.when(i < nu_ref[0])
    def _():
        changed = jnp.logical_or(i == 0, be_ref[i] != be_ref[jnp.maximum(i - 1, 0)])

        @pl.when(changed)
        def _():
            rows = 128
            for s in range(wgu_ref.shape[0] // rows):
                wgu_s[s * rows:(s + 1) * rows, :] = wgu_ref[s * rows:(s + 1) * rows, :].astype(BF16)
                wdn_s[s * rows:(s + 1) * rows, :] = wdn_ref[s * rows:(s + 1) * rows, :].astype(BF16)

        gu = _dot(x_ref[...].astype(BF16), wgu_s[...]) + bgu_ref[...]
        gate = jnp.minimum(gu[:, :D_FF], SWIGLU_LIMIT)
        up = jnp.clip(gu[:, D_FF:], -SWIGLU_LIMIT, SWIGLU_LIMIT)
        act = gate * _sigmoid(SWIGLU_ALPHA * gate) * (up + 1.0)
        o_ref[...] = (_dot(act.astype(BF16), wdn_s[...]) + bdn_ref[...]).astype(o_ref.dtype)


def _expert_call(block_e, n_used, xs, layer, w_gu, b_gu, w_dn, b_dn):
    rows, d = xs.shape
    tm = TM_EXPERT
    nblk = rows // tm
    depth, e, _, f2 = w_gu.shape
    live = lambda i, nu: jnp.minimum(i, nu[0] - 1)
    wmap = lambda i, be, nu: (layer, be[live(i, nu)], 0, 0)
    grid_spec = pltpu.PrefetchScalarGridSpec(
        num_scalar_prefetch=2,
        grid=(nblk,),
        in_specs=[
            pl.BlockSpec((tm, d), lambda i, be, nu: (live(i, nu), 0)),
            pl.BlockSpec((None, None, d, f2), wmap),
            pl.BlockSpec((None, None, 1, f2), wmap),
            pl.BlockSpec((None, None, f2 // 2, d), wmap),
            pl.BlockSpec((None, None, 1, d), wmap),
        ],
        out_specs=pl.BlockSpec((tm, d), lambda i, be, nu: (i, 0)),
        scratch_shapes=[pltpu.VMEM((d, f2), BF16), pltpu.VMEM((f2 // 2, d), BF16)],
    )
    return pl.pallas_call(
        _expert_kernel,
        grid_spec=grid_spec,
        out_shape=jax.ShapeDtypeStruct((rows, d), F32),
        compiler_params=_cparams("arbitrary"),
    )(block_e, n_used, xs, w_gu, b_gu.reshape(depth, e, 1, f2), w_dn, b_dn.reshape(depth, e, 1, d))


def _rank_kernel(idx_ref, rank_ref, cnt_ref, base_ref):
    @pl.when(pl.program_id(0) == 0)
    def _():
        base_ref[...] = jnp.zeros_like(base_ref)

    tb = idx_ref.shape[1]
    row = lax.broadcasted_iota(jnp.int32, (tb, tb), 0)
    col = lax.broadcasted_iota(jnp.int32, (tb, tb), 1)
    earlier = jnp.where(row < col, 1.0, 0.0).astype(BF16)
    expert = lax.broadcasted_iota(jnp.int32, (N_EXPERTS, tb), 0)
    base = base_ref[...]
    ranks = []
    for kk in range(TOP_K):
        hit = expert == idx_ref[kk:kk + 1, :]
        onehot = hit.astype(F32)
        before = _dot(onehot.astype(BF16), earlier) + base
        ranks.append(jnp.sum(before * onehot, axis=0, keepdims=True))
        base = base + jnp.sum(onehot, axis=1, keepdims=True)
    rank_ref[...] = jnp.concatenate(ranks, axis=0).astype(jnp.int32)
    base_ref[...] = base
    cnt_ref[...] = base.astype(jnp.int32)


def _slot_kernel(idx_ref, rank_ref, start_ref, slot_ref):
    expert = lax.broadcasted_iota(jnp.int32, (N_EXPERTS, idx_ref.shape[1]), 0)
    rows = []
    for kk in range(TOP_K):
        start = jnp.sum(jnp.where(expert == idx_ref[kk:kk + 1, :], start_ref[...], 0), axis=0,
                        keepdims=True)
        rows.append(rank_ref[kk:kk + 1, :] + start)
    slot_ref[...] = jnp.concatenate(rows, axis=1)


def _row_copy(src, s, dst, t, sem):
    return pltpu.make_async_copy(src.at[pl.ds(s, 1), :], dst.at[pl.ds(t, 1), :], sem)


def _dispatch_kernel(pe_ref, nu_ref, slot_hbm, h_ref, xs_ref, slot_s, zero_ref, sem, zsem, ssem, *,
                     n_blocks):
    i = pl.program_id(0)
    tb = h_ref.shape[0]
    tm = zero_ref.shape[0]
    fetch = pltpu.make_async_copy(slot_hbm.at[i, 0], slot_s, ssem)
    fetch.start()

    def zero_copy(blk):
        return pltpu.make_async_copy(zero_ref, xs_ref.at[pl.ds(pl.multiple_of(blk * tm, tm), tm), :], zsem)

    @pl.when(i == 0)
    def _():
        zero_ref[...] = jnp.zeros_like(zero_ref)

        def per_expert(fn):
            for e in range(N_EXPERTS):
                end = pe_ref[e]
                start = pe_ref[e - 1] if e else 0

                @pl.when(end > start)
                def _():
                    fn(end // tm - 1)

        def per_tail(fn):
            def body(blk, carry):
                fn(blk)
                return carry
            lax.fori_loop(nu_ref[0], n_blocks, body, 0)

        per_expert(lambda blk: zero_copy(blk).start())
        per_tail(lambda blk: zero_copy(blk).start())
        per_expert(lambda blk: zero_copy(blk).wait())
        per_tail(lambda blk: zero_copy(blk).wait())

    fetch.wait()

    def issue(g, carry):
        t0 = pl.multiple_of(g * 8, 8)
        for j in range(8):
            for kk in range(TOP_K):
                _row_copy(h_ref, t0 + j, xs_ref, slot_s[kk * tb + t0 + j], sem).start()
        return carry

    lax.fori_loop(0, tb // 8, issue, 0)
    for kk in range(TOP_K):
        pltpu.make_async_copy(h_ref, xs_ref.at[pl.ds(0, tb), :], sem).wait()


def _dispatch_call(pad_end, n_used, slot_blocks, h2, n_blocks):
    n_tok, d = h2.shape
    nblk, _, ktb = slot_blocks.shape
    tb = ktb // TOP_K
    tm = TM_EXPERT
    grid_spec = pltpu.PrefetchScalarGridSpec(
        num_scalar_prefetch=2,
        grid=(nblk,),
        in_specs=[pl.BlockSpec(memory_space=pl.ANY),
                  pl.BlockSpec((tb, d), lambda i, pe, nu: (i, 0))],
        out_specs=pl.BlockSpec(memory_space=pl.ANY),
        scratch_shapes=[pltpu.SMEM((ktb,), jnp.int32), pltpu.VMEM((tm, d), F32),
                        pltpu.SemaphoreType.DMA, pltpu.SemaphoreType.DMA, pltpu.SemaphoreType.DMA],
    )
    return pl.pallas_call(
        functools.partial(_dispatch_kernel, n_blocks=n_blocks),
        grid_spec=grid_spec,
        out_shape=jax.ShapeDtypeStruct((n_blocks * tm, d), F32),
        compiler_params=_cparams("arbitrary"),
    )(pad_end, n_used, slot_blocks, h2)


def _routing_tables(idx_t, n_tok):
    tm = TM_EXPERT
    tb = TM_ROWS
    assert n_tok % tb == 0
    blk = pl.BlockSpec((TOP_K, tb), lambda i: (0, i))
    rank_t, counts = pl.pallas_call(
        _rank_kernel,
        grid=(n_tok // tb,),
        in_specs=[blk],
        out_specs=[blk, pl.BlockSpec((N_EXPERTS, 1), lambda i: (0, 0))],
        out_shape=[jax.ShapeDtypeStruct((TOP_K, n_tok), jnp.int32),
                   jax.ShapeDtypeStruct((N_EXPERTS, 1), jnp.int32)],
        scratch_shapes=[pltpu.VMEM((N_EXPERTS, 1), F32)],
        compiler_params=_cparams("arbitrary"),
    )(idx_t)
    counts = counts[:, 0]
    padded = (counts + tm - 1) // tm * tm
    pad_end = jnp.cumsum(padded)
    pad_start = pad_end - padded
    slot_blocks = pl.pallas_call(
        _slot_kernel,
        grid=(n_tok // tb,),
        in_specs=[blk, blk, pl.BlockSpec((N_EXPERTS, 1), lambda i: (0, 0))],
        out_specs=pl.BlockSpec((None, 1, TOP_K * tb), lambda i: (i, 0, 0)),
        out_shape=jax.ShapeDtypeStruct((n_tok // tb, 1, TOP_K * tb), jnp.int32),
        compiler_params=_cparams("arbitrary"),
    )(idx_t, rank_t, pad_start[:, None])
    n_blocks = -(-TOP_K * n_tok // tm) + N_EXPERTS
    first_row = jnp.arange(n_blocks, dtype=jnp.int32) * tm
    block_e = jnp.minimum(jnp.sum(pad_end[None, :] <= first_row[:, None], axis=1),
                          N_EXPERTS - 1).astype(jnp.int32)
    n_used = (pad_end[-1:] // tm).astype(jnp.int32)
    return pad_end.astype(jnp.int32), block_e, n_used, slot_blocks, n_blocks


def _combine_kernel(slot_hbm, yb_hbm, p_ref, x_ref, gate_ref, fg_ref, o_ref, slot_s, ybuf, sem, ssem, *,
                    final_norm, blk0):
    i = pl.program_id(0)
    n = pl.num_programs(0)
    tb = x_ref.shape[0]

    def request(blk, buf):
        fetch = pltpu.make_async_copy(slot_hbm.at[blk0 + blk, 0], slot_s, ssem)
        fetch.start()
        fetch.wait()

        def issue(g, carry):
            t0 = pl.multiple_of(g * 8, 8)
            for j in range(8):
                for kk in range(TOP_K):
                    _row_copy(yb_hbm, slot_s[kk * tb + t0 + j], ybuf.at[buf, kk], t0 + j,
                              sem.at[buf]).start()
            return carry

        lax.fori_loop(0, tb // 8, issue, 0)

    @pl.when(i == 0)
    def _():
        request(0, 0)

    @pl.when(i + 1 < n)
    def _():
        request(i + 1, (i + 1) % 2)

    cur = i % 2
    for kk in range(TOP_K):
        pltpu.make_async_copy(yb_hbm.at[pl.ds(0, tb), :], ybuf.at[cur, kk], sem.at[cur]).wait()
    f = jnp.zeros(x_ref.shape, F32)
    for kk in range(TOP_K):
        f = f + ybuf[cur, kk] * p_ref[:, kk:kk + 1]
    xn = x_ref[...] + gate_ref[...] * f
    if final_norm:
        xn = _rms(xn, fg_ref[...])
    o_ref[...] = xn


def _combine_call(slot_blocks, yb, probs, x, mods, group_of_block, final_g, tok0, final_norm):
    bsz, n, d = x.shape
    ktb = slot_blocks.shape[2]
    tb = ktb // TOP_K
    rows = bsz * n
    assert rows % tb == 0 and tok0 % tb == 0 and (n % tb == 0 or tb % n == 0)
    blk0 = tok0 // tb
    out = pl.pallas_call(
        functools.partial(_combine_kernel, final_norm=final_norm, blk0=blk0),
        grid=(rows // tb,),
        in_specs=[pl.BlockSpec(memory_space=pl.ANY),
                  pl.BlockSpec(memory_space=pl.ANY),
                  pl.BlockSpec((tb, TOP_K), lambda i: (blk0 + i, 0)),
                  pl.BlockSpec((tb, d), lambda i: (i, 0)),
                  pl.BlockSpec((None, None, 1, d), lambda i: (group_of_block(i), 5, 0, 0)),
                  pl.BlockSpec((1, d), lambda i: (0, 0))],
        out_specs=pl.BlockSpec((tb, d), lambda i: (i, 0)),
        out_shape=jax.ShapeDtypeStruct((rows, d), F32),
        scratch_shapes=[pltpu.SMEM((ktb,), jnp.int32), pltpu.VMEM((2, TOP_K, tb, d), F32),
                        pltpu.SemaphoreType.DMA((2,)), pltpu.SemaphoreType.DMA],
        compiler_params=_cparams("arbitrary"),
    )(slot_blocks, yb, probs, x.reshape(rows, d), mods, final_g.reshape(1, d))
    return out.reshape(bsz, n, d)


def kernel(x, c, ctx, c_ctx, norm1_g, norm2_g, w_mod, b_mod, w_in, gla_wg2_f, gla_bg_f, gla_wg2_b,
           gla_bg_b, gla_norm_g, conv_w, conv_b, conv_ln_g, conv_ln_b, w_out, router_w, router_b,
           exp_w_gu, exp_b_gu, exp_w_dn, exp_b_dn, final_norm_g):
    bsz, seq, d = x.shape
    n_ctx = ctx.shape[1]
    depth = w_mod.shape[0]
    assert d == D_MODEL and seq % (FFT_N2 * FFT_KB) == 0 and seq % n_ctx == 0
    lt = seq + n_ctx
    ctx_group = bsz

    rows = 8
    cvec = jnp.concatenate([c, c_ctx[None, :], jnp.zeros((rows - bsz - 1, d), F32)], axis=0)
    mods_all = _mod_call(cvec, w_mod, b_mod).reshape(depth, rows, 6, 1, d)

    def pack_w_in(w):
        o = [0, 256, 512, 768, 1280, 1792, 1808, 1824, 2336]
        parts = [w[:, o[0]:o[1]], w[:, o[1]:o[2]], w[:, o[2]:o[3]], w[:, o[3]:o[4]], w[:, o[4]:o[5]],
                 w[:, o[7]:o[7] + CONV_WIDTH], w[:, o[7] + CONV_WIDTH:o[8]], w[:, o[5]:o[7]],
                 jnp.zeros((d, GLR_PAD - 2 * GLA_GATE_RANK), w.dtype)]
        return jnp.concatenate(parts, axis=1).astype(BF16)

    long_tables = _dft_tables(seq)
    chan = _channel_tables()
    x_lat, x_ctx = x, ctx
    lat_group = lambda b: b
    ctx_group_fn = lambda b: ctx_group

    for layer in range(depth):
        last = layer == depth - 1
        mods = mods_all[layer]
        w_in_p = pack_w_in(w_in[layer])
        wg_pad = jnp.zeros((GLR_PAD, 2 * GLA_KEY_WIDTH), F32)
        wg_pad = wg_pad.at[:GLA_GATE_RANK, :GLA_KEY_WIDTH].set(gla_wg2_f[layer])
        wg_pad = wg_pad.at[GLA_GATE_RANK:2 * GLA_GATE_RANK, GLA_KEY_WIDTH:].set(gla_wg2_b[layer])
        bg_cat = jnp.concatenate([gla_bg_f[layer], gla_bg_b[layer]])[None, :]
        w_out_b = w_out[layer].astype(BF16)
        gn_tiled = jnp.tile(gla_norm_g[layer], GLA_HEADS)[None, :]
        rw_t = router_w[layer].T
        rw_hi = rw_t.astype(BF16)
        rw_lo = (rw_t - rw_hi.astype(F32)).astype(BF16)
        rb = router_b[layer][:, None]

        (u_l, r_l, glu_l), comb = _inproj_call(x_lat, mods, lat_group, norm1_g[layer], w_in_p, lt, 0)
        (u_c, r_c, glu_c), comb = _inproj_call(x_ctx, mods, ctx_group_fn, norm1_g[layer], w_in_p, lt,
                                               seq, combined=comb)
        o_f, o_b = _gla_call(*comb, wg_pad, bg_cat, seq, n_ctx)
        yf_l = _fourier_long(u_l, long_tables, chan)
        cv_l = _conv_call(glu_l, seq // (seq // GRID_W), conv_w[layer], conv_b[layer],
                          conv_ln_g[layer], conv_ln_b[layer])
        n_tok = bsz * seq + (0 if last else bsz * n_ctx)
        x_lat, routed = _outproj_call(yf_l, o_f, o_b, 0, r_l, cv_l, x_lat, mods, lat_group, gn_tiled,
                                      w_out_b, norm2_g[layer], rw_hi, rw_lo, rb, n_tok, 0)
        if not last:
            yf_c = _fourier_short(u_c, chan)
            cv_c = _conv_call(glu_c, n_ctx, conv_w[layer], conv_b[layer], conv_ln_g[layer],
                              conv_ln_b[layer])
            x_ctx, routed = _outproj_call(yf_c, o_f, o_b, seq, r_c, cv_c, x_ctx, mods, ctx_group_fn,
                                          gn_tiled, w_out_b, norm2_g[layer], rw_hi, rw_lo, rb, n_tok,
                                          bsz * seq, carried=routed)

        h2, idx_t, prob_t = routed
        pad_end, block_e, n_used, slot_blocks, n_blocks = _routing_tables(idx_t, n_tok)
        xs = _dispatch_call(pad_end, n_used, slot_blocks, h2, n_blocks)
        yb = _expert_call(block_e, n_used, xs, layer, exp_w_gu, exp_b_gu, exp_w_dn, exp_b_dn)
        probs = prob_t.T
        lat_blocks = seq // (slot_blocks.shape[2] // TOP_K)
        x_lat = _combine_call(slot_blocks, yb, probs, x_lat, mods, lambda i: i // lat_blocks,
                              final_norm_g, 0, last)
        if not last:
            x_ctx = _combine_call(slot_blocks, yb, probs, x_ctx, mods, lambda i: ctx_group,
                                  final_norm_g, bsz * seq, False)

    return x_lat
```

```python
import functools

import jax
import jax.numpy as jnp
from jax import lax
from jax.experimental import pallas as pl
from jax.experimental.pallas import tpu as pltpu

F32 = jnp.float32
BF16 = jnp.bfloat16

D_MODEL = 1024
DEPTH = 2
GRID_W = 64
FOURIER_WIDTH = 256
FOURIER_HEADS = 4
FOURIER_HEAD_DIM = FOURIER_WIDTH // FOURIER_HEADS
GLA_HEADS = 4
GLA_KEY_WIDTH = 256
GLA_VALUE_WIDTH = 512
GLA_DK = GLA_KEY_WIDTH // GLA_HEADS
GLA_DV = GLA_VALUE_WIDTH // GLA_HEADS
GLA_GATE_RANK = 16
GLA_GATE_NORMALIZER = 16.0
CONV_WIDTH = 256
CONV_KERNEL = 31
N_EXPERTS = 32
TOP_K = 4
D_FF = D_MODEL
SWIGLU_LIMIT = 7.0
SWIGLU_ALPHA = 1.702
NORM_EPS = 1e-6

LANES = 128
VMEM_LIMIT = 56 * 1024 * 1024

COL_U = 0
COL_Q = COL_U + FOURIER_WIDTH
COL_K = COL_Q + GLA_KEY_WIDTH
COL_V = COL_K + GLA_KEY_WIDTH
COL_R = COL_V + GLA_VALUE_WIDTH
COL_CA = COL_R + GLA_VALUE_WIDTH
COL_CG = COL_CA + CONV_WIDTH
COL_GL = COL_CG + CONV_WIDTH
GLR_PAD = LANES
IN_PAD = COL_GL + GLR_PAD

GLA_CHUNK = 128
GLA_STEP_CHUNKS = 2
TM_ROWS = 512
TM_EXPERT = 512
FFT_N2 = 128
FFT_KB = 8
CONV_HALO = 16


def _cparams(*sem):
    return pltpu.CompilerParams(dimension_semantics=sem, vmem_limit_bytes=VMEM_LIMIT)


def _dot(a, b):
    return jnp.dot(a, b, preferred_element_type=F32)


def _dot_nt(a, b):
    return lax.dot_general(a, b, (((1,), (1,)), ((), ())), preferred_element_type=F32)


def _dot_tn(a, b):
    return lax.dot_general(a, b, (((0,), (0,)), ((), ())), preferred_element_type=F32)


def _split(a):
    hi = a.astype(BF16)
    lo = (a - hi.astype(F32)).astype(BF16)
    return hi, lo


def _dot3(a, b):
    ah, al = _split(a)
    bh, bl = _split(b)
    return _dot(ah, bh) + _dot(ah, bl) + _dot(al, bh)


def _sigmoid(x):
    return 1.0 / (1.0 + jnp.exp(-x))


def _rms(x, g):
    ms = jnp.mean(x * x, axis=-1, keepdims=True)
    return x * lax.rsqrt(ms + NORM_EPS) * g


def _mod_kernel(cv_ref, w_ref, b_ref, o_ref):
    cv = cv_ref[...]
    a = cv * _sigmoid(cv)
    o_ref[...] = _dot3(a, w_ref[...]) + b_ref[...]


def _mod_call(cvec, w_mod, b_mod):
    depth, d, n = w_mod.shape
    rows = cvec.shape[0]
    tn = 1536
    return pl.pallas_call(
        _mod_kernel,
        grid=(depth, n // tn),
        in_specs=[
            pl.BlockSpec((rows, d), lambda l, j: (0, 0)),
            pl.BlockSpec((None, d, tn), lambda l, j: (l, 0, j)),
            pl.BlockSpec((None, 1, tn), lambda l, j: (l, 0, j)),
        ],
        out_specs=pl.BlockSpec((None, rows, tn), lambda l, j: (l, 0, j)),
        out_shape=jax.ShapeDtypeStruct((depth, rows, n), F32),
        compiler_params=_cparams("arbitrary", "arbitrary"),
    )(cvec, w_mod, b_mod.reshape(depth, 1, n))


def _inproj_kernel(x_ref, g_ref, sh_ref, sc_ref, w_ref, *rest):
    u_ref, r_ref, glu_ref, q_ref, k_ref, v_ref, gl_ref = rest[-7:]
    x = x_ref[...]
    h = _rms(x, g_ref[...]) * (1.0 + sc_ref[...]) + sh_ref[...]
    p = _dot(h.astype(BF16), w_ref[...])
    u_ref[...] = p[:, COL_U:COL_Q].astype(u_ref.dtype)
    q_ref[...] = (p[:, COL_Q:COL_K] * (GLA_DK ** -0.5)).astype(q_ref.dtype)
    k_ref[...] = p[:, COL_K:COL_V].astype(k_ref.dtype)
    v_ref[...] = p[:, COL_V:COL_R].astype(v_ref.dtype)
    r_ref[...] = p[:, COL_R:COL_CA].astype(r_ref.dtype)
    glu_ref[...] = (p[:, COL_CA:COL_CG] * _sigmoid(p[:, COL_CG:COL_GL])).astype(glu_ref.dtype)
    gl_ref[...] = p[:, COL_GL:IN_PAD]


def _inproj_call(x, mods, group_of_batch, norm_g, w_in_p, lt, row0, combined=None):
    bsz, n, d = x.shape
    tm = min(TM_ROWS, n)
    assert n % tm == 0 and row0 % tm == 0
    blk0 = row0 // tm
    widths = (GLA_KEY_WIDTH, GLA_KEY_WIDTH, GLA_VALUE_WIDTH, GLR_PAD)
    dtypes = (BF16, BF16, BF16, F32)
    row_spec = lambda w: pl.BlockSpec((None, tm, w), lambda b, i: (b, i, 0))
    comb_spec = lambda w: pl.BlockSpec((None, tm, w), lambda b, i: (b, blk0 + i, 0))
    mod_spec = lambda which: pl.BlockSpec(
        (None, None, 1, d), lambda b, i: (group_of_batch(b), which, 0, 0))
    in_specs = [
        row_spec(d),
        pl.BlockSpec((1, d), lambda b, i: (0, 0)),
        mod_spec(0), mod_spec(1),
        pl.BlockSpec((d, IN_PAD), lambda b, i: (0, 0)),
    ]
    args = [x, norm_g.reshape(1, d), mods, mods, w_in_p]
    aliases = {}
    if combined is not None:
        for t, arr in enumerate(combined):
            in_specs.append(pl.BlockSpec(memory_space=pl.ANY))
            aliases[len(args)] = 3 + t
            args.append(arr)
    out_shape = [
        jax.ShapeDtypeStruct((bsz, n, FOURIER_WIDTH), BF16),
        jax.ShapeDtypeStruct((bsz, n, GLA_VALUE_WIDTH), BF16),
        jax.ShapeDtypeStruct((bsz, n, CONV_WIDTH), BF16),
    ] + [jax.ShapeDtypeStruct((bsz, lt, w), dt) for w, dt in zip(widths, dtypes)]
    out_specs = [row_spec(FOURIER_WIDTH), row_spec(GLA_VALUE_WIDTH), row_spec(CONV_WIDTH)] + [
        comb_spec(w) for w in widths]
    outs = pl.pallas_call(
        _inproj_kernel,
        grid=(bsz, n // tm),
        in_specs=in_specs,
        out_specs=out_specs,
        out_shape=out_shape,
        input_output_aliases=aliases,
        compiler_params=_cparams("arbitrary", "arbitrary"),
    )(*args)
    return outs[:3], outs[3:]


def _dft_tables(length):
    n2 = FFT_N2
    n1 = length // n2
    two_pi = 2.0 * jnp.pi

    def cs(num, den):
        ang = (num % den).astype(F32) * (two_pi / den)
        return jnp.cos(ang), jnp.sin(ang)

    k1 = jnp.arange(n1, dtype=jnp.int32)
    c1, s1 = cs(k1[:, None] * k1[None, :], n1)
    stage1 = (jnp.concatenate([c1, -s1], axis=0) * (n1 ** -0.5)).astype(BF16)
    k2 = jnp.arange(n2, dtype=jnp.int32)
    ct, st = cs(k1[:, None] * k2[None, :], length)
    cf, sf = cs(k2[:, None] * k2[None, :], n2)
    scale = n2 ** -0.5
    mr = (ct[:, None, :] * cf[None] - st[:, None, :] * sf[None]) * scale
    mi = -(st[:, None, :] * cf[None] + ct[:, None, :] * sf[None]) * scale
    stage2 = jnp.concatenate([jnp.concatenate([mr, -mi], axis=2),
                              jnp.concatenate([mi, mr], axis=2)], axis=1).astype(BF16)
    return stage1, stage2


def _channel_tables():
    hd = FOURIER_HEAD_DIM
    c = jnp.arange(FOURIER_WIDTH, dtype=jnp.int32)
    same_head = (c[:, None] // hd) == (c[None, :] // hd)
    ang = (((c[:, None] % hd) * (c[None, :] % hd)) % hd).astype(F32) * (2.0 * jnp.pi / hd)
    scale = hd ** -0.5
    bdc = jnp.where(same_head, jnp.cos(ang) * scale, 0.0).astype(BF16)
    bds = jnp.where(same_head, jnp.sin(ang) * scale, 0.0).astype(BF16)
    return bdc, bds


def _fft1_kernel(x_ref, cs_ref, zr_ref, zi_ref):
    n1 = x_ref.shape[0]
    z = _dot(cs_ref[...], x_ref[...])
    zr_ref[...] = z[:n1].astype(zr_ref.dtype)
    zi_ref[...] = z[n1:].astype(zi_ref.dtype)


def _fft2_kernel(zr_ref, zi_ref, m_ref, bdc_ref, bds_ref, o_ref):
    kb, n2, w = zr_ref.shape
    for j in range(kb):
        z = jnp.concatenate([zr_ref[j], zi_ref[j]], axis=0)
        a = _dot(m_ref[j], z)
        y = _dot(a[:n2].astype(BF16), bdc_ref[...]) + _dot(a[n2:].astype(BF16), bds_ref[...])
        o_ref[:, j * w:(j + 1) * w] = y.astype(o_ref.dtype)


def _fourier_long(u, tables, chan):
    bsz, length, w = u.shape
    stage1, stage2 = tables
    bdc, bds = chan
    n2 = FFT_N2
    n1 = length // n2
    tn = 4096
    cols = n2 * w
    zr, zi = pl.pallas_call(
        _fft1_kernel,
        grid=(bsz, cols // tn),
        in_specs=[pl.BlockSpec((None, n1, tn), lambda b, j: (b, 0, j)),
                  pl.BlockSpec((2 * n1, n1), lambda b, j: (0, 0))],
        out_specs=[pl.BlockSpec((None, n1, tn), lambda b, j: (b, 0, j))] * 2,
        out_shape=[jax.ShapeDtypeStruct((bsz, n1, cols), BF16)] * 2,
        compiler_params=_cparams("arbitrary", "arbitrary"),
    )(u.reshape(bsz, n1, cols), stage1)
    kb = FFT_KB
    z_spec = pl.BlockSpec((None, kb, n2, w), lambda b, j: (b, j, 0, 0))
    y = pl.pallas_call(
        _fft2_kernel,
        grid=(bsz, n1 // kb),
        in_specs=[z_spec, z_spec,
                  pl.BlockSpec((kb, 2 * n2, 2 * n2), lambda b, j: (j, 0, 0)),
                  pl.BlockSpec((w, w), lambda b, j: (0, 0)),
                  pl.BlockSpec((w, w), lambda b, j: (0, 0))],
        out_specs=pl.BlockSpec((None, n2, kb * w), lambda b, j: (b, 0, j)),
        out_shape=jax.ShapeDtypeStruct((bsz, n2, n1 * w), BF16),
        compiler_params=_cparams("arbitrary", "arbitrary"),
    )(zr.reshape(bsz, n1, n2, w), zi.reshape(bsz, n1, n2, w), stage2, bdc, bds)
    return y.reshape(bsz, length, w)


def _dft_short_kernel(u_ref, c_ref, s_ref, bdc_ref, bds_ref, o_ref):
    u = u_ref[...]
    p = _dot(u, bdc_ref[...]).astype(BF16)
    q = _dot(u, bds_ref[...]).astype(BF16)
    o_ref[...] = (_dot(c_ref[...], p) - _dot(s_ref[...], q)).astype(o_ref.dtype)


def _fourier_short(u, chan):
    bsz, length, w = u.shape
    bdc, bds = chan
    k = jnp.arange(length, dtype=jnp.int32)
    ang = ((k[:, None] * k[None, :]) % length).astype(F32) * (2.0 * jnp.pi / length)
    c = (jnp.cos(ang) * length ** -0.5).astype(BF16)
    s = (jnp.sin(ang) * length ** -0.5).astype(BF16)
    full = lambda n: pl.BlockSpec((n, n), lambda b: (0, 0))
    return pl.pallas_call(
        _dft_short_kernel,
        grid=(bsz,),
        in_specs=[pl.BlockSpec((None, length, w), lambda b: (b, 0, 0)),
                  full(length), full(length), full(w), full(w)],
        out_specs=pl.BlockSpec((None, length, w), lambda b: (b, 0, 0)),
        out_shape=jax.ShapeDtypeStruct((bsz, length, w), BF16),
        compiler_params=_cparams("arbitrary"),
    )(u, c, s, bdc, bds)


def _conv_kernel(x_ref, w_ref, cb_ref, lg_ref, lb_ref, o_ref, pad_ref, *, seg):
    nseg = x_ref.shape[0] // seg
    width = x_ref.shape[1]
    halo = jnp.zeros((CONV_HALO, width), F32)
    for s in range(nseg):
        pad_ref[s, 0:CONV_HALO, :] = halo
        pad_ref[s, CONV_HALO:CONV_HALO + seg, :] = x_ref[s * seg:(s + 1) * seg, :].astype(F32)
        pad_ref[s, CONV_HALO + seg:2 * CONV_HALO + seg, :] = halo
    first = CONV_HALO - CONV_KERNEL // 2
    sub = 8
    span = seg + 2 * CONV_HALO - sub
    for s in range(nseg):
        acc = jnp.zeros((seg, width), F32)
        for r in range(sub):
            shifted = pad_ref[s, r:r + span, :]
            for a in range((span - seg) // sub + 1):
                j = a * sub + r - first
                if 0 <= j < CONV_KERNEL:
                    acc = acc + shifted[a * sub:a * sub + seg, :] * w_ref[j:j + 1, :]
        y = acc + cb_ref[...]
        mu = jnp.mean(y, axis=-1, keepdims=True)
        yc = y - mu
        var = jnp.mean(yc * yc, axis=-1, keepdims=True)
        z = yc * lax.rsqrt(var + NORM_EPS) * lg_ref[...] + lb_ref[...]
        o_ref[s * seg:(s + 1) * seg, :] = (z * _sigmoid(z)).astype(o_ref.dtype)


def _conv_call(glu, seg, conv_w, conv_b, ln_g, ln_b):
    bsz, n, w = glu.shape
    t = max(seg, min(TM_ROWS, n))
    assert n % t == 0 and t % seg == 0
    vec = lambda: pl.BlockSpec((1, w), lambda b, i: (0, 0))
    return pl.pallas_call(
        functools.partial(_conv_kernel, seg=seg),
        grid=(bsz, n // t),
        in_specs=[pl.BlockSpec((None, t, w), lambda b, i: (b, i, 0)),
                  pl.BlockSpec((CONV_KERNEL, w), lambda b, i: (0, 0)),
                  vec(), vec(), vec()],
        out_specs=pl.BlockSpec((None, t, w), lambda b, i: (b, i, 0)),
        out_shape=jax.ShapeDtypeStruct((bsz, n, w), BF16),
        scratch_shapes=[pltpu.VMEM((t // seg, seg + 2 * CONV_HALO, w), F32)],
        compiler_params=_cparams("arbitrary", "arbitrary"),
    )(glu, conv_w, conv_b.reshape(1, w), ln_g.reshape(1, w), ln_b.reshape(1, w))


def _gla_direction(q_ref, k_ref, v_ref, gl_ref, wg_ref, bg_ref, o_ref, st_ref, reverse):
    c = GLA_CHUNK
    n_sub = q_ref.shape[0] // c
    st = st_ref[...]
    for s in (reversed(range(n_sub)) if reverse else range(n_sub)):
        rows = slice(s * c, (s + 1) * c)
        st = _gla_chunk(q_ref[rows, :], k_ref[rows, :], v_ref[rows, :], gl_ref[rows, :], wg_ref, bg_ref,
                        o_ref.at[rows, :], st, reverse)
    st_ref[...] = st


def _gla_chunk(q, k, v, glr, wg_ref, bg_ref, o_ref, st, reverse):
    c = GLA_CHUNK
    kw = GLA_KEY_WIDTH
    col0 = kw if reverse else 0
    pre = _dot3(glr, wg_ref[:, col0:col0 + kw]) + bg_ref[:, col0:col0 + kw]
    g = (jnp.minimum(pre, 0.0) - jnp.log(1.0 + jnp.exp(-jnp.abs(pre)))) * (1.0 / GLA_GATE_NORMALIZER)
    row = lax.broadcasted_iota(jnp.int32, (c, c), 0)
    col = lax.broadcasted_iota(jnp.int32, (c, c), 1)
    seen = (col >= row) if reverse else (col <= row)
    tri = jnp.where(seen, 1.0, 0.0).astype(BF16)
    gh, gl = _split(g)
    b = _dot(tri, gh) + _dot(tri, gl)
    mid = c // 2 if reverse else c // 2 - 1
    last = 0 if reverse else c - 1
    b_mid = b[mid:mid + 1, :]
    b_last = b[last:last + 1, :]
    q = q.astype(F32)
    k = k.astype(F32)
    qe = q * jnp.exp(b - b_mid)
    ke = k * jnp.exp(b_mid - b)
    kd = k * jnp.exp(b_last - b)
    head_of_lane = lax.broadcasted_iota(jnp.int32, (1, kw), 1) // GLA_DK
    q_heads = jnp.concatenate(
        [jnp.where(head_of_lane == h, qe, 0.0) for h in range(GLA_HEADS)], axis=0).astype(BF16)
    rhs = jnp.concatenate([ke, st * jnp.exp(b_mid)], axis=0).astype(BF16)
    res = _dot_nt(q_heads, rhs)
    outs = []
    for h in range(GLA_HEADS):
        blk = res[h * c:(h + 1) * c, :]
        scores = jnp.where(seen, blk[:, :c], 0.0).astype(BF16)
        outs.append(_dot(scores, v[:, h * GLA_DV:(h + 1) * GLA_DV]) + blk[:, c:])
    o_ref[...] = jnp.concatenate(outs, axis=1).astype(o_ref.dtype)
    kv = _dot_tn(v, kd.astype(BF16))
    ds = jnp.zeros_like(st)
    for h in range(GLA_HEADS):
        ds = ds + jnp.where(head_of_lane == h, kv[h * GLA_DV:(h + 1) * GLA_DV, :], 0.0)
    return st * jnp.exp(b_last) + ds


def _gla_kernel(qf, kf, vf, gf, qb, kb, vb, gb, wg_ref, bg_ref, of_ref, ob_ref, sf_ref, sb_ref):
    @pl.when(pl.program_id(1) == 0)
    def _():
        sf_ref[...] = jnp.zeros_like(sf_ref)
        sb_ref[...] = jnp.zeros_like(sb_ref)

    _gla_direction(qf, kf, vf, gf, wg_ref, bg_ref, of_ref, sf_ref, False)
    _gla_direction(qb, kb, vb, gb, wg_ref, bg_ref, ob_ref, sb_ref, True)


def _gla_call(q, k, v, gl, wg_pad, bg_cat, n_lat, n_ctx):
    bsz, lt, _ = q.shape
    c = GLA_STEP_CHUNKS * GLA_CHUNK
    assert n_lat % c == 0 and n_ctx % c == 0
    cl, cc = n_lat // c, n_ctx // c

    def fwd_blk(j):
        return jnp.where(j < cc, cl + j, j - cc)

    def bwd_blk(j):
        return jnp.where(j < cc, cl + cc - 1 - j, cl - 1 - (j - cc))

    def spec(w, blk):
        return pl.BlockSpec((None, c, w), lambda b, j: (b, blk(j), 0))

    widths = (GLA_KEY_WIDTH, GLA_KEY_WIDTH, GLA_VALUE_WIDTH, GLR_PAD)
    in_specs = [spec(w, fwd_blk) for w in widths] + [spec(w, bwd_blk) for w in widths] + [
        pl.BlockSpec(wg_pad.shape, lambda b, j: (0, 0)),
        pl.BlockSpec(bg_cat.shape, lambda b, j: (0, 0))]
    return pl.pallas_call(
        _gla_kernel,
        grid=(bsz, cl + cc),
        in_specs=in_specs,
        out_specs=[spec(GLA_VALUE_WIDTH, fwd_blk), spec(GLA_VALUE_WIDTH, bwd_blk)],
        out_shape=[jax.ShapeDtypeStruct((bsz, lt, GLA_VALUE_WIDTH), F32)] * 2,
        scratch_shapes=[pltpu.VMEM((GLA_DV, GLA_KEY_WIDTH), F32)] * 2,
        compiler_params=_cparams("arbitrary", "arbitrary"),
    )(q, k, v, gl, q, k, v, gl, wg_pad, bg_cat)


def _outproj_kernel(yf_ref, of_ref, ob_ref, r_ref, cv_ref, x_ref, gate_ref, sh_ref, sc_ref,
                    gn_ref, wo_ref, n2_ref, rwh_ref, rwl_ref, rb_ref, *rest):
    xo_ref, h2_ref, idx_ref, prob_ref = rest[-4:]
    o = of_ref[...] + ob_ref[...]
    heads = []
    for h in range(GLA_HEADS):
        oh = o[:, h * GLA_DV:(h + 1) * GLA_DV]
        heads.append(oh * lax.rsqrt(jnp.mean(oh * oh, axis=-1, keepdims=True) + NORM_EPS))
    r = r_ref[...].astype(F32)
    gla = jnp.concatenate(heads, axis=1) * gn_ref[...] * (r * _sigmoid(r))
    c0, c1 = FOURIER_WIDTH, FOURIER_WIDTH + GLA_VALUE_WIDTH
    y = (_dot(yf_ref[...], wo_ref[0:c0, :]) + _dot(gla.astype(BF16), wo_ref[c0:c1, :])
         + _dot(cv_ref[...], wo_ref[c1:, :]))
    xn = x_ref[...] + gate_ref[...] * y
    xo_ref[...] = xn
    h2 = _rms(xn, n2_ref[...]) * (1.0 + sc_ref[...]) + sh_ref[...]
    hh, hl = _split(h2)
    h2_ref[...] = h2
    logits = (_dot_nt(rwh_ref[...], hh) + _dot_nt(rwh_ref[...], hl) + _dot_nt(rwl_ref[...], hh)
              + rb_ref[...])
    expert = lax.broadcasted_iota(jnp.int32, logits.shape, 0)
    vals, idxs = [], []
    cur = logits
    for _ in range(TOP_K):
        m = jnp.max(cur, axis=0, keepdims=True)
        ix = jnp.min(jnp.where(cur == m, expert, N_EXPERTS), axis=0, keepdims=True)
        vals.append(m)
        idxs.append(ix)
        cur = jnp.where(expert == ix, -jnp.inf, cur)
    es = [jnp.exp(vv - vals[0]) for vv in vals]
    inv = 1.0 / functools.reduce(lambda a, b: a + b, es)
    idx_ref[...] = jnp.concatenate(idxs, axis=0)
    prob_ref[...] = jnp.concatenate([e * inv for e in es], axis=0)


def _outproj_call(yf, o_f, o_b, o_row0, r, cv, x, mods, group_of_batch, gn_tiled, w_out, norm2_g,
                  rw_hi, rw_lo, rb, n_tok, tok0, carried=None):
    bsz, n, d = x.shape
    tm = min(TM_ROWS, n)
    assert n % tm == 0 and o_row0 % tm == 0 and tok0 % tm == 0
    nb = n // tm
    row = lambda w: pl.BlockSpec((None, tm, w), lambda b, i: (b, i, 0))
    orow = pl.BlockSpec((None, tm, GLA_VALUE_WIDTH), lambda b, i: (b, o_row0 // tm + i, 0))
    mod = lambda which: pl.BlockSpec((None, None, 1, d), lambda b, i: (group_of_batch(b), which, 0, 0))
    const = lambda a: pl.BlockSpec(a.shape, lambda b, i: (0,) * a.ndim)
    consts = [gn_tiled, w_out, norm2_g.reshape(1, d), rw_hi, rw_lo, rb]
    in_specs = [row(FOURIER_WIDTH), orow, orow, row(GLA_VALUE_WIDTH), row(CONV_WIDTH), row(d),
                mod(2), mod(3), mod(4)] + [const(a) for a in consts]
    args = [yf, o_f, o_b, r, cv, x, mods, mods, mods] + consts
    aliases = {}
    if carried is not None:
        for t, arr in enumerate(carried):
            in_specs.append(pl.BlockSpec(memory_space=pl.ANY))
            aliases[len(args)] = 1 + t
            args.append(arr)
    tokblk = lambda b, i: tok0 // tm + b * nb + i
    out_specs = [row(d),
                 pl.BlockSpec((tm, d), lambda b, i: (tokblk(b, i), 0)),
                 pl.BlockSpec((TOP_K, tm), lambda b, i: (0, tokblk(b, i))),
                 pl.BlockSpec((TOP_K, tm), lambda b, i: (0, tokblk(b, i)))]
    out_shape = [jax.ShapeDtypeStruct((bsz, n, d), F32),
                 jax.ShapeDtypeStruct((n_tok, d), F32),
                 jax.ShapeDtypeStruct((TOP_K, n_tok), jnp.int32),
                 jax.ShapeDtypeStruct((TOP_K, n_tok), F32)]
    outs = pl.pallas_call(
        _outproj_kernel,
        grid=(bsz, nb),
        in_specs=in_specs,
        out_specs=out_specs,
        out_shape=out_shape,
        input_output_aliases=aliases,
        compiler_params=_cparams("arbitrary", "arbitrary"),
    )(*args)
    return outs[0], outs[1:]


def _expert_kernel(be_ref, nu_ref, x_ref, wgu_ref, bgu_ref, wdn_ref, bdn_ref, o_ref, wgu_s, wdn_s):
    i = pl.program_id(0)

    @pl.when(i >= nu_ref[0])
    def _():
        o_ref[...] = jnp.zeros_like(o_ref)

    @pl.when(i < nu_ref[0])
    def _():
        changed = jnp.logical_or(i == 0, be_ref[i] != be_ref[jnp.maximum(i - 1, 0)])

        @pl.when(changed)
        def _():
            rows = 128
            for s in range(wgu_ref.shape[0] // rows):
                wgu_s[s * rows:(s + 1) * rows, :] = wgu_ref[s * rows:(s + 1) * rows, :].astype(BF16)
                wdn_s[s * rows:(s + 1) * rows, :] = wdn_ref[s * rows:(s + 1) * rows, :].astype(BF16)

        gu = _dot(x_ref[...].astype(BF16), wgu_s[...]) + bgu_ref[...]
        gate = jnp.minimum(gu[:, :D_FF], SWIGLU_LIMIT)
        up = jnp.clip(gu[:, D_FF:], -SWIGLU_LIMIT, SWIGLU_LIMIT)
        act = gate * _sigmoid(SWIGLU_ALPHA * gate) * (up + 1.0)
        o_ref[...] = (_dot(act.astype(BF16), wdn_s[...]) + bdn_ref[...]).astype(o_ref.dtype)


def _expert_call(block_e, n_used, xs, layer, w_gu, b_gu, w_dn, b_dn):
    rows, d = xs.shape
    tm = TM_EXPERT
    nblk = rows // tm
    depth, e, _, f2 = w_gu.shape
    live = lambda i, nu: jnp.minimum(i, nu[0] - 1)
    wmap = lambda i, be, nu: (layer, be[live(i, nu)], 0, 0)
    grid_spec = pltpu.PrefetchScalarGridSpec(
        num_scalar_prefetch=2,
        grid=(nblk,),
        in_specs=[
            pl.BlockSpec((tm, d), lambda i, be, nu: (live(i, nu), 0)),
            pl.BlockSpec((None, None, d, f2), wmap),
            pl.BlockSpec((None, None, 1, f2), wmap),
            pl.BlockSpec((None, None, f2 // 2, d), wmap),
            pl.BlockSpec((None, None, 1, d), wmap),
        ],
        out_specs=pl.BlockSpec((tm, d), lambda i, be, nu: (i, 0)),
        scratch_shapes=[pltpu.VMEM((d, f2), BF16), pltpu.VMEM((f2 // 2, d), BF16)],
    )
    return pl.pallas_call(
        _expert_kernel,
        grid_spec=grid_spec,
        out_shape=jax.ShapeDtypeStruct((rows, d), F32),
        compiler_params=_cparams("arbitrary"),
    )(block_e, n_used, xs, w_gu, b_gu.reshape(depth, e, 1, f2), w_dn, b_dn.reshape(depth, e, 1, d))


def _rank_kernel(idx_ref, rank_ref, cnt_ref, base_ref):
    @pl.when(pl.program_id(0) == 0)
    def _():
        base_ref[...] = jnp.zeros_like(base_ref)

    tb = idx_ref.shape[1]
    row = lax.broadcasted_iota(jnp.int32, (tb, tb), 0)
    col = lax.broadcasted_iota(jnp.int32, (tb, tb), 1)
    earlier = jnp.where(row < col, 1.0, 0.0).astype(BF16)
    expert = lax.broadcasted_iota(jnp.int32, (N_EXPERTS, tb), 0)
    base = base_ref[...]
    ranks = []
    for kk in range(TOP_K):
        hit = expert == idx_ref[kk:kk + 1, :]
        onehot = hit.astype(F32)
        before = _dot(onehot.astype(BF16), earlier) + base
        ranks.append(jnp.sum(before * onehot, axis=0, keepdims=True))
        base = base + jnp.sum(onehot, axis=1, keepdims=True)
    rank_ref[...] = jnp.concatenate(ranks, axis=0).astype(jnp.int32)
    base_ref[...] = base
    cnt_ref[...] = base.astype(jnp.int32)


def _slot_kernel(idx_ref, rank_ref, start_ref, slot_ref):
    expert = lax.broadcasted_iota(jnp.int32, (N_EXPERTS, idx_ref.shape[1]), 0)
    rows = []
    for kk in range(TOP_K):
        start = jnp.sum(jnp.where(expert == idx_ref[kk:kk + 1, :], start_ref[...], 0), axis=0,
                        keepdims=True)
        rows.append(rank_ref[kk:kk + 1, :] + start)
    slot_ref[...] = jnp.concatenate(rows, axis=1)


def _row_copy(src, s, dst, t, sem):
    return pltpu.make_async_copy(src.at[pl.ds(s, 1), :], dst.at[pl.ds(t, 1), :], sem)


def _dispatch_kernel(pe_ref, nu_ref, slot_hbm, h_ref, xs_ref, slot_s, zero_ref, sem, zsem, ssem, *,
                     n_blocks):
    i = pl.program_id(0)
    tb = h_ref.shape[0]
    tm = zero_ref.shape[0]
    fetch = pltpu.make_async_copy(slot_hbm.at[i, 0], slot_s, ssem)
    fetch.start()

    def zero_copy(blk):
        return pltpu.make_async_copy(zero_ref, xs_ref.at[pl.ds(pl.multiple_of(blk * tm, tm), tm), :], zsem)

    @pl.when(i == 0)
    def _():
        zero_ref[...] = jnp.zeros_like(zero_ref)

        def per_expert(fn):
            for e in range(N_EXPERTS):
                end = pe_ref[e]
                start = pe_ref[e - 1] if e else 0

                @pl.when(end > start)
                def _():
                    fn(end // tm - 1)

        def per_tail(fn):
            def body(blk, carry):
                fn(blk)
                return carry
            lax.fori_loop(nu_ref[0], n_blocks, body, 0)

        per_expert(lambda blk: zero_copy(blk).start())
        per_tail(lambda blk: zero_copy(blk).start())
        per_expert(lambda blk: zero_copy(blk).wait())
        per_tail(lambda blk: zero_copy(blk).wait())

    fetch.wait()

    def issue(g, carry):
        t0 = pl.multiple_of(g * 8, 8)
        for j in range(8):
            for kk in range(TOP_K):
                _row_copy(h_ref, t0 + j, xs_ref, slot_s[kk * tb + t0 + j], sem).start()
        return carry

    lax.fori_loop(0, tb // 8, issue, 0)
    for kk in range(TOP_K):
        pltpu.make_async_copy(h_ref, xs_ref.at[pl.ds(0, tb), :], sem).wait()


def _dispatch_call(pad_end, n_used, slot_blocks, h2, n_blocks):
    n_tok, d = h2.shape
    nblk, _, ktb = slot_blocks.shape
    tb = ktb // TOP_K
    tm = TM_EXPERT
    grid_spec = pltpu.PrefetchScalarGridSpec(
        num_scalar_prefetch=2,
        grid=(nblk,),
        in_specs=[pl.BlockSpec(memory_space=pl.ANY),
                  pl.BlockSpec((tb, d), lambda i, pe, nu: (i, 0))],
        out_specs=pl.BlockSpec(memory_space=pl.ANY),
        scratch_shapes=[pltpu.SMEM((ktb,), jnp.int32), pltpu.VMEM((tm, d), F32),
                        pltpu.SemaphoreType.DMA, pltpu.SemaphoreType.DMA, pltpu.SemaphoreType.DMA],
    )
    return pl.pallas_call(
        functools.partial(_dispatch_kernel, n_blocks=n_blocks),
        grid_spec=grid_spec,
        out_shape=jax.ShapeDtypeStruct((n_blocks * tm, d), F32),
        compiler_params=_cparams("arbitrary"),
    )(pad_end, n_used, slot_blocks, h2)


def _routing_tables(idx_t, n_tok):
    tm = TM_EXPERT
    tb = TM_ROWS
    assert n_tok % tb == 0
    blk = pl.BlockSpec((TOP_K, tb), lambda i: (0, i))
    rank_t, counts = pl.pallas_call(
        _rank_kernel,
        grid=(n_tok // tb,),
        in_specs=[blk],
        out_specs=[blk, pl.BlockSpec((N_EXPERTS, 1), lambda i: (0, 0))],
        out_shape=[jax.ShapeDtypeStruct((TOP_K, n_tok), jnp.int32),
                   jax.ShapeDtypeStruct((N_EXPERTS, 1), jnp.int32)],
        scratch_shapes=[pltpu.VMEM((N_EXPERTS, 1), F32)],
        compiler_params=_cparams("arbitrary"),
    )(idx_t)
    counts = counts[:, 0]
    padded = (counts + tm - 1) // tm * tm
    pad_end = jnp.cumsum(padded)
    pad_start = pad_end - padded
    slot_blocks = pl.pallas_call(
        _slot_kernel,
        grid=(n_tok // tb,),
        in_specs=[blk, blk, pl.BlockSpec((N_EXPERTS, 1), lambda i: (0, 0))],
        out_specs=pl.BlockSpec((None, 1, TOP_K * tb), lambda i: (i, 0, 0)),
        out_shape=jax.ShapeDtypeStruct((n_tok // tb, 1, TOP_K * tb), jnp.int32),
        compiler_params=_cparams("arbitrary"),
    )(idx_t, rank_t, pad_start[:, None])
    n_blocks = -(-TOP_K * n_tok // tm) + N_EXPERTS
    first_row = jnp.arange(n_blocks, dtype=jnp.int32) * tm
    block_e = jnp.minimum(jnp.sum(pad_end[None, :] <= first_row[:, None], axis=1),
                          N_EXPERTS - 1).astype(jnp.int32)
    n_used = (pad_end[-1:] // tm).astype(jnp.int32)
    return pad_end.astype(jnp.int32), block_e, n_used, slot_blocks, n_blocks


def _combine_kernel(slot_hbm, yb_hbm, p_ref, x_ref, gate_ref, fg_ref, o_ref, slot_s, ybuf, sem, ssem, *,
                    final_norm, blk0):
    i = pl.program_id(0)
    n = pl.num_programs(0)
    tb = x_ref.shape[0]

    def request(blk, buf):
        fetch = pltpu.make_async_copy(slot_hbm.at[blk0 + blk, 0], slot_s, ssem)
        fetch.start()
        fetch.wait()

        def issue(g, carry):
            t0 = pl.multiple_of(g * 8, 8)
            for j in range(8):
                for kk in range(TOP_K):
                    _row_copy(yb_hbm, slot_s[kk * tb + t0 + j], ybuf.at[buf, kk], t0 + j,
                              sem.at[buf]).start()
            return carry

        lax.fori_loop(0, tb // 8, issue, 0)

    @pl.when(i == 0)
    def _():
        request(0, 0)

    @pl.when(i + 1 < n)
    def _():
        request(i + 1, (i + 1) % 2)

    cur = i % 2
    for kk in range(TOP_K):
        pltpu.make_async_copy(yb_hbm.at[pl.ds(0, tb), :], ybuf.at[cur, kk], sem.at[cur]).wait()
    f = jnp.zeros(x_ref.shape, F32)
    for kk in range(TOP_K):
        f = f + ybuf[cur, kk] * p_ref[:, kk:kk + 1]
    xn = x_ref[...] + gate_ref[...] * f
    if final_norm:
        xn = _rms(xn, fg_ref[...])
    o_ref[...] = xn


def _combine_call(slot_blocks, yb, probs, x, mods, group_of_block, final_g, tok0, final_norm):
    bsz, n, d = x.shape
    ktb = slot_blocks.shape[2]
    tb = ktb // TOP_K
    rows = bsz * n
    assert rows % tb == 0 and tok0 % tb == 0 and (n % tb == 0 or tb % n == 0)
    blk0 = tok0 // tb
    out = pl.pallas_call(
        functools.partial(_combine_kernel, final_norm=final_norm, blk0=blk0),
        grid=(rows // tb,),
        in_specs=[pl.BlockSpec(memory_space=pl.ANY),
                  pl.BlockSpec(memory_space=pl.ANY),
                  pl.BlockSpec((tb, TOP_K), lambda i: (blk0 + i, 0)),
                  pl.BlockSpec((tb, d), lambda i: (i, 0)),
                  pl.BlockSpec((None, None, 1, d), lambda i: (group_of_block(i), 5, 0, 0)),
                  pl.BlockSpec((1, d), lambda i: (0, 0))],
        out_specs=pl.BlockSpec((tb, d), lambda i: (i, 0)),
        out_shape=jax.ShapeDtypeStruct((rows, d), F32),
        scratch_shapes=[pltpu.SMEM((ktb,), jnp.int32), pltpu.VMEM((2, TOP_K, tb, d), F32),
                        pltpu.SemaphoreType.DMA((2,)), pltpu.SemaphoreType.DMA],
        compiler_params=_cparams("arbitrary"),
    )(slot_blocks, yb, probs, x.reshape(rows, d), mods, final_g.reshape(1, d))
    return out.reshape(bsz, n, d)


def kernel(x, c, ctx, c_ctx, norm1_g, norm2_g, w_mod, b_mod, w_in, gla_wg2_f, gla_bg_f, gla_wg2_b,
           gla_bg_b, gla_norm_g, conv_w, conv_b, conv_ln_g, conv_ln_b, w_out, router_w, router_b,
           exp_w_gu, exp_b_gu, exp_w_dn, exp_b_dn, final_norm_g):
    bsz, seq, d = x.shape
    n_ctx = ctx.shape[1]
    depth = w_mod.shape[0]
    assert d == D_MODEL and seq % (FFT_N2 * FFT_KB) == 0 and seq % n_ctx == 0
    lt = seq + n_ctx
    ctx_group = bsz

    rows = 8
    cvec = jnp.concatenate([c, c_ctx[None, :], jnp.zeros((rows - bsz - 1, d), F32)], axis=0)
    mods_all = _mod_call(cvec, w_mod, b_mod).reshape(depth, rows, 6, 1, d)

    def pack_w_in(w):
        o = [0, 256, 512, 768, 1280, 1792, 1808, 1824, 2336]
        parts = [w[:, o[0]:o[1]], w[:, o[1]:o[2]], w[:, o[2]:o[3]], w[:, o[3]:o[4]], w[:, o[4]:o[5]],
                 w[:, o[7]:o[7] + CONV_WIDTH], w[:, o[7] + CONV_WIDTH:o[8]], w[:, o[5]:o[7]],
                 jnp.zeros((d, GLR_PAD - 2 * GLA_GATE_RANK), w.dtype)]
        return jnp.concatenate(parts, axis=1).astype(BF16)

    long_tables = _dft_tables(seq)
    chan = _channel_tables()
    x_lat, x_ctx = x, ctx
    lat_group = lambda b: b
    ctx_group_fn = lambda b: ctx_group

    for layer in range(depth):
        last = layer == depth - 1
        mods = mods_all[layer]
        w_in_p = pack_w_in(w_in[layer])
        wg_pad = jnp.zeros((GLR_PAD, 2 * GLA_KEY_WIDTH), F32)
        wg_pad = wg_pad.at[:GLA_GATE_RANK, :GLA_KEY_WIDTH].set(gla_wg2_f[layer])
        wg_pad = wg_pad.at[GLA_GATE_RANK:2 * GLA_GATE_RANK, GLA_KEY_WIDTH:].set(gla_wg2_b[layer])
        bg_cat = jnp.concatenate([gla_bg_f[layer], gla_bg_b[layer]])[None, :]
        w_out_b = w_out[layer].astype(BF16)
        gn_tiled = jnp.tile(gla_norm_g[layer], GLA_HEADS)[None, :]
        rw_t = router_w[layer].T
        rw_hi = rw_t.astype(BF16)
        rw_lo = (rw_t - rw_hi.astype(F32)).astype(BF16)
        rb = router_b[layer][:, None]

        (u_l, r_l, glu_l), comb = _inproj_call(x_lat, mods, lat_group, norm1_g[layer], w_in_p, lt, 0)
        (u_c, r_c, glu_c), comb = _inproj_call(x_ctx, mods, ctx_group_fn, norm1_g[layer], w_in_p, lt,
                                               seq, combined=comb)
        o_f, o_b = _gla_call(*comb, wg_pad, bg_cat, seq, n_ctx)
        yf_l = _fourier_long(u_l, long_tables, chan)
        cv_l = _conv_call(glu_l, seq // (seq // GRID_W), conv_w[layer], conv_b[layer],
                          conv_ln_g[layer], conv_ln_b[layer])
        n_tok = bsz * seq + (0 if last else bsz * n_ctx)
        x_lat, routed = _outproj_call(yf_l, o_f, o_b, 0, r_l, cv_l, x_lat, mods, lat_group, gn_tiled,
                                      w_out_b, norm2_g[layer], rw_hi, rw_lo, rb, n_tok, 0)
        if not last:
            yf_c = _fourier_short(u_c, chan)
            cv_c = _conv_call(glu_c, n_ctx, conv_w[layer], conv_b[layer], conv_ln_g[layer],
                              conv_ln_b[layer])
            x_ctx, routed = _outproj_call(yf_c, o_f, o_b, seq, r_c, cv_c, x_ctx, mods, ctx_group_fn,
                                          gn_tiled, w_out_b, norm2_g[layer], rw_hi, rw_lo, rb, n_tok,
                                          bsz * seq, carried=routed)

        h2, idx_t, prob_t = routed
        pad_end, block_e, n_used, slot_blocks, n_blocks = _routing_tables(idx_t, n_tok)
        xs = _dispatch_call(pad_end, n_used, slot_blocks, h2, n_blocks)
        yb = _expert_call(block_e, n_used, xs, layer, exp_w_gu, exp_b_gu, exp_w_dn, exp_b_dn)
        probs = prob_t.T
        lat_blocks = seq // (slot_blocks.shape[2] // TOP_K)
        x_lat = _combine_call(slot_blocks, yb, probs, x_lat, mods, lambda i: i // lat_blocks,
                              final_norm_g, 0, last)
        if not last:
            x_ctx = _combine_call(slot_blocks, yb, probs, x_ctx, mods, lambda i: ctx_group,
                                  final_norm_g, bsz * seq, False)

    return x_lat
```

```python
import functools

import jax
import jax.numpy as jnp
from jax import lax
from jax.experimental import pallas as pl
from jax.experimental.pallas import tpu as pltpu

F32 = jnp.float32
BF16 = jnp.bfloat16

D_MODEL = 1024
DEPTH = 2
GRID_W = 64
FOURIER_WIDTH = 256
FOURIER_HEADS = 4
FOURIER_HEAD_DIM = FOURIER_WIDTH // FOURIER_HEADS
GLA_HEADS = 4
GLA_KEY_WIDTH = 256
GLA_VALUE_WIDTH = 512
GLA_DK = GLA_KEY_WIDTH // GLA_HEADS
GLA_DV = GLA_VALUE_WIDTH // GLA_HEADS
GLA_GATE_RANK = 16
GLA_GATE_NORMALIZER = 16.0
CONV_WIDTH = 256
CONV_KERNEL = 31
N_EXPERTS = 32
TOP_K = 4
D_FF = D_MODEL
SWIGLU_LIMIT = 7.0
SWIGLU_ALPHA = 1.702
NORM_EPS = 1e-6

LANES = 128
VMEM_LIMIT = 56 * 1024 * 1024

COL_U = 0
COL_Q = COL_U + FOURIER_WIDTH
COL_K = COL_Q + GLA_KEY_WIDTH
COL_V = COL_K + GLA_KEY_WIDTH
COL_R = COL_V + GLA_VALUE_WIDTH
COL_CA = COL_R + GLA_VALUE_WIDTH
COL_CG = COL_CA + CONV_WIDTH
COL_GL = COL_CG + CONV_WIDTH
GLR_PAD = LANES
IN_PAD = COL_GL + GLR_PAD

GLA_CHUNK = 128
GLA_STEP_CHUNKS = 2
TM_ROWS = 512
TM_EXPERT = 512
FFT_N2 = 128
FFT_KB = 8
CONV_HALO = 16


def _cparams(*sem):
    return pltpu.CompilerParams(dimension_semantics=sem, vmem_limit_bytes=VMEM_LIMIT)


def _dot(a, b):
    return jnp.dot(a, b, preferred_element_type=F32)


def _dot_nt(a, b):
    return lax.dot_general(a, b, (((1,), (1,)), ((), ())), preferred_element_type=F32)


def _dot_tn(a, b):
    return lax.dot_general(a, b, (((0,), (0,)), ((), ())), preferred_element_type=F32)


def _split(a):
    hi = a.astype(BF16)
    lo = (a - hi.astype(F32)).astype(BF16)
    return hi, lo


def _dot3(a, b):
    ah, al = _split(a)
    bh, bl = _split(b)
    return _dot(ah, bh) + _dot(ah, bl) + _dot(al, bh)


def _sigmoid(x):
    return 1.0 / (1.0 + jnp.exp(-x))


def _rms(x, g):
    ms = jnp.mean(x * x, axis=-1, keepdims=True)
    return x * lax.rsqrt(ms + NORM_EPS) * g


def _mod_kernel(cv_ref, w_ref, b_ref, o_ref):
    cv = cv_ref[...]
    a = cv * _sigmoid(cv)
    o_ref[...] = _dot3(a, w_ref[...]) + b_ref[...]


def _mod_call(cvec, w_mod, b_mod):
    depth, d, n = w_mod.shape
    rows = cvec.shape[0]
    tn = 1536
    return pl.pallas_call(
        _mod_kernel,
        grid=(depth, n // tn),
        in_specs=[
            pl.BlockSpec((rows, d), lambda l, j: (0, 0)),
            pl.BlockSpec((None, d, tn), lambda l, j: (l, 0, j)),
            pl.BlockSpec((None, 1, tn), lambda l, j: (l, 0, j)),
        ],
        out_specs=pl.BlockSpec((None, rows, tn), lambda l, j: (l, 0, j)),
        out_shape=jax.ShapeDtypeStruct((depth, rows, n), F32),
        compiler_params=_cparams("arbitrary", "arbitrary"),
    )(cvec, w_mod, b_mod.reshape(depth, 1, n))


def _inproj_kernel(x_ref, g_ref, sh_ref, sc_ref, w_ref, *rest):
    u_ref, r_ref, glu_ref, q_ref, k_ref, v_ref, gl_ref = rest[-7:]
    x = x_ref[...]
    h = _rms(x, g_ref[...]) * (1.0 + sc_ref[...]) + sh_ref[...]
    p = _dot(h.astype(BF16), w_ref[...])
    u_ref[...] = p[:, COL_U:COL_Q].astype(u_ref.dtype)
    q_ref[...] = (p[:, COL_Q:COL_K] * (GLA_DK ** -0.5)).astype(q_ref.dtype)
    k_ref[...] = p[:, COL_K:COL_V].astype(k_ref.dtype)
    v_ref[...] = p[:, COL_V:COL_R].astype(v_ref.dtype)
    r_ref[...] = p[:, COL_R:COL_CA].astype(r_ref.dtype)
    glu_ref[...] = (p[:, COL_CA:COL_CG] * _sigmoid(p[:, COL_CG:COL_GL])).astype(glu_ref.dtype)
    gl_ref[...] = p[:, COL_GL:IN_PAD]


def _inproj_call(x, mods, group_of_batch, norm_g, w_in_p, lt, row0, combined=None):
    bsz, n, d = x.shape
    tm = min(TM_ROWS, n)
    assert n % tm == 0 and row0 % tm == 0
    blk0 = row0 // tm
    widths = (GLA_KEY_WIDTH, GLA_KEY_WIDTH, GLA_VALUE_WIDTH, GLR_PAD)
    dtypes = (BF16, BF16, BF16, F32)
    row_spec = lambda w: pl.BlockSpec((None, tm, w), lambda b, i: (b, i, 0))
    comb_spec = lambda w: pl.BlockSpec((None, tm, w), lambda b, i: (b, blk0 + i, 0))
    mod_spec = lambda which: pl.BlockSpec(
        (None, None, 1, d), lambda b, i: (group_of_batch(b), which, 0, 0))
    in_specs = [
        row_spec(d),
        pl.BlockSpec((1, d), lambda b, i: (0, 0)),
        mod_spec(0), mod_spec(1),
        pl.BlockSpec((d, IN_PAD), lambda b, i: (0, 0)),
    ]
    args = [x, norm_g.reshape(1, d), mods, mods, w_in_p]
    aliases = {}
    if combined is not None:
        for t, arr in enumerate(combined):
            in_specs.append(pl.BlockSpec(memory_space=pl.ANY))
            aliases[len(args)] = 3 + t
            args.append(arr)
    out_shape = [
        jax.ShapeDtypeStruct((bsz, n, FOURIER_WIDTH), BF16),
        jax.ShapeDtypeStruct((bsz, n, GLA_VALUE_WIDTH), BF16),
        jax.ShapeDtypeStruct((bsz, n, CONV_WIDTH), BF16),
    ] + [jax.ShapeDtypeStruct((bsz, lt, w), dt) for w, dt in zip(widths, dtypes)]
    out_specs = [row_spec(FOURIER_WIDTH), row_spec(GLA_VALUE_WIDTH), row_spec(CONV_WIDTH)] + [
        comb_spec(w) for w in widths]
    outs = pl.pallas_call(
        _inproj_kernel,
        grid=(bsz, n // tm),
        in_specs=in_specs,
        out_specs=out_specs,
        out_shape=out_shape,
        input_output_aliases=aliases,
        compiler_params=_cparams("arbitrary", "arbitrary"),
    )(*args)
    return outs[:3], outs[3:]


def _dft_tables(length):
    n2 = FFT_N2
    n1 = length // n2
    two_pi = 2.0 * jnp.pi

    def cs(num, den):
        ang = (num % den).astype(F32) * (two_pi / den)
        return jnp.cos(ang), jnp.sin(ang)

    k1 = jnp.arange(n1, dtype=jnp.int32)
    c1, s1 = cs(k1[:, None] * k1[None, :], n1)
    stage1 = (jnp.concatenate([c1, -s1], axis=0) * (n1 ** -0.5)).astype(BF16)
    k2 = jnp.arange(n2, dtype=jnp.int32)
    ct, st = cs(k1[:, None] * k2[None, :], length)
    cf, sf = cs(k2[:, None] * k2[None, :], n2)
    scale = n2 ** -0.5
    mr = (ct[:, None, :] * cf[None] - st[:, None, :] * sf[None]) * scale
    mi = -(st[:, None, :] * cf[None] + ct[:, None, :] * sf[None]) * scale
    stage2 = jnp.concatenate([jnp.concatenate([mr, -mi], axis=2),
                              jnp.concatenate([mi, mr], axis=2)], axis=1).astype(BF16)
    return stage1, stage2


def _channel_tables():
    hd = FOURIER_HEAD_DIM
    c = jnp.arange(FOURIER_WIDTH, dtype=jnp.int32)
    same_head = (c[:, None] // hd) == (c[None, :] // hd)
    ang = (((c[:, None] % hd) * (c[None, :] % hd)) % hd).astype(F32) * (2.0 * jnp.pi / hd)
    scale = hd ** -0.5
    bdc = jnp.where(same_head, jnp.cos(ang) * scale, 0.0).astype(BF16)
    bds = jnp.where(same_head, jnp.sin(ang) * scale, 0.0).astype(BF16)
    return bdc, bds


def _fft1_kernel(x_ref, cs_ref, zr_ref, zi_ref):
    n1 = x_ref.shape[0]
    z = _dot(cs_ref[...], x_ref[...])
    zr_ref[...] = z[:n1].astype(zr_ref.dtype)
    zi_ref[...] = z[n1:].astype(zi_ref.dtype)


def _fft2_kernel(zr_ref, zi_ref, m_ref, bdc_ref, bds_ref, o_ref):
    kb, n2, w = zr_ref.shape
    for j in range(kb):
        z = jnp.concatenate([zr_ref[j], zi_ref[j]], axis=0)
        a = _dot(m_ref[j], z)
        y = _dot(a[:n2].astype(BF16), bdc_ref[...]) + _dot(a[n2:].astype(BF16), bds_ref[...])
        o_ref[:, j * w:(j + 1) * w] = y.astype(o_ref.dtype)


def _fourier_long(u, tables, chan):
    bsz, length, w = u.shape
    stage1, stage2 = tables
    bdc, bds = chan
    n2 = FFT_N2
    n1 = length // n2
    tn = 4096
    cols = n2 * w
    zr, zi = pl.pallas_call(
        _fft1_kernel,
        grid=(bsz, cols // tn),
        in_specs=[pl.BlockSpec((None, n1, tn), lambda b, j: (b, 0, j)),
                  pl.BlockSpec((2 * n1, n1), lambda b, j: (0, 0))],
        out_specs=[pl.BlockSpec((None, n1, tn), lambda b, j: (b, 0, j))] * 2,
        out_shape=[jax.ShapeDtypeStruct((bsz, n1, cols), BF16)] * 2,
        compiler_params=_cparams("arbitrary", "arbitrary"),
    )(u.reshape(bsz, n1, cols), stage1)
    kb = FFT_KB
    z_spec = pl.BlockSpec((None, kb, n2, w), lambda b, j: (b, j, 0, 0))
    y = pl.pallas_call(
        _fft2_kernel,
        grid=(bsz, n1 // kb),
        in_specs=[z_spec, z_spec,
                  pl.BlockSpec((kb, 2 * n2, 2 * n2), lambda b, j: (j, 0, 0)),
                  pl.BlockSpec((w, w), lambda b, j: (0, 0)),
                  pl.BlockSpec((w, w), lambda b, j: (0, 0))],
        out_specs=pl.BlockSpec((None, n2, kb * w), lambda b, j: (b, 0, j)),
        out_shape=jax.ShapeDtypeStruct((bsz, n2, n1 * w), BF16),
        compiler_params=_cparams("arbitrary", "arbitrary"),
    )(zr.reshape(bsz, n1, n2, w), zi.reshape(bsz, n1, n2, w), stage2, bdc, bds)
    return y.reshape(bsz, length, w)


def _dft_short_kernel(u_ref, c_ref, s_ref, bdc_ref, bds_ref, o_ref):
    u = u_ref[...]
    p = _dot(u, bdc_ref[...]).astype(BF16)
    q = _dot(u, bds_ref[...]).astype(BF16)
    o_ref[...] = (_dot(c_ref[...], p) - _dot(s_ref[...], q)).astype(o_ref.dtype)


def _fourier_short(u, chan):
    bsz, length, w = u.shape
    bdc, bds = chan
    k = jnp.arange(length, dtype=jnp.int32)
    ang = ((k[:, None] * k[None, :]) % length).astype(F32) * (2.0 * jnp.pi / length)
    c = (jnp.cos(ang) * length ** -0.5).astype(BF16)
    s = (jnp.sin(ang) * length ** -0.5).astype(BF16)
    full = lambda n: pl.BlockSpec((n, n), lambda b: (0, 0))
    return pl.pallas_call(
        _dft_short_kernel,
        grid=(bsz,),
        in_specs=[pl.BlockSpec((None, length, w), lambda b: (b, 0, 0)),
                  full(length), full(length), full(w), full(w)],
        out_specs=pl.BlockSpec((None, length, w), lambda b: (b, 0, 0)),
        out_shape=jax.ShapeDtypeStruct((bsz, length, w), BF16),
        compiler_params=_cparams("arbitrary"),
    )(u, c, s, bdc, bds)


def _conv_kernel(x_ref, w_ref, cb_ref, lg_ref, lb_ref, o_ref, pad_ref, *, seg):
    nseg = x_ref.shape[0] // seg
    width = x_ref.shape[1]
    halo = jnp.zeros((CONV_HALO, width), F32)
    for s in range(nseg):
        pad_ref[s, 0:CONV_HALO, :] = halo
        pad_ref[s, CONV_HALO:CONV_HALO + seg, :] = x_ref[s * seg:(s + 1) * seg, :].astype(F32)
        pad_ref[s, CONV_HALO + seg:2 * CONV_HALO + seg, :] = halo
    first = CONV_HALO - CONV_KERNEL // 2
    sub = 8
    span = seg + 2 * CONV_HALO - sub
    for s in range(nseg):
        acc = jnp.zeros((seg, width), F32)
        for r in range(sub):
            shifted = pad_ref[s, r:r + span, :]
            for a in range((span - seg) // sub + 1):
                j = a * sub + r - first
                if 0 <= j < CONV_KERNEL:
                    acc = acc + shifted[a * sub:a * sub + seg, :] * w_ref[j:j + 1, :]
        y = acc + cb_ref[...]
        mu = jnp.mean(y, axis=-1, keepdims=True)
        yc = y - mu
        var = jnp.mean(yc * yc, axis=-1, keepdims=True)
        z = yc * lax.rsqrt(var + NORM_EPS) * lg_ref[...] + lb_ref[...]
        o_ref[s * seg:(s + 1) * seg, :] = (z * _sigmoid(z)).astype(o_ref.dtype)


def _conv_call(glu, seg, conv_w, conv_b, ln_g, ln_b):
    bsz, n, w = glu.shape
    t = max(seg, min(TM_ROWS, n))
    assert n % t == 0 and t % seg == 0
    vec = lambda: pl.BlockSpec((1, w), lambda b, i: (0, 0))
    return pl.pallas_call(
        functools.partial(_conv_kernel, seg=seg),
        grid=(bsz, n // t),
        in_specs=[pl.BlockSpec((None, t, w), lambda b, i: (b, i, 0)),
                  pl.BlockSpec((CONV_KERNEL, w), lambda b, i: (0, 0)),
                  vec(), vec(), vec()],
        out_specs=pl.BlockSpec((None, t, w), lambda b, i: (b, i, 0)),
        out_shape=jax.ShapeDtypeStruct((bsz, n, w), BF16),
        scratch_shapes=[pltpu.VMEM((t // seg, seg + 2 * CONV_HALO, w), F32)],
        compiler_params=_cparams("arbitrary", "arbitrary"),
    )(glu, conv_w, conv_b.reshape(1, w), ln_g.reshape(1, w), ln_b.reshape(1, w))


def _gla_direction(q_ref, k_ref, v_ref, gl_ref, wg_ref, bg_ref, o_ref, st_ref, reverse):
    c = GLA_CHUNK
    n_sub = q_ref.shape[0] // c
    st = st_ref[...]
    for s in (reversed(range(n_sub)) if reverse else range(n_sub)):
        rows = slice(s * c, (s + 1) * c)
        st = _gla_chunk(q_ref[rows, :], k_ref[rows, :], v_ref[rows, :], gl_ref[rows, :], wg_ref, bg_ref,
                        o_ref.at[rows, :], st, reverse)
    st_ref[...] = st


def _gla_chunk(q, k, v, glr, wg_ref, bg_ref, o_ref, st, reverse):
    c = GLA_CHUNK
    kw = GLA_KEY_WIDTH
    col0 = kw if reverse else 0
    pre = _dot3(glr, wg_ref[:, col0:col0 + kw]) + bg_ref[:, col0:col0 + kw]
    g = (jnp.minimum(pre, 0.0) - jnp.log(1.0 + jnp.exp(-jnp.abs(pre)))) * (1.0 / GLA_GATE_NORMALIZER)
    row = lax.broadcasted_iota(jnp.int32, (c, c), 0)
    col = lax.broadcasted_iota(jnp.int32, (c, c), 1)
    seen = (col >= row) if reverse else (col <= row)
    tri = jnp.where(seen, 1.0, 0.0).astype(BF16)
    gh, gl = _split(g)
    b = _dot(tri, gh) + _dot(tri, gl)
    mid = c // 2 if reverse else c // 2 - 1
    last = 0 if reverse else c - 1
    b_mid = b[mid:mid + 1, :]
    b_last = b[last:last + 1, :]
    q = q.astype(F32)
    k = k.astype(F32)
    qe = q * jnp.exp(b - b_mid)
    ke = k * jnp.exp(b_mid - b)
    kd = k * jnp.exp(b_last - b)
    head_of_lane = lax.broadcasted_iota(jnp.int32, (1, kw), 1) // GLA_DK
    q_heads = jnp.concatenate(
        [jnp.where(head_of_lane == h, qe, 0.0) for h in range(GLA_HEADS)], axis=0).astype(BF16)
    rhs = jnp.concatenate([ke, st * jnp.exp(b_mid)], axis=0).astype(BF16)
    res = _dot_nt(q_heads, rhs)
    outs = []
    for h in range(GLA_HEADS):
        blk = res[h * c:(h + 1) * c, :]
        scores = jnp.where(seen, blk[:, :c], 0.0).astype(BF16)
        outs.append(_dot(scores, v[:, h * GLA_DV:(h + 1) * GLA_DV]) + blk[:, c:])
    o_ref[...] = jnp.concatenate(outs, axis=1).astype(o_ref.dtype)
    kv = _dot_tn(v, kd.astype(BF16))
    ds = jnp.zeros_like(st)
    for h in range(GLA_HEADS):
        ds = ds + jnp.where(head_of_lane == h, kv[h * GLA_DV:(h + 1) * GLA_DV, :], 0.0)
    return st * jnp.exp(b_last) + ds


def _gla_kernel(qf, kf, vf, gf, qb, kb, vb, gb, wg_ref, bg_ref, of_ref, ob_ref, sf_ref, sb_ref):
    @pl.when(pl.program_id(1) == 0)
    def _():
        sf_ref[...] = jnp.zeros_like(sf_ref)
        sb_ref[...] = jnp.zeros_like(sb_ref)

    _gla_direction(qf, kf, vf, gf, wg_ref, bg_ref, of_ref, sf_ref, False)
    _gla_direction(qb, kb, vb, gb, wg_ref, bg_ref, ob_ref, sb_ref, True)


def _gla_call(q, k, v, gl, wg_pad, bg_cat, n_lat, n_ctx):
    bsz, lt, _ = q.shape
    c = GLA_STEP_CHUNKS * GLA_CHUNK
    assert n_lat % c == 0 and n_ctx % c == 0
    cl, cc = n_lat // c, n_ctx // c

    def fwd_blk(j):
        return jnp.where(j < cc, cl + j, j - cc)

    def bwd_blk(j):
        return jnp.where(j < cc, cl + cc - 1 - j, cl - 1 - (j - cc))

    def spec(w, blk):
        return pl.BlockSpec((None, c, w), lambda b, j: (b, blk(j), 0))

    widths = (GLA_KEY_WIDTH, GLA_KEY_WIDTH, GLA_VALUE_WIDTH, GLR_PAD)
    in_specs = [spec(w, fwd_blk) for w in widths] + [spec(w, bwd_blk) for w in widths] + [
        pl.BlockSpec(wg_pad.shape, lambda b, j: (0, 0)),
        pl.BlockSpec(bg_cat.shape, lambda b, j: (0, 0))]
    return pl.pallas_call(
        _gla_kernel,
        grid=(bsz, cl + cc),
        in_specs=in_specs,
        out_specs=[spec(GLA_VALUE_WIDTH, fwd_blk), spec(GLA_VALUE_WIDTH, bwd_blk)],
        out_shape=[jax.ShapeDtypeStruct((bsz, lt, GLA_VALUE_WIDTH), F32)] * 2,
        scratch_shapes=[pltpu.VMEM((GLA_DV, GLA_KEY_WIDTH), F32)] * 2,
        compiler_params=_cparams("arbitrary", "arbitrary"),
    )(q, k, v, gl, q, k, v, gl, wg_pad, bg_cat)


def _outproj_kernel(yf_ref, of_ref, ob_ref, r_ref, cv_ref, x_ref, gate_ref, sh_ref, sc_ref,
                    gn_ref, wo_ref, n2_ref, rwh_ref, rwl_ref, rb_ref, *rest):
    xo_ref, h2_ref, idx_ref, prob_ref = rest[-4:]
    o = of_ref[...] + ob_ref[...]
    heads = []
    for h in range(GLA_HEADS):
        oh = o[:, h * GLA_DV:(h + 1) * GLA_DV]
        heads.append(oh * lax.rsqrt(jnp.mean(oh * oh, axis=-1, keepdims=True) + NORM_EPS))
    r = r_ref[...].astype(F32)
    gla = jnp.concatenate(heads, axis=1) * gn_ref[...] * (r * _sigmoid(r))
    c0, c1 = FOURIER_WIDTH, FOURIER_WIDTH + GLA_VALUE_WIDTH
    y = (_dot(yf_ref[...], wo_ref[0:c0, :]) + _dot(gla.astype(BF16), wo_ref[c0:c1, :])
         + _dot(cv_ref[...], wo_ref[c1:, :]))
    xn = x_ref[...] + gate_ref[...] * y
    xo_ref[...] = xn
    h2 = _rms(xn, n2_ref[...]) * (1.0 + sc_ref[...]) + sh_ref[...]
    hh, hl = _split(h2)
    h2_ref[...] = h2
    logits = (_dot_nt(rwh_ref[...], hh) + _dot_nt(rwh_ref[...], hl) + _dot_nt(rwl_ref[...], hh)
              + rb_ref[...])
    expert = lax.broadcasted_iota(jnp.int32, logits.shape, 0)
    vals, idxs = [], []
    cur = logits
    for _ in range(TOP_K):
        m = jnp.max(cur, axis=0, keepdims=True)
        ix = jnp.min(jnp.where(cur == m, expert, N_EXPERTS), axis=0, keepdims=True)
        vals.append(m)
        idxs.append(ix)
        cur = jnp.where(expert == ix, -jnp.inf, cur)
    es = [jnp.exp(vv - vals[0]) for vv in vals]
    inv = 1.0 / functools.reduce(lambda a, b: a + b, es)
    idx_ref[...] = jnp.concatenate(idxs, axis=0)
    prob_ref[...] = jnp.concatenate([e * inv for e in es], axis=0)


def _outproj_call(yf, o_f, o_b, o_row0, r, cv, x, mods, group_of_batch, gn_tiled, w_out, norm2_g,
                  rw_hi, rw_lo, rb, n_tok, tok0, carried=None):
    bsz, n, d = x.shape
    tm = min(TM_ROWS, n)
    assert n % tm == 0 and o_row0 % tm == 0 and tok0 % tm == 0
    nb = n // tm
    row = lambda w: pl.BlockSpec((None, tm, w), lambda b, i: (b, i, 0))
    orow = pl.BlockSpec((None, tm, GLA_VALUE_WIDTH), lambda b, i: (b, o_row0 // tm + i, 0))
    mod = lambda which: pl.BlockSpec((None, None, 1, d), lambda b, i: (group_of_batch(b), which, 0, 0))
    const = lambda a: pl.BlockSpec(a.shape, lambda b, i: (0,) * a.ndim)
    consts = [gn_tiled, w_out, norm2_g.reshape(1, d), rw_hi, rw_lo, rb]
    in_specs = [row(FOURIER_WIDTH), orow, orow, row(GLA_VALUE_WIDTH), row(CONV_WIDTH), row(d),
                mod(2), mod(3), mod(4)] + [const(a) for a in consts]
    args = [yf, o_f, o_b, r, cv, x, mods, mods, mods] + consts
    aliases = {}
    if carried is not None:
        for t, arr in enumerate(carried):
            in_specs.append(pl.BlockSpec(memory_space=pl.ANY))
            aliases[len(args)] = 1 + t
            args.append(arr)
    tokblk = lambda b, i: tok0 // tm + b * nb + i
    out_specs = [row(d),
                 pl.BlockSpec((tm, d), lambda b, i: (tokblk(b, i), 0)),
                 pl.BlockSpec((TOP_K, tm), lambda b, i: (0, tokblk(b, i))),
                 pl.BlockSpec((TOP_K, tm), lambda b, i: (0, tokblk(b, i)))]
    out_shape = [jax.ShapeDtypeStruct((bsz, n, d), F32),
                 jax.ShapeDtypeStruct((n_tok, d), F32),
                 jax.ShapeDtypeStruct((TOP_K, n_tok), jnp.int32),
                 jax.ShapeDtypeStruct((TOP_K, n_tok), F32)]
    outs = pl.pallas_call(
        _outproj_kernel,
        grid=(bsz, nb),
        in_specs=in_specs,
        out_specs=out_specs,
        out_shape=out_shape,
        input_output_aliases=aliases,
        compiler_params=_cparams("arbitrary", "arbitrary"),
    )(*args)
    return outs[0], outs[1:]


def _row_copy(src, s, dst, t, sem):
    return pltpu.make_async_copy(src.at[pl.ds(s, 1), :], dst.at[pl.ds(t, 1), :], sem)


def _expert_kernel(be_ref, nu_ref, inv_hbm, h_hbm, wgu_ref, bgu_ref, wdn_ref, bdn_ref, yk_hbm,
                   wgu_s, wdn_s, xbuf, obuf, idx_s, gsem, ssem, isem, zsem, *, n_rows):
    i = pl.program_id(0)
    nu = nu_ref[0]
    tm = xbuf.shape[1]
    n_chunks = wgu_s.shape[0]
    fc = D_FF // n_chunks

    def idx_fetch(blk):
        slot = blk % 3
        return pltpu.make_async_copy(inv_hbm.at[blk, 0], idx_s.at[pl.ds(pl.multiple_of(slot * tm, tm), tm)],
                                     isem.at[slot])

    def gather_start(blk):
        base = (blk % 3) * tm
        buf = blk % 2
        for r in range(tm):
            tok = jnp.maximum(idx_s[base + r] - 1, 0) >> 2
            _row_copy(h_hbm, tok, xbuf.at[buf], r, gsem.at[buf]).start()

    def gather_wait(blk):
        buf = blk % 2
        pltpu.make_async_copy(h_hbm.at[pl.ds(0, tm), :], xbuf.at[buf], gsem.at[buf]).wait()

    def scatter_start(blk):
        base = (blk % 3) * tm
        buf = blk % 2
        for r in range(tm):
            v = idx_s[base + r]
            dst = jnp.where(v > 0, v - 1, n_rows + buf * tm + r)
            _row_copy(obuf.at[buf], r, yk_hbm, dst, ssem.at[buf]).start()

    def scatter_wait(blk):
        buf = blk % 2
        pltpu.make_async_copy(obuf.at[buf], yk_hbm.at[pl.ds(0, tm), :], ssem.at[buf]).wait()

    @pl.when(i == 0)
    def _():
        xbuf[...] = jnp.zeros_like(xbuf)
        spares = [pltpu.make_async_copy(xbuf.at[b], yk_hbm.at[pl.ds(n_rows + b * tm, tm), :], zsem)
                  for b in range(2)]
        for spare in spares:
            spare.start()
        idx_fetch(0).start()
        idx_fetch(0).wait()

        @pl.when(nu > 1)
        def _():
            idx_fetch(1).start()
            idx_fetch(1).wait()

        for spare in spares:
            spare.wait()
        gather_start(0)

    @pl.when(i < nu)
    def _():
        @pl.when(i + 2 < nu)
        def _():
            idx_fetch(i + 2).start()

        changed = jnp.logical_or(i == 0, be_ref[i] != be_ref[jnp.maximum(i - 1, 0)])

        @pl.when(changed)
        def _():
            rows = 128
            for s in range(wgu_ref.shape[0] // rows):
                rs = slice(s * rows, (s + 1) * rows)
                for n in range(n_chunks):
                    wgu_s[n, rs, 0:fc] = wgu_ref[rs, n * fc:(n + 1) * fc].astype(BF16)
                    wgu_s[n, rs, fc:2 * fc] = wgu_ref[rs, D_FF + n * fc:D_FF + (n + 1) * fc].astype(BF16)
                wdn_s[rs, :] = wdn_ref[rs, :].astype(BF16)

        gather_wait(i)

        @pl.when(i >= 2)
        def _():
            scatter_wait(i - 2)

        @pl.when(i + 1 < nu)
        def _():
            gather_start(i + 1)

        buf = i % 2
        x = xbuf[buf].astype(BF16)
        acts = []
        for n in range(n_chunks):
            gate = _dot(x, wgu_s[n, :, 0:fc]) + bgu_ref[:, n * fc:(n + 1) * fc]
            up = _dot(x, wgu_s[n, :, fc:2 * fc]) + bgu_ref[:, D_FF + n * fc:D_FF + (n + 1) * fc]
            gate = jnp.minimum(gate, SWIGLU_LIMIT)
            up = jnp.clip(up, -SWIGLU_LIMIT, SWIGLU_LIMIT)
            acts.append((gate * _sigmoid(SWIGLU_ALPHA * gate) * (up + 1.0)).astype(BF16))
        half = n_chunks // 2
        y = (_dot(jnp.concatenate(acts[:half], axis=1), wdn_s[0:half * fc, :])
             + _dot(jnp.concatenate(acts[half:], axis=1), wdn_s[half * fc:, :]) + bdn_ref[...])
        obuf[buf] = y
        scatter_start(i)

        @pl.when(i + 2 < nu)
        def _():
            idx_fetch(i + 2).wait()

        @pl.when(i == nu - 1)
        def _():
            @pl.when(i >= 1)
            def _():
                scatter_wait(i - 1)
            scatter_wait(i)


def _expert_call(block_e, n_used, inv, h2, layer, w_gu, b_gu, w_dn, b_dn):
    n_tok, d = h2.shape
    nblk, _, tm = inv.shape
    depth, e, _, f2 = w_gu.shape
    n_rows = TOP_K * n_tok
    n_chunks = 4
    live = lambda i, nu: jnp.minimum(i, nu[0] - 1)
    wmap = lambda i, be, nu: (layer, be[live(i, nu)], 0, 0)
    grid_spec = pltpu.PrefetchScalarGridSpec(
        num_scalar_prefetch=2,
        grid=(nblk,),
        in_specs=[
            pl.BlockSpec(memory_space=pl.ANY),
            pl.BlockSpec(memory_space=pl.ANY),
            pl.BlockSpec((None, None, d, f2), wmap),
            pl.BlockSpec((None, None, 1, f2), wmap),
            pl.BlockSpec((None, None, f2 // 2, d), wmap),
            pl.BlockSpec((None, None, 1, d), wmap),
        ],
        out_specs=pl.BlockSpec(memory_space=pl.ANY),
        scratch_shapes=[pltpu.VMEM((n_chunks, d, f2 // n_chunks), BF16), pltpu.VMEM((f2 // 2, d), BF16),
                        pltpu.VMEM((2, tm, d), F32), pltpu.VMEM((2, tm, d), F32),
                        pltpu.SMEM((3 * tm,), jnp.int32),
                        pltpu.SemaphoreType.DMA((2,)), pltpu.SemaphoreType.DMA((2,)),
                        pltpu.SemaphoreType.DMA((3,)), pltpu.SemaphoreType.DMA],
    )
    return pl.pallas_call(
        functools.partial(_expert_kernel, n_rows=n_rows),
        grid_spec=grid_spec,
        out_shape=jax.ShapeDtypeStruct((n_rows + 2 * tm, d), F32),
        compiler_params=_cparams("arbitrary"),
    )(block_e, n_used, inv, h2, w_gu, b_gu.reshape(depth, e, 1, f2), w_dn, b_dn.reshape(depth, e, 1, d))


def _rank_kernel(idx_ref, rank_ref, cnt_ref, base_ref):
    @pl.when(pl.program_id(0) == 0)
    def _():
        base_ref[...] = jnp.zeros_like(base_ref)

    tb = idx_ref.shape[1]
    row = lax.broadcasted_iota(jnp.int32, (tb, tb), 0)
    col = lax.broadcasted_iota(jnp.int32, (tb, tb), 1)
    earlier = jnp.where(row < col, 1.0, 0.0).astype(BF16)
    expert = lax.broadcasted_iota(jnp.int32, (N_EXPERTS, tb), 0)
    base = base_ref[...]
    ranks = []
    for kk in range(TOP_K):
        hit = expert == idx_ref[kk:kk + 1, :]
        onehot = hit.astype(F32)
        before = _dot(onehot.astype(BF16), earlier) + base
        ranks.append(jnp.sum(before * onehot, axis=0, keepdims=True))
        base = base + jnp.sum(onehot, axis=1, keepdims=True)
    rank_ref[...] = jnp.concatenate(ranks, axis=0).astype(jnp.int32)
    base_ref[...] = base
    cnt_ref[...] = base.astype(jnp.int32)


def _slot_kernel(idx_ref, rank_ref, start_ref, slot_ref):
    expert = lax.broadcasted_iota(jnp.int32, (N_EXPERTS, idx_ref.shape[1]), 0)
    rows = []
    for kk in range(TOP_K):
        start = jnp.sum(jnp.where(expert == idx_ref[kk:kk + 1, :], start_ref[...], 0), axis=0,
                        keepdims=True)
        rows.append(rank_ref[kk:kk + 1, :] + start)
    slot_ref[...] = jnp.concatenate(rows, axis=1)


def _routing_tables(idx_t, n_tok):
    tm = TM_EXPERT
    tb = TM_ROWS
    assert n_tok % tb == 0
    blk = pl.BlockSpec((TOP_K, tb), lambda i: (0, i))
    rank_t, counts = pl.pallas_call(
        _rank_kernel,
        grid=(n_tok // tb,),
        in_specs=[blk],
        out_specs=[blk, pl.BlockSpec((N_EXPERTS, 1), lambda i: (0, 0))],
        out_shape=[jax.ShapeDtypeStruct((TOP_K, n_tok), jnp.int32),
                   jax.ShapeDtypeStruct((N_EXPERTS, 1), jnp.int32)],
        scratch_shapes=[pltpu.VMEM((N_EXPERTS, 1), F32)],
        compiler_params=_cparams("arbitrary"),
    )(idx_t)
    counts = counts[:, 0]
    padded = (counts + tm - 1) // tm * tm
    pad_end = jnp.cumsum(padded)
    pad_start = pad_end - padded
    slot_blocks = pl.pallas_call(
        _slot_kernel,
        grid=(n_tok // tb,),
        in_specs=[blk, blk, pl.BlockSpec((N_EXPERTS, 1), lambda i: (0, 0))],
        out_specs=pl.BlockSpec((None, 1, TOP_K * tb), lambda i: (i, 0, 0)),
        out_shape=jax.ShapeDtypeStruct((n_tok // tb, 1, TOP_K * tb), jnp.int32),
        compiler_params=_cparams("arbitrary"),
    )(idx_t, rank_t, pad_start[:, None])
    n_blocks = -(-TOP_K * n_tok // tm) + N_EXPERTS
    first_row = jnp.arange(n_blocks, dtype=jnp.int32) * tm
    block_e = jnp.minimum(jnp.sum(pad_end[None, :] <= first_row[:, None], axis=1),
                          N_EXPERTS - 1).astype(jnp.int32)
    n_used = (pad_end[-1:] // tm).astype(jnp.int32)
    blk = jnp.arange(n_tok // tb, dtype=jnp.int32)[:, None, None]
    kk = jnp.arange(TOP_K, dtype=jnp.int32)[None, :, None]
    tt = jnp.arange(tb, dtype=jnp.int32)[None, None, :]
    code = TOP_K * (blk * tb + tt) + kk + 1
    inv = jnp.zeros((n_blocks * tm,), jnp.int32).at[slot_blocks.reshape(-1)].add(
        code.reshape(-1), unique_indices=True)
    return block_e, n_used, inv.reshape(n_blocks, 1, tm)


def _combine_kernel(y_ref, p_ref, x_ref, gate_ref, fg_ref, o_ref, *, final_norm):
    d = x_ref.shape[1]
    f = jnp.zeros(x_ref.shape, F32)
    for kk in range(TOP_K):
        f = f + y_ref[:, kk * d:(kk + 1) * d] * p_ref[:, kk:kk + 1]
    xn = x_ref[...] + gate_ref[...] * f
    if final_norm:
        xn = _rms(xn, fg_ref[...])
    o_ref[...] = xn


def _combine_call(yk, probs, x, mods, group_of_block, final_g, tok0, final_norm):
    bsz, n, d = x.shape
    tb = min(TM_ROWS, bsz * n)
    rows = bsz * n
    assert rows % tb == 0 and tok0 % tb == 0 and (n % tb == 0 or tb % n == 0)
    blk0 = tok0 // tb
    out = pl.pallas_call(
        functools.partial(_combine_kernel, final_norm=final_norm),
        grid=(rows // tb,),
        in_specs=[pl.BlockSpec((tb, TOP_K * d), lambda i: (blk0 + i, 0)),
                  pl.BlockSpec((tb, TOP_K), lambda i: (blk0 + i, 0)),
                  pl.BlockSpec((tb, d), lambda i: (i, 0)),
                  pl.BlockSpec((None, None, 1, d), lambda i: (group_of_block(i), 5, 0, 0)),
                  pl.BlockSpec((1, d), lambda i: (0, 0))],
        out_specs=pl.BlockSpec((tb, d), lambda i: (i, 0)),
        out_shape=jax.ShapeDtypeStruct((rows, d), F32),
        compiler_params=_cparams("arbitrary"),
    )(yk.reshape(-1, TOP_K * d), probs, x.reshape(rows, d), mods, final_g.reshape(1, d))
    return out.reshape(bsz, n, d)


def kernel(x, c, ctx, c_ctx, norm1_g, norm2_g, w_mod, b_mod, w_in, gla_wg2_f, gla_bg_f, gla_wg2_b,
           gla_bg_b, gla_norm_g, conv_w, conv_b, conv_ln_g, conv_ln_b, w_out, router_w, router_b,
           exp_w_gu, exp_b_gu, exp_w_dn, exp_b_dn, final_norm_g):
    bsz, seq, d = x.shape
    n_ctx = ctx.shape[1]
    depth = w_mod.shape[0]
    assert d == D_MODEL and seq % (FFT_N2 * FFT_KB) == 0 and seq % n_ctx == 0
    lt = seq + n_ctx
    ctx_group = bsz

    rows = 8
    cvec = jnp.concatenate([c, c_ctx[None, :], jnp.zeros((rows - bsz - 1, d), F32)], axis=0)
    mods_all = _mod_call(cvec, w_mod, b_mod).reshape(depth, rows, 6, 1, d)

    def pack_w_in(w):
        o = [0, 256, 512, 768, 1280, 1792, 1808, 1824, 2336]
        parts = [w[:, o[0]:o[1]], w[:, o[1]:o[2]], w[:, o[2]:o[3]], w[:, o[3]:o[4]], w[:, o[4]:o[5]],
                 w[:, o[7]:o[7] + CONV_WIDTH], w[:, o[7] + CONV_WIDTH:o[8]], w[:, o[5]:o[7]],
                 jnp.zeros((d, GLR_PAD - 2 * GLA_GATE_RANK), w.dtype)]
        return jnp.concatenate(parts, axis=1).astype(BF16)

    long_tables = _dft_tables(seq)
    chan = _channel_tables()
    x_lat, x_ctx = x, ctx
    lat_group = lambda b: b
    ctx_group_fn = lambda b: ctx_group

    for layer in range(depth):
        last = layer == depth - 1
        mods = mods_all[layer]
        w_in_p = pack_w_in(w_in[layer])
        wg_pad = jnp.zeros((GLR_PAD, 2 * GLA_KEY_WIDTH), F32)
        wg_pad = wg_pad.at[:GLA_GATE_RANK, :GLA_KEY_WIDTH].set(gla_wg2_f[layer])
        wg_pad = wg_pad.at[GLA_GATE_RANK:2 * GLA_GATE_RANK, GLA_KEY_WIDTH:].set(gla_wg2_b[layer])
        bg_cat = jnp.concatenate([gla_bg_f[layer], gla_bg_b[layer]])[None, :]
        w_out_b = w_out[layer].astype(BF16)
        gn_tiled = jnp.tile(gla_norm_g[layer], GLA_HEADS)[None, :]
        rw_t = router_w[layer].T
        rw_hi = rw_t.astype(BF16)
        rw_lo = (rw_t - rw_hi.astype(F32)).astype(BF16)
        rb = router_b[layer][:, None]

        (u_l, r_l, glu_l), comb = _inproj_call(x_lat, mods, lat_group, norm1_g[layer], w_in_p, lt, 0)
        (u_c, r_c, glu_c), comb = _inproj_call(x_ctx, mods, ctx_group_fn, norm1_g[layer], w_in_p, lt,
                                               seq, combined=comb)
        o_f, o_b = _gla_call(*comb, wg_pad, bg_cat, seq, n_ctx)
        yf_l = _fourier_long(u_l, long_tables, chan)
        cv_l = _conv_call(glu_l, seq // (seq // GRID_W), conv_w[layer], conv_b[layer],
                          conv_ln_g[layer], conv_ln_b[layer])
        n_tok = bsz * seq + (0 if last else bsz * n_ctx)
        x_lat, routed = _outproj_call(yf_l, o_f, o_b, 0, r_l, cv_l, x_lat, mods, lat_group, gn_tiled,
                                      w_out_b, norm2_g[layer], rw_hi, rw_lo, rb, n_tok, 0)
        if not last:
            yf_c = _fourier_short(u_c, chan)
            cv_c = _conv_call(glu_c, n_ctx, conv_w[layer], conv_b[layer], conv_ln_g[layer],
                              conv_ln_b[layer])
            x_ctx, routed = _outproj_call(yf_c, o_f, o_b, seq, r_c, cv_c, x_ctx, mods, ctx_group_fn,
                                          gn_tiled, w_out_b, norm2_g[layer], rw_hi, rw_lo, rb, n_tok,
                                          bsz * seq, carried=routed)

        h2, idx_t, prob_t = routed
        block_e, n_used, inv = _routing_tables(idx_t, n_tok)
        yk = _expert_call(block_e, n_used, inv, h2, layer, exp_w_gu, exp_b_gu, exp_w_dn, exp_b_dn)
        probs = prob_t.T
        lat_blocks = seq // TM_ROWS
        x_lat = _combine_call(yk, probs, x_lat, mods, lambda i: i // lat_blocks, final_norm_g, 0, last)
        if not last:
            x_ctx = _combine_call(yk, probs, x_ctx, mods, lambda i: ctx_group, final_norm_g, bsz * seq,
                                  False)

    return x_lat
```

```python
import functools

import jax
import jax.numpy as jnp
from jax import lax
from jax.experimental import pallas as pl
from jax.experimental.pallas import tpu as pltpu

F32 = jnp.float32
BF16 = jnp.bfloat16

D_MODEL = 1024
DEPTH = 2
GRID_W = 64
FOURIER_WIDTH = 256
FOURIER_HEADS = 4
FOURIER_HEAD_DIM = FOURIER_WIDTH // FOURIER_HEADS
GLA_HEADS = 4
GLA_KEY_WIDTH = 256
GLA_VALUE_WIDTH = 512
GLA_DK = GLA_KEY_WIDTH // GLA_HEADS
GLA_DV = GLA_VALUE_WIDTH // GLA_HEADS
GLA_GATE_RANK = 16
GLA_GATE_NORMALIZER = 16.0
CONV_WIDTH = 256
CONV_KERNEL = 31
N_EXPERTS = 32
TOP_K = 4
D_FF = D_MODEL
SWIGLU_LIMIT = 7.0
SWIGLU_ALPHA = 1.702
NORM_EPS = 1e-6

LANES = 128
VMEM_LIMIT = 56 * 1024 * 1024

COL_U = 0
COL_Q = COL_U + FOURIER_WIDTH
COL_K = COL_Q + GLA_KEY_WIDTH
COL_V = COL_K + GLA_KEY_WIDTH
COL_R = COL_V + GLA_VALUE_WIDTH
COL_CA = COL_R + GLA_VALUE_WIDTH
COL_CG = COL_CA + CONV_WIDTH
COL_GL = COL_CG + CONV_WIDTH
GLR_PAD = LANES
IN_PAD = COL_GL + GLR_PAD

GLA_CHUNK = 128
GLA_STEP_CHUNKS = 2
TM_ROWS = 512
TM_EXPERT = 512
EXPERT_CHUNKS = 4
FFT_N2 = 128
FFT_KB = 8
CONV_HALO = 16


def _cparams(*sem):
    return pltpu.CompilerParams(dimension_semantics=sem, vmem_limit_bytes=VMEM_LIMIT)


def _dot(a, b):
    return jnp.dot(a, b, preferred_element_type=F32)


def _dot_nt(a, b):
    return lax.dot_general(a, b, (((1,), (1,)), ((), ())), preferred_element_type=F32)


def _dot_tn(a, b):
    return lax.dot_general(a, b, (((0,), (0,)), ((), ())), preferred_element_type=F32)


def _split(a):
    hi = a.astype(BF16)
    lo = (a - hi.astype(F32)).astype(BF16)
    return hi, lo


def _dot3(a, b):
    ah, al = _split(a)
    bh, bl = _split(b)
    return _dot(ah, bh) + _dot(ah, bl) + _dot(al, bh)


def _sigmoid(x):
    return 1.0 / (1.0 + jnp.exp(-x))


def _rms(x, g):
    ms = jnp.mean(x * x, axis=-1, keepdims=True)
    return x * lax.rsqrt(ms + NORM_EPS) * g


def _mod_kernel(cv_ref, w_ref, b_ref, o_ref):
    cv = cv_ref[...]
    a = cv * _sigmoid(cv)
    o_ref[...] = _dot3(a, w_ref[...]) + b_ref[...]


def _mod_call(cvec, w_mod, b_mod):
    depth, d, n = w_mod.shape
    rows = cvec.shape[0]
    tn = 1536
    return pl.pallas_call(
        _mod_kernel,
        grid=(depth, n // tn),
        in_specs=[
            pl.BlockSpec((rows, d), lambda l, j: (0, 0)),
            pl.BlockSpec((None, d, tn), lambda l, j: (l, 0, j)),
            pl.BlockSpec((None, 1, tn), lambda l, j: (l, 0, j)),
        ],
        out_specs=pl.BlockSpec((None, rows, tn), lambda l, j: (l, 0, j)),
        out_shape=jax.ShapeDtypeStruct((depth, rows, n), F32),
        compiler_params=_cparams("arbitrary", "arbitrary"),
    )(cvec, w_mod, b_mod.reshape(depth, 1, n))


def _inproj_kernel(x_ref, g_ref, sh_ref, sc_ref, w_ref, *rest):
    u_ref, r_ref, glu_ref, q_ref, k_ref, v_ref, gl_ref = rest[-7:]
    x = x_ref[...]
    h = _rms(x, g_ref[...]) * (1.0 + sc_ref[...]) + sh_ref[...]
    p = _dot(h.astype(BF16), w_ref[...])
    u_ref[...] = p[:, COL_U:COL_Q].astype(u_ref.dtype)
    q_ref[...] = (p[:, COL_Q:COL_K] * (GLA_DK ** -0.5)).astype(q_ref.dtype)
    k_ref[...] = p[:, COL_K:COL_V].astype(k_ref.dtype)
    v_ref[...] = p[:, COL_V:COL_R].astype(v_ref.dtype)
    r_ref[...] = p[:, COL_R:COL_CA].astype(r_ref.dtype)
    glu_ref[...] = (p[:, COL_CA:COL_CG] * _sigmoid(p[:, COL_CG:COL_GL])).astype(glu_ref.dtype)
    gl_ref[...] = p[:, COL_GL:IN_PAD]


def _inproj_call(x, mods, group_of_batch, norm_g, w_in_p, lt, row0, combined=None):
    bsz, n, d = x.shape
    tm = min(TM_ROWS, n)
    assert n % tm == 0 and row0 % tm == 0
    blk0 = row0 // tm
    nb = n // tm
    steps = nb + (-(-(lt - n) // tm) if combined is None else 0)
    widths = (GLA_KEY_WIDTH, GLA_KEY_WIDTH, GLA_VALUE_WIDTH, GLR_PAD)
    dtypes = (BF16, BF16, BF16, F32)
    row_spec = lambda w: pl.BlockSpec((None, tm, w), lambda b, i: (b, jnp.minimum(i, nb - 1), 0))
    comb_spec = lambda w: pl.BlockSpec((None, tm, w), lambda b, i: (b, blk0 + i, 0))
    mod_spec = lambda which: pl.BlockSpec(
        (None, None, 1, d), lambda b, i: (group_of_batch(b), which, 0, 0))
    in_specs = [
        row_spec(d),
        pl.BlockSpec((1, d), lambda b, i: (0, 0)),
        mod_spec(0), mod_spec(1),
        pl.BlockSpec((d, IN_PAD), lambda b, i: (0, 0)),
    ]
    args = [x, norm_g.reshape(1, d), mods, mods, w_in_p]
    aliases = {}
    if combined is not None:
        for t, arr in enumerate(combined):
            in_specs.append(pl.BlockSpec(memory_space=pl.ANY))
            aliases[len(args)] = 3 + t
            args.append(arr)
    out_shape = [
        jax.ShapeDtypeStruct((bsz, n, FOURIER_WIDTH), BF16),
        jax.ShapeDtypeStruct((bsz, n, GLA_VALUE_WIDTH), BF16),
        jax.ShapeDtypeStruct((bsz, n, CONV_WIDTH), BF16),
    ] + [jax.ShapeDtypeStruct((bsz, lt, w), dt) for w, dt in zip(widths, dtypes)]
    out_specs = [row_spec(FOURIER_WIDTH), row_spec(GLA_VALUE_WIDTH), row_spec(CONV_WIDTH)] + [
        comb_spec(w) for w in widths]
    outs = pl.pallas_call(
        _inproj_kernel,
        grid=(bsz, steps),
        in_specs=in_specs,
        out_specs=out_specs,
        out_shape=out_shape,
        input_output_aliases=aliases,
        compiler_params=_cparams("arbitrary", "arbitrary"),
    )(*args)
    return outs[:3], outs[3:]


def _dft_tables(length):
    n2 = FFT_N2
    n1 = length // n2
    two_pi = 2.0 * jnp.pi

    def cs(num, den):
        ang = (num % den).astype(F32) * (two_pi / den)
        return jnp.cos(ang), jnp.sin(ang)

    k1 = jnp.arange(n1, dtype=jnp.int32)
    c1, s1 = cs(k1[:, None] * k1[None, :], n1)
    stage1 = (jnp.concatenate([c1, -s1], axis=0) * (n1 ** -0.5)).astype(BF16)
    k2 = jnp.arange(n2, dtype=jnp.int32)
    ct, st = cs(k1[:, None] * k2[None, :], length)
    cf, sf = cs(k2[:, None] * k2[None, :], n2)
    scale = n2 ** -0.5
    mr = (ct[:, None, :] * cf[None] - st[:, None, :] * sf[None]) * scale
    mi = -(st[:, None, :] * cf[None] + ct[:, None, :] * sf[None]) * scale
    stage2 = jnp.concatenate([jnp.concatenate([mr, -mi], axis=2),
                              jnp.concatenate([mi, mr], axis=2)], axis=1).astype(BF16)
    return stage1, stage2


def _channel_tables():
    hd = FOURIER_HEAD_DIM
    c = jnp.arange(FOURIER_WIDTH, dtype=jnp.int32)
    same_head = (c[:, None] // hd) == (c[None, :] // hd)
    ang = (((c[:, None] % hd) * (c[None, :] % hd)) % hd).astype(F32) * (2.0 * jnp.pi / hd)
    scale = hd ** -0.5
    bdc = jnp.where(same_head, jnp.cos(ang) * scale, 0.0).astype(BF16)
    bds = jnp.where(same_head, jnp.sin(ang) * scale, 0.0).astype(BF16)
    return bdc, bds


def _fft1_kernel(x_ref, cs_ref, zr_ref, zi_ref):
    n1 = x_ref.shape[0]
    z = _dot(cs_ref[...], x_ref[...])
    zr_ref[...] = z[:n1].astype(zr_ref.dtype)
    zi_ref[...] = z[n1:].astype(zi_ref.dtype)


def _fft2_kernel(zr_ref, zi_ref, m_ref, bdc_ref, bds_ref, o_ref):
    kb, n2, w = zr_ref.shape
    for j in range(kb):
        z = jnp.concatenate([zr_ref[j], zi_ref[j]], axis=0)
        a = _dot(m_ref[j], z)
        y = _dot(a[:n2].astype(BF16), bdc_ref[...]) + _dot(a[n2:].astype(BF16), bds_ref[...])
        o_ref[:, j * w:(j + 1) * w] = y.astype(o_ref.dtype)


def _fourier_long(u, tables, chan):
    bsz, length, w = u.shape
    stage1, stage2 = tables
    bdc, bds = chan
    n2 = FFT_N2
    n1 = length // n2
    tn = 4096
    cols = n2 * w
    zr, zi = pl.pallas_call(
        _fft1_kernel,
        grid=(bsz, cols // tn),
        in_specs=[pl.BlockSpec((None, n1, tn), lambda b, j: (b, 0, j)),
                  pl.BlockSpec((2 * n1, n1), lambda b, j: (0, 0))],
        out_specs=[pl.BlockSpec((None, n1, tn), lambda b, j: (b, 0, j))] * 2,
        out_shape=[jax.ShapeDtypeStruct((bsz, n1, cols), BF16)] * 2,
        compiler_params=_cparams("arbitrary", "arbitrary"),
    )(u.reshape(bsz, n1, cols), stage1)
    kb = FFT_KB
    z_spec = pl.BlockSpec((None, kb, n2, w), lambda b, j: (b, j, 0, 0))
    y = pl.pallas_call(
        _fft2_kernel,
        grid=(bsz, n1 // kb),
        in_specs=[z_spec, z_spec,
                  pl.BlockSpec((kb, 2 * n2, 2 * n2), lambda b, j: (j, 0, 0)),
                  pl.BlockSpec((w, w), lambda b, j: (0, 0)),
                  pl.BlockSpec((w, w), lambda b, j: (0, 0))],
        out_specs=pl.BlockSpec((None, n2, kb * w), lambda b, j: (b, 0, j)),
        out_shape=jax.ShapeDtypeStruct((bsz, n2, n1 * w), BF16),
        compiler_params=_cparams("arbitrary", "arbitrary"),
    )(zr.reshape(bsz, n1, n2, w), zi.reshape(bsz, n1, n2, w), stage2, bdc, bds)
    return y.reshape(bsz, length, w)


def _dft_short_kernel(u_ref, c_ref, s_ref, bdc_ref, bds_ref, o_ref):
    u = u_ref[...]
    p = _dot(u, bdc_ref[...]).astype(BF16)
    q = _dot(u, bds_ref[...]).astype(BF16)
    o_ref[...] = (_dot(c_ref[...], p) - _dot(s_ref[...], q)).astype(o_ref.dtype)


def _fourier_short(u, chan):
    bsz, length, w = u.shape
    bdc, bds = chan
    k = jnp.arange(length, dtype=jnp.int32)
    ang = ((k[:, None] * k[None, :]) % length).astype(F32) * (2.0 * jnp.pi / length)
    c = (jnp.cos(ang) * length ** -0.5).astype(BF16)
    s = (jnp.sin(ang) * length ** -0.5).astype(BF16)
    full = lambda n: pl.BlockSpec((n, n), lambda b: (0, 0))
    return pl.pallas_call(
        _dft_short_kernel,
        grid=(bsz,),
        in_specs=[pl.BlockSpec((None, length, w), lambda b: (b, 0, 0)),
                  full(length), full(length), full(w), full(w)],
        out_specs=pl.BlockSpec((None, length, w), lambda b: (b, 0, 0)),
        out_shape=jax.ShapeDtypeStruct((bsz, length, w), BF16),
        compiler_params=_cparams("arbitrary"),
    )(u, c, s, bdc, bds)


def _conv_kernel(x_ref, w_ref, cb_ref, lg_ref, lb_ref, o_ref, pad_ref, *, seg):
    nseg = x_ref.shape[0] // seg
    width = x_ref.shape[1]
    halo = jnp.zeros((CONV_HALO, width), F32)
    for s in range(nseg):
        pad_ref[s, 0:CONV_HALO, :] = halo
        pad_ref[s, CONV_HALO:CONV_HALO + seg, :] = x_ref[s * seg:(s + 1) * seg, :].astype(F32)
        pad_ref[s, CONV_HALO + seg:2 * CONV_HALO + seg, :] = halo
    first = CONV_HALO - CONV_KERNEL // 2
    sub = 8
    span = seg + 2 * CONV_HALO - sub
    for s in range(nseg):
        acc = jnp.zeros((seg, width), F32)
        for r in range(sub):
            shifted = pad_ref[s, r:r + span, :]
            for a in range((span - seg) // sub + 1):
                j = a * sub + r - first
                if 0 <= j < CONV_KERNEL:
                    acc = acc + shifted[a * sub:a * sub + seg, :] * w_ref[j:j + 1, :]
        y = acc + cb_ref[...]
        mu = jnp.mean(y, axis=-1, keepdims=True)
        yc = y - mu
        var = jnp.mean(yc * yc, axis=-1, keepdims=True)
        z = yc * lax.rsqrt(var + NORM_EPS) * lg_ref[...] + lb_ref[...]
        o_ref[s * seg:(s + 1) * seg, :] = (z * _sigmoid(z)).astype(o_ref.dtype)


def _conv_call(glu, seg, conv_w, conv_b, ln_g, ln_b):
    bsz, n, w = glu.shape
    t = max(seg, min(TM_ROWS, n))
    assert n % t == 0 and t % seg == 0
    vec = lambda: pl.BlockSpec((1, w), lambda b, i: (0, 0))
    return pl.pallas_call(
        functools.partial(_conv_kernel, seg=seg),
        grid=(bsz, n // t),
        in_specs=[pl.BlockSpec((None, t, w), lambda b, i: (b, i, 0)),
                  pl.BlockSpec((CONV_KERNEL, w), lambda b, i: (0, 0)),
                  vec(), vec(), vec()],
        out_specs=pl.BlockSpec((None, t, w), lambda b, i: (b, i, 0)),
        out_shape=jax.ShapeDtypeStruct((bsz, n, w), BF16),
        scratch_shapes=[pltpu.VMEM((t // seg, seg + 2 * CONV_HALO, w), F32)],
        compiler_params=_cparams("arbitrary", "arbitrary"),
    )(glu, conv_w, conv_b.reshape(1, w), ln_g.reshape(1, w), ln_b.reshape(1, w))


def _gla_direction(q_ref, k_ref, v_ref, gl_ref, wg_ref, bg_ref, o_ref, st_ref, reverse):
    c = GLA_CHUNK
    n_sub = q_ref.shape[0] // c
    st = st_ref[...]
    for s in (reversed(range(n_sub)) if reverse else range(n_sub)):
        rows = slice(s * c, (s + 1) * c)
        st = _gla_chunk(q_ref[rows, :], k_ref[rows, :], v_ref[rows, :], gl_ref[rows, :], wg_ref, bg_ref,
                        o_ref.at[rows, :], st, reverse)
    st_ref[...] = st


def _gla_chunk(q, k, v, glr, wg_ref, bg_ref, o_ref, st, reverse):
    c = GLA_CHUNK
    kw = GLA_KEY_WIDTH
    col0 = kw if reverse else 0
    pre = _dot3(glr, wg_ref[:, col0:col0 + kw]) + bg_ref[:, col0:col0 + kw]
    g = (jnp.minimum(pre, 0.0) - jnp.log(1.0 + jnp.exp(-jnp.abs(pre)))) * (1.0 / GLA_GATE_NORMALIZER)
    row = lax.broadcasted_iota(jnp.int32, (c, c), 0)
    col = lax.broadcasted_iota(jnp.int32, (c, c), 1)
    seen = (col >= row) if reverse else (col <= row)
    tri = jnp.where(seen, 1.0, 0.0).astype(BF16)
    gh, gl = _split(g)
    b = _dot(tri, gh) + _dot(tri, gl)
    mid = c // 2 if reverse else c // 2 - 1
    last = 0 if reverse else c - 1
    b_mid = b[mid:mid + 1, :]
    b_last = b[last:last + 1, :]
    q = q.astype(F32)
    k = k.astype(F32)
    qe = q * jnp.exp(b - b_mid)
    ke = k * jnp.exp(b_mid - b)
    kd = k * jnp.exp(b_last - b)
    head_of_lane = lax.broadcasted_iota(jnp.int32, (1, kw), 1) // GLA_DK
    q_heads = jnp.concatenate(
        [jnp.where(head_of_lane == h, qe, 0.0) for h in range(GLA_HEADS)], axis=0).astype(BF16)
    rhs = jnp.concatenate([ke, st * jnp.exp(b_mid)], axis=0).astype(BF16)
    res = _dot_nt(q_heads, rhs)
    outs = []
    for h in range(GLA_HEADS):
        blk = res[h * c:(h + 1) * c, :]
        scores = jnp.where(seen, blk[:, :c], 0.0).astype(BF16)
        outs.append(_dot(scores, v[:, h * GLA_DV:(h + 1) * GLA_DV]) + blk[:, c:])
    o_ref[...] = jnp.concatenate(outs, axis=1).astype(o_ref.dtype)
    kv = _dot_tn(v, kd.astype(BF16))
    ds = jnp.zeros_like(st)
    for h in range(GLA_HEADS):
        ds = ds + jnp.where(head_of_lane == h, kv[h * GLA_DV:(h + 1) * GLA_DV, :], 0.0)
    return st * jnp.exp(b_last) + ds


def _gla_kernel(qf, kf, vf, gf, qb, kb, vb, gb, wg_ref, bg_ref, of_ref, ob_ref, sf_ref, sb_ref):
    @pl.when(pl.program_id(1) == 0)
    def _():
        sf_ref[...] = jnp.zeros_like(sf_ref)
        sb_ref[...] = jnp.zeros_like(sb_ref)

    _gla_direction(qf, kf, vf, gf, wg_ref, bg_ref, of_ref, sf_ref, False)
    _gla_direction(qb, kb, vb, gb, wg_ref, bg_ref, ob_ref, sb_ref, True)


def _gla_call(q, k, v, gl, wg_pad, bg_cat, n_lat, n_ctx):
    bsz, lt, _ = q.shape
    c = GLA_STEP_CHUNKS * GLA_CHUNK
    assert n_lat % c == 0 and n_ctx % c == 0
    cl, cc = n_lat // c, n_ctx // c

    def fwd_blk(j):
        return jnp.where(j < cc, cl + j, j - cc)

    def bwd_blk(j):
        return jnp.where(j < cc, cl + cc - 1 - j, cl - 1 - (j - cc))

    def spec(w, blk):
        return pl.BlockSpec((None, c, w), lambda b, j: (b, blk(j), 0))

    widths = (GLA_KEY_WIDTH, GLA_KEY_WIDTH, GLA_VALUE_WIDTH, GLR_PAD)
    in_specs = [spec(w, fwd_blk) for w in widths] + [spec(w, bwd_blk) for w in widths] + [
        pl.BlockSpec(wg_pad.shape, lambda b, j: (0, 0)),
        pl.BlockSpec(bg_cat.shape, lambda b, j: (0, 0))]
    return pl.pallas_call(
        _gla_kernel,
        grid=(bsz, cl + cc),
        in_specs=in_specs,
        out_specs=[spec(GLA_VALUE_WIDTH, fwd_blk), spec(GLA_VALUE_WIDTH, bwd_blk)],
        out_shape=[jax.ShapeDtypeStruct((bsz, lt, GLA_VALUE_WIDTH), F32)] * 2,
        scratch_shapes=[pltpu.VMEM((GLA_DV, GLA_KEY_WIDTH), F32)] * 2,
        compiler_params=_cparams("arbitrary", "arbitrary"),
    )(q, k, v, gl, q, k, v, gl, wg_pad, bg_cat)


def _outproj_kernel(yf_ref, of_ref, ob_ref, r_ref, cv_ref, x_ref, gate_ref, sh_ref, sc_ref,
                    gn_ref, wo_ref, n2_ref, rwh_ref, rwl_ref, rb_ref, *rest):
    xo_ref, h2_ref, idx_ref, prob_ref = rest[-4:]
    o = of_ref[...] + ob_ref[...]
    heads = []
    for h in range(GLA_HEADS):
        oh = o[:, h * GLA_DV:(h + 1) * GLA_DV]
        heads.append(oh * lax.rsqrt(jnp.mean(oh * oh, axis=-1, keepdims=True) + NORM_EPS))
    r = r_ref[...].astype(F32)
    gla = jnp.concatenate(heads, axis=1) * gn_ref[...] * (r * _sigmoid(r))
    c0, c1 = FOURIER_WIDTH, FOURIER_WIDTH + GLA_VALUE_WIDTH
    y = (_dot(yf_ref[...], wo_ref[0:c0, :]) + _dot(gla.astype(BF16), wo_ref[c0:c1, :])
         + _dot(cv_ref[...], wo_ref[c1:, :]))
    xn = x_ref[...] + gate_ref[...] * y
    xo_ref[...] = xn
    h2 = _rms(xn, n2_ref[...]) * (1.0 + sc_ref[...]) + sh_ref[...]
    hh, hl = _split(h2)
    h2_ref[...] = h2
    logits = (_dot_nt(rwh_ref[...], hh) + _dot_nt(rwh_ref[...], hl) + _dot_nt(rwl_ref[...], hh)
              + rb_ref[...])
    expert = lax.broadcasted_iota(jnp.int32, logits.shape, 0)
    vals, idxs = [], []
    cur = logits
    for _ in range(TOP_K):
        m = jnp.max(cur, axis=0, keepdims=True)
        ix = jnp.min(jnp.where(cur == m, expert, N_EXPERTS), axis=0, keepdims=True)
        vals.append(m)
        idxs.append(ix)
        cur = jnp.where(expert == ix, -jnp.inf, cur)
    es = [jnp.exp(vv - vals[0]) for vv in vals]
    inv = 1.0 / functools.reduce(lambda a, b: a + b, es)
    idx_ref[...] = jnp.concatenate(idxs, axis=0)
    prob_ref[...] = jnp.concatenate([e * inv for e in es], axis=0)


def _outproj_call(yf, o_f, o_b, o_row0, r, cv, x, mods, group_of_batch, gn_tiled, w_out, norm2_g,
                  rw_hi, rw_lo, rb, n_tok, tok0, carried=None):
    bsz, n, d = x.shape
    tm = min(TM_ROWS, n)
    assert n % tm == 0 and o_row0 % tm == 0 and tok0 % tm == 0
    nb = n // tm
    spare = (n_tok - bsz * n) if carried is None else 0
    assert spare in (0, tm)
    steps = nb + spare // tm
    last = lambda i: jnp.minimum(i, nb - 1)
    row = lambda w: pl.BlockSpec((None, tm, w), lambda b, i: (b, last(i), 0))
    orow = pl.BlockSpec((None, tm, GLA_VALUE_WIDTH), lambda b, i: (b, o_row0 // tm + last(i), 0))
    mod = lambda which: pl.BlockSpec((None, None, 1, d), lambda b, i: (group_of_batch(b), which, 0, 0))
    const = lambda a: pl.BlockSpec(a.shape, lambda b, i: (0,) * a.ndim)
    consts = [gn_tiled, w_out, norm2_g.reshape(1, d), rw_hi, rw_lo, rb]
    in_specs = [row(FOURIER_WIDTH), orow, orow, row(GLA_VALUE_WIDTH), row(CONV_WIDTH), row(d),
                mod(2), mod(3), mod(4)] + [const(a) for a in consts]
    args = [yf, o_f, o_b, r, cv, x, mods, mods, mods] + consts
    aliases = {}
    if carried is not None:
        for t, arr in enumerate(carried):
            in_specs.append(pl.BlockSpec(memory_space=pl.ANY))
            aliases[len(args)] = 1 + t
            args.append(arr)
    tokblk = lambda b, i: tok0 // tm + jnp.where(
        jnp.logical_and(i == nb, b == bsz - 1), bsz * nb, b * nb + last(i))
    out_specs = [row(d),
                 pl.BlockSpec((tm, d), lambda b, i: (tokblk(b, i), 0)),
                 pl.BlockSpec((TOP_K, tm), lambda b, i: (0, tokblk(b, i))),
                 pl.BlockSpec((TOP_K, tm), lambda b, i: (0, tokblk(b, i)))]
    out_shape = [jax.ShapeDtypeStruct((bsz, n, d), F32),
                 jax.ShapeDtypeStruct((n_tok, d), F32),
                 jax.ShapeDtypeStruct((TOP_K, n_tok), jnp.int32),
                 jax.ShapeDtypeStruct((TOP_K, n_tok), F32)]
    outs = pl.pallas_call(
        _outproj_kernel,
        grid=(bsz, steps),
        in_specs=in_specs,
        out_specs=out_specs,
        out_shape=out_shape,
        input_output_aliases=aliases,
        compiler_params=_cparams("arbitrary", "arbitrary"),
    )(*args)
    return outs[0], outs[1:]


def _expert_kernel(be_ref, nu_ref, x_ref, wgu_ref, bgu_ref, wdn_ref, bdn_ref, o_ref, wgu_s, wdn_s):
    i = pl.program_id(0)
    nu = nu_ref[0]
    n_chunks = wgu_s.shape[0]
    fc = D_FF // n_chunks

    @pl.when(i >= nu)
    def _():
        o_ref[...] = jnp.zeros_like(o_ref)

    @pl.when(i < nu)
    def _():
        changed = jnp.logical_or(i == 0, be_ref[i] != be_ref[jnp.maximum(i - 1, 0)])

        @pl.when(changed)
        def _():
            rows = 128
            for s in range(wgu_ref.shape[0] // rows):
                rs = slice(s * rows, (s + 1) * rows)
                for n in range(n_chunks):
                    wgu_s[n, rs, 0:fc] = wgu_ref[rs, n * fc:(n + 1) * fc].astype(BF16)
                    wgu_s[n, rs, fc:2 * fc] = wgu_ref[rs, D_FF + n * fc:D_FF + (n + 1) * fc].astype(BF16)
                wdn_s[rs, :] = wdn_ref[rs, :].astype(BF16)

        x = x_ref[...].astype(BF16)
        acts = []
        for n in range(n_chunks):
            gate = _dot(x, wgu_s[n, :, 0:fc]) + bgu_ref[:, n * fc:(n + 1) * fc]
            up = _dot(x, wgu_s[n, :, fc:2 * fc]) + bgu_ref[:, D_FF + n * fc:D_FF + (n + 1) * fc]
            gate = jnp.minimum(gate, SWIGLU_LIMIT)
            up = jnp.clip(up, -SWIGLU_LIMIT, SWIGLU_LIMIT)
            acts.append((gate * _sigmoid(SWIGLU_ALPHA * gate) * (up + 1.0)).astype(BF16))
        half = n_chunks // 2
        y = (_dot(jnp.concatenate(acts[:half], axis=1), wdn_s[0:half * fc, :])
             + _dot(jnp.concatenate(acts[half:], axis=1), wdn_s[half * fc:, :]) + bdn_ref[...])
        o_ref[...] = y


def _expert_call(block_e, n_used, xs, layer, w_gu, b_gu, w_dn, b_dn):
    rows, d = xs.shape
    tm = TM_EXPERT
    nblk = rows // tm
    depth, e, _, f2 = w_gu.shape
    n_chunks = EXPERT_CHUNKS
    live = lambda i, nu: jnp.minimum(i, nu[0] - 1)
    wmap = lambda i, be, nu: (layer, be[live(i, nu)], 0, 0)
    grid_spec = pltpu.PrefetchScalarGridSpec(
        num_scalar_prefetch=2,
        grid=(nblk,),
        in_specs=[
            pl.BlockSpec((tm, d), lambda i, be, nu: (live(i, nu), 0)),
            pl.BlockSpec((None, None, d, f2), wmap),
            pl.BlockSpec((None, None, 1, f2), wmap),
            pl.BlockSpec((None, None, f2 // 2, d), wmap),
            pl.BlockSpec((None, None, 1, d), wmap),
        ],
        out_specs=pl.BlockSpec((tm, d), lambda i, be, nu: (i, 0)),
        scratch_shapes=[pltpu.VMEM((n_chunks, d, f2 // n_chunks), BF16), pltpu.VMEM((f2 // 2, d), BF16)],
    )
    return pl.pallas_call(
        _expert_kernel,
        grid_spec=grid_spec,
        out_shape=jax.ShapeDtypeStruct((rows, d), F32),
        compiler_params=_cparams("arbitrary"),
    )(block_e, n_used, xs, w_gu, b_gu.reshape(depth, e, 1, f2), w_dn, b_dn.reshape(depth, e, 1, d))


def _rank_kernel(idx_ref, rank_ref, cnt_ref, base_ref):
    @pl.when(pl.program_id(0) == 0)
    def _():
        base_ref[...] = jnp.zeros_like(base_ref)

    tb = idx_ref.shape[1]
    row = lax.broadcasted_iota(jnp.int32, (tb, tb), 0)
    col = lax.broadcasted_iota(jnp.int32, (tb, tb), 1)
    earlier = jnp.where(row < col, 1.0, 0.0).astype(BF16)
    expert = lax.broadcasted_iota(jnp.int32, (N_EXPERTS, tb), 0)
    base = base_ref[...]
    ranks = []
    for kk in range(TOP_K):
        hit = expert == idx_ref[kk:kk + 1, :]
        onehot = hit.astype(F32)
        before = _dot(onehot.astype(BF16), earlier) + base
        ranks.append(jnp.sum(before * onehot, axis=0, keepdims=True))
        base = base + jnp.sum(onehot, axis=1, keepdims=True)
    rank_ref[...] = jnp.concatenate(ranks, axis=0).astype(jnp.int32)
    base_ref[...] = base
    cnt_ref[...] = base.astype(jnp.int32)


def _slot_kernel(idx_ref, rank_ref, start_ref, slot_ref):
    expert = lax.broadcasted_iota(jnp.int32, (N_EXPERTS, idx_ref.shape[1]), 0)
    rows = []
    for kk in range(TOP_K):
        start = jnp.sum(jnp.where(expert == idx_ref[kk:kk + 1, :], start_ref[...], 0), axis=0,
                        keepdims=True)
        rows.append(rank_ref[kk:kk + 1, :] + start)
    slot_ref[...] = jnp.concatenate(rows, axis=1)


def _row_copy(src, s, dst, t, sem):
    return pltpu.make_async_copy(src.at[pl.ds(s, 1), :], dst.at[pl.ds(t, 1), :], sem)


def _dispatch_kernel(pe_ref, nu_ref, slot_hbm, h_ref, xs_ref, slot_s, zero_ref, sem, zsem, ssem, *,
                     n_blocks):
    i = pl.program_id(0)
    tb = h_ref.shape[0]
    tm = zero_ref.shape[0]
    fetch = pltpu.make_async_copy(slot_hbm.at[i, 0], slot_s, ssem)
    fetch.start()

    def zero_copy(blk):
        return pltpu.make_async_copy(zero_ref, xs_ref.at[pl.ds(pl.multiple_of(blk * tm, tm), tm), :], zsem)

    @pl.when(i == 0)
    def _():
        zero_ref[...] = jnp.zeros_like(zero_ref)

        def per_expert(fn):
            for e in range(N_EXPERTS):
                end = pe_ref[e]
                start = pe_ref[e - 1] if e else 0

                @pl.when(end > start)
                def _():
                    fn(end // tm - 1)

        def per_tail(fn):
            def body(blk, carry):
                fn(blk)
                return carry
            lax.fori_loop(nu_ref[0], n_blocks, body, 0)

        per_expert(lambda blk: zero_copy(blk).start())
        per_tail(lambda blk: zero_copy(blk).start())
        per_expert(lambda blk: zero_copy(blk).wait())
        per_tail(lambda blk: zero_copy(blk).wait())

    fetch.wait()

    def issue(g, carry):
        t0 = pl.multiple_of(g * 8, 8)
        for j in range(8):
            for kk in range(TOP_K):
                _row_copy(h_ref, t0 + j, xs_ref, slot_s[kk * tb + t0 + j], sem).start(priority=kk % 2)
        return carry

    lax.fori_loop(0, tb // 8, issue, 0)
    for kk in range(TOP_K):
        pltpu.make_async_copy(h_ref, xs_ref.at[pl.ds(0, tb), :], sem).wait()


def _dispatch_call(pad_end, n_used, slot_blocks, h2, n_blocks):
    n_tok, d = h2.shape
    nblk, _, ktb = slot_blocks.shape
    tb = ktb // TOP_K
    tm = TM_EXPERT
    grid_spec = pltpu.PrefetchScalarGridSpec(
        num_scalar_prefetch=2,
        grid=(nblk,),
        in_specs=[pl.BlockSpec(memory_space=pl.ANY),
                  pl.BlockSpec((tb, d), lambda i, pe, nu: (i, 0))],
        out_specs=pl.BlockSpec(memory_space=pl.ANY),
        scratch_shapes=[pltpu.SMEM((ktb,), jnp.int32), pltpu.VMEM((tm, d), F32),
                        pltpu.SemaphoreType.DMA, pltpu.SemaphoreType.DMA, pltpu.SemaphoreType.DMA],
    )
    return pl.pallas_call(
        functools.partial(_dispatch_kernel, n_blocks=n_blocks),
        grid_spec=grid_spec,
        out_shape=jax.ShapeDtypeStruct((n_blocks * tm, d), F32),
        compiler_params=_cparams("arbitrary"),
    )(pad_end, n_used, slot_blocks, h2)


def _routing_tables(idx_t, n_tok):
    tm = TM_EXPERT
    tb = TM_ROWS
    assert n_tok % tb == 0
    blk = pl.BlockSpec((TOP_K, tb), lambda i: (0, i))
    rank_t, counts = pl.pallas_call(
        _rank_kernel,
        grid=(n_tok // tb,),
        in_specs=[blk],
        out_specs=[blk, pl.BlockSpec((N_EXPERTS, 1), lambda i: (0, 0))],
        out_shape=[jax.ShapeDtypeStruct((TOP_K, n_tok), jnp.int32),
                   jax.ShapeDtypeStruct((N_EXPERTS, 1), jnp.int32)],
        scratch_shapes=[pltpu.VMEM((N_EXPERTS, 1), F32)],
        compiler_params=_cparams("arbitrary"),
    )(idx_t)
    counts = counts[:, 0]
    padded = (counts + tm - 1) // tm * tm
    pad_end = jnp.cumsum(padded)
    pad_start = pad_end - padded
    slot_blocks = pl.pallas_call(
        _slot_kernel,
        grid=(n_tok // tb,),
        in_specs=[blk, blk, pl.BlockSpec((N_EXPERTS, 1), lambda i: (0, 0))],
        out_specs=pl.BlockSpec((None, 1, TOP_K * tb), lambda i: (i, 0, 0)),
        out_shape=jax.ShapeDtypeStruct((n_tok // tb, 1, TOP_K * tb), jnp.int32),
        compiler_params=_cparams("arbitrary"),
    )(idx_t, rank_t, pad_start[:, None])
    n_blocks = -(-TOP_K * n_tok // tm) + N_EXPERTS
    first_row = jnp.arange(n_blocks, dtype=jnp.int32) * tm
    block_e = jnp.minimum(jnp.sum(pad_end[None, :] <= first_row[:, None], axis=1),
                          N_EXPERTS - 1).astype(jnp.int32)
    n_used = (pad_end[-1:] // tm).astype(jnp.int32)
    return pad_end.astype(jnp.int32), block_e, n_used, slot_blocks, n_blocks


def _combine_kernel(slot_hbm, yb_hbm, p_ref, x_ref, gate_ref, fg_ref, o_ref, slot_s, ybuf, sem, ssem, *,
                    final_norm, blk0):
    i = pl.program_id(0)
    n = pl.num_programs(0)
    tb = x_ref.shape[0]

    def request(blk, buf):
        fetch = pltpu.make_async_copy(slot_hbm.at[blk0 + blk, 0], slot_s, ssem)
        fetch.start()
        fetch.wait()

        def issue(g, carry):
            t0 = pl.multiple_of(g * 8, 8)
            for j in range(8):
                for kk in range(TOP_K):
                    _row_copy(yb_hbm, slot_s[kk * tb + t0 + j], ybuf.at[buf, kk], t0 + j,
                              sem.at[buf]).start(priority=kk % 2)
            return carry

        lax.fori_loop(0, tb // 8, issue, 0)

    @pl.when(i == 0)
    def _():
        request(0, 0)

    @pl.when(i + 1 < n)
    def _():
        request(i + 1, (i + 1) % 2)

    cur = i % 2
    for kk in range(TOP_K):
        pltpu.make_async_copy(yb_hbm.at[pl.ds(0, tb), :], ybuf.at[cur, kk], sem.at[cur]).wait()
    f = jnp.zeros(x_ref.shape, F32)
    for kk in range(TOP_K):
        f = f + ybuf[cur, kk] * p_ref[:, kk:kk + 1]
    xn = x_ref[...] + gate_ref[...] * f
    if final_norm:
        xn = _rms(xn, fg_ref[...])
    o_ref[...] = xn


def _combine_call(slot_blocks, yb, probs, x, mods, group_of_block, final_g, tok0, final_norm):
    bsz, n, d = x.shape
    ktb = slot_blocks.shape[2]
    tb = ktb // TOP_K
    rows = bsz * n
    assert rows % tb == 0 and tok0 % tb == 0 and (n % tb == 0 or tb % n == 0)
    blk0 = tok0 // tb
    out = pl.pallas_call(
        functools.partial(_combine_kernel, final_norm=final_norm, blk0=blk0),
        grid=(rows // tb,),
        in_specs=[pl.BlockSpec(memory_space=pl.ANY),
                  pl.BlockSpec(memory_space=pl.ANY),
                  pl.BlockSpec((tb, TOP_K), lambda i: (blk0 + i, 0)),
                  pl.BlockSpec((tb, d), lambda i: (i, 0)),
                  pl.BlockSpec((None, None, 1, d), lambda i: (group_of_block(i), 5, 0, 0)),
                  pl.BlockSpec((1, d), lambda i: (0, 0))],
        out_specs=pl.BlockSpec((tb, d), lambda i: (i, 0)),
        out_shape=jax.ShapeDtypeStruct((rows, d), F32),
        scratch_shapes=[pltpu.SMEM((ktb,), jnp.int32), pltpu.VMEM((2, TOP_K, tb, d), F32),
                        pltpu.SemaphoreType.DMA((2,)), pltpu.SemaphoreType.DMA],
        compiler_params=_cparams("arbitrary"),
    )(slot_blocks, yb, probs, x.reshape(rows, d), mods, final_g.reshape(1, d))
    return out.reshape(bsz, n, d)


def kernel(x, c, ctx, c_ctx, norm1_g, norm2_g, w_mod, b_mod, w_in, gla_wg2_f, gla_bg_f, gla_wg2_b,
           gla_bg_b, gla_norm_g, conv_w, conv_b, conv_ln_g, conv_ln_b, w_out, router_w, router_b,
           exp_w_gu, exp_b_gu, exp_w_dn, exp_b_dn, final_norm_g):
    bsz, seq, d = x.shape
    n_ctx = ctx.shape[1]
    depth = w_mod.shape[0]
    assert d == D_MODEL and seq % (FFT_N2 * FFT_KB) == 0 and seq % n_ctx == 0
    lt = seq + n_ctx
    ctx_group = bsz

    rows = 8
    cvec = jnp.concatenate([c, c_ctx[None, :], jnp.zeros((rows - bsz - 1, d), F32)], axis=0)
    mods_all = _mod_call(cvec, w_mod, b_mod).reshape(depth, rows, 6, 1, d)

    def pack_w_in(w):
        o = [0, 256, 512, 768, 1280, 1792, 1808, 1824, 2336]
        parts = [w[:, o[0]:o[1]], w[:, o[1]:o[2]], w[:, o[2]:o[3]], w[:, o[3]:o[4]], w[:, o[4]:o[5]],
                 w[:, o[7]:o[7] + CONV_WIDTH], w[:, o[7] + CONV_WIDTH:o[8]], w[:, o[5]:o[7]],
                 jnp.zeros((d, GLR_PAD - 2 * GLA_GATE_RANK), w.dtype)]
        return jnp.concatenate(parts, axis=1).astype(BF16)

    long_tables = _dft_tables(seq)
    chan = _channel_tables()
    x_lat, x_ctx = x, ctx
    lat_group = lambda b: b
    ctx_group_fn = lambda b: ctx_group

    for layer in range(depth):
        last = layer == depth - 1
        mods = mods_all[layer]
        w_in_p = pack_w_in(w_in[layer])
        wg_pad = jnp.zeros((GLR_PAD, 2 * GLA_KEY_WIDTH), F32)
        wg_pad = wg_pad.at[:GLA_GATE_RANK, :GLA_KEY_WIDTH].set(gla_wg2_f[layer])
        wg_pad = wg_pad.at[GLA_GATE_RANK:2 * GLA_GATE_RANK, GLA_KEY_WIDTH:].set(gla_wg2_b[layer])
        bg_cat = jnp.concatenate([gla_bg_f[layer], gla_bg_b[layer]])[None, :]
        w_out_b = w_out[layer].astype(BF16)
        gn_tiled = jnp.tile(gla_norm_g[layer], GLA_HEADS)[None, :]
        rw_t = router_w[layer].T
        rw_hi = rw_t.astype(BF16)
        rw_lo = (rw_t - rw_hi.astype(F32)).astype(BF16)
        rb = router_b[layer][:, None]

        (u_l, r_l, glu_l), comb = _inproj_call(x_lat, mods, lat_group, norm1_g[layer], w_in_p, lt, 0)
        (u_c, r_c, glu_c), comb = _inproj_call(x_ctx, mods, ctx_group_fn, norm1_g[layer], w_in_p, lt,
                                               seq, combined=comb)
        o_f, o_b = _gla_call(*comb, wg_pad, bg_cat, seq, n_ctx)
        yf_l = _fourier_long(u_l, long_tables, chan)
        cv_l = _conv_call(glu_l, seq // (seq // GRID_W), conv_w[layer], conv_b[layer],
                          conv_ln_g[layer], conv_ln_b[layer])
        n_tok = bsz * seq + (0 if last else bsz * n_ctx)
        x_lat, routed = _outproj_call(yf_l, o_f, o_b, 0, r_l, cv_l, x_lat, mods, lat_group, gn_tiled,
                                      w_out_b, norm2_g[layer], rw_hi, rw_lo, rb, n_tok, 0)
        if not last:
            yf_c = _fourier_short(u_c, chan)
            cv_c = _conv_call(glu_c, n_ctx, conv_w[layer], conv_b[layer], conv_ln_g[layer],
                              conv_ln_b[layer])
            x_ctx, routed = _outproj_call(yf_c, o_f, o_b, seq, r_c, cv_c, x_ctx, mods, ctx_group_fn,
                                          gn_tiled, w_out_b, norm2_g[layer], rw_hi, rw_lo, rb, n_tok,
                                          bsz * seq, carried=routed)

        h2, idx_t, prob_t = routed
        pad_end, block_e, n_used, slot_blocks, n_blocks = _routing_tables(idx_t, n_tok)
        xs = _dispatch_call(pad_end, n_used, slot_blocks, h2, n_blocks)
        yb = _expert_call(block_e, n_used, xs, layer, exp_w_gu, exp_b_gu, exp_w_dn, exp_b_dn)
        probs = prob_t.T
        lat_blocks = seq // (slot_blocks.shape[2] // TOP_K)
        x_lat = _combine_call(slot_blocks, yb, probs, x_lat, mods, lambda i: i // lat_blocks,
                              final_norm_g, 0, last)
        if not last:
            x_ctx = _combine_call(slot_blocks, yb, probs, x_ctx, mods, lambda i: ctx_group,
                                  final_norm_g, bsz * seq, False)

    return x_lat
```

```python
import functools

import jax
import jax.numpy as jnp
from jax import lax
from jax.experimental import pallas as pl
from jax.experimental.pallas import tpu as pltpu

F32 = jnp.float32
BF16 = jnp.bfloat16
U32 = jnp.uint32
HIGH_HALF = 0xFFFF0000

D_MODEL = 1024
DEPTH = 2
GRID_W = 64
FOURIER_WIDTH = 256
FOURIER_HEADS = 4
FOURIER_HEAD_DIM = FOURIER_WIDTH // FOURIER_HEADS
GLA_HEADS = 4
GLA_KEY_WIDTH = 256
GLA_VALUE_WIDTH = 512
GLA_DK = GLA_KEY_WIDTH // GLA_HEADS
GLA_DV = GLA_VALUE_WIDTH // GLA_HEADS
GLA_GATE_RANK = 16
GLA_GATE_NORMALIZER = 16.0
CONV_WIDTH = 256
CONV_KERNEL = 31
N_EXPERTS = 32
TOP_K = 4
D_FF = D_MODEL
SWIGLU_LIMIT = 7.0
SWIGLU_ALPHA = 1.702
NORM_EPS = 1e-6

LANES = 128
VMEM_LIMIT = 56 * 1024 * 1024

COL_U = 0
COL_Q = COL_U + FOURIER_WIDTH
COL_K = COL_Q + GLA_KEY_WIDTH
COL_V = COL_K + GLA_KEY_WIDTH
COL_R = COL_V + GLA_VALUE_WIDTH
COL_CA = COL_R + GLA_VALUE_WIDTH
COL_CG = COL_CA + CONV_WIDTH
COL_GL = COL_CG + CONV_WIDTH
GLR_PAD = LANES
IN_PAD = COL_GL + GLR_PAD

GLA_CHUNK = 128
GLA_STEP_CHUNKS = 2
TM_ROWS = 512
TM_EXPERT = 512
EXPERT_CHUNKS = 4
FFT_N2 = 128
FFT_KB = 8
CONV_HALO = 16


def _cparams(*sem):
    return pltpu.CompilerParams(dimension_semantics=sem, vmem_limit_bytes=VMEM_LIMIT)


def _dot(a, b):
    return jnp.dot(a, b, preferred_element_type=F32)


def _dot_nt(a, b):
    return lax.dot_general(a, b, (((1,), (1,)), ((), ())), preferred_element_type=F32)


def _dot_tn(a, b):
    return lax.dot_general(a, b, (((0,), (0,)), ((), ())), preferred_element_type=F32)


def _split(a):
    hi = a.astype(BF16)
    lo = (a - hi.astype(F32)).astype(BF16)
    return hi, lo


def _dot3(a, b):
    ah, al = _split(a)
    bh, bl = _split(b)
    return _dot(ah, bh) + _dot(ah, bl) + _dot(al, bh)


def _sigmoid(x):
    return 1.0 / (1.0 + jnp.exp(-x))


def _pack_bf16_pairs(a):
    h = a.shape[1] // 2
    bits = lax.bitcast_convert_type(a.astype(BF16).astype(F32), U32)
    return (bits[:, :h] >> 16) | (bits[:, h:] & U32(HIGH_HALF))


def _unpack_bf16_pairs(w):
    return (lax.bitcast_convert_type(w << 16, F32), lax.bitcast_convert_type(w & U32(HIGH_HALF), F32))


def _rms(x, g):
    ms = jnp.mean(x * x, axis=-1, keepdims=True)
    return x * lax.rsqrt(ms + NORM_EPS) * g


def _mod_kernel(cv_ref, w_ref, b_ref, o_ref):
    cv = cv_ref[...]
    a = cv * _sigmoid(cv)
    o_ref[...] = _dot3(a, w_ref[...]) + b_ref[...]


def _mod_call(cvec, w_mod, b_mod):
    depth, d, n = w_mod.shape
    rows = cvec.shape[0]
    tn = 1536
    return pl.pallas_call(
        _mod_kernel,
        grid=(depth, n // tn),
        in_specs=[
            pl.BlockSpec((rows, d), lambda l, j: (0, 0)),
            pl.BlockSpec((None, d, tn), lambda l, j: (l, 0, j)),
            pl.BlockSpec((None, 1, tn), lambda l, j: (l, 0, j)),
        ],
        out_specs=pl.BlockSpec((None, rows, tn), lambda l, j: (l, 0, j)),
        out_shape=jax.ShapeDtypeStruct((depth, rows, n), F32),
        compiler_params=_cparams("arbitrary", "arbitrary"),
    )(cvec, w_mod, b_mod.reshape(depth, 1, n))


def _inproj_kernel(x_ref, g_ref, sh_ref, sc_ref, w_ref, *rest):
    u_ref, r_ref, glu_ref, q_ref, k_ref, v_ref, gl_ref = rest[-7:]
    x = x_ref[...]
    h = _rms(x, g_ref[...]) * (1.0 + sc_ref[...]) + sh_ref[...]
    p = _dot(h.astype(BF16), w_ref[...])
    u_ref[...] = p[:, COL_U:COL_Q].astype(u_ref.dtype)
    q_ref[...] = (p[:, COL_Q:COL_K] * (GLA_DK ** -0.5)).astype(q_ref.dtype)
    k_ref[...] = p[:, COL_K:COL_V].astype(k_ref.dtype)
    v_ref[...] = p[:, COL_V:COL_R].astype(v_ref.dtype)
    r_ref[...] = p[:, COL_R:COL_CA].astype(r_ref.dtype)
    glu_ref[...] = (p[:, COL_CA:COL_CG] * _sigmoid(p[:, COL_CG:COL_GL])).astype(glu_ref.dtype)
    gl_ref[...] = p[:, COL_GL:IN_PAD]


def _inproj_call(x, mods, group_of_batch, norm_g, w_in_p, lt, row0, combined=None):
    bsz, n, d = x.shape
    tm = min(TM_ROWS, n)
    assert n % tm == 0 and row0 % tm == 0
    blk0 = row0 // tm
    nb = n // tm
    steps = nb + (-(-(lt - n) // tm) if combined is None else 0)
    widths = (GLA_KEY_WIDTH, GLA_KEY_WIDTH, GLA_VALUE_WIDTH, GLR_PAD)
    dtypes = (BF16, BF16, BF16, F32)
    row_spec = lambda w: pl.BlockSpec((None, tm, w), lambda b, i: (b, jnp.minimum(i, nb - 1), 0))
    comb_spec = lambda w: pl.BlockSpec((None, tm, w), lambda b, i: (b, blk0 + i, 0))
    mod_spec = lambda which: pl.BlockSpec(
        (None, None, 1, d), lambda b, i: (group_of_batch(b), which, 0, 0))
    in_specs = [
        row_spec(d),
        pl.BlockSpec((1, d), lambda b, i: (0, 0)),
        mod_spec(0), mod_spec(1),
        pl.BlockSpec((d, IN_PAD), lambda b, i: (0, 0)),
    ]
    args = [x, norm_g.reshape(1, d), mods, mods, w_in_p]
    aliases = {}
    if combined is not None:
        for t, arr in enumerate(combined):
            in_specs.append(pl.BlockSpec(memory_space=pl.ANY))
            aliases[len(args)] = 3 + t
            args.append(arr)
    out_shape = [
        jax.ShapeDtypeStruct((bsz, n, FOURIER_WIDTH), BF16),
        jax.ShapeDtypeStruct((bsz, n, GLA_VALUE_WIDTH), BF16),
        jax.ShapeDtypeStruct((bsz, n, CONV_WIDTH), BF16),
    ] + [jax.ShapeDtypeStruct((bsz, lt, w), dt) for w, dt in zip(widths, dtypes)]
    out_specs = [row_spec(FOURIER_WIDTH), row_spec(GLA_VALUE_WIDTH), row_spec(CONV_WIDTH)] + [
        comb_spec(w) for w in widths]
    outs = pl.pallas_call(
        _inproj_kernel,
        grid=(bsz, steps),
        in_specs=in_specs,
        out_specs=out_specs,
        out_shape=out_shape,
        input_output_aliases=aliases,
        compiler_params=_cparams("arbitrary", "arbitrary"),
    )(*args)
    return outs[:3], outs[3:]


def _dft_tables(length):
    n2 = FFT_N2
    n1 = length // n2
    two_pi = 2.0 * jnp.pi

    def cs(num, den):
        ang = (num % den).astype(F32) * (two_pi / den)
        return jnp.cos(ang), jnp.sin(ang)

    k1 = jnp.arange(n1, dtype=jnp.int32)
    c1, s1 = cs(k1[:, None] * k1[None, :], n1)
    stage1 = (jnp.concatenate([c1, -s1], axis=0) * (n1 ** -0.5)).astype(BF16)
    k2 = jnp.arange(n2, dtype=jnp.int32)
    ct, st = cs(k1[:, None] * k2[None, :], length)
    cf, sf = cs(k2[:, None] * k2[None, :], n2)
    scale = n2 ** -0.5
    mr = (ct[:, None, :] * cf[None] - st[:, None, :] * sf[None]) * scale
    mi = -(st[:, None, :] * cf[None] + ct[:, None, :] * sf[None]) * scale
    stage2 = jnp.concatenate([jnp.concatenate([mr, -mi], axis=2),
                              jnp.concatenate([mi, mr], axis=2)], axis=1).astype(BF16)
    return stage1, stage2


def _channel_tables():
    hd = FOURIER_HEAD_DIM
    c = jnp.arange(FOURIER_WIDTH, dtype=jnp.int32)
    same_head = (c[:, None] // hd) == (c[None, :] // hd)
    ang = (((c[:, None] % hd) * (c[None, :] % hd)) % hd).astype(F32) * (2.0 * jnp.pi / hd)
    scale = hd ** -0.5
    bdc = jnp.where(same_head, jnp.cos(ang) * scale, 0.0).astype(BF16)
    bds = jnp.where(same_head, jnp.sin(ang) * scale, 0.0).astype(BF16)
    return bdc, bds


def _fft1_kernel(x_ref, cs_ref, zr_ref, zi_ref):
    n1 = x_ref.shape[0]
    z = _dot(cs_ref[...], x_ref[...])
    zr_ref[...] = z[:n1].astype(zr_ref.dtype)
    zi_ref[...] = z[n1:].astype(zi_ref.dtype)


def _fft2_kernel(zr_ref, zi_ref, m_ref, bdc_ref, bds_ref, o_ref):
    kb, n2, w = zr_ref.shape
    for j in range(kb):
        z = jnp.concatenate([zr_ref[j], zi_ref[j]], axis=0)
        a = _dot(m_ref[j], z)
        y = _dot(a[:n2].astype(BF16), bdc_ref[...]) + _dot(a[n2:].astype(BF16), bds_ref[...])
        o_ref[:, j * w:(j + 1) * w] = y.astype(o_ref.dtype)


def _fourier_long(u, tables, chan):
    bsz, length, w = u.shape
    stage1, stage2 = tables
    bdc, bds = chan
    n2 = FFT_N2
    n1 = length // n2
    tn = 4096
    cols = n2 * w
    zr, zi = pl.pallas_call(
        _fft1_kernel,
        grid=(bsz, cols // tn),
        in_specs=[pl.BlockSpec((None, n1, tn), lambda b, j: (b, 0, j)),
                  pl.BlockSpec((2 * n1, n1), lambda b, j: (0, 0))],
        out_specs=[pl.BlockSpec((None, n1, tn), lambda b, j: (b, 0, j))] * 2,
        out_shape=[jax.ShapeDtypeStruct((bsz, n1, cols), BF16)] * 2,
        compiler_params=_cparams("arbitrary", "arbitrary"),
    )(u.reshape(bsz, n1, cols), stage1)
    kb = FFT_KB
    z_spec = pl.BlockSpec((None, kb, n2, w), lambda b, j: (b, j, 0, 0))
    y = pl.pallas_call(
        _fft2_kernel,
        grid=(bsz, n1 // kb),
        in_specs=[z_spec, z_spec,
                  pl.BlockSpec((kb, 2 * n2, 2 * n2), lambda b, j: (j, 0, 0)),
                  pl.BlockSpec((w, w), lambda b, j: (0, 0)),
                  pl.BlockSpec((w, w), lambda b, j: (0, 0))],
        out_specs=pl.BlockSpec((None, n2, kb * w), lambda b, j: (b, 0, j)),
        out_shape=jax.ShapeDtypeStruct((bsz, n2, n1 * w), BF16),
        compiler_params=_cparams("arbitrary", "arbitrary"),
    )(zr.reshape(bsz, n1, n2, w), zi.reshape(bsz, n1, n2, w), stage2, bdc, bds)
    return y.reshape(bsz, length, w)


def _dft_short_kernel(u_ref, c_ref, s_ref, bdc_ref, bds_ref, o_ref):
    u = u_ref[...]
    p = _dot(u, bdc_ref[...]).astype(BF16)
    q = _dot(u, bds_ref[...]).astype(BF16)
    o_ref[...] = (_dot(c_ref[...], p) - _dot(s_ref[...], q)).astype(o_ref.dtype)


def _fourier_short(u, chan):
    bsz, length, w = u.shape
    bdc, bds = chan
    k = jnp.arange(length, dtype=jnp.int32)
    ang = ((k[:, None] * k[None, :]) % length).astype(F32) * (2.0 * jnp.pi / length)
    c = (jnp.cos(ang) * length ** -0.5).astype(BF16)
    s = (jnp.sin(ang) * length ** -0.5).astype(BF16)
    full = lambda n: pl.BlockSpec((n, n), lambda b: (0, 0))
    return pl.pallas_call(
        _dft_short_kernel,
        grid=(bsz,),
        in_specs=[pl.BlockSpec((None, length, w), lambda b: (b, 0, 0)),
                  full(length), full(length), full(w), full(w)],
        out_specs=pl.BlockSpec((None, length, w), lambda b: (b, 0, 0)),
        out_shape=jax.ShapeDtypeStruct((bsz, length, w), BF16),
        compiler_params=_cparams("arbitrary"),
    )(u, c, s, bdc, bds)


def _conv_kernel(x_ref, w_ref, cb_ref, lg_ref, lb_ref, o_ref, pad_ref, *, seg):
    nseg = x_ref.shape[0] // seg
    width = x_ref.shape[1]
    halo = jnp.zeros((CONV_HALO, width), F32)
    for s in range(nseg):
        pad_ref[s, 0:CONV_HALO, :] = halo
        pad_ref[s, CONV_HALO:CONV_HALO + seg, :] = x_ref[s * seg:(s + 1) * seg, :].astype(F32)
        pad_ref[s, CONV_HALO + seg:2 * CONV_HALO + seg, :] = halo
    first = CONV_HALO - CONV_KERNEL // 2
    sub = 8
    span = seg + 2 * CONV_HALO - sub
    for s in range(nseg):
        acc = jnp.zeros((seg, width), F32)
        for r in range(sub):
            shifted = pad_ref[s, r:r + span, :]
            for a in range((span - seg) // sub + 1):
                j = a * sub + r - first
                if 0 <= j < CONV_KERNEL:
                    acc = acc + shifted[a * sub:a * sub + seg, :] * w_ref[j:j + 1, :]
        y = acc + cb_ref[...]
        mu = jnp.mean(y, axis=-1, keepdims=True)
        yc = y - mu
        var = jnp.mean(yc * yc, axis=-1, keepdims=True)
        z = yc * lax.rsqrt(var + NORM_EPS) * lg_ref[...] + lb_ref[...]
        o_ref[s * seg:(s + 1) * seg, :] = (z * _sigmoid(z)).astype(o_ref.dtype)


def _conv_call(glu, seg, conv_w, conv_b, ln_g, ln_b):
    bsz, n, w = glu.shape
    t = max(seg, min(TM_ROWS, n))
    assert n % t == 0 and t % seg == 0
    vec = lambda: pl.BlockSpec((1, w), lambda b, i: (0, 0))
    return pl.pallas_call(
        functools.partial(_conv_kernel, seg=seg),
        grid=(bsz, n // t),
        in_specs=[pl.BlockSpec((None, t, w), lambda b, i: (b, i, 0)),
                  pl.BlockSpec((CONV_KERNEL, w), lambda b, i: (0, 0)),
                  vec(), vec(), vec()],
        out_specs=pl.BlockSpec((None, t, w), lambda b, i: (b, i, 0)),
        out_shape=jax.ShapeDtypeStruct((bsz, n, w), BF16),
        scratch_shapes=[pltpu.VMEM((t // seg, seg + 2 * CONV_HALO, w), F32)],
        compiler_params=_cparams("arbitrary", "arbitrary"),
    )(glu, conv_w, conv_b.reshape(1, w), ln_g.reshape(1, w), ln_b.reshape(1, w))


def _gla_direction(q_ref, k_ref, v_ref, gl_ref, wg_ref, bg_ref, o_ref, st_ref, reverse):
    c = GLA_CHUNK
    n_sub = q_ref.shape[0] // c
    st = st_ref[...]
    for s in (reversed(range(n_sub)) if reverse else range(n_sub)):
        rows = slice(s * c, (s + 1) * c)
        st = _gla_chunk(q_ref[rows, :], k_ref[rows, :], v_ref[rows, :], gl_ref[rows, :], wg_ref, bg_ref,
                        o_ref.at[rows, :], st, reverse)
    st_ref[...] = st


def _gla_chunk(q, k, v, glr, wg_ref, bg_ref, o_ref, st, reverse):
    c = GLA_CHUNK
    kw = GLA_KEY_WIDTH
    col0 = kw if reverse else 0
    pre = _dot3(glr, wg_ref[:, col0:col0 + kw]) + bg_ref[:, col0:col0 + kw]
    g = (jnp.minimum(pre, 0.0) - jnp.log(1.0 + jnp.exp(-jnp.abs(pre)))) * (1.0 / GLA_GATE_NORMALIZER)
    row = lax.broadcasted_iota(jnp.int32, (c, c), 0)
    col = lax.broadcasted_iota(jnp.int32, (c, c), 1)
    seen = (col >= row) if reverse else (col <= row)
    tri = jnp.where(seen, 1.0, 0.0).astype(BF16)
    gh, gl = _split(g)
    b = _dot(tri, gh) + _dot(tri, gl)
    mid = c // 2 if reverse else c // 2 - 1
    last = 0 if reverse else c - 1
    b_mid = b[mid:mid + 1, :]
    b_last = b[last:last + 1, :]
    q = q.astype(F32)
    k = k.astype(F32)
    qe = q * jnp.exp(b - b_mid)
    ke = k * jnp.exp(b_mid - b)
    kd = k * jnp.exp(b_last - b)
    head_of_lane = lax.broadcasted_iota(jnp.int32, (1, kw), 1) // GLA_DK
    q_heads = jnp.concatenate(
        [jnp.where(head_of_lane == h, qe, 0.0) for h in range(GLA_HEADS)], axis=0).astype(BF16)
    rhs = jnp.concatenate([ke, st * jnp.exp(b_mid)], axis=0).astype(BF16)
    res = _dot_nt(q_heads, rhs)
    outs = []
    for h in range(GLA_HEADS):
        blk = res[h * c:(h + 1) * c, :]
        scores = jnp.where(seen, blk[:, :c], 0.0).astype(BF16)
        outs.append(_dot(scores, v[:, h * GLA_DV:(h + 1) * GLA_DV]) + blk[:, c:])
    o_ref[...] = jnp.concatenate(outs, axis=1).astype(o_ref.dtype)
    kv = _dot_tn(v, kd.astype(BF16))
    ds = jnp.zeros_like(st)
    for h in range(GLA_HEADS):
        ds = ds + jnp.where(head_of_lane == h, kv[h * GLA_DV:(h + 1) * GLA_DV, :], 0.0)
    return st * jnp.exp(b_last) + ds


def _gla_kernel(qf, kf, vf, gf, qb, kb, vb, gb, wg_ref, bg_ref, of_ref, ob_ref, sf_ref, sb_ref):
    @pl.when(pl.program_id(1) == 0)
    def _():
        sf_ref[...] = jnp.zeros_like(sf_ref)
        sb_ref[...] = jnp.zeros_like(sb_ref)

    _gla_direction(qf, kf, vf, gf, wg_ref, bg_ref, of_ref, sf_ref, False)
    _gla_direction(qb, kb, vb, gb, wg_ref, bg_ref, ob_ref, sb_ref, True)


def _gla_call(q, k, v, gl, wg_pad, bg_cat, n_lat, n_ctx):
    bsz, lt, _ = q.shape
    c = GLA_STEP_CHUNKS * GLA_CHUNK
    assert n_lat % c == 0 and n_ctx % c == 0
    cl, cc = n_lat // c, n_ctx // c

    def fwd_blk(j):
        return jnp.where(j < cc, cl + j, j - cc)

    def bwd_blk(j):
        return jnp.where(j < cc, cl + cc - 1 - j, cl - 1 - (j - cc))

    def spec(w, blk):
        return pl.BlockSpec((None, c, w), lambda b, j: (b, blk(j), 0))

    widths = (GLA_KEY_WIDTH, GLA_KEY_WIDTH, GLA_VALUE_WIDTH, GLR_PAD)
    in_specs = [spec(w, fwd_blk) for w in widths] + [spec(w, bwd_blk) for w in widths] + [
        pl.BlockSpec(wg_pad.shape, lambda b, j: (0, 0)),
        pl.BlockSpec(bg_cat.shape, lambda b, j: (0, 0))]
    return pl.pallas_call(
        _gla_kernel,
        grid=(bsz, cl + cc),
        in_specs=in_specs,
        out_specs=[spec(GLA_VALUE_WIDTH, fwd_blk), spec(GLA_VALUE_WIDTH, bwd_blk)],
        out_shape=[jax.ShapeDtypeStruct((bsz, lt, GLA_VALUE_WIDTH), F32)] * 2,
        scratch_shapes=[pltpu.VMEM((GLA_DV, GLA_KEY_WIDTH), F32)] * 2,
        compiler_params=_cparams("arbitrary", "arbitrary"),
    )(q, k, v, gl, q, k, v, gl, wg_pad, bg_cat)


def _outproj_kernel(yf_ref, of_ref, ob_ref, r_ref, cv_ref, x_ref, gate_ref, sh_ref, sc_ref,
                    gn_ref, wo_ref, n2_ref, rwh_ref, rwl_ref, rb_ref, *rest):
    xo_ref, h2_ref, idx_ref, prob_ref = rest[-4:]
    o = of_ref[...] + ob_ref[...]
    heads = []
    for h in range(GLA_HEADS):
        oh = o[:, h * GLA_DV:(h + 1) * GLA_DV]
        heads.append(oh * lax.rsqrt(jnp.mean(oh * oh, axis=-1, keepdims=True) + NORM_EPS))
    r = r_ref[...].astype(F32)
    gla = jnp.concatenate(heads, axis=1) * gn_ref[...] * (r * _sigmoid(r))
    c0, c1 = FOURIER_WIDTH, FOURIER_WIDTH + GLA_VALUE_WIDTH
    y = (_dot(yf_ref[...], wo_ref[0:c0, :]) + _dot(gla.astype(BF16), wo_ref[c0:c1, :])
         + _dot(cv_ref[...], wo_ref[c1:, :]))
    xn = x_ref[...] + gate_ref[...] * y
    xo_ref[...] = xn
    h2 = _rms(xn, n2_ref[...]) * (1.0 + sc_ref[...]) + sh_ref[...]
    hh, hl = _split(h2)
    h2_ref[...] = _pack_bf16_pairs(h2)
    logits = (_dot_nt(rwh_ref[...], hh) + _dot_nt(rwh_ref[...], hl) + _dot_nt(rwl_ref[...], hh)
              + rb_ref[...])
    expert = lax.broadcasted_iota(jnp.int32, logits.shape, 0)
    vals, idxs = [], []
    cur = logits
    for _ in range(TOP_K):
        m = jnp.max(cur, axis=0, keepdims=True)
        ix = jnp.min(jnp.where(cur == m, expert, N_EXPERTS), axis=0, keepdims=True)
        vals.append(m)
        idxs.append(ix)
        cur = jnp.where(expert == ix, -jnp.inf, cur)
    es = [jnp.exp(vv - vals[0]) for vv in vals]
    inv = 1.0 / functools.reduce(lambda a, b: a + b, es)
    idx_ref[...] = jnp.concatenate(idxs, axis=0)
    prob_ref[...] = jnp.concatenate([e * inv for e in es], axis=0)


def _outproj_call(yf, o_f, o_b, o_row0, r, cv, x, mods, group_of_batch, gn_tiled, w_out, norm2_g,
                  rw_hi, rw_lo, rb, n_tok, tok0, carried=None):
    bsz, n, d = x.shape
    tm = min(TM_ROWS, n)
    assert n % tm == 0 and o_row0 % tm == 0 and tok0 % tm == 0
    nb = n // tm
    spare = (n_tok - bsz * n) if carried is None else 0
    assert spare in (0, tm)
    steps = nb + spare // tm
    last = lambda i: jnp.minimum(i, nb - 1)
    row = lambda w: pl.BlockSpec((None, tm, w), lambda b, i: (b, last(i), 0))
    orow = pl.BlockSpec((None, tm, GLA_VALUE_WIDTH), lambda b, i: (b, o_row0 // tm + last(i), 0))
    mod = lambda which: pl.BlockSpec((None, None, 1, d), lambda b, i: (group_of_batch(b), which, 0, 0))
    const = lambda a: pl.BlockSpec(a.shape, lambda b, i: (0,) * a.ndim)
    consts = [gn_tiled, w_out, norm2_g.reshape(1, d), rw_hi, rw_lo, rb]
    in_specs = [row(FOURIER_WIDTH), orow, orow, row(GLA_VALUE_WIDTH), row(CONV_WIDTH), row(d),
                mod(2), mod(3), mod(4)] + [const(a) for a in consts]
    args = [yf, o_f, o_b, r, cv, x, mods, mods, mods] + consts
    aliases = {}
    if carried is not None:
        for t, arr in enumerate(carried):
            in_specs.append(pl.BlockSpec(memory_space=pl.ANY))
            aliases[len(args)] = 1 + t
            args.append(arr)
    tokblk = lambda b, i: tok0 // tm + jnp.where(
        jnp.logical_and(i == nb, b == bsz - 1), bsz * nb, b * nb + last(i))
    out_specs = [row(d),
                 pl.BlockSpec((tm, d // 2), lambda b, i: (tokblk(b, i), 0)),
                 pl.BlockSpec((TOP_K, tm), lambda b, i: (0, tokblk(b, i))),
                 pl.BlockSpec((TOP_K, tm), lambda b, i: (0, tokblk(b, i)))]
    out_shape = [jax.ShapeDtypeStruct((bsz, n, d), F32),
                 jax.ShapeDtypeStruct((n_tok, d // 2), U32),
                 jax.ShapeDtypeStruct((TOP_K, n_tok), jnp.int32),
                 jax.ShapeDtypeStruct((TOP_K, n_tok), F32)]
    outs = pl.pallas_call(
        _outproj_kernel,
        grid=(bsz, steps),
        in_specs=in_specs,
        out_specs=out_specs,
        out_shape=out_shape,
        input_output_aliases=aliases,
        compiler_params=_cparams("arbitrary", "arbitrary"),
    )(*args)
    return outs[0], outs[1:]


def _expert_kernel(be_ref, nu_ref, x_ref, wgu_ref, bgu_ref, wdn_ref, bdn_ref, o_ref, wgu_s, wdn_s):
    i = pl.program_id(0)
    nu = nu_ref[0]
    n_chunks = wgu_s.shape[0]
    fc = D_FF // n_chunks

    @pl.when(i >= nu)
    def _():
        o_ref[...] = jnp.zeros_like(o_ref)

    @pl.when(i < nu)
    def _():
        changed = jnp.logical_or(i == 0, be_ref[i] != be_ref[jnp.maximum(i - 1, 0)])

        @pl.when(changed)
        def _():
            rows = 128
            for s in range(wgu_ref.shape[0] // rows):
                rs = slice(s * rows, (s + 1) * rows)
                for n in range(n_chunks):
                    wgu_s[n, rs, 0:fc] = wgu_ref[rs, n * fc:(n + 1) * fc].astype(BF16)
                    wgu_s[n, rs, fc:2 * fc] = wgu_ref[rs, D_FF + n * fc:D_FF + (n + 1) * fc].astype(BF16)
                wdn_s[rs, :] = wdn_ref[rs, :].astype(BF16)

        x = jnp.concatenate(_unpack_bf16_pairs(x_ref[...]), axis=1).astype(BF16)
        acts = []
        for n in range(n_chunks):
            gate = _dot(x, wgu_s[n, :, 0:fc]) + bgu_ref[:, n * fc:(n + 1) * fc]
            up = _dot(x, wgu_s[n, :, fc:2 * fc]) + bgu_ref[:, D_FF + n * fc:D_FF + (n + 1) * fc]
            gate = jnp.minimum(gate, SWIGLU_LIMIT)
            up = jnp.clip(up, -SWIGLU_LIMIT, SWIGLU_LIMIT)
            acts.append((gate * _sigmoid(SWIGLU_ALPHA * gate) * (up + 1.0)).astype(BF16))
        half = n_chunks // 2
        y = (_dot(jnp.concatenate(acts[:half], axis=1), wdn_s[0:half * fc, :])
             + _dot(jnp.concatenate(acts[half:], axis=1), wdn_s[half * fc:, :]) + bdn_ref[...])
        o_ref[...] = _pack_bf16_pairs(y)


def _expert_call(block_e, n_used, xs, layer, w_gu, b_gu, w_dn, b_dn):
    rows, dh = xs.shape
    d = 2 * dh
    tm = TM_EXPERT
    nblk = rows // tm
    depth, e, _, f2 = w_gu.shape
    n_chunks = EXPERT_CHUNKS
    live = lambda i, nu: jnp.minimum(i, nu[0] - 1)
    wmap = lambda i, be, nu: (layer, be[live(i, nu)], 0, 0)
    grid_spec = pltpu.PrefetchScalarGridSpec(
        num_scalar_prefetch=2,
        grid=(nblk,),
        in_specs=[
            pl.BlockSpec((tm, dh), lambda i, be, nu: (live(i, nu), 0)),
            pl.BlockSpec((None, None, d, f2), wmap),
            pl.BlockSpec((None, None, 1, f2), wmap),
            pl.BlockSpec((None, None, f2 // 2, d), wmap),
            pl.BlockSpec((None, None, 1, d), wmap),
        ],
        out_specs=pl.BlockSpec((tm, dh), lambda i, be, nu: (i, 0)),
        scratch_shapes=[pltpu.VMEM((n_chunks, d, f2 // n_chunks), BF16), pltpu.VMEM((f2 // 2, d), BF16)],
    )
    return pl.pallas_call(
        _expert_kernel,
        grid_spec=grid_spec,
        out_shape=jax.ShapeDtypeStruct((rows, dh), xs.dtype),
        compiler_params=_cparams("arbitrary"),
    )(block_e, n_used, xs, w_gu, b_gu.reshape(depth, e, 1, f2), w_dn, b_dn.reshape(depth, e, 1, d))


def _rank_kernel(idx_ref, rank_ref, cnt_ref, base_ref):
    @pl.when(pl.program_id(0) == 0)
    def _():
        base_ref[...] = jnp.zeros_like(base_ref)

    tb = idx_ref.shape[1]
    row = lax.broadcasted_iota(jnp.int32, (tb, tb), 0)
    col = lax.broadcasted_iota(jnp.int32, (tb, tb), 1)
    earlier = jnp.where(row < col, 1.0, 0.0).astype(BF16)
    expert = lax.broadcasted_iota(jnp.int32, (N_EXPERTS, tb), 0)
    base = base_ref[...]
    ranks = []
    for kk in range(TOP_K):
        hit = expert == idx_ref[kk:kk + 1, :]
        onehot = hit.astype(F32)
        before = _dot(onehot.astype(BF16), earlier) + base
        ranks.append(jnp.sum(before * onehot, axis=0, keepdims=True))
        base = base + jnp.sum(onehot, axis=1, keepdims=True)
    rank_ref[...] = jnp.concatenate(ranks, axis=0).astype(jnp.int32)
    base_ref[...] = base
    cnt_ref[...] = base.astype(jnp.int32)


def _slot_kernel(idx_ref, rank_ref, start_ref, slot_ref):
    expert = lax.broadcasted_iota(jnp.int32, (N_EXPERTS, idx_ref.shape[1]), 0)
    rows = []
    for kk in range(TOP_K):
        start = jnp.sum(jnp.where(expert == idx_ref[kk:kk + 1, :], start_ref[...], 0), axis=0,
                        keepdims=True)
        rows.append(rank_ref[kk:kk + 1, :] + start)
    slot_ref[...] = jnp.concatenate(rows, axis=1)


def _row_copy(src, s, dst, t, sem):
    return pltpu.make_async_copy(src.at[pl.ds(s, 1), :], dst.at[pl.ds(t, 1), :], sem)


def _dispatch_kernel(pe_ref, nu_ref, slot_hbm, h_ref, xs_ref, slot_s, zero_ref, sem, zsem, ssem, *,
                     n_blocks):
    i = pl.program_id(0)
    tb = h_ref.shape[0]
    tm = zero_ref.shape[0]
    fetch = pltpu.make_async_copy(slot_hbm.at[i, 0], slot_s, ssem)
    fetch.start()

    def zero_copy(blk):
        return pltpu.make_async_copy(zero_ref, xs_ref.at[pl.ds(pl.multiple_of(blk * tm, tm), tm), :], zsem)

    @pl.when(i == 0)
    def _():
        zero_ref[...] = jnp.zeros_like(zero_ref)

        def per_expert(fn):
            for e in range(N_EXPERTS):
                end = pe_ref[e]
                start = pe_ref[e - 1] if e else 0

                @pl.when(end > start)
                def _():
                    fn(end // tm - 1)

        def per_tail(fn):
            def body(blk, carry):
                fn(blk)
                return carry
            lax.fori_loop(nu_ref[0], n_blocks, body, 0)

        per_expert(lambda blk: zero_copy(blk).start())
        per_tail(lambda blk: zero_copy(blk).start())
        per_expert(lambda blk: zero_copy(blk).wait())
        per_tail(lambda blk: zero_copy(blk).wait())

    fetch.wait()

    def issue(g, carry):
        t0 = pl.multiple_of(g * 8, 8)
        for j in range(8):
            for kk in range(TOP_K):
                _row_copy(h_ref, t0 + j, xs_ref, slot_s[kk * tb + t0 + j], sem).start(priority=kk % 2)
        return carry

    lax.fori_loop(0, tb // 8, issue, 0)
    for kk in range(TOP_K):
        pltpu.make_async_copy(h_ref, xs_ref.at[pl.ds(0, tb), :], sem).wait()


def _dispatch_call(pad_end, n_used, slot_blocks, h2, n_blocks):
    n_tok, d = h2.shape
    nblk, _, ktb = slot_blocks.shape
    tb = ktb // TOP_K
    tm = TM_EXPERT
    grid_spec = pltpu.PrefetchScalarGridSpec(
        num_scalar_prefetch=2,
        grid=(nblk,),
        in_specs=[pl.BlockSpec(memory_space=pl.ANY),
                  pl.BlockSpec((tb, d), lambda i, pe, nu: (i, 0))],
        out_specs=pl.BlockSpec(memory_space=pl.ANY),
        scratch_shapes=[pltpu.SMEM((ktb,), jnp.int32), pltpu.VMEM((tm, d), h2.dtype),
                        pltpu.SemaphoreType.DMA, pltpu.SemaphoreType.DMA, pltpu.SemaphoreType.DMA],
    )
    return pl.pallas_call(
        functools.partial(_dispatch_kernel, n_blocks=n_blocks),
        grid_spec=grid_spec,
        out_shape=jax.ShapeDtypeStruct((n_blocks * tm, d), h2.dtype),
        compiler_params=_cparams("arbitrary"),
    )(pad_end, n_used, slot_blocks, h2)


def _routing_tables(idx_t, n_tok):
    tm = TM_EXPERT
    tb = TM_ROWS
    assert n_tok % tb == 0
    blk = pl.BlockSpec((TOP_K, tb), lambda i: (0, i))
    rank_t, counts = pl.pallas_call(
        _rank_kernel,
        grid=(n_tok // tb,),
        in_specs=[blk],
        out_specs=[blk, pl.BlockSpec((N_EXPERTS, 1), lambda i: (0, 0))],
        out_shape=[jax.ShapeDtypeStruct((TOP_K, n_tok), jnp.int32),
                   jax.ShapeDtypeStruct((N_EXPERTS, 1), jnp.int32)],
        scratch_shapes=[pltpu.VMEM((N_EXPERTS, 1), F32)],
        compiler_params=_cparams("arbitrary"),
    )(idx_t)
    counts = counts[:, 0]
    padded = (counts + tm - 1) // tm * tm
    pad_end = jnp.cumsum(padded)
    pad_start = pad_end - padded
    slot_blocks = pl.pallas_call(
        _slot_kernel,
        grid=(n_tok // tb,),
        in_specs=[blk, blk, pl.BlockSpec((N_EXPERTS, 1), lambda i: (0, 0))],
        out_specs=pl.BlockSpec((None, 1, TOP_K * tb), lambda i: (i, 0, 0)),
        out_shape=jax.ShapeDtypeStruct((n_tok // tb, 1, TOP_K * tb), jnp.int32),
        compiler_params=_cparams("arbitrary"),
    )(idx_t, rank_t, pad_start[:, None])
    n_blocks = -(-TOP_K * n_tok // tm) + N_EXPERTS
    first_row = jnp.arange(n_blocks, dtype=jnp.int32) * tm
    block_e = jnp.minimum(jnp.sum(pad_end[None, :] <= first_row[:, None], axis=1),
                          N_EXPERTS - 1).astype(jnp.int32)
    n_used = (pad_end[-1:] // tm).astype(jnp.int32)
    return pad_end.astype(jnp.int32), block_e, n_used, slot_blocks, n_blocks


def _combine_kernel(slot_hbm, yb_hbm, p_ref, x_ref, gate_ref, fg_ref, o_ref, slot_s, ybuf, sem, ssem, *,
                    final_norm, blk0):
    i = pl.program_id(0)
    n = pl.num_programs(0)
    tb = x_ref.shape[0]

    def request(blk, buf):
        fetch = pltpu.make_async_copy(slot_hbm.at[blk0 + blk, 0], slot_s, ssem)
        fetch.start()
        fetch.wait()

        def issue(g, carry):
            t0 = pl.multiple_of(g * 8, 8)
            for j in range(8):
                for kk in range(TOP_K):
                    _row_copy(yb_hbm, slot_s[kk * tb + t0 + j], ybuf.at[buf, kk], t0 + j,
                              sem.at[buf]).start(priority=kk % 2)
            return carry

        lax.fori_loop(0, tb // 8, issue, 0)

    @pl.when(i == 0)
    def _():
        request(0, 0)

    @pl.when(i + 1 < n)
    def _():
        request(i + 1, (i + 1) % 2)

    cur = i % 2
    for kk in range(TOP_K):
        pltpu.make_async_copy(yb_hbm.at[pl.ds(0, tb), :], ybuf.at[cur, kk], sem.at[cur]).wait()
    f_lo = jnp.zeros(ybuf.shape[2:], F32)
    f_hi = jnp.zeros(ybuf.shape[2:], F32)
    for kk in range(TOP_K):
        lo, hi = _unpack_bf16_pairs(ybuf[cur, kk])
        f_lo = f_lo + lo * p_ref[:, kk:kk + 1]
        f_hi = f_hi + hi * p_ref[:, kk:kk + 1]
    xn = x_ref[...] + gate_ref[...] * jnp.concatenate([f_lo, f_hi], axis=1)
    if final_norm:
        xn = _rms(xn, fg_ref[...])
    o_ref[...] = xn


def _combine_call(slot_blocks, yb, probs, x, mods, group_of_block, final_g, tok0, final_norm):
    bsz, n, d = x.shape
    ktb = slot_blocks.shape[2]
    tb = ktb // TOP_K
    rows = bsz * n
    assert rows % tb == 0 and tok0 % tb == 0 and (n % tb == 0 or tb % n == 0)
    blk0 = tok0 // tb
    out = pl.pallas_call(
        functools.partial(_combine_kernel, final_norm=final_norm, blk0=blk0),
        grid=(rows // tb,),
        in_specs=[pl.BlockSpec(memory_space=pl.ANY),
                  pl.BlockSpec(memory_space=pl.ANY),
                  pl.BlockSpec((tb, TOP_K), lambda i: (blk0 + i, 0)),
                  pl.BlockSpec((tb, d), lambda i: (i, 0)),
                  pl.BlockSpec((None, None, 1, d), lambda i: (group_of_block(i), 5, 0, 0)),
                  pl.BlockSpec((1, d), lambda i: (0, 0))],
        out_specs=pl.BlockSpec((tb, d), lambda i: (i, 0)),
        out_shape=jax.ShapeDtypeStruct((rows, d), F32),
        scratch_shapes=[pltpu.SMEM((ktb,), jnp.int32), pltpu.VMEM((2, TOP_K, tb, yb.shape[1]), yb.dtype),
                        pltpu.SemaphoreType.DMA((2,)), pltpu.SemaphoreType.DMA],
        compiler_params=_cparams("arbitrary"),
    )(slot_blocks, yb, probs, x.reshape(rows, d), mods, final_g.reshape(1, d))
    return out.reshape(bsz, n, d)


def kernel(x, c, ctx, c_ctx, norm1_g, norm2_g, w_mod, b_mod, w_in, gla_wg2_f, gla_bg_f, gla_wg2_b,
           gla_bg_b, gla_norm_g, conv_w, conv_b, conv_ln_g, conv_ln_b, w_out, router_w, router_b,
           exp_w_gu, exp_b_gu, exp_w_dn, exp_b_dn, final_norm_g):
    bsz, seq, d = x.shape
    n_ctx = ctx.shape[1]
    depth = w_mod.shape[0]
    assert d == D_MODEL and seq % (FFT_N2 * FFT_KB) == 0 and seq % n_ctx == 0
    lt = seq + n_ctx
    ctx_group = bsz

    rows = 8
    cvec = jnp.concatenate([c, c_ctx[None, :], jnp.zeros((rows - bsz - 1, d), F32)], axis=0)
    mods_all = _mod_call(cvec, w_mod, b_mod).reshape(depth, rows, 6, 1, d)

    def pack_w_in(w):
        o = [0, 256, 512, 768, 1280, 1792, 1808, 1824, 2336]
        parts = [w[:, o[0]:o[1]], w[:, o[1]:o[2]], w[:, o[2]:o[3]], w[:, o[3]:o[4]], w[:, o[4]:o[5]],
                 w[:, o[7]:o[7] + CONV_WIDTH], w[:, o[7] + CONV_WIDTH:o[8]], w[:, o[5]:o[7]],
                 jnp.zeros((d, GLR_PAD - 2 * GLA_GATE_RANK), w.dtype)]
        return jnp.concatenate(parts, axis=1).astype(BF16)

    long_tables = _dft_tables(seq)
    chan = _channel_tables()
    x_lat, x_ctx = x, ctx
    lat_group = lambda b: b
    ctx_group_fn = lambda b: ctx_group

    for layer in range(depth):
        last = layer == depth - 1
        mods = mods_all[layer]
        w_in_p = pack_w_in(w_in[layer])
        wg_pad = jnp.zeros((GLR_PAD, 2 * GLA_KEY_WIDTH), F32)
        wg_pad = wg_pad.at[:GLA_GATE_RANK, :GLA_KEY_WIDTH].set(gla_wg2_f[layer])
        wg_pad = wg_pad.at[GLA_GATE_RANK:2 * GLA_GATE_RANK, GLA_KEY_WIDTH:].set(gla_wg2_b[layer])
        bg_cat = jnp.concatenate([gla_bg_f[layer], gla_bg_b[layer]])[None, :]
        w_out_b = w_out[layer].astype(BF16)
        gn_tiled = jnp.tile(gla_norm_g[layer], GLA_HEADS)[None, :]
        rw_t = router_w[layer].T
        rw_hi = rw_t.astype(BF16)
        rw_lo = (rw_t - rw_hi.astype(F32)).astype(BF16)
        rb = router_b[layer][:, None]

        (u_l, r_l, glu_l), comb = _inproj_call(x_lat, mods, lat_group, norm1_g[layer], w_in_p, lt, 0)
        (u_c, r_c, glu_c), comb = _inproj_call(x_ctx, mods, ctx_group_fn, norm1_g[layer], w_in_p, lt,
                                               seq, combined=comb)
        o_f, o_b = _gla_call(*comb, wg_pad, bg_cat, seq, n_ctx)
        yf_l = _fourier_long(u_l, long_tables, chan)
        cv_l = _conv_call(glu_l, seq // (seq // GRID_W), conv_w[layer], conv_b[layer],
                          conv_ln_g[layer], conv_ln_b[layer])
        n_tok = bsz * seq + (0 if last else bsz * n_ctx)
        x_lat, routed = _outproj_call(yf_l, o_f, o_b, 0, r_l, cv_l, x_lat, mods, lat_group, gn_tiled,
                                      w_out_b, norm2_g[layer], rw_hi, rw_lo, rb, n_tok, 0)
        if not last:
            yf_c = _fourier_short(u_c, chan)
            cv_c = _conv_call(glu_c, n_ctx, conv_w[layer], conv_b[layer], conv_ln_g[layer],
                              conv_ln_b[layer])
            x_ctx, routed = _outproj_call(yf_c, o_f, o_b, seq, r_c, cv_c, x_ctx, mods, ctx_group_fn,
                                          gn_tiled, w_out_b, norm2_g[layer], rw_hi, rw_lo, rb, n_tok,
                                          bsz * seq, carried=routed)

        h2, idx_t, prob_t = routed
        pad_end, block_e, n_used, slot_blocks, n_blocks = _routing_tables(idx_t, n_tok)
        xs = _dispatch_call(pad_end, n_used, slot_blocks, h2, n_blocks)
        yb = _expert_call(block_e, n_used, xs, layer, exp_w_gu, exp_b_gu, exp_w_dn, exp_b_dn)
        probs = prob_t.T
        lat_blocks = seq // (slot_blocks.shape[2] // TOP_K)
        x_lat = _combine_call(slot_blocks, yb, probs, x_lat, mods, lambda i: i // lat_blocks,
                              final_norm_g, 0, last)
        if not last:
            x_ctx = _combine_call(slot_blocks, yb, probs, x_ctx, mods, lambda i: ctx_group,
                                  final_norm_g, bsz * seq, False)

    return x_lat
```

```python
import functools

import jax
import jax.numpy as jnp
from jax import lax
from jax.experimental import pallas as pl
from jax.experimental.pallas import tpu as pltpu

F32 = jnp.float32
BF16 = jnp.bfloat16
U32 = jnp.uint32
HIGH_HALF = 0xFFFF0000

D_MODEL = 1024
DEPTH = 2
GRID_W = 64
FOURIER_WIDTH = 256
FOURIER_HEADS = 4
FOURIER_HEAD_DIM = FOURIER_WIDTH // FOURIER_HEADS
GLA_HEADS = 4
GLA_KEY_WIDTH = 256
GLA_VALUE_WIDTH = 512
GLA_DK = GLA_KEY_WIDTH // GLA_HEADS
GLA_DV = GLA_VALUE_WIDTH // GLA_HEADS
GLA_GATE_RANK = 16
GLA_GATE_NORMALIZER = 16.0
CONV_WIDTH = 256
CONV_KERNEL = 31
N_EXPERTS = 32
TOP_K = 4
D_FF = D_MODEL
SWIGLU_LIMIT = 7.0
SWIGLU_ALPHA = 1.702
NORM_EPS = 1e-6

LANES = 128
VMEM_LIMIT = 56 * 1024 * 1024

COL_U = 0
COL_Q = COL_U + FOURIER_WIDTH
COL_K = COL_Q + GLA_KEY_WIDTH
COL_V = COL_K + GLA_KEY_WIDTH
COL_R = COL_V + GLA_VALUE_WIDTH
COL_CA = COL_R + GLA_VALUE_WIDTH
COL_CG = COL_CA + CONV_WIDTH
COL_GL = COL_CG + CONV_WIDTH
GLR_PAD = LANES
IN_PAD = COL_GL + GLR_PAD

GLA_CHUNK = 128
GLA_STEP_CHUNKS = 2
TM_ROWS = 512
TM_EXPERT = 512
EXPERT_CHUNKS = 4
FFT_N2 = 128
FFT_KB = 8
CONV_HALO = 16
CONV_GROUP = 8
CONV_PITCH = 100


def _cparams(*sem):
    return pltpu.CompilerParams(dimension_semantics=sem, vmem_limit_bytes=VMEM_LIMIT)


def _dot(a, b):
    return jnp.dot(a, b, preferred_element_type=F32)


def _dot_nt(a, b):
    return lax.dot_general(a, b, (((1,), (1,)), ((), ())), preferred_element_type=F32)


def _dot_tn(a, b):
    return lax.dot_general(a, b, (((0,), (0,)), ((), ())), preferred_element_type=F32)


def _split(a):
    hi = a.astype(BF16)
    lo = (a - hi.astype(F32)).astype(BF16)
    return hi, lo


def _dot3(a, b):
    ah, al = _split(a)
    bh, bl = _split(b)
    return _dot(ah, bh) + _dot(ah, bl) + _dot(al, bh)


def _sigmoid(x):
    return 1.0 / (1.0 + jnp.exp(-x))


def _pack_bf16_pairs(a):
    h = a.shape[1] // 2
    bits = lax.bitcast_convert_type(a.astype(BF16).astype(F32), U32)
    return (bits[:, :h] >> 16) | (bits[:, h:] & U32(HIGH_HALF))


def _unpack_bf16_pairs(w):
    return (lax.bitcast_convert_type(w << 16, F32), lax.bitcast_convert_type(w & U32(HIGH_HALF), F32))


def _rms(x, g):
    ms = jnp.mean(x * x, axis=-1, keepdims=True)
    return x * lax.rsqrt(ms + NORM_EPS) * g


def _mod_kernel(cv_ref, w_ref, b_ref, o_ref):
    cv = cv_ref[...]
    a = cv * _sigmoid(cv)
    o_ref[...] = _dot3(a, w_ref[...]) + b_ref[...]


def _mod_call(cvec, w_mod, b_mod):
    depth, d, n = w_mod.shape
    rows = cvec.shape[0]
    tn = 1536
    return pl.pallas_call(
        _mod_kernel,
        grid=(depth, n // tn),
        in_specs=[
            pl.BlockSpec((rows, d), lambda l, j: (0, 0)),
            pl.BlockSpec((None, d, tn), lambda l, j: (l, 0, j)),
            pl.BlockSpec((None, 1, tn), lambda l, j: (l, 0, j)),
        ],
        out_specs=pl.BlockSpec((None, rows, tn), lambda l, j: (l, 0, j)),
        out_shape=jax.ShapeDtypeStruct((depth, rows, n), F32),
        compiler_params=_cparams("arbitrary", "arbitrary"),
    )(cvec, w_mod, b_mod.reshape(depth, 1, n))


def _inproj_kernel(x_ref, g_ref, sh_ref, sc_ref, w_ref, *rest):
    u_ref, r_ref, glu_ref, q_ref, k_ref, v_ref, gl_ref = rest[-7:]
    x = x_ref[...]
    h = _rms(x, g_ref[...]) * (1.0 + sc_ref[...]) + sh_ref[...]
    p = _dot(h.astype(BF16), w_ref[...])
    u_ref[...] = p[:, COL_U:COL_Q].astype(u_ref.dtype)
    q_ref[...] = (p[:, COL_Q:COL_K] * (GLA_DK ** -0.5)).astype(q_ref.dtype)
    k_ref[...] = p[:, COL_K:COL_V].astype(k_ref.dtype)
    v_ref[...] = p[:, COL_V:COL_R].astype(v_ref.dtype)
    r_ref[...] = p[:, COL_R:COL_CA].astype(r_ref.dtype)
    glu_ref[...] = (p[:, COL_CA:COL_CG] * _sigmoid(p[:, COL_CG:COL_GL])).astype(glu_ref.dtype)
    gl_ref[...] = p[:, COL_GL:IN_PAD]


def _inproj_call(x, mods, group_of_batch, norm_g, w_in_p, lt, row0, combined=None):
    bsz, n, d = x.shape
    tm = min(TM_ROWS, n)
    assert n % tm == 0 and row0 % tm == 0
    blk0 = row0 // tm
    nb = n // tm
    steps = nb + (-(-(lt - n) // tm) if combined is None else 0)
    widths = (GLA_KEY_WIDTH, GLA_KEY_WIDTH, GLA_VALUE_WIDTH, GLR_PAD)
    dtypes = (BF16, BF16, BF16, F32)
    row_spec = lambda w: pl.BlockSpec((None, tm, w), lambda b, i: (b, jnp.minimum(i, nb - 1), 0))
    comb_spec = lambda w: pl.BlockSpec((None, tm, w), lambda b, i: (b, blk0 + i, 0))
    mod_spec = lambda which: pl.BlockSpec(
        (None, None, 1, d), lambda b, i: (group_of_batch(b), which, 0, 0))
    in_specs = [
        row_spec(d),
        pl.BlockSpec((1, d), lambda b, i: (0, 0)),
        mod_spec(0), mod_spec(1),
        pl.BlockSpec((d, IN_PAD), lambda b, i: (0, 0)),
    ]
    args = [x, norm_g.reshape(1, d), mods, mods, w_in_p]
    aliases = {}
    if combined is not None:
        for t, arr in enumerate(combined):
            in_specs.append(pl.BlockSpec(memory_space=pl.ANY))
            aliases[len(args)] = 3 + t
            args.append(arr)
    out_shape = [
        jax.ShapeDtypeStruct((bsz, n, FOURIER_WIDTH), BF16),
        jax.ShapeDtypeStruct((bsz, n, GLA_VALUE_WIDTH), BF16),
        jax.ShapeDtypeStruct((bsz, n, CONV_WIDTH), BF16),
    ] + [jax.ShapeDtypeStruct((bsz, lt, w), dt) for w, dt in zip(widths, dtypes)]
    out_specs = [row_spec(FOURIER_WIDTH), row_spec(GLA_VALUE_WIDTH), row_spec(CONV_WIDTH)] + [
        comb_spec(w) for w in widths]
    outs = pl.pallas_call(
        _inproj_kernel,
        grid=(bsz, steps),
        in_specs=in_specs,
        out_specs=out_specs,
        out_shape=out_shape,
        input_output_aliases=aliases,
        compiler_params=_cparams("arbitrary", "arbitrary"),
    )(*args)
    return outs[:3], outs[3:]


def _dft_tables(length):
    n2 = FFT_N2
    n1 = length // n2
    two_pi = 2.0 * jnp.pi

    def cs(num, den):
        ang = (num % den).astype(F32) * (two_pi / den)
        return jnp.cos(ang), jnp.sin(ang)

    k1 = jnp.arange(n1, dtype=jnp.int32)
    c1, s1 = cs(k1[:, None] * k1[None, :], n1)
    stage1 = (jnp.concatenate([c1, -s1], axis=0) * (n1 ** -0.5)).astype(BF16)
    k2 = jnp.arange(n2, dtype=jnp.int32)
    ct, st = cs(k1[:, None] * k2[None, :], length)
    cf, sf = cs(k2[:, None] * k2[None, :], n2)
    scale = n2 ** -0.5
    mr = (ct[:, None, :] * cf[None] - st[:, None, :] * sf[None]) * scale
    mi = -(st[:, None, :] * cf[None] + ct[:, None, :] * sf[None]) * scale
    stage2 = jnp.concatenate([jnp.concatenate([mr, -mi], axis=2),
                              jnp.concatenate([mi, mr], axis=2)], axis=1).astype(BF16)
    return stage1, stage2


def _channel_tables():
    hd = FOURIER_HEAD_DIM
    c = jnp.arange(FOURIER_WIDTH, dtype=jnp.int32)
    same_head = (c[:, None] // hd) == (c[None, :] // hd)
    ang = (((c[:, None] % hd) * (c[None, :] % hd)) % hd).astype(F32) * (2.0 * jnp.pi / hd)
    scale = hd ** -0.5
    bdc = jnp.where(same_head, jnp.cos(ang) * scale, 0.0).astype(BF16)
    bds = jnp.where(same_head, jnp.sin(ang) * scale, 0.0).astype(BF16)
    return bdc, bds


def _fft1_kernel(x_ref, cs_ref, zr_ref, zi_ref):
    n1 = x_ref.shape[0]
    z = _dot(cs_ref[...], x_ref[...])
    zr_ref[...] = z[:n1].astype(zr_ref.dtype)
    zi_ref[...] = z[n1:].astype(zi_ref.dtype)


def _fft2_kernel(zr_ref, zi_ref, m_ref, bdc_ref, bds_ref, o_ref):
    kb, n2, w = zr_ref.shape
    for j in range(kb):
        z = jnp.concatenate([zr_ref[j], zi_ref[j]], axis=0)
        a = _dot(m_ref[j], z)
        y = _dot(a[:n2].astype(BF16), bdc_ref[...]) + _dot(a[n2:].astype(BF16), bds_ref[...])
        o_ref[:, j * w:(j + 1) * w] = y.astype(o_ref.dtype)


def _fourier_long(u, tables, chan):
    bsz, length, w = u.shape
    stage1, stage2 = tables
    bdc, bds = chan
    n2 = FFT_N2
    n1 = length // n2
    tn = 4096
    cols = n2 * w
    zr, zi = pl.pallas_call(
        _fft1_kernel,
        grid=(bsz, cols // tn),
        in_specs=[pl.BlockSpec((None, n1, tn), lambda b, j: (b, 0, j)),
                  pl.BlockSpec((2 * n1, n1), lambda b, j: (0, 0))],
        out_specs=[pl.BlockSpec((None, n1, tn), lambda b, j: (b, 0, j))] * 2,
        out_shape=[jax.ShapeDtypeStruct((bsz, n1, cols), BF16)] * 2,
        compiler_params=_cparams("arbitrary", "arbitrary"),
    )(u.reshape(bsz, n1, cols), stage1)
    kb = FFT_KB
    z_spec = pl.BlockSpec((None, kb, n2, w), lambda b, j: (b, j, 0, 0))
    y = pl.pallas_call(
        _fft2_kernel,
        grid=(bsz, n1 // kb),
        in_specs=[z_spec, z_spec,
                  pl.BlockSpec((kb, 2 * n2, 2 * n2), lambda b, j: (j, 0, 0)),
                  pl.BlockSpec((w, w), lambda b, j: (0, 0)),
                  pl.BlockSpec((w, w), lambda b, j: (0, 0))],
        out_specs=pl.BlockSpec((None, n2, kb * w), lambda b, j: (b, 0, j)),
        out_shape=jax.ShapeDtypeStruct((bsz, n2, n1 * w), BF16),
        compiler_params=_cparams("arbitrary", "arbitrary"),
    )(zr.reshape(bsz, n1, n2, w), zi.reshape(bsz, n1, n2, w), stage2, bdc, bds)
    return y.reshape(bsz, length, w)


def _dft_short_kernel(u_ref, c_ref, s_ref, bdc_ref, bds_ref, o_ref):
    u = u_ref[...]
    p = _dot(u, bdc_ref[...]).astype(BF16)
    q = _dot(u, bds_ref[...]).astype(BF16)
    o_ref[...] = (_dot(c_ref[...], p) - _dot(s_ref[...], q)).astype(o_ref.dtype)


def _fourier_short(u, chan):
    bsz, length, w = u.shape
    bdc, bds = chan
    k = jnp.arange(length, dtype=jnp.int32)
    ang = ((k[:, None] * k[None, :]) % length).astype(F32) * (2.0 * jnp.pi / length)
    c = (jnp.cos(ang) * length ** -0.5).astype(BF16)
    s = (jnp.sin(ang) * length ** -0.5).astype(BF16)
    full = lambda n: pl.BlockSpec((n, n), lambda b: (0, 0))
    return pl.pallas_call(
        _dft_short_kernel,
        grid=(bsz,),
        in_specs=[pl.BlockSpec((None, length, w), lambda b: (b, 0, 0)),
                  full(length), full(length), full(w), full(w)],
        out_specs=pl.BlockSpec((None, length, w), lambda b: (b, 0, 0)),
        out_shape=jax.ShapeDtypeStruct((bsz, length, w), BF16),
        compiler_params=_cparams("arbitrary"),
    )(u, c, s, bdc, bds)


def _conv_kernel(x_ref, w_ref, cb_ref, lg_ref, lb_ref, o_ref, pad_ref, *, seg):
    nseg = x_ref.shape[0] // seg
    width = x_ref.shape[1]
    halo = jnp.zeros((CONV_HALO, width), F32)
    for s in range(nseg):
        pad_ref[s, 0:CONV_HALO, :] = halo
        pad_ref[s, CONV_HALO:CONV_HALO + seg, :] = x_ref[s * seg:(s + 1) * seg, :].astype(F32)
        pad_ref[s, CONV_HALO + seg:2 * CONV_HALO + seg, :] = halo
    first = CONV_HALO - CONV_KERNEL // 2
    sub = 8
    span = seg + 2 * CONV_HALO - sub
    for s in range(nseg):
        acc = jnp.zeros((seg, width), F32)
        for r in range(sub):
            shifted = pad_ref[s, r:r + span, :]
            for a in range((span - seg) // sub + 1):
                j = a * sub + r - first
                if 0 <= j < CONV_KERNEL:
                    acc = acc + shifted[a * sub:a * sub + seg, :] * w_ref[j:j + 1, :]
        y = acc + cb_ref[...]
        mu = jnp.mean(y, axis=-1, keepdims=True)
        yc = y - mu
        var = jnp.mean(yc * yc, axis=-1, keepdims=True)
        z = yc * lax.rsqrt(var + NORM_EPS) * lg_ref[...] + lb_ref[...]
        o_ref[s * seg:(s + 1) * seg, :] = (z * _sigmoid(z)).astype(o_ref.dtype)


def _conv_group_kernel(x_ref, w_ref, cb_ref, lg_ref, lb_ref, o_ref, pad_ref, acc_ref, *, seg):
    g, p = CONV_GROUP, CONV_PITCH
    halves = x_ref.shape[1] // LANES
    pad_ref[...] = jnp.zeros_like(pad_ref)
    for s in range(g):
        for h in range(halves):
            pad_ref[h, s * p + CONV_HALO:s * p + CONV_HALO + seg, :] = (
                x_ref[s * seg:(s + 1) * seg, h * LANES:(h + 1) * LANES].astype(F32))
    first = CONV_HALO - CONV_KERNEL // 2
    unroll = 4

    def body(i, carry):
        t0 = unroll * i
        for h in range(halves):
            accs = [None] * unroll
            for j in range(CONV_KERNEL):
                tap = w_ref[h, j]
                for u in range(unroll):
                    term = pad_ref[h, pl.ds(t0 + u + first + j, g, stride=p), :] * tap
                    accs[u] = term if accs[u] is None else accs[u] + term
            for u in range(unroll):
                acc_ref[h, pl.ds(t0 + u, g, stride=p), :] = accs[u]
        return carry

    lax.fori_loop(0, seg // unroll, body, 0)
    for s in range(g):
        y = jnp.concatenate([acc_ref[h, s * p:s * p + seg, :] for h in range(halves)], axis=1) + cb_ref[...]
        mu = jnp.mean(y, axis=-1, keepdims=True)
        yc = y - mu
        var = jnp.mean(yc * yc, axis=-1, keepdims=True)
        z = yc * lax.rsqrt(var + NORM_EPS) * lg_ref[...] + lb_ref[...]
        o_ref[s * seg:(s + 1) * seg, :] = (z * _sigmoid(z)).astype(o_ref.dtype)


def _conv_call(glu, seg, conv_w, conv_b, ln_g, ln_b):
    bsz, n, w = glu.shape
    t = max(seg, min(TM_ROWS, n))
    assert n % t == 0 and t % seg == 0
    vec = lambda: pl.BlockSpec((1, w), lambda b, i: (0, 0))
    if t // seg == CONV_GROUP and seg + 2 * CONV_HALO <= CONV_PITCH and w % LANES == 0:
        halves = w // LANES
        taps = jnp.broadcast_to(conv_w.reshape(CONV_KERNEL, halves, 1, LANES).transpose(1, 0, 2, 3),
                                (halves, CONV_KERNEL, 8, LANES))
        slab = pltpu.VMEM((halves, CONV_GROUP * CONV_PITCH, LANES), F32)
        return pl.pallas_call(
            functools.partial(_conv_group_kernel, seg=seg),
            grid=(bsz, n // t),
            in_specs=[pl.BlockSpec((None, t, w), lambda b, i: (b, i, 0)),
                      pl.BlockSpec(taps.shape, lambda b, i: (0, 0, 0, 0)),
                      vec(), vec(), vec()],
            out_specs=pl.BlockSpec((None, t, w), lambda b, i: (b, i, 0)),
            out_shape=jax.ShapeDtypeStruct((bsz, n, w), BF16),
            scratch_shapes=[slab, slab],
            compiler_params=_cparams("arbitrary", "arbitrary"),
        )(glu, taps, conv_b.reshape(1, w), ln_g.reshape(1, w), ln_b.reshape(1, w))
    return pl.pallas_call(
        functools.partial(_conv_kernel, seg=seg),
        grid=(bsz, n // t),
        in_specs=[pl.BlockSpec((None, t, w), lambda b, i: (b, i, 0)),
                  pl.BlockSpec((CONV_KERNEL, w), lambda b, i: (0, 0)),
                  vec(), vec(), vec()],
        out_specs=pl.BlockSpec((None, t, w), lambda b, i: (b, i, 0)),
        out_shape=jax.ShapeDtypeStruct((bsz, n, w), BF16),
        scratch_shapes=[pltpu.VMEM((t // seg, seg + 2 * CONV_HALO, w), F32)],
        compiler_params=_cparams("arbitrary", "arbitrary"),
    )(glu, conv_w, conv_b.reshape(1, w), ln_g.reshape(1, w), ln_b.reshape(1, w))


def _gla_direction(q_ref, k_ref, v_ref, gl_ref, wg_ref, bg_ref, o_ref, st_ref, reverse):
    c = GLA_CHUNK
    n_sub = q_ref.shape[0] // c
    st = st_ref[...]
    for s in (reversed(range(n_sub)) if reverse else range(n_sub)):
        rows = slice(s * c, (s + 1) * c)
        st = _gla_chunk(q_ref[rows, :], k_ref[rows, :], v_ref[rows, :], gl_ref[rows, :], wg_ref, bg_ref,
                        o_ref.at[rows, :], st, reverse)
    st_ref[...] = st


def _gla_chunk(q, k, v, glr, wg_ref, bg_ref, o_ref, st, reverse):
    c = GLA_CHUNK
    kw = GLA_KEY_WIDTH
    col0 = kw if reverse else 0
    pre = _dot3(glr, wg_ref[:, col0:col0 + kw]) + bg_ref[:, col0:col0 + kw]
    g = (jnp.minimum(pre, 0.0) - jnp.log(1.0 + jnp.exp(-jnp.abs(pre)))) * (1.0 / GLA_GATE_NORMALIZER)
    row = lax.broadcasted_iota(jnp.int32, (c, c), 0)
    col = lax.broadcasted_iota(jnp.int32, (c, c), 1)
    seen = (col >= row) if reverse else (col <= row)
    tri = jnp.where(seen, 1.0, 0.0).astype(BF16)
    gh, gl = _split(g)
    b = _dot(tri, gh) + _dot(tri, gl)
    mid = c // 2 if reverse else c // 2 - 1
    last = 0 if reverse else c - 1
    b_mid = b[mid:mid + 1, :]
    b_last = b[last:last + 1, :]
    q = q.astype(F32)
    k = k.astype(F32)
    qe = q * jnp.exp(b - b_mid)
    ke = k * jnp.exp(b_mid - b)
    kd = k * jnp.exp(b_last - b)
    head_of_lane = lax.broadcasted_iota(jnp.int32, (1, kw), 1) // GLA_DK
    q_heads = jnp.concatenate(
        [jnp.where(head_of_lane == h, qe, 0.0) for h in range(GLA_HEADS)], axis=0).astype(BF16)
    rhs = jnp.concatenate([ke, st * jnp.exp(b_mid)], axis=0).astype(BF16)
    res = _dot_nt(q_heads, rhs)
    outs = []
    for h in range(GLA_HEADS):
        blk = res[h * c:(h + 1) * c, :]
        scores = jnp.where(seen, blk[:, :c], 0.0).astype(BF16)
        outs.append(_dot(scores, v[:, h * GLA_DV:(h + 1) * GLA_DV]) + blk[:, c:])
    o_ref[...] = jnp.concatenate(outs, axis=1).astype(o_ref.dtype)
    kv = _dot_tn(v, kd.astype(BF16))
    ds = jnp.zeros_like(st)
    for h in range(GLA_HEADS):
        ds = ds + jnp.where(head_of_lane == h, kv[h * GLA_DV:(h + 1) * GLA_DV, :], 0.0)
    return st * jnp.exp(b_last) + ds


def _gla_kernel(qf, kf, vf, gf, qb, kb, vb, gb, wg_ref, bg_ref, of_ref, ob_ref, sf_ref, sb_ref):
    @pl.when(pl.program_id(1) == 0)
    def _():
        sf_ref[...] = jnp.zeros_like(sf_ref)
        sb_ref[...] = jnp.zeros_like(sb_ref)

    _gla_direction(qf, kf, vf, gf, wg_ref, bg_ref, of_ref, sf_ref, False)
    _gla_direction(qb, kb, vb, gb, wg_ref, bg_ref, ob_ref, sb_ref, True)


def _gla_call(q, k, v, gl, wg_pad, bg_cat, n_lat, n_ctx):
    bsz, lt, _ = q.shape
    c = GLA_STEP_CHUNKS * GLA_CHUNK
    assert n_lat % c == 0 and n_ctx % c == 0
    cl, cc = n_lat // c, n_ctx // c

    def fwd_blk(j):
        return jnp.where(j < cc, cl + j, j - cc)

    def bwd_blk(j):
        return jnp.where(j < cc, cl + cc - 1 - j, cl - 1 - (j - cc))

    def spec(w, blk):
        return pl.BlockSpec((None, c, w), lambda b, j: (b, blk(j), 0))

    widths = (GLA_KEY_WIDTH, GLA_KEY_WIDTH, GLA_VALUE_WIDTH, GLR_PAD)
    in_specs = [spec(w, fwd_blk) for w in widths] + [spec(w, bwd_blk) for w in widths] + [
        pl.BlockSpec(wg_pad.shape, lambda b, j: (0, 0)),
        pl.BlockSpec(bg_cat.shape, lambda b, j: (0, 0))]
    return pl.pallas_call(
        _gla_kernel,
        grid=(bsz, cl + cc),
        in_specs=in_specs,
        out_specs=[spec(GLA_VALUE_WIDTH, fwd_blk), spec(GLA_VALUE_WIDTH, bwd_blk)],
        out_shape=[jax.ShapeDtypeStruct((bsz, lt, GLA_VALUE_WIDTH), F32)] * 2,
        scratch_shapes=[pltpu.VMEM((GLA_DV, GLA_KEY_WIDTH), F32)] * 2,
        compiler_params=_cparams("arbitrary", "arbitrary"),
    )(q, k, v, gl, q, k, v, gl, wg_pad, bg_cat)


def _outproj_kernel(yf_ref, of_ref, ob_ref, r_ref, cv_ref, x_ref, gate_ref, sh_ref, sc_ref,
                    gn_ref, wo_ref, n2_ref, rwh_ref, rwl_ref, rb_ref, *rest):
    xo_ref, h2_ref, idx_ref, prob_ref = rest[-4:]
    o = of_ref[...] + ob_ref[...]
    heads = []
    for h in range(GLA_HEADS):
        oh = o[:, h * GLA_DV:(h + 1) * GLA_DV]
        heads.append(oh * lax.rsqrt(jnp.mean(oh * oh, axis=-1, keepdims=True) + NORM_EPS))
    r = r_ref[...].astype(F32)
    gla = jnp.concatenate(heads, axis=1) * gn_ref[...] * (r * _sigmoid(r))
    c0, c1 = FOURIER_WIDTH, FOURIER_WIDTH + GLA_VALUE_WIDTH
    y = (_dot(yf_ref[...], wo_ref[0:c0, :]) + _dot(gla.astype(BF16), wo_ref[c0:c1, :])
         + _dot(cv_ref[...], wo_ref[c1:, :]))
    xn = x_ref[...] + gate_ref[...] * y
    xo_ref[...] = xn
    h2 = _rms(xn, n2_ref[...]) * (1.0 + sc_ref[...]) + sh_ref[...]
    hh, hl = _split(h2)
    h2_ref[...] = _pack_bf16_pairs(h2)
    logits = (_dot_nt(rwh_ref[...], hh) + _dot_nt(rwh_ref[...], hl) + _dot_nt(rwl_ref[...], hh)
              + rb_ref[...])
    expert = lax.broadcasted_iota(jnp.int32, logits.shape, 0)
    vals, idxs = [], []
    cur = logits
    for _ in range(TOP_K):
        m = jnp.max(cur, axis=0, keepdims=True)
        ix = jnp.min(jnp.where(cur == m, expert, N_EXPERTS), axis=0, keepdims=True)
        vals.append(m)
        idxs.append(ix)
        cur = jnp.where(expert == ix, -jnp.inf, cur)
    es = [jnp.exp(vv - vals[0]) for vv in vals]
    inv = 1.0 / functools.reduce(lambda a, b: a + b, es)
    idx_ref[...] = jnp.concatenate(idxs, axis=0)
    prob_ref[...] = jnp.concatenate([e * inv for e in es], axis=0)


def _outproj_call(yf, o_f, o_b, o_row0, r, cv, x, mods, group_of_batch, gn_tiled, w_out, norm2_g,
                  rw_hi, rw_lo, rb, n_tok, tok0, carried=None):
    bsz, n, d = x.shape
    tm = min(TM_ROWS, n)
    assert n % tm == 0 and o_row0 % tm == 0 and tok0 % tm == 0
    nb = n // tm
    spare = (n_tok - bsz * n) if carried is None else 0
    assert spare in (0, tm)
    steps = nb + spare // tm
    last = lambda i: jnp.minimum(i, nb - 1)
    row = lambda w: pl.BlockSpec((None, tm, w), lambda b, i: (b, last(i), 0))
    orow = pl.BlockSpec((None, tm, GLA_VALUE_WIDTH), lambda b, i: (b, o_row0 // tm + last(i), 0))
    mod = lambda which: pl.BlockSpec((None, None, 1, d), lambda b, i: (group_of_batch(b), which, 0, 0))
    const = lambda a: pl.BlockSpec(a.shape, lambda b, i: (0,) * a.ndim)
    consts = [gn_tiled, w_out, norm2_g.reshape(1, d), rw_hi, rw_lo, rb]
    in_specs = [row(FOURIER_WIDTH), orow, orow, row(GLA_VALUE_WIDTH), row(CONV_WIDTH), row(d),
                mod(2), mod(3), mod(4)] + [const(a) for a in consts]
    args = [yf, o_f, o_b, r, cv, x, mods, mods, mods] + consts
    aliases = {}
    if carried is not None:
        for t, arr in enumerate(carried):
            in_specs.append(pl.BlockSpec(memory_space=pl.ANY))
            aliases[len(args)] = 1 + t
            args.append(arr)
    tokblk = lambda b, i: tok0 // tm + jnp.where(
        jnp.logical_and(i == nb, b == bsz - 1), bsz * nb, b * nb + last(i))
    out_specs = [row(d),
                 pl.BlockSpec((tm, d // 2), lambda b, i: (tokblk(b, i), 0)),
                 pl.BlockSpec((TOP_K, tm), lambda b, i: (0, tokblk(b, i))),
                 pl.BlockSpec((TOP_K, tm), lambda b, i: (0, tokblk(b, i)))]
    out_shape = [jax.ShapeDtypeStruct((bsz, n, d), F32),
                 jax.ShapeDtypeStruct((n_tok, d // 2), U32),
                 jax.ShapeDtypeStruct((TOP_K, n_tok), jnp.int32),
                 jax.ShapeDtypeStruct((TOP_K, n_tok), F32)]
    outs = pl.pallas_call(
        _outproj_kernel,
        grid=(bsz, steps),
        in_specs=in_specs,
        out_specs=out_specs,
        out_shape=out_shape,
        input_output_aliases=aliases,
        compiler_params=_cparams("arbitrary", "arbitrary"),
    )(*args)
    return outs[0], outs[1:]


def _expert_kernel(be_ref, nu_ref, x_ref, wgu_ref, bgu_ref, wdn_ref, bdn_ref, o_ref, wgu_s, wdn_s):
    i = pl.program_id(0)
    nu = nu_ref[0]
    n_chunks = wgu_s.shape[0]
    fc = D_FF // n_chunks

    @pl.when(i >= nu)
    def _():
        o_ref[...] = jnp.zeros_like(o_ref)

    @pl.when(i < nu)
    def _():
        changed = jnp.logical_or(i == 0, be_ref[i] != be_ref[jnp.maximum(i - 1, 0)])

        @pl.when(changed)
        def _():
            rows = 128
            for s in range(wgu_ref.shape[0] // rows):
                rs = slice(s * rows, (s + 1) * rows)
                for n in range(n_chunks):
                    wgu_s[n, rs, 0:fc] = wgu_ref[rs, n * fc:(n + 1) * fc].astype(BF16)
                    wgu_s[n, rs, fc:2 * fc] = wgu_ref[rs, D_FF + n * fc:D_FF + (n + 1) * fc].astype(BF16)
                wdn_s[rs, :] = wdn_ref[rs, :].astype(BF16)

        x = jnp.concatenate(_unpack_bf16_pairs(x_ref[...]), axis=1).astype(BF16)
        acts = []
        for n in range(n_chunks):
            gate = _dot(x, wgu_s[n, :, 0:fc]) + bgu_ref[:, n * fc:(n + 1) * fc]
            up = _dot(x, wgu_s[n, :, fc:2 * fc]) + bgu_ref[:, D_FF + n * fc:D_FF + (n + 1) * fc]
            gate = jnp.minimum(gate, SWIGLU_LIMIT)
            up = jnp.clip(up, -SWIGLU_LIMIT, SWIGLU_LIMIT)
            acts.append((gate * _sigmoid(SWIGLU_ALPHA * gate) * (up + 1.0)).astype(BF16))
        half = n_chunks // 2
        y = (_dot(jnp.concatenate(acts[:half], axis=1), wdn_s[0:half * fc, :])
             + _dot(jnp.concatenate(acts[half:], axis=1), wdn_s[half * fc:, :]) + bdn_ref[...])
        o_ref[...] = _pack_bf16_pairs(y)


def _expert_call(block_e, n_used, xs, layer, w_gu, b_gu, w_dn, b_dn):
    rows, dh = xs.shape
    d = 2 * dh
    tm = TM_EXPERT
    nblk = rows // tm
    depth, e, _, f2 = w_gu.shape
    n_chunks = EXPERT_CHUNKS
    live = lambda i, nu: jnp.minimum(i, nu[0] - 1)
    wmap = lambda i, be, nu: (layer, be[live(i, nu)], 0, 0)
    grid_spec = pltpu.PrefetchScalarGridSpec(
        num_scalar_prefetch=2,
        grid=(nblk,),
        in_specs=[
            pl.BlockSpec((tm, dh), lambda i, be, nu: (live(i, nu), 0)),
            pl.BlockSpec((None, None, d, f2), wmap),
            pl.BlockSpec((None, None, 1, f2), wmap),
            pl.BlockSpec((None, None, f2 // 2, d), wmap),
            pl.BlockSpec((None, None, 1, d), wmap),
        ],
        out_specs=pl.BlockSpec((tm, dh), lambda i, be, nu: (i, 0)),
        scratch_shapes=[pltpu.VMEM((n_chunks, d, f2 // n_chunks), BF16), pltpu.VMEM((f2 // 2, d), BF16)],
    )
    return pl.pallas_call(
        _expert_kernel,
        grid_spec=grid_spec,
        out_shape=jax.ShapeDtypeStruct((rows, dh), xs.dtype),
        compiler_params=_cparams("arbitrary"),
    )(block_e, n_used, xs, w_gu, b_gu.reshape(depth, e, 1, f2), w_dn, b_dn.reshape(depth, e, 1, d))


def _rank_kernel(idx_ref, rank_ref, cnt_ref, base_ref):
    @pl.when(pl.program_id(0) == 0)
    def _():
        base_ref[...] = jnp.zeros_like(base_ref)

    tb = idx_ref.shape[1]
    row = lax.broadcasted_iota(jnp.int32, (tb, tb), 0)
    col = lax.broadcasted_iota(jnp.int32, (tb, tb), 1)
    earlier = jnp.where(row < col, 1.0, 0.0).astype(BF16)
    expert = lax.broadcasted_iota(jnp.int32, (N_EXPERTS, tb), 0)
    base = base_ref[...]
    ranks = []
    for kk in range(TOP_K):
        hit = expert == idx_ref[kk:kk + 1, :]
        onehot = hit.astype(F32)
        before = _dot(onehot.astype(BF16), earlier) + base
        ranks.append(jnp.sum(before * onehot, axis=0, keepdims=True))
        base = base + jnp.sum(onehot, axis=1, keepdims=True)
    rank_ref[...] = jnp.concatenate(ranks, axis=0).astype(jnp.int32)
    base_ref[...] = base
    cnt_ref[...] = base.astype(jnp.int32)


def _slot_kernel(idx_ref, rank_ref, start_ref, slot_ref):
    expert = lax.broadcasted_iota(jnp.int32, (N_EXPERTS, idx_ref.shape[1]), 0)
    rows = []
    for kk in range(TOP_K):
        start = jnp.sum(jnp.where(expert == idx_ref[kk:kk + 1, :], start_ref[...], 0), axis=0,
                        keepdims=True)
        rows.append(rank_ref[kk:kk + 1, :] + start)
    slot_ref[...] = jnp.concatenate(rows, axis=1)


def _row_copy(src, s, dst, t, sem):
    return pltpu.make_async_copy(src.at[pl.ds(s, 1), :], dst.at[pl.ds(t, 1), :], sem)


def _dispatch_kernel(pe_ref, nu_ref, slot_hbm, h_ref, xs_ref, slot_s, zero_ref, sem, zsem, ssem, *,
                     n_blocks):
    i = pl.program_id(0)
    tb = h_ref.shape[0]
    tm = zero_ref.shape[0]
    fetch = pltpu.make_async_copy(slot_hbm.at[i, 0], slot_s, ssem)
    fetch.start()

    def zero_copy(blk):
        return pltpu.make_async_copy(zero_ref, xs_ref.at[pl.ds(pl.multiple_of(blk * tm, tm), tm), :], zsem)

    @pl.when(i == 0)
    def _():
        zero_ref[...] = jnp.zeros_like(zero_ref)

        def per_expert(fn):
            for e in range(N_EXPERTS):
                end = pe_ref[e]
                start = pe_ref[e - 1] if e else 0

                @pl.when(end > start)
                def _():
                    fn(end // tm - 1)

        def per_tail(fn):
            def body(blk, carry):
                fn(blk)
                return carry
            lax.fori_loop(nu_ref[0], n_blocks, body, 0)

        per_expert(lambda blk: zero_copy(blk).start())
        per_tail(lambda blk: zero_copy(blk).start())
        per_expert(lambda blk: zero_copy(blk).wait())
        per_tail(lambda blk: zero_copy(blk).wait())

    fetch.wait()

    def issue(g, carry):
        t0 = pl.multiple_of(g * 8, 8)
        for j in range(8):
            for kk in range(TOP_K):
                _row_copy(h_ref, t0 + j, xs_ref, slot_s[kk * tb + t0 + j], sem).start(priority=kk % 2)
        return carry

    lax.fori_loop(0, tb // 8, issue, 0)
    for kk in range(TOP_K):
        pltpu.make_async_copy(h_ref, xs_ref.at[pl.ds(0, tb), :], sem).wait()


def _dispatch_call(pad_end, n_used, slot_blocks, h2, n_blocks):
    n_tok, d = h2.shape
    nblk, _, ktb = slot_blocks.shape
    tb = ktb // TOP_K
    tm = TM_EXPERT
    grid_spec = pltpu.PrefetchScalarGridSpec(
        num_scalar_prefetch=2,
        grid=(nblk,),
        in_specs=[pl.BlockSpec(memory_space=pl.ANY),
                  pl.BlockSpec((tb, d), lambda i, pe, nu: (i, 0))],
        out_specs=pl.BlockSpec(memory_space=pl.ANY),
        scratch_shapes=[pltpu.SMEM((ktb,), jnp.int32), pltpu.VMEM((tm, d), h2.dtype),
                        pltpu.SemaphoreType.DMA, pltpu.SemaphoreType.DMA, pltpu.SemaphoreType.DMA],
    )
    return pl.pallas_call(
        functools.partial(_dispatch_kernel, n_blocks=n_blocks),
        grid_spec=grid_spec,
        out_shape=jax.ShapeDtypeStruct((n_blocks * tm, d), h2.dtype),
        compiler_params=_cparams("arbitrary"),
    )(pad_end, n_used, slot_blocks, h2)


def _routing_tables(idx_t, n_tok):
    tm = TM_EXPERT
    tb = TM_ROWS
    assert n_tok % tb == 0
    blk = pl.BlockSpec((TOP_K, tb), lambda i: (0, i))
    rank_t, counts = pl.pallas_call(
        _rank_kernel,
        grid=(n_tok // tb,),
        in_specs=[blk],
        out_specs=[blk, pl.BlockSpec((N_EXPERTS, 1), lambda i: (0, 0))],
        out_shape=[jax.ShapeDtypeStruct((TOP_K, n_tok), jnp.int32),
                   jax.ShapeDtypeStruct((N_EXPERTS, 1), jnp.int32)],
        scratch_shapes=[pltpu.VMEM((N_EXPERTS, 1), F32)],
        compiler_params=_cparams("arbitrary"),
    )(idx_t)
    counts = counts[:, 0]
    padded = (counts + tm - 1) // tm * tm
    pad_end = jnp.cumsum(padded)
    pad_start = pad_end - padded
    slot_blocks = pl.pallas_call(
        _slot_kernel,
        grid=(n_tok // tb,),
        in_specs=[blk, blk, pl.BlockSpec((N_EXPERTS, 1), lambda i: (0, 0))],
        out_specs=pl.BlockSpec((None, 1, TOP_K * tb), lambda i: (i, 0, 0)),
        out_shape=jax.ShapeDtypeStruct((n_tok // tb, 1, TOP_K * tb), jnp.int32),
        compiler_params=_cparams("arbitrary"),
    )(idx_t, rank_t, pad_start[:, None])
    n_blocks = -(-TOP_K * n_tok // tm) + N_EXPERTS
    first_row = jnp.arange(n_blocks, dtype=jnp.int32) * tm
    block_e = jnp.minimum(jnp.sum(pad_end[None, :] <= first_row[:, None], axis=1),
                          N_EXPERTS - 1).astype(jnp.int32)
    n_used = (pad_end[-1:] // tm).astype(jnp.int32)
    return pad_end.astype(jnp.int32), block_e, n_used, slot_blocks, n_blocks


def _combine_kernel(slot_hbm, yb_hbm, p_ref, x_ref, gate_ref, fg_ref, o_ref, slot_s, ybuf, sem, ssem, *,
                    final_norm, blk0):
    i = pl.program_id(0)
    n = pl.num_programs(0)
    tb = x_ref.shape[0]

    def request(blk, buf):
        fetch = pltpu.make_async_copy(slot_hbm.at[blk0 + blk, 0], slot_s, ssem)
        fetch.start()
        fetch.wait()

        def issue(g, carry):
            t0 = pl.multiple_of(g * 8, 8)
            for j in range(8):
                for kk in range(TOP_K):
                    _row_copy(yb_hbm, slot_s[kk * tb + t0 + j], ybuf.at[buf, kk], t0 + j,
                              sem.at[buf]).start(priority=kk % 2)
            return carry

        lax.fori_loop(0, tb // 8, issue, 0)

    @pl.when(i == 0)
    def _():
        request(0, 0)

    @pl.when(i + 1 < n)
    def _():
        request(i + 1, (i + 1) % 2)

    cur = i % 2
    for kk in range(TOP_K):
        pltpu.make_async_copy(yb_hbm.at[pl.ds(0, tb), :], ybuf.at[cur, kk], sem.at[cur]).wait()
    f_lo = jnp.zeros(ybuf.shape[2:], F32)
    f_hi = jnp.zeros(ybuf.shape[2:], F32)
    for kk in range(TOP_K):
        lo, hi = _unpack_bf16_pairs(ybuf[cur, kk])
        f_lo = f_lo + lo * p_ref[:, kk:kk + 1]
        f_hi = f_hi + hi * p_ref[:, kk:kk + 1]
    xn = x_ref[...] + gate_ref[...] * jnp.concatenate([f_lo, f_hi], axis=1)
    if final_norm:
        xn = _rms(xn, fg_ref[...])
    o_ref[...] = xn


def _combine_call(slot_blocks, yb, probs, x, mods, group_of_block, final_g, tok0, final_norm):
    bsz, n, d = x.shape
    ktb = slot_blocks.shape[2]
    tb = ktb // TOP_K
    rows = bsz * n
    assert rows % tb == 0 and tok0 % tb == 0 and (n % tb == 0 or tb % n == 0)
    blk0 = tok0 // tb
    out = pl.pallas_call(
        functools.partial(_combine_kernel, final_norm=final_norm, blk0=blk0),
        grid=(rows // tb,),
        in_specs=[pl.BlockSpec(memory_space=pl.ANY),
                  pl.BlockSpec(memory_space=pl.ANY),
                  pl.BlockSpec((tb, TOP_K), lambda i: (blk0 + i, 0)),
                  pl.BlockSpec((tb, d), lambda i: (i, 0)),
                  pl.BlockSpec((None, None, 1, d), lambda i: (group_of_block(i), 5, 0, 0)),
                  pl.BlockSpec((1, d), lambda i: (0, 0))],
        out_specs=pl.BlockSpec((tb, d), lambda i: (i, 0)),
        out_shape=jax.ShapeDtypeStruct((rows, d), F32),
        scratch_shapes=[pltpu.SMEM((ktb,), jnp.int32), pltpu.VMEM((2, TOP_K, tb, yb.shape[1]), yb.dtype),
                        pltpu.SemaphoreType.DMA((2,)), pltpu.SemaphoreType.DMA],
        compiler_params=_cparams("arbitrary"),
    )(slot_blocks, yb, probs, x.reshape(rows, d), mods, final_g.reshape(1, d))
    return out.reshape(bsz, n, d)


def kernel(x, c, ctx, c_ctx, norm1_g, norm2_g, w_mod, b_mod, w_in, gla_wg2_f, gla_bg_f, gla_wg2_b,
           gla_bg_b, gla_norm_g, conv_w, conv_b, conv_ln_g, conv_ln_b, w_out, router_w, router_b,
           exp_w_gu, exp_b_gu, exp_w_dn, exp_b_dn, final_norm_g):
    bsz, seq, d = x.shape
    n_ctx = ctx.shape[1]
    depth = w_mod.shape[0]
    assert d == D_MODEL and seq % (FFT_N2 * FFT_KB) == 0 and seq % n_ctx == 0
    lt = seq + n_ctx
    ctx_group = bsz

    rows = 8
    cvec = jnp.concatenate([c, c_ctx[None, :], jnp.zeros((rows - bsz - 1, d), F32)], axis=0)
    mods_all = _mod_call(cvec, w_mod, b_mod).reshape(depth, rows, 6, 1, d)

    def pack_w_in(w):
        o = [0, 256, 512, 768, 1280, 1792, 1808, 1824, 2336]
        parts = [w[:, o[0]:o[1]], w[:, o[1]:o[2]], w[:, o[2]:o[3]], w[:, o[3]:o[4]], w[:, o[4]:o[5]],
                 w[:, o[7]:o[7] + CONV_WIDTH], w[:, o[7] + CONV_WIDTH:o[8]], w[:, o[5]:o[7]],
                 jnp.zeros((d, GLR_PAD - 2 * GLA_GATE_RANK), w.dtype)]
        return jnp.concatenate(parts, axis=1).astype(BF16)

    long_tables = _dft_tables(seq)
    chan = _channel_tables()
    x_lat, x_ctx = x, ctx
    lat_group = lambda b: b
    ctx_group_fn = lambda b: ctx_group

    for layer in range(depth):
        last = layer == depth - 1
        mods = mods_all[layer]
        w_in_p = pack_w_in(w_in[layer])
        wg_pad = jnp.zeros((GLR_PAD, 2 * GLA_KEY_WIDTH), F32)
        wg_pad = wg_pad.at[:GLA_GATE_RANK, :GLA_KEY_WIDTH].set(gla_wg2_f[layer])
        wg_pad = wg_pad.at[GLA_GATE_RANK:2 * GLA_GATE_RANK, GLA_KEY_WIDTH:].set(gla_wg2_b[layer])
        bg_cat = jnp.concatenate([gla_bg_f[layer], gla_bg_b[layer]])[None, :]
        w_out_b = w_out[layer].astype(BF16)
        gn_tiled = jnp.tile(gla_norm_g[layer], GLA_HEADS)[None, :]
        rw_t = router_w[layer].T
        rw_hi = rw_t.astype(BF16)
        rw_lo = (rw_t - rw_hi.astype(F32)).astype(BF16)
        rb = router_b[layer][:, None]

        (u_l, r_l, glu_l), comb = _inproj_call(x_lat, mods, lat_group, norm1_g[layer], w_in_p, lt, 0)
        (u_c, r_c, glu_c), comb = _inproj_call(x_ctx, mods, ctx_group_fn, norm1_g[layer], w_in_p, lt,
                                               seq, combined=comb)
        o_f, o_b = _gla_call(*comb, wg_pad, bg_cat, seq, n_ctx)
        yf_l = _fourier_long(u_l, long_tables, chan)
        cv_l = _conv_call(glu_l, seq // (seq // GRID_W), conv_w[layer], conv_b[layer],
                          conv_ln_g[layer], conv_ln_b[layer])
        n_tok = bsz * seq + (0 if last else bsz * n_ctx)
        x_lat, routed = _outproj_call(yf_l, o_f, o_b, 0, r_l, cv_l, x_lat, mods, lat_group, gn_tiled,
                                      w_out_b, norm2_g[layer], rw_hi, rw_lo, rb, n_tok, 0)
        if not last:
            yf_c = _fourier_short(u_c, chan)
            cv_c = _conv_call(glu_c, n_ctx, conv_w[layer], conv_b[layer], conv_ln_g[layer],
                              conv_ln_b[layer])
            x_ctx, routed = _outproj_call(yf_c, o_f, o_b, seq, r_c, cv_c, x_ctx, mods, ctx_group_fn,
                                          gn_tiled, w_out_b, norm2_g[layer], rw_hi, rw_lo, rb, n_tok,
                                          bsz * seq, carried=routed)

        h2, idx_t, prob_t = routed
        pad_end, block_e, n_used, slot_blocks, n_blocks = _routing_tables(idx_t, n_tok)
        xs = _dispatch_call(pad_end, n_used, slot_blocks, h2, n_blocks)
        yb = _expert_call(block_e, n_used, xs, layer, exp_w_gu, exp_b_gu, exp_w_dn, exp_b_dn)
        probs = prob_t.T
        lat_blocks = seq // (slot_blocks.shape[2] // TOP_K)
        x_lat = _combine_call(slot_blocks, yb, probs, x_lat, mods, lambda i: i // lat_blocks,
                              final_norm_g, 0, last)
        if not last:
            x_ctx = _combine_call(slot_blocks, yb, probs, x_ctx, mods, lambda i: ctx_group,
                                  final_norm_g, bsz * seq, False)

    return x_lat
```

```python
import functools

import jax
import jax.numpy as jnp
from jax import lax
from jax.experimental import pallas as pl
from jax.experimental.pallas import tpu as pltpu

F32 = jnp.float32
BF16 = jnp.bfloat16
U32 = jnp.uint32
HIGH_HALF = 0xFFFF0000

D_MODEL = 1024
DEPTH = 2
GRID_W = 64
FOURIER_WIDTH = 256
FOURIER_HEADS = 4
FOURIER_HEAD_DIM = FOURIER_WIDTH // FOURIER_HEADS
GLA_HEADS = 4
GLA_KEY_WIDTH = 256
GLA_VALUE_WIDTH = 512
GLA_DK = GLA_KEY_WIDTH // GLA_HEADS
GLA_DV = GLA_VALUE_WIDTH // GLA_HEADS
GLA_GATE_RANK = 16
GLA_GATE_NORMALIZER = 16.0
CONV_WIDTH = 256
CONV_KERNEL = 31
N_EXPERTS = 32
TOP_K = 4
D_FF = D_MODEL
SWIGLU_LIMIT = 7.0
SWIGLU_ALPHA = 1.702
NORM_EPS = 1e-6

LANES = 128
VMEM_LIMIT = 56 * 1024 * 1024

COL_U = 0
COL_Q = COL_U + FOURIER_WIDTH
COL_K = COL_Q + GLA_KEY_WIDTH
COL_V = COL_K + GLA_KEY_WIDTH
COL_R = COL_V + GLA_VALUE_WIDTH
COL_CA = COL_R + GLA_VALUE_WIDTH
COL_CG = COL_CA + CONV_WIDTH
COL_GL = COL_CG + CONV_WIDTH
GLR_PAD = LANES
IN_PAD = COL_GL + GLR_PAD

GLA_CHUNK = 128
GLA_STEP_CHUNKS = 2
TM_ROWS = 512
TM_EXPERT = 512
EXPERT_CHUNKS = 4
FFT_N2 = 128
FFT_KB = 8
CONV_HALO = 16
CONV_GROUP = 8
CONV_PITCH = 100


def _cparams(*sem):
    return pltpu.CompilerParams(dimension_semantics=sem, vmem_limit_bytes=VMEM_LIMIT)


def _dot(a, b):
    return jnp.dot(a, b, preferred_element_type=F32)


def _dot_nt(a, b):
    return lax.dot_general(a, b, (((1,), (1,)), ((), ())), preferred_element_type=F32)


def _dot_tn(a, b):
    return lax.dot_general(a, b, (((0,), (0,)), ((), ())), preferred_element_type=F32)


def _split(a):
    hi = a.astype(BF16)
    lo = (a - hi.astype(F32)).astype(BF16)
    return hi, lo


def _dot3(a, b):
    ah, al = _split(a)
    bh, bl = _split(b)
    return _dot(ah, bh) + _dot(ah, bl) + _dot(al, bh)


def _sigmoid(x):
    return 1.0 / (1.0 + jnp.exp(-x))


def _pack_bf16_pairs(a):
    h = a.shape[1] // 2
    bits = lax.bitcast_convert_type(a.astype(BF16).astype(F32), U32)
    return (bits[:, :h] >> 16) | (bits[:, h:] & U32(HIGH_HALF))


def _unpack_bf16_pairs(w):
    return (lax.bitcast_convert_type(w << 16, F32), lax.bitcast_convert_type(w & U32(HIGH_HALF), F32))


def _rms(x, g):
    ms = jnp.mean(x * x, axis=-1, keepdims=True)
    return x * lax.rsqrt(ms + NORM_EPS) * g


def _mod_kernel(cv_ref, w_ref, b_ref, o_ref):
    cv = cv_ref[...]
    a = cv * _sigmoid(cv)
    o_ref[...] = _dot3(a, w_ref[...]) + b_ref[...]


def _mod_call(cvec, w_mod, b_mod):
    depth, d, n = w_mod.shape
    rows = cvec.shape[0]
    tn = 1536
    return pl.pallas_call(
        _mod_kernel,
        grid=(depth, n // tn),
        in_specs=[
            pl.BlockSpec((rows, d), lambda l, j: (0, 0)),
            pl.BlockSpec((None, d, tn), lambda l, j: (l, 0, j)),
            pl.BlockSpec((None, 1, tn), lambda l, j: (l, 0, j)),
        ],
        out_specs=pl.BlockSpec((None, rows, tn), lambda l, j: (l, 0, j)),
        out_shape=jax.ShapeDtypeStruct((depth, rows, n), F32),
        compiler_params=_cparams("arbitrary", "arbitrary"),
    )(cvec, w_mod, b_mod.reshape(depth, 1, n))


def _inproj_kernel(x_ref, g_ref, sh_ref, sc_ref, w_ref, *rest):
    u_ref, r_ref, glu_ref, q_ref, k_ref, v_ref, gl_ref = rest[-7:]
    x = x_ref[...]
    h = _rms(x, g_ref[...]) * (1.0 + sc_ref[...]) + sh_ref[...]
    p = _dot(h.astype(BF16), w_ref[...])
    u_ref[...] = p[:, COL_U:COL_Q].astype(u_ref.dtype)
    q_ref[...] = (p[:, COL_Q:COL_K] * (GLA_DK ** -0.5)).astype(q_ref.dtype)
    k_ref[...] = p[:, COL_K:COL_V].astype(k_ref.dtype)
    v_ref[...] = p[:, COL_V:COL_R].astype(v_ref.dtype)
    r_ref[...] = p[:, COL_R:COL_CA].astype(r_ref.dtype)
    glu_ref[...] = (p[:, COL_CA:COL_CG] * _sigmoid(p[:, COL_CG:COL_GL])).astype(glu_ref.dtype)
    gl_ref[...] = p[:, COL_GL:IN_PAD]


def _inproj_call(x, mods, group_of_batch, norm_g, w_in_p, lt, row0, combined=None):
    bsz, n, d = x.shape
    tm = min(TM_ROWS, n)
    assert n % tm == 0 and row0 % tm == 0
    blk0 = row0 // tm
    nb = n // tm
    steps = nb + (-(-(lt - n) // tm) if combined is None else 0)
    widths = (GLA_KEY_WIDTH, GLA_KEY_WIDTH, GLA_VALUE_WIDTH, GLR_PAD)
    dtypes = (BF16, BF16, BF16, F32)
    row_spec = lambda w: pl.BlockSpec((None, tm, w), lambda b, i: (b, jnp.minimum(i, nb - 1), 0))
    comb_spec = lambda w: pl.BlockSpec((None, tm, w), lambda b, i: (b, blk0 + i, 0))
    mod_spec = lambda which: pl.BlockSpec(
        (None, None, 1, d), lambda b, i: (group_of_batch(b), which, 0, 0))
    in_specs = [
        row_spec(d),
        pl.BlockSpec((1, d), lambda b, i: (0, 0)),
        mod_spec(0), mod_spec(1),
        pl.BlockSpec((d, IN_PAD), lambda b, i: (0, 0)),
    ]
    args = [x, norm_g.reshape(1, d), mods, mods, w_in_p]
    aliases = {}
    if combined is not None:
        for t, arr in enumerate(combined):
            in_specs.append(pl.BlockSpec(memory_space=pl.ANY))
            aliases[len(args)] = 3 + t
            args.append(arr)
    out_shape = [
        jax.ShapeDtypeStruct((bsz, n, FOURIER_WIDTH), BF16),
        jax.ShapeDtypeStruct((bsz, n, GLA_VALUE_WIDTH), BF16),
        jax.ShapeDtypeStruct((bsz, n, CONV_WIDTH), BF16),
    ] + [jax.ShapeDtypeStruct((bsz, lt, w), dt) for w, dt in zip(widths, dtypes)]
    out_specs = [row_spec(FOURIER_WIDTH), row_spec(GLA_VALUE_WIDTH), row_spec(CONV_WIDTH)] + [
        comb_spec(w) for w in widths]
    outs = pl.pallas_call(
        _inproj_kernel,
        grid=(bsz, steps),
        in_specs=in_specs,
        out_specs=out_specs,
        out_shape=out_shape,
        input_output_aliases=aliases,
        compiler_params=_cparams("arbitrary", "arbitrary"),
    )(*args)
    return outs[:3], outs[3:]


def _dft_tables(length):
    n2 = FFT_N2
    n1 = length // n2
    two_pi = 2.0 * jnp.pi

    def cs(num, den):
        ang = (num % den).astype(F32) * (two_pi / den)
        return jnp.cos(ang), jnp.sin(ang)

    k1 = jnp.arange(n1, dtype=jnp.int32)
    c1, s1 = cs(k1[:, None] * k1[None, :], n1)
    stage1 = (jnp.concatenate([c1, -s1], axis=0) * (n1 ** -0.5)).astype(BF16)
    k2 = jnp.arange(n2, dtype=jnp.int32)
    ct, st = cs(k1[:, None] * k2[None, :], length)
    cf, sf = cs(k2[:, None] * k2[None, :], n2)
    scale = n2 ** -0.5
    mr = (ct[:, None, :] * cf[None] - st[:, None, :] * sf[None]) * scale
    mi = -(st[:, None, :] * cf[None] + ct[:, None, :] * sf[None]) * scale
    stage2 = jnp.concatenate([jnp.concatenate([mr, -mi], axis=2),
                              jnp.concatenate([mi, mr], axis=2)], axis=1).astype(BF16)
    return stage1, stage2


def _channel_tables():
    hd = FOURIER_HEAD_DIM
    c = jnp.arange(FOURIER_WIDTH, dtype=jnp.int32)
    same_head = (c[:, None] // hd) == (c[None, :] // hd)
    ang = (((c[:, None] % hd) * (c[None, :] % hd)) % hd).astype(F32) * (2.0 * jnp.pi / hd)
    scale = hd ** -0.5
    bdc = jnp.where(same_head, jnp.cos(ang) * scale, 0.0).astype(BF16)
    bds = jnp.where(same_head, jnp.sin(ang) * scale, 0.0).astype(BF16)
    return bdc, bds


def _fft1_kernel(x_ref, cs_ref, zr_ref, zi_ref):
    n1 = x_ref.shape[0]
    z = _dot(cs_ref[...], x_ref[...])
    zr_ref[...] = z[:n1].astype(zr_ref.dtype)
    zi_ref[...] = z[n1:].astype(zi_ref.dtype)


def _fft2_kernel(zr_ref, zi_ref, m_ref, bdc_ref, bds_ref, o_ref):
    kb, n2, w = zr_ref.shape
    for j in range(kb):
        z = jnp.concatenate([zr_ref[j], zi_ref[j]], axis=0)
        a = _dot(m_ref[j], z)
        y = _dot(a[:n2].astype(BF16), bdc_ref[...]) + _dot(a[n2:].astype(BF16), bds_ref[...])
        o_ref[:, j * w:(j + 1) * w] = y.astype(o_ref.dtype)


def _fourier_long(u, tables, chan):
    bsz, length, w = u.shape
    stage1, stage2 = tables
    bdc, bds = chan
    n2 = FFT_N2
    n1 = length // n2
    tn = 4096
    cols = n2 * w
    zr, zi = pl.pallas_call(
        _fft1_kernel,
        grid=(bsz, cols // tn),
        in_specs=[pl.BlockSpec((None, n1, tn), lambda b, j: (b, 0, j)),
                  pl.BlockSpec((2 * n1, n1), lambda b, j: (0, 0))],
        out_specs=[pl.BlockSpec((None, n1, tn), lambda b, j: (b, 0, j))] * 2,
        out_shape=[jax.ShapeDtypeStruct((bsz, n1, cols), BF16)] * 2,
        compiler_params=_cparams("arbitrary", "arbitrary"),
    )(u.reshape(bsz, n1, cols), stage1)
    kb = FFT_KB
    z_spec = pl.BlockSpec((None, kb, n2, w), lambda b, j: (b, j, 0, 0))
    y = pl.pallas_call(
        _fft2_kernel,
        grid=(bsz, n1 // kb),
        in_specs=[z_spec, z_spec,
                  pl.BlockSpec((kb, 2 * n2, 2 * n2), lambda b, j: (j, 0, 0)),
                  pl.BlockSpec((w, w), lambda b, j: (0, 0)),
                  pl.BlockSpec((w, w), lambda b, j: (0, 0))],
        out_specs=pl.BlockSpec((None, n2, kb * w), lambda b, j: (b, 0, j)),
        out_shape=jax.ShapeDtypeStruct((bsz, n2, n1 * w), BF16),
        compiler_params=_cparams("arbitrary", "arbitrary"),
    )(zr.reshape(bsz, n1, n2, w), zi.reshape(bsz, n1, n2, w), stage2, bdc, bds)
    return y.reshape(bsz, length, w)


def _dft_short_kernel(u_ref, c_ref, s_ref, bdc_ref, bds_ref, o_ref):
    u = u_ref[...]
    p = _dot(u, bdc_ref[...]).astype(BF16)
    q = _dot(u, bds_ref[...]).astype(BF16)
    o_ref[...] = (_dot(c_ref[...], p) - _dot(s_ref[...], q)).astype(o_ref.dtype)


def _fourier_short(u, chan):
    bsz, length, w = u.shape
    bdc, bds = chan
    k = jnp.arange(length, dtype=jnp.int32)
    ang = ((k[:, None] * k[None, :]) % length).astype(F32) * (2.0 * jnp.pi / length)
    c = (jnp.cos(ang) * length ** -0.5).astype(BF16)
    s = (jnp.sin(ang) * length ** -0.5).astype(BF16)
    full = lambda n: pl.BlockSpec((n, n), lambda b: (0, 0))
    return pl.pallas_call(
        _dft_short_kernel,
        grid=(bsz,),
        in_specs=[pl.BlockSpec((None, length, w), lambda b: (b, 0, 0)),
                  full(length), full(length), full(w), full(w)],
        out_specs=pl.BlockSpec((None, length, w), lambda b: (b, 0, 0)),
        out_shape=jax.ShapeDtypeStruct((bsz, length, w), BF16),
        compiler_params=_cparams("arbitrary"),
    )(u, c, s, bdc, bds)


def _conv_kernel(x_ref, w_ref, cb_ref, lg_ref, lb_ref, o_ref, pad_ref, *, seg):
    nseg = x_ref.shape[0] // seg
    width = x_ref.shape[1]
    halo = jnp.zeros((CONV_HALO, width), F32)
    for s in range(nseg):
        pad_ref[s, 0:CONV_HALO, :] = halo
        pad_ref[s, CONV_HALO:CONV_HALO + seg, :] = x_ref[s * seg:(s + 1) * seg, :].astype(F32)
        pad_ref[s, CONV_HALO + seg:2 * CONV_HALO + seg, :] = halo
    first = CONV_HALO - CONV_KERNEL // 2
    sub = 8
    span = seg + 2 * CONV_HALO - sub
    for s in range(nseg):
        acc = jnp.zeros((seg, width), F32)
        for r in range(sub):
            shifted = pad_ref[s, r:r + span, :]
            for a in range((span - seg) // sub + 1):
                j = a * sub + r - first
                if 0 <= j < CONV_KERNEL:
                    acc = acc + shifted[a * sub:a * sub + seg, :] * w_ref[j:j + 1, :]
        y = acc + cb_ref[...]
        mu = jnp.mean(y, axis=-1, keepdims=True)
        yc = y - mu
        var = jnp.mean(yc * yc, axis=-1, keepdims=True)
        z = yc * lax.rsqrt(var + NORM_EPS) * lg_ref[...] + lb_ref[...]
        o_ref[s * seg:(s + 1) * seg, :] = (z * _sigmoid(z)).astype(o_ref.dtype)


def _conv_group_kernel(x_ref, w_ref, cb_ref, lg_ref, lb_ref, o_ref, pad_ref, acc_ref, *, seg):
    g, p = CONV_GROUP, CONV_PITCH
    halves = x_ref.shape[1] // LANES
    pad_ref[...] = jnp.zeros_like(pad_ref)
    for s in range(g):
        for h in range(halves):
            pad_ref[h, s * p + CONV_HALO:s * p + CONV_HALO + seg, :] = (
                x_ref[s * seg:(s + 1) * seg, h * LANES:(h + 1) * LANES].astype(F32))
    first = CONV_HALO - CONV_KERNEL // 2
    unroll = 4

    def body(i, carry):
        t0 = unroll * i
        for h in range(halves):
            accs = [None] * unroll
            for j in range(CONV_KERNEL):
                tap = w_ref[h, j]
                for u in range(unroll):
                    term = pad_ref[h, pl.ds(t0 + u + first + j, g, stride=p), :] * tap
                    accs[u] = term if accs[u] is None else accs[u] + term
            for u in range(unroll):
                acc_ref[h, pl.ds(t0 + u, g, stride=p), :] = accs[u]
        return carry

    lax.fori_loop(0, seg // unroll, body, 0)
    for s in range(g):
        y = jnp.concatenate([acc_ref[h, s * p:s * p + seg, :] for h in range(halves)], axis=1) + cb_ref[...]
        mu = jnp.mean(y, axis=-1, keepdims=True)
        yc = y - mu
        var = jnp.mean(yc * yc, axis=-1, keepdims=True)
        z = yc * lax.rsqrt(var + NORM_EPS) * lg_ref[...] + lb_ref[...]
        o_ref[s * seg:(s + 1) * seg, :] = (z * _sigmoid(z)).astype(o_ref.dtype)


def _conv_call(glu, seg, conv_w, conv_b, ln_g, ln_b):
    bsz, n, w = glu.shape
    t = max(seg, min(TM_ROWS, n))
    assert n % t == 0 and t % seg == 0
    vec = lambda: pl.BlockSpec((1, w), lambda b, i: (0, 0))
    if t // seg == CONV_GROUP and seg + 2 * CONV_HALO <= CONV_PITCH and w % LANES == 0:
        halves = w // LANES
        taps = jnp.broadcast_to(conv_w.reshape(CONV_KERNEL, halves, 1, LANES).transpose(1, 0, 2, 3),
                                (halves, CONV_KERNEL, 8, LANES))
        slab = pltpu.VMEM((halves, CONV_GROUP * CONV_PITCH, LANES), F32)
        return pl.pallas_call(
            functools.partial(_conv_group_kernel, seg=seg),
            grid=(bsz, n // t),
            in_specs=[pl.BlockSpec((None, t, w), lambda b, i: (b, i, 0)),
                      pl.BlockSpec(taps.shape, lambda b, i: (0, 0, 0, 0)),
                      vec(), vec(), vec()],
            out_specs=pl.BlockSpec((None, t, w), lambda b, i: (b, i, 0)),
            out_shape=jax.ShapeDtypeStruct((bsz, n, w), BF16),
            scratch_shapes=[slab, slab],
            compiler_params=_cparams("arbitrary", "arbitrary"),
        )(glu, taps, conv_b.reshape(1, w), ln_g.reshape(1, w), ln_b.reshape(1, w))
    return pl.pallas_call(
        functools.partial(_conv_kernel, seg=seg),
        grid=(bsz, n // t),
        in_specs=[pl.BlockSpec((None, t, w), lambda b, i: (b, i, 0)),
                  pl.BlockSpec((CONV_KERNEL, w), lambda b, i: (0, 0)),
                  vec(), vec(), vec()],
        out_specs=pl.BlockSpec((None, t, w), lambda b, i: (b, i, 0)),
        out_shape=jax.ShapeDtypeStruct((bsz, n, w), BF16),
        scratch_shapes=[pltpu.VMEM((t // seg, seg + 2 * CONV_HALO, w), F32)],
        compiler_params=_cparams("arbitrary", "arbitrary"),
    )(glu, conv_w, conv_b.reshape(1, w), ln_g.reshape(1, w), ln_b.reshape(1, w))


def _gla_direction(q_ref, k_ref, v_ref, gl_ref, wg_ref, bg_ref, o_ref, st, reverse):
    c = GLA_CHUNK
    n_sub = q_ref.shape[0] // c
    for s in (reversed(range(n_sub)) if reverse else range(n_sub)):
        rows = slice(s * c, (s + 1) * c)
        st = _gla_chunk(q_ref[rows, :], k_ref[rows, :], v_ref[rows, :], gl_ref[rows, :], wg_ref, bg_ref,
                        o_ref.at[rows, :], st, reverse)
    return st


def _gla_chunk(q, k, v, glr, wg_ref, bg_ref, o_ref, st, reverse):
    c = GLA_CHUNK
    kw = GLA_KEY_WIDTH
    col0 = kw if reverse else 0
    pre = _dot3(glr, wg_ref[:, col0:col0 + kw]) + bg_ref[:, col0:col0 + kw]
    g = (jnp.minimum(pre, 0.0) - jnp.log(1.0 + jnp.exp(-jnp.abs(pre)))) * (1.0 / GLA_GATE_NORMALIZER)
    row = lax.broadcasted_iota(jnp.int32, (c, c), 0)
    col = lax.broadcasted_iota(jnp.int32, (c, c), 1)
    seen = (col >= row) if reverse else (col <= row)
    tri = jnp.where(seen, 1.0, 0.0).astype(BF16)
    gh, gl = _split(g)
    b = _dot(tri, gh) + _dot(tri, gl)
    mid = c // 2 if reverse else c // 2 - 1
    last = 0 if reverse else c - 1
    b_mid = b[mid:mid + 1, :]
    b_last = b[last:last + 1, :]
    q = q.astype(F32)
    k = k.astype(F32)
    qe = q * jnp.exp(b - b_mid)
    ke = k * jnp.exp(b_mid - b)
    kd = k * jnp.exp(b_last - b)
    head_of_lane = lax.broadcasted_iota(jnp.int32, (1, kw), 1) // GLA_DK
    q_heads = jnp.concatenate(
        [jnp.where(head_of_lane == h, qe, 0.0) for h in range(GLA_HEADS)], axis=0).astype(BF16)
    rhs = jnp.concatenate([ke, st * jnp.exp(b_mid)], axis=0).astype(BF16)
    res = _dot_nt(q_heads, rhs)
    outs = []
    for h in range(GLA_HEADS):
        blk = res[h * c:(h + 1) * c, :]
        scores = jnp.where(seen, blk[:, :c], 0.0).astype(BF16)
        outs.append(_dot(scores, v[:, h * GLA_DV:(h + 1) * GLA_DV]) + blk[:, c:])
    o_ref[...] = jnp.concatenate(outs, axis=1).astype(o_ref.dtype)
    kv = _dot_tn(v, kd.astype(BF16))
    ds = jnp.zeros_like(st)
    for h in range(GLA_HEADS):
        ds = ds + jnp.where(head_of_lane == h, kv[h * GLA_DV:(h + 1) * GLA_DV, :], 0.0)
    return st * jnp.exp(b_last) + ds


def _gla_kernel(qf, kf, vf, gf, qb, kb, vb, gb, wg_ref, bg_ref, of_ref, ob_ref, sf_ref, sb_ref):
    @pl.when(pl.program_id(0) == 0)
    def _():
        sf_ref[...] = jnp.zeros_like(sf_ref)
        sb_ref[...] = jnp.zeros_like(sb_ref)

    bsz = qf.shape[0]
    states = [(sf_ref[b], sb_ref[b]) for b in range(bsz)]
    new = []
    for b, (s_f, s_b) in enumerate(states):
        new.append((
            _gla_direction(qf.at[b], kf.at[b], vf.at[b], gf.at[b], wg_ref, bg_ref, of_ref.at[b], s_f, False),
            _gla_direction(qb.at[b], kb.at[b], vb.at[b], gb.at[b], wg_ref, bg_ref, ob_ref.at[b], s_b, True)))
    for b, (s_f, s_b) in enumerate(new):
        sf_ref[b] = s_f
        sb_ref[b] = s_b


def _gla_call(q, k, v, gl, wg_pad, bg_cat, n_lat, n_ctx):
    bsz, lt, _ = q.shape
    c = GLA_STEP_CHUNKS * GLA_CHUNK
    assert n_lat % c == 0 and n_ctx % c == 0
    cl, cc = n_lat // c, n_ctx // c

    def fwd_blk(j):
        return jnp.where(j < cc, cl + j, j - cc)

    def bwd_blk(j):
        return jnp.where(j < cc, cl + cc - 1 - j, cl - 1 - (j - cc))

    def spec(w, blk):
        return pl.BlockSpec((bsz, c, w), lambda j: (0, blk(j), 0))

    widths = (GLA_KEY_WIDTH, GLA_KEY_WIDTH, GLA_VALUE_WIDTH, GLR_PAD)
    in_specs = [spec(w, fwd_blk) for w in widths] + [spec(w, bwd_blk) for w in widths] + [
        pl.BlockSpec(wg_pad.shape, lambda j: (0, 0)),
        pl.BlockSpec(bg_cat.shape, lambda j: (0, 0))]
    return pl.pallas_call(
        _gla_kernel,
        grid=(cl + cc,),
        in_specs=in_specs,
        out_specs=[spec(GLA_VALUE_WIDTH, fwd_blk), spec(GLA_VALUE_WIDTH, bwd_blk)],
        out_shape=[jax.ShapeDtypeStruct((bsz, lt, GLA_VALUE_WIDTH), F32)] * 2,
        scratch_shapes=[pltpu.VMEM((bsz, GLA_DV, GLA_KEY_WIDTH), F32)] * 2,
        compiler_params=_cparams("arbitrary"),
    )(q, k, v, gl, q, k, v, gl, wg_pad, bg_cat)


def _outproj_kernel(yf_ref, of_ref, ob_ref, r_ref, cv_ref, x_ref, gate_ref, sh_ref, sc_ref,
                    gn_ref, wo_ref, n2_ref, rwh_ref, rwl_ref, rb_ref, *rest):
    xo_ref, h2_ref, idx_ref, prob_ref = rest[-4:]
    o = of_ref[...] + ob_ref[...]
    heads = []
    for h in range(GLA_HEADS):
        oh = o[:, h * GLA_DV:(h + 1) * GLA_DV]
        heads.append(oh * lax.rsqrt(jnp.mean(oh * oh, axis=-1, keepdims=True) + NORM_EPS))
    r = r_ref[...].astype(F32)
    gla = jnp.concatenate(heads, axis=1) * gn_ref[...] * (r * _sigmoid(r))
    c0, c1 = FOURIER_WIDTH, FOURIER_WIDTH + GLA_VALUE_WIDTH
    y = (_dot(yf_ref[...], wo_ref[0:c0, :]) + _dot(gla.astype(BF16), wo_ref[c0:c1, :])
         + _dot(cv_ref[...], wo_ref[c1:, :]))
    xn = x_ref[...] + gate_ref[...] * y
    xo_ref[...] = xn
    h2 = _rms(xn, n2_ref[...]) * (1.0 + sc_ref[...]) + sh_ref[...]
    hh, hl = _split(h2)
    h2_ref[...] = _pack_bf16_pairs(h2)
    logits = (_dot_nt(rwh_ref[...], hh) + _dot_nt(rwh_ref[...], hl) + _dot_nt(rwl_ref[...], hh)
              + rb_ref[...])
    expert = lax.broadcasted_iota(jnp.int32, logits.shape, 0)
    vals, idxs = [], []
    cur = logits
    for _ in range(TOP_K):
        m = jnp.max(cur, axis=0, keepdims=True)
        ix = jnp.min(jnp.where(cur == m, expert, N_EXPERTS), axis=0, keepdims=True)
        vals.append(m)
        idxs.append(ix)
        cur = jnp.where(expert == ix, -jnp.inf, cur)
    es = [jnp.exp(vv - vals[0]) for vv in vals]
    inv = 1.0 / functools.reduce(lambda a, b: a + b, es)
    idx_ref[...] = jnp.concatenate(idxs, axis=0)
    prob_ref[...] = jnp.concatenate([e * inv for e in es], axis=0)


def _outproj_call(yf, o_f, o_b, o_row0, r, cv, x, mods, group_of_batch, gn_tiled, w_out, norm2_g,
                  rw_hi, rw_lo, rb, n_tok, tok0, carried=None):
    bsz, n, d = x.shape
    tm = min(TM_ROWS, n)
    assert n % tm == 0 and o_row0 % tm == 0 and tok0 % tm == 0
    nb = n // tm
    spare = (n_tok - bsz * n) if carried is None else 0
    assert spare in (0, tm)
    steps = nb + spare // tm
    last = lambda i: jnp.minimum(i, nb - 1)
    row = lambda w: pl.BlockSpec((None, tm, w), lambda b, i: (b, last(i), 0))
    orow = pl.BlockSpec((None, tm, GLA_VALUE_WIDTH), lambda b, i: (b, o_row0 // tm + last(i), 0))
    mod = lambda which: pl.BlockSpec((None, None, 1, d), lambda b, i: (group_of_batch(b), which, 0, 0))
    const = lambda a: pl.BlockSpec(a.shape, lambda b, i: (0,) * a.ndim)
    consts = [gn_tiled, w_out, norm2_g.reshape(1, d), rw_hi, rw_lo, rb]
    in_specs = [row(FOURIER_WIDTH), orow, orow, row(GLA_VALUE_WIDTH), row(CONV_WIDTH), row(d),
                mod(2), mod(3), mod(4)] + [const(a) for a in consts]
    args = [yf, o_f, o_b, r, cv, x, mods, mods, mods] + consts
    aliases = {}
    if carried is not None:
        for t, arr in enumerate(carried):
            in_specs.append(pl.BlockSpec(memory_space=pl.ANY))
            aliases[len(args)] = 1 + t
            args.append(arr)
    tokblk = lambda b, i: tok0 // tm + jnp.where(
        jnp.logical_and(i == nb, b == bsz - 1), bsz * nb, b * nb + last(i))
    out_specs = [row(d),
                 pl.BlockSpec((tm, d // 2), lambda b, i: (tokblk(b, i), 0)),
                 pl.BlockSpec((TOP_K, tm), lambda b, i: (0, tokblk(b, i))),
                 pl.BlockSpec((TOP_K, tm), lambda b, i: (0, tokblk(b, i)))]
    out_shape = [jax.ShapeDtypeStruct((bsz, n, d), F32),
                 jax.ShapeDtypeStruct((n_tok, d // 2), U32),
                 jax.ShapeDtypeStruct((TOP_K, n_tok), jnp.int32),
                 jax.ShapeDtypeStruct((TOP_K, n_tok), F32)]
    outs = pl.pallas_call(
        _outproj_kernel,
        grid=(bsz, steps),
        in_specs=in_specs,
        out_specs=out_specs,
        out_shape=out_shape,
        input_output_aliases=aliases,
        compiler_params=_cparams("arbitrary", "arbitrary"),
    )(*args)
    return outs[0], outs[1:]


def _expert_kernel(be_ref, nu_ref, x_ref, wgu_ref, bgu_ref, wdn_ref, bdn_ref, o_ref, wgu_s, wdn_s):
    i = pl.program_id(0)
    nu = nu_ref[0]
    n_chunks = wgu_s.shape[0]
    fc = D_FF // n_chunks

    @pl.when(i >= nu)
    def _():
        o_ref[...] = jnp.zeros_like(o_ref)

    @pl.when(i < nu)
    def _():
        changed = jnp.logical_or(i == 0, be_ref[i] != be_ref[jnp.maximum(i - 1, 0)])

        @pl.when(changed)
        def _():
            rows = 128
            for s in range(wgu_ref.shape[0] // rows):
                rs = slice(s * rows, (s + 1) * rows)
                for n in range(n_chunks):
                    wgu_s[n, rs, 0:fc] = wgu_ref[rs, n * fc:(n + 1) * fc].astype(BF16)
                    wgu_s[n, rs, fc:2 * fc] = wgu_ref[rs, D_FF + n * fc:D_FF + (n + 1) * fc].astype(BF16)
                wdn_s[rs, :] = wdn_ref[rs, :].astype(BF16)

        x = jnp.concatenate(_unpack_bf16_pairs(x_ref[...]), axis=1).astype(BF16)
        acts = []
        for n in range(n_chunks):
            gate = _dot(x, wgu_s[n, :, 0:fc]) + bgu_ref[:, n * fc:(n + 1) * fc]
            up = _dot(x, wgu_s[n, :, fc:2 * fc]) + bgu_ref[:, D_FF + n * fc:D_FF + (n + 1) * fc]
            gate = jnp.minimum(gate, SWIGLU_LIMIT)
            up = jnp.clip(up, -SWIGLU_LIMIT, SWIGLU_LIMIT)
            acts.append((gate * _sigmoid(SWIGLU_ALPHA * gate) * (up + 1.0)).astype(BF16))
        half = n_chunks // 2
        y = (_dot(jnp.concatenate(acts[:half], axis=1), wdn_s[0:half * fc, :])
             + _dot(jnp.concatenate(acts[half:], axis=1), wdn_s[half * fc:, :]) + bdn_ref[...])
        o_ref[...] = _pack_bf16_pairs(y)


def _expert_call(block_e, n_used, xs, layer, w_gu, b_gu, w_dn, b_dn):
    rows, dh = xs.shape
    d = 2 * dh
    tm = TM_EXPERT
    nblk = rows // tm
    depth, e, _, f2 = w_gu.shape
    n_chunks = EXPERT_CHUNKS
    live = lambda i, nu: jnp.minimum(i, nu[0] - 1)
    wmap = lambda i, be, nu: (layer, be[live(i, nu)], 0, 0)
    grid_spec = pltpu.PrefetchScalarGridSpec(
        num_scalar_prefetch=2,
        grid=(nblk,),
        in_specs=[
            pl.BlockSpec((tm, dh), lambda i, be, nu: (live(i, nu), 0)),
            pl.BlockSpec((None, None, d, f2), wmap),
            pl.BlockSpec((None, None, 1, f2), wmap),
            pl.BlockSpec((None, None, f2 // 2, d), wmap),
            pl.BlockSpec((None, None, 1, d), wmap),
        ],
        out_specs=pl.BlockSpec((tm, dh), lambda i, be, nu: (i, 0)),
        scratch_shapes=[pltpu.VMEM((n_chunks, d, f2 // n_chunks), BF16), pltpu.VMEM((f2 // 2, d), BF16)],
    )
    return pl.pallas_call(
        _expert_kernel,
        grid_spec=grid_spec,
        out_shape=jax.ShapeDtypeStruct((rows, dh), xs.dtype),
        compiler_params=_cparams("arbitrary"),
    )(block_e, n_used, xs, w_gu, b_gu.reshape(depth, e, 1, f2), w_dn, b_dn.reshape(depth, e, 1, d))


def _rank_kernel(idx_ref, rank_ref, cnt_ref, base_ref):
    @pl.when(pl.program_id(0) == 0)
    def _():
        base_ref[...] = jnp.zeros_like(base_ref)

    tb = idx_ref.shape[1]
    row = lax.broadcasted_iota(jnp.int32, (tb, tb), 0)
    col = lax.broadcasted_iota(jnp.int32, (tb, tb), 1)
    earlier = jnp.where(row < col, 1.0, 0.0).astype(BF16)
    expert = lax.broadcasted_iota(jnp.int32, (N_EXPERTS, tb), 0)
    base = base_ref[...]
    ranks = []
    for kk in range(TOP_K):
        hit = expert == idx_ref[kk:kk + 1, :]
        onehot = hit.astype(F32)
        before = _dot(onehot.astype(BF16), earlier) + base
        ranks.append(jnp.sum(before * onehot, axis=0, keepdims=True))
        base = base + jnp.sum(onehot, axis=1, keepdims=True)
    rank_ref[...] = jnp.concatenate(ranks, axis=0).astype(jnp.int32)
    base_ref[...] = base
    cnt_ref[...] = base.astype(jnp.int32)


def _slot_kernel(idx_ref, rank_ref, start_ref, slot_ref):
    expert = lax.broadcasted_iota(jnp.int32, (N_EXPERTS, idx_ref.shape[1]), 0)
    rows = []
    for kk in range(TOP_K):
        start = jnp.sum(jnp.where(expert == idx_ref[kk:kk + 1, :], start_ref[...], 0), axis=0,
                        keepdims=True)
        rows.append(rank_ref[kk:kk + 1, :] + start)
    slot_ref[...] = jnp.concatenate(rows, axis=1)


def _row_copy(src, s, dst, t, sem):
    return pltpu.make_async_copy(src.at[pl.ds(s, 1), :], dst.at[pl.ds(t, 1), :], sem)


def _dispatch_kernel(pe_ref, nu_ref, slot_hbm, h_ref, xs_ref, slot_s, zero_ref, sem, zsem, ssem, *,
                     n_blocks):
    i = pl.program_id(0)
    tb = h_ref.shape[0]
    tm = zero_ref.shape[0]
    fetch = pltpu.make_async_copy(slot_hbm.at[i, 0], slot_s, ssem)
    fetch.start()

    def zero_copy(blk):
        return pltpu.make_async_copy(zero_ref, xs_ref.at[pl.ds(pl.multiple_of(blk * tm, tm), tm), :], zsem)

    @pl.when(i == 0)
    def _():
        zero_ref[...] = jnp.zeros_like(zero_ref)

        def per_expert(fn):
            for e in range(N_EXPERTS):
                end = pe_ref[e]
                start = pe_ref[e - 1] if e else 0

                @pl.when(end > start)
                def _():
                    fn(end // tm - 1)

        def per_tail(fn):
            def body(blk, carry):
                fn(blk)
                return carry
            lax.fori_loop(nu_ref[0], n_blocks, body, 0)

        per_expert(lambda blk: zero_copy(blk).start())
        per_tail(lambda blk: zero_copy(blk).start())
        per_expert(lambda blk: zero_copy(blk).wait())
        per_tail(lambda blk: zero_copy(blk).wait())

    fetch.wait()

    def issue(g, carry):
        t0 = pl.multiple_of(g * 8, 8)
        for j in range(8):
            for kk in range(TOP_K):
                _row_copy(h_ref, t0 + j, xs_ref, slot_s[kk * tb + t0 + j], sem).start(priority=kk % 2)
        return carry

    lax.fori_loop(0, tb // 8, issue, 0)
    for kk in range(TOP_K):
        pltpu.make_async_copy(h_ref, xs_ref.at[pl.ds(0, tb), :], sem).wait()


def _dispatch_call(pad_end, n_used, slot_blocks, h2, n_blocks):
    n_tok, d = h2.shape
    nblk, _, ktb = slot_blocks.shape
    tb = ktb // TOP_K
    tm = TM_EXPERT
    grid_spec = pltpu.PrefetchScalarGridSpec(
        num_scalar_prefetch=2,
        grid=(nblk,),
        in_specs=[pl.BlockSpec(memory_space=pl.ANY),
                  pl.BlockSpec((tb, d), lambda i, pe, nu: (i, 0))],
        out_specs=pl.BlockSpec(memory_space=pl.ANY),
        scratch_shapes=[pltpu.SMEM((ktb,), jnp.int32), pltpu.VMEM((tm, d), h2.dtype),
                        pltpu.SemaphoreType.DMA, pltpu.SemaphoreType.DMA, pltpu.SemaphoreType.DMA],
    )
    return pl.pallas_call(
        functools.partial(_dispatch_kernel, n_blocks=n_blocks),
        grid_spec=grid_spec,
        out_shape=jax.ShapeDtypeStruct((n_blocks * tm, d), h2.dtype),
        compiler_params=_cparams("arbitrary"),
    )(pad_end, n_used, slot_blocks, h2)


def _routing_tables(idx_t, n_tok):
    tm = TM_EXPERT
    tb = TM_ROWS
    assert n_tok % tb == 0
    blk = pl.BlockSpec((TOP_K, tb), lambda i: (0, i))
    rank_t, counts = pl.pallas_call(
        _rank_kernel,
        grid=(n_tok // tb,),
        in_specs=[blk],
        out_specs=[blk, pl.BlockSpec((N_EXPERTS, 1), lambda i: (0, 0))],
        out_shape=[jax.ShapeDtypeStruct((TOP_K, n_tok), jnp.int32),
                   jax.ShapeDtypeStruct((N_EXPERTS, 1), jnp.int32)],
        scratch_shapes=[pltpu.VMEM((N_EXPERTS, 1), F32)],
        compiler_params=_cparams("arbitrary"),
    )(idx_t)
    counts = counts[:, 0]
    padded = (counts + tm - 1) // tm * tm
    pad_end = jnp.cumsum(padded)
    pad_start = pad_end - padded
    slot_blocks = pl.pallas_call(
        _slot_kernel,
        grid=(n_tok // tb,),
        in_specs=[blk, blk, pl.BlockSpec((N_EXPERTS, 1), lambda i: (0, 0))],
        out_specs=pl.BlockSpec((None, 1, TOP_K * tb), lambda i: (i, 0, 0)),
        out_shape=jax.ShapeDtypeStruct((n_tok // tb, 1, TOP_K * tb), jnp.int32),
        compiler_params=_cparams("arbitrary"),
    )(idx_t, rank_t, pad_start[:, None])
    n_blocks = -(-TOP_K * n_tok // tm) + N_EXPERTS
    first_row = jnp.arange(n_blocks, dtype=jnp.int32) * tm
    block_e = jnp.minimum(jnp.sum(pad_end[None, :] <= first_row[:, None], axis=1),
                          N_EXPERTS - 1).astype(jnp.int32)
    n_used = (pad_end[-1:] // tm).astype(jnp.int32)
    return pad_end.astype(jnp.int32), block_e, n_used, slot_blocks, n_blocks


def _combine_kernel(slot_hbm, yb_hbm, p_ref, x_ref, gate_ref, fg_ref, o_ref, slot_s, ybuf, sem, ssem, *,
                    final_norm, blk0):
    i = pl.program_id(0)
    n = pl.num_programs(0)
    tb = x_ref.shape[0]

    def request(blk, buf):
        fetch = pltpu.make_async_copy(slot_hbm.at[blk0 + blk, 0], slot_s, ssem)
        fetch.start()
        fetch.wait()

        def issue(g, carry):
            t0 = pl.multiple_of(g * 8, 8)
            for j in range(8):
                for kk in range(TOP_K):
                    _row_copy(yb_hbm, slot_s[kk * tb + t0 + j], ybuf.at[buf, kk], t0 + j,
                              sem.at[buf]).start(priority=kk % 2)
            return carry

        lax.fori_loop(0, tb // 8, issue, 0)

    @pl.when(i == 0)
    def _():
        request(0, 0)

    @pl.when(i + 1 < n)
    def _():
        request(i + 1, (i + 1) % 2)

    cur = i % 2
    for kk in range(TOP_K):
        pltpu.make_async_copy(yb_hbm.at[pl.ds(0, tb), :], ybuf.at[cur, kk], sem.at[cur]).wait()
    f_lo = jnp.zeros(ybuf.shape[2:], F32)
    f_hi = jnp.zeros(ybuf.shape[2:], F32)
    for kk in range(TOP_K):
        lo, hi = _unpack_bf16_pairs(ybuf[cur, kk])
        f_lo = f_lo + lo * p_ref[:, kk:kk + 1]
        f_hi = f_hi + hi * p_ref[:, kk:kk + 1]
    xn = x_ref[...] + gate_ref[...] * jnp.concatenate([f_lo, f_hi], axis=1)
    if final_norm:
        xn = _rms(xn, fg_ref[...])
    o_ref[...] = xn


def _combine_call(slot_blocks, yb, probs, x, mods, group_of_block, final_g, tok0, final_norm):
    bsz, n, d = x.shape
    ktb = slot_blocks.shape[2]
    tb = ktb // TOP_K
    rows = bsz * n
    assert rows % tb == 0 and tok0 % tb == 0 and (n % tb == 0 or tb % n == 0)
    blk0 = tok0 // tb
    out = pl.pallas_call(
        functools.partial(_combine_kernel, final_norm=final_norm, blk0=blk0),
        grid=(rows // tb,),
        in_specs=[pl.BlockSpec(memory_space=pl.ANY),
                  pl.BlockSpec(memory_space=pl.ANY),
                  pl.BlockSpec((tb, TOP_K), lambda i: (blk0 + i, 0)),
                  pl.BlockSpec((tb, d), lambda i: (i, 0)),
                  pl.BlockSpec((None, None, 1, d), lambda i: (group_of_block(i), 5, 0, 0)),
                  pl.BlockSpec((1, d), lambda i: (0, 0))],
        out_specs=pl.BlockSpec((tb, d), lambda i: (i, 0)),
        out_shape=jax.ShapeDtypeStruct((rows, d), F32),
        scratch_shapes=[pltpu.SMEM((ktb,), jnp.int32), pltpu.VMEM((2, TOP_K, tb, yb.shape[1]), yb.dtype),
                        pltpu.SemaphoreType.DMA((2,)), pltpu.SemaphoreType.DMA],
        compiler_params=_cparams("arbitrary"),
    )(slot_blocks, yb, probs, x.reshape(rows, d), mods, final_g.reshape(1, d))
    return out.reshape(bsz, n, d)


def kernel(x, c, ctx, c_ctx, norm1_g, norm2_g, w_mod, b_mod, w_in, gla_wg2_f, gla_bg_f, gla_wg2_b,
           gla_bg_b, gla_norm_g, conv_w, conv_b, conv_ln_g, conv_ln_b, w_out, router_w, router_b,
           exp_w_gu, exp_b_gu, exp_w_dn, exp_b_dn, final_norm_g):
    bsz, seq, d = x.shape
    n_ctx = ctx.shape[1]
    depth = w_mod.shape[0]
    assert d == D_MODEL and seq % (FFT_N2 * FFT_KB) == 0 and seq % n_ctx == 0
    lt = seq + n_ctx
    ctx_group = bsz

    rows = 8
    cvec = jnp.concatenate([c, c_ctx[None, :], jnp.zeros((rows - bsz - 1, d), F32)], axis=0)
    mods_all = _mod_call(cvec, w_mod, b_mod).reshape(depth, rows, 6, 1, d)

    def pack_w_in(w):
        o = [0, 256, 512, 768, 1280, 1792, 1808, 1824, 2336]
        parts = [w[:, o[0]:o[1]], w[:, o[1]:o[2]], w[:, o[2]:o[3]], w[:, o[3]:o[4]], w[:, o[4]:o[5]],
                 w[:, o[7]:o[7] + CONV_WIDTH], w[:, o[7] + CONV_WIDTH:o[8]], w[:, o[5]:o[7]],
                 jnp.zeros((d, GLR_PAD - 2 * GLA_GATE_RANK), w.dtype)]
        return jnp.concatenate(parts, axis=1).astype(BF16)

    long_tables = _dft_tables(seq)
    chan = _channel_tables()
    x_lat, x_ctx = x, ctx
    lat_group = lambda b: b
    ctx_group_fn = lambda b: ctx_group

    for layer in range(depth):
        last = layer == depth - 1
        mods = mods_all[layer]
        w_in_p = pack_w_in(w_in[layer])
        wg_pad = jnp.zeros((GLR_PAD, 2 * GLA_KEY_WIDTH), F32)
        wg_pad = wg_pad.at[:GLA_GATE_RANK, :GLA_KEY_WIDTH].set(gla_wg2_f[layer])
        wg_pad = wg_pad.at[GLA_GATE_RANK:2 * GLA_GATE_RANK, GLA_KEY_WIDTH:].set(gla_wg2_b[layer])
        bg_cat = jnp.concatenate([gla_bg_f[layer], gla_bg_b[layer]])[None, :]
        w_out_b = w_out[layer].astype(BF16)
        gn_tiled = jnp.tile(gla_norm_g[layer], GLA_HEADS)[None, :]
        rw_t = router_w[layer].T
        rw_hi = rw_t.astype(BF16)
        rw_lo = (rw_t - rw_hi.astype(F32)).astype(BF16)
        rb = router_b[layer][:, None]

        (u_l, r_l, glu_l), comb = _inproj_call(x_lat, mods, lat_group, norm1_g[layer], w_in_p, lt, 0)
        (u_c, r_c, glu_c), comb = _inproj_call(x_ctx, mods, ctx_group_fn, norm1_g[layer], w_in_p, lt,
                                               seq, combined=comb)
        o_f, o_b = _gla_call(*comb, wg_pad, bg_cat, seq, n_ctx)
        yf_l = _fourier_long(u_l, long_tables, chan)
        cv_l = _conv_call(glu_l, seq // (seq // GRID_W), conv_w[layer], conv_b[layer],
                          conv_ln_g[layer], conv_ln_b[layer])
        n_tok = bsz * seq + (0 if last else bsz * n_ctx)
        x_lat, routed = _outproj_call(yf_l, o_f, o_b, 0, r_l, cv_l, x_lat, mods, lat_group, gn_tiled,
                                      w_out_b, norm2_g[layer], rw_hi, rw_lo, rb, n_tok, 0)
        if not last:
            yf_c = _fourier_short(u_c, chan)
            cv_c = _conv_call(glu_c, n_ctx, conv_w[layer], conv_b[layer], conv_ln_g[layer],
                              conv_ln_b[layer])
            x_ctx, routed = _outproj_call(yf_c, o_f, o_b, seq, r_c, cv_c, x_ctx, mods, ctx_group_fn,
                                          gn_tiled, w_out_b, norm2_g[layer], rw_hi, rw_lo, rb, n_tok,
                                          bsz * seq, carried=routed)

        h2, idx_t, prob_t = routed
        pad_end, block_e, n_used, slot_blocks, n_blocks = _routing_tables(idx_t, n_tok)
        xs = _dispatch_call(pad_end, n_used, slot_blocks, h2, n_blocks)
        yb = _expert_call(block_e, n_used, xs, layer, exp_w_gu, exp_b_gu, exp_w_dn, exp_b_dn)
        probs = prob_t.T
        lat_blocks = seq // (slot_blocks.shape[2] // TOP_K)
        x_lat = _combine_call(slot_blocks, yb, probs, x_lat, mods, lambda i: i // lat_blocks,
                              final_norm_g, 0, last)
        if not last:
            x_ctx = _combine_call(slot_blocks, yb, probs, x_ctx, mods, lambda i: ctx_group,
                                  final_norm_g, bsz * seq, False)

    return x_lat
```

```python
import functools

import jax
import jax.numpy as jnp
from jax import lax
from jax.experimental import pallas as pl
from jax.experimental.pallas import tpu as pltpu

F32 = jnp.float32
BF16 = jnp.bfloat16
U32 = jnp.uint32
HIGH_HALF = 0xFFFF0000

D_MODEL = 1024
DEPTH = 2
GRID_W = 64
FOURIER_WIDTH = 256
FOURIER_HEADS = 4
FOURIER_HEAD_DIM = FOURIER_WIDTH // FOURIER_HEADS
GLA_HEADS = 4
GLA_KEY_WIDTH = 256
GLA_VALUE_WIDTH = 512
GLA_DK = GLA_KEY_WIDTH // GLA_HEADS
GLA_DV = GLA_VALUE_WIDTH // GLA_HEADS
GLA_GATE_RANK = 16
GLA_GATE_NORMALIZER = 16.0
CONV_WIDTH = 256
CONV_KERNEL = 31
N_EXPERTS = 32
TOP_K = 4
D_FF = D_MODEL
SWIGLU_LIMIT = 7.0
SWIGLU_ALPHA = 1.702
NORM_EPS = 1e-6

LANES = 128
VMEM_LIMIT = 56 * 1024 * 1024

COL_U = 0
COL_Q = COL_U + FOURIER_WIDTH
COL_K = COL_Q + GLA_KEY_WIDTH
COL_V = COL_K + GLA_KEY_WIDTH
COL_R = COL_V + GLA_VALUE_WIDTH
COL_CA = COL_R + GLA_VALUE_WIDTH
COL_CG = COL_CA + CONV_WIDTH
COL_GL = COL_CG + CONV_WIDTH
GLR_PAD = LANES
IN_PAD = COL_GL + GLR_PAD

GLA_CHUNK = 128
GLA_STEP_CHUNKS = 2
TM_ROWS = 512
TM_EXPERT = 512
EXPERT_CHUNKS = 4
RUN_ALIGN = 8
SORT_CHUNK = 256
SORT_ROWS = TOP_K * TM_ROWS + RUN_ALIGN * N_EXPERTS
FFT_N2 = 128
FFT_KB = 8
CONV_HALO = 16
CONV_GROUP = 8
CONV_PITCH = 100


def _cparams(*sem):
    return pltpu.CompilerParams(dimension_semantics=sem, vmem_limit_bytes=VMEM_LIMIT)


def _dot(a, b):
    return jnp.dot(a, b, preferred_element_type=F32)


def _dot_nt(a, b):
    return lax.dot_general(a, b, (((1,), (1,)), ((), ())), preferred_element_type=F32)


def _dot_tn(a, b):
    return lax.dot_general(a, b, (((0,), (0,)), ((), ())), preferred_element_type=F32)


def _split(a):
    hi = a.astype(BF16)
    lo = (a - hi.astype(F32)).astype(BF16)
    return hi, lo


def _dot3(a, b):
    ah, al = _split(a)
    bh, bl = _split(b)
    return _dot(ah, bh) + _dot(ah, bl) + _dot(al, bh)


def _sigmoid(x):
    return 1.0 / (1.0 + jnp.exp(-x))


def _pack_bf16_pairs(a):
    h = a.shape[1] // 2
    bits = lax.bitcast_convert_type(a.astype(BF16).astype(F32), U32)
    return (bits[:, :h] >> 16) | (bits[:, h:] & U32(HIGH_HALF))


def _unpack_bf16_pairs(w):
    return (lax.bitcast_convert_type(w << 16, F32), lax.bitcast_convert_type(w & U32(HIGH_HALF), F32))


def _rms(x, g):
    ms = jnp.mean(x * x, axis=-1, keepdims=True)
    return x * lax.rsqrt(ms + NORM_EPS) * g


def _mod_kernel(cv_ref, w_ref, b_ref, o_ref):
    cv = cv_ref[...]
    a = cv * _sigmoid(cv)
    o_ref[...] = _dot3(a, w_ref[...]) + b_ref[...]


def _mod_call(cvec, w_mod, b_mod):
    depth, d, n = w_mod.shape
    rows = cvec.shape[0]
    tn = 1536
    return pl.pallas_call(
        _mod_kernel,
        grid=(depth, n // tn),
        in_specs=[
            pl.BlockSpec((rows, d), lambda l, j: (0, 0)),
            pl.BlockSpec((None, d, tn), lambda l, j: (l, 0, j)),
            pl.BlockSpec((None, 1, tn), lambda l, j: (l, 0, j)),
        ],
        out_specs=pl.BlockSpec((None, rows, tn), lambda l, j: (l, 0, j)),
        out_shape=jax.ShapeDtypeStruct((depth, rows, n), F32),
        compiler_params=_cparams("arbitrary", "arbitrary"),
    )(cvec, w_mod, b_mod.reshape(depth, 1, n))


def _inproj_kernel(x_ref, g_ref, sh_ref, sc_ref, w_ref, *rest):
    u_ref, r_ref, glu_ref, q_ref, k_ref, v_ref, gl_ref = rest[-7:]
    x = x_ref[...]
    h = _rms(x, g_ref[...]) * (1.0 + sc_ref[...]) + sh_ref[...]
    p = _dot(h.astype(BF16), w_ref[...])
    u_ref[...] = p[:, COL_U:COL_Q].astype(u_ref.dtype)
    q_ref[...] = (p[:, COL_Q:COL_K] * (GLA_DK ** -0.5)).astype(q_ref.dtype)
    k_ref[...] = p[:, COL_K:COL_V].astype(k_ref.dtype)
    v_ref[...] = p[:, COL_V:COL_R].astype(v_ref.dtype)
    r_ref[...] = p[:, COL_R:COL_CA].astype(r_ref.dtype)
    glu_ref[...] = (p[:, COL_CA:COL_CG] * _sigmoid(p[:, COL_CG:COL_GL])).astype(glu_ref.dtype)
    gl_ref[...] = p[:, COL_GL:IN_PAD]


def _inproj_call(x, mods, group_of_batch, norm_g, w_in_p, lt, row0, combined=None):
    bsz, n, d = x.shape
    tm = min(TM_ROWS, n)
    assert n % tm == 0 and row0 % tm == 0
    blk0 = row0 // tm
    nb = n // tm
    steps = nb + (-(-(lt - n) // tm) if combined is None else 0)
    widths = (GLA_KEY_WIDTH, GLA_KEY_WIDTH, GLA_VALUE_WIDTH, GLR_PAD)
    dtypes = (BF16, BF16, BF16, F32)
    row_spec = lambda w: pl.BlockSpec((None, tm, w), lambda b, i: (b, jnp.minimum(i, nb - 1), 0))
    comb_spec = lambda w: pl.BlockSpec((None, tm, w), lambda b, i: (b, blk0 + i, 0))
    mod_spec = lambda which: pl.BlockSpec(
        (None, None, 1, d), lambda b, i: (group_of_batch(b), which, 0, 0))
    in_specs = [
        row_spec(d),
        pl.BlockSpec((1, d), lambda b, i: (0, 0)),
        mod_spec(0), mod_spec(1),
        pl.BlockSpec((d, IN_PAD), lambda b, i: (0, 0)),
    ]
    args = [x, norm_g.reshape(1, d), mods, mods, w_in_p]
    aliases = {}
    if combined is not None:
        for t, arr in enumerate(combined):
            in_specs.append(pl.BlockSpec(memory_space=pl.ANY))
            aliases[len(args)] = 3 + t
            args.append(arr)
    out_shape = [
        jax.ShapeDtypeStruct((bsz, n, FOURIER_WIDTH), BF16),
        jax.ShapeDtypeStruct((bsz, n, GLA_VALUE_WIDTH), BF16),
        jax.ShapeDtypeStruct((bsz, n, CONV_WIDTH), BF16),
    ] + [jax.ShapeDtypeStruct((bsz, lt, w), dt) for w, dt in zip(widths, dtypes)]
    out_specs = [row_spec(FOURIER_WIDTH), row_spec(GLA_VALUE_WIDTH), row_spec(CONV_WIDTH)] + [
        comb_spec(w) for w in widths]
    outs = pl.pallas_call(
        _inproj_kernel,
        grid=(bsz, steps),
        in_specs=in_specs,
        out_specs=out_specs,
        out_shape=out_shape,
        input_output_aliases=aliases,
        compiler_params=_cparams("arbitrary", "arbitrary"),
    )(*args)
    return outs[:3], outs[3:]


def _dft_tables(length):
    n2 = FFT_N2
    n1 = length // n2
    two_pi = 2.0 * jnp.pi

    def cs(num, den):
        ang = (num % den).astype(F32) * (two_pi / den)
        return jnp.cos(ang), jnp.sin(ang)

    k1 = jnp.arange(n1, dtype=jnp.int32)
    c1, s1 = cs(k1[:, None] * k1[None, :], n1)
    stage1 = (jnp.concatenate([c1, -s1], axis=0) * (n1 ** -0.5)).astype(BF16)
    k2 = jnp.arange(n2, dtype=jnp.int32)
    ct, st = cs(k1[:, None] * k2[None, :], length)
    cf, sf = cs(k2[:, None] * k2[None, :], n2)
    scale = n2 ** -0.5
    mr = (ct[:, None, :] * cf[None] - st[:, None, :] * sf[None]) * scale
    mi = -(st[:, None, :] * cf[None] + ct[:, None, :] * sf[None]) * scale
    stage2 = jnp.concatenate([jnp.concatenate([mr, -mi], axis=2),
                              jnp.concatenate([mi, mr], axis=2)], axis=1).astype(BF16)
    return stage1, stage2


def _channel_tables():
    hd = FOURIER_HEAD_DIM
    c = jnp.arange(FOURIER_WIDTH, dtype=jnp.int32)
    same_head = (c[:, None] // hd) == (c[None, :] // hd)
    ang = (((c[:, None] % hd) * (c[None, :] % hd)) % hd).astype(F32) * (2.0 * jnp.pi / hd)
    scale = hd ** -0.5
    bdc = jnp.where(same_head, jnp.cos(ang) * scale, 0.0).astype(BF16)
    bds = jnp.where(same_head, jnp.sin(ang) * scale, 0.0).astype(BF16)
    return bdc, bds


def _fft1_kernel(x_ref, cs_ref, zr_ref, zi_ref):
    n1 = x_ref.shape[0]
    z = _dot(cs_ref[...], x_ref[...])
    zr_ref[...] = z[:n1].astype(zr_ref.dtype)
    zi_ref[...] = z[n1:].astype(zi_ref.dtype)


def _fft2_kernel(zr_ref, zi_ref, m_ref, bdc_ref, bds_ref, o_ref):
    kb, n2, w = zr_ref.shape
    for j in range(kb):
        z = jnp.concatenate([zr_ref[j], zi_ref[j]], axis=0)
        a = _dot(m_ref[j], z)
        y = _dot(a[:n2].astype(BF16), bdc_ref[...]) + _dot(a[n2:].astype(BF16), bds_ref[...])
        o_ref[:, j * w:(j + 1) * w] = y.astype(o_ref.dtype)


def _fourier_long(u, tables, chan):
    bsz, length, w = u.shape
    stage1, stage2 = tables
    bdc, bds = chan
    n2 = FFT_N2
    n1 = length // n2
    tn = 4096
    cols = n2 * w
    zr, zi = pl.pallas_call(
        _fft1_kernel,
        grid=(bsz, cols // tn),
        in_specs=[pl.BlockSpec((None, n1, tn), lambda b, j: (b, 0, j)),
                  pl.BlockSpec((2 * n1, n1), lambda b, j: (0, 0))],
        out_specs=[pl.BlockSpec((None, n1, tn), lambda b, j: (b, 0, j))] * 2,
        out_shape=[jax.ShapeDtypeStruct((bsz, n1, cols), BF16)] * 2,
        compiler_params=_cparams("arbitrary", "arbitrary"),
    )(u.reshape(bsz, n1, cols), stage1)
    kb = FFT_KB
    z_spec = pl.BlockSpec((None, kb, n2, w), lambda b, j: (b, j, 0, 0))
    y = pl.pallas_call(
        _fft2_kernel,
        grid=(bsz, n1 // kb),
        in_specs=[z_spec, z_spec,
                  pl.BlockSpec((kb, 2 * n2, 2 * n2), lambda b, j: (j, 0, 0)),
                  pl.BlockSpec((w, w), lambda b, j: (0, 0)),
                  pl.BlockSpec((w, w), lambda b, j: (0, 0))],
        out_specs=pl.BlockSpec((None, n2, kb * w), lambda b, j: (b, 0, j)),
        out_shape=jax.ShapeDtypeStruct((bsz, n2, n1 * w), BF16),
        compiler_params=_cparams("arbitrary", "arbitrary"),
    )(zr.reshape(bsz, n1, n2, w), zi.reshape(bsz, n1, n2, w), stage2, bdc, bds)
    return y.reshape(bsz, length, w)


def _dft_short_kernel(u_ref, c_ref, s_ref, bdc_ref, bds_ref, o_ref):
    u = u_ref[...]
    p = _dot(u, bdc_ref[...]).astype(BF16)
    q = _dot(u, bds_ref[...]).astype(BF16)
    o_ref[...] = (_dot(c_ref[...], p) - _dot(s_ref[...], q)).astype(o_ref.dtype)


def _fourier_short(u, chan):
    bsz, length, w = u.shape
    bdc, bds = chan
    k = jnp.arange(length, dtype=jnp.int32)
    ang = ((k[:, None] * k[None, :]) % length).astype(F32) * (2.0 * jnp.pi / length)
    c = (jnp.cos(ang) * length ** -0.5).astype(BF16)
    s = (jnp.sin(ang) * length ** -0.5).astype(BF16)
    full = lambda n: pl.BlockSpec((n, n), lambda b: (0, 0))
    return pl.pallas_call(
        _dft_short_kernel,
        grid=(bsz,),
        in_specs=[pl.BlockSpec((None, length, w), lambda b: (b, 0, 0)),
                  full(length), full(length), full(w), full(w)],
        out_specs=pl.BlockSpec((None, length, w), lambda b: (b, 0, 0)),
        out_shape=jax.ShapeDtypeStruct((bsz, length, w), BF16),
        compiler_params=_cparams("arbitrary"),
    )(u, c, s, bdc, bds)


def _conv_kernel(x_ref, w_ref, cb_ref, lg_ref, lb_ref, o_ref, pad_ref, *, seg):
    nseg = x_ref.shape[0] // seg
    width = x_ref.shape[1]
    halo = jnp.zeros((CONV_HALO, width), F32)
    for s in range(nseg):
        pad_ref[s, 0:CONV_HALO, :] = halo
        pad_ref[s, CONV_HALO:CONV_HALO + seg, :] = x_ref[s * seg:(s + 1) * seg, :].astype(F32)
        pad_ref[s, CONV_HALO + seg:2 * CONV_HALO + seg, :] = halo
    first = CONV_HALO - CONV_KERNEL // 2
    sub = 8
    span = seg + 2 * CONV_HALO - sub
    for s in range(nseg):
        acc = jnp.zeros((seg, width), F32)
        for r in range(sub):
            shifted = pad_ref[s, r:r + span, :]
            for a in range((span - seg) // sub + 1):
                j = a * sub + r - first
                if 0 <= j < CONV_KERNEL:
                    acc = acc + shifted[a * sub:a * sub + seg, :] * w_ref[j:j + 1, :]
        y = acc + cb_ref[...]
        mu = jnp.mean(y, axis=-1, keepdims=True)
        yc = y - mu
        var = jnp.mean(yc * yc, axis=-1, keepdims=True)
        z = yc * lax.rsqrt(var + NORM_EPS) * lg_ref[...] + lb_ref[...]
        o_ref[s * seg:(s + 1) * seg, :] = (z * _sigmoid(z)).astype(o_ref.dtype)


def _conv_group_kernel(x_ref, w_ref, cb_ref, lg_ref, lb_ref, o_ref, pad_ref, acc_ref, *, seg):
    g, p = CONV_GROUP, CONV_PITCH
    halves = x_ref.shape[1] // LANES
    pad_ref[...] = jnp.zeros_like(pad_ref)
    for s in range(g):
        for h in range(halves):
            pad_ref[h, s * p + CONV_HALO:s * p + CONV_HALO + seg, :] = (
                x_ref[s * seg:(s + 1) * seg, h * LANES:(h + 1) * LANES].astype(F32))
    first = CONV_HALO - CONV_KERNEL // 2
    unroll = 4

    def body(i, carry):
        t0 = unroll * i
        for h in range(halves):
            accs = [None] * unroll
            for j in range(CONV_KERNEL):
                tap = w_ref[h, j]
                for u in range(unroll):
                    term = pad_ref[h, pl.ds(t0 + u + first + j, g, stride=p), :] * tap
                    accs[u] = term if accs[u] is None else accs[u] + term
            for u in range(unroll):
                acc_ref[h, pl.ds(t0 + u, g, stride=p), :] = accs[u]
        return carry

    lax.fori_loop(0, seg // unroll, body, 0)
    for s in range(g):
        y = jnp.concatenate([acc_ref[h, s * p:s * p + seg, :] for h in range(halves)], axis=1) + cb_ref[...]
        mu = jnp.mean(y, axis=-1, keepdims=True)
        yc = y - mu
        var = jnp.mean(yc * yc, axis=-1, keepdims=True)
        z = yc * lax.rsqrt(var + NORM_EPS) * lg_ref[...] + lb_ref[...]
        o_ref[s * seg:(s + 1) * seg, :] = (z * _sigmoid(z)).astype(o_ref.dtype)


def _conv_call(glu, seg, conv_w, conv_b, ln_g, ln_b):
    bsz, n, w = glu.shape
    t = max(seg, min(TM_ROWS, n))
    assert n % t == 0 and t % seg == 0
    vec = lambda: pl.BlockSpec((1, w), lambda b, i: (0, 0))
    if t // seg == CONV_GROUP and seg + 2 * CONV_HALO <= CONV_PITCH and w % LANES == 0:
        halves = w // LANES
        taps = jnp.broadcast_to(conv_w.reshape(CONV_KERNEL, halves, 1, LANES).transpose(1, 0, 2, 3),
                                (halves, CONV_KERNEL, 8, LANES))
        slab = pltpu.VMEM((halves, CONV_GROUP * CONV_PITCH, LANES), F32)
        return pl.pallas_call(
            functools.partial(_conv_group_kernel, seg=seg),
            grid=(bsz, n // t),
            in_specs=[pl.BlockSpec((None, t, w), lambda b, i: (b, i, 0)),
                      pl.BlockSpec(taps.shape, lambda b, i: (0, 0, 0, 0)),
                      vec(), vec(), vec()],
            out_specs=pl.BlockSpec((None, t, w), lambda b, i: (b, i, 0)),
            out_shape=jax.ShapeDtypeStruct((bsz, n, w), BF16),
            scratch_shapes=[slab, slab],
            compiler_params=_cparams("arbitrary", "arbitrary"),
        )(glu, taps, conv_b.reshape(1, w), ln_g.reshape(1, w), ln_b.reshape(1, w))
    return pl.pallas_call(
        functools.partial(_conv_kernel, seg=seg),
        grid=(bsz, n // t),
        in_specs=[pl.BlockSpec((None, t, w), lambda b, i: (b, i, 0)),
                  pl.BlockSpec((CONV_KERNEL, w), lambda b, i: (0, 0)),
                  vec(), vec(), vec()],
        out_specs=pl.BlockSpec((None, t, w), lambda b, i: (b, i, 0)),
        out_shape=jax.ShapeDtypeStruct((bsz, n, w), BF16),
        scratch_shapes=[pltpu.VMEM((t // seg, seg + 2 * CONV_HALO, w), F32)],
        compiler_params=_cparams("arbitrary", "arbitrary"),
    )(glu, conv_w, conv_b.reshape(1, w), ln_g.reshape(1, w), ln_b.reshape(1, w))


def _gla_direction(q_ref, k_ref, v_ref, gl_ref, wg_ref, bg_ref, o_ref, st, reverse):
    c = GLA_CHUNK
    n_sub = q_ref.shape[0] // c
    for s in (reversed(range(n_sub)) if reverse else range(n_sub)):
        rows = slice(s * c, (s + 1) * c)
        st = _gla_chunk(q_ref[rows, :], k_ref[rows, :], v_ref[rows, :], gl_ref[rows, :], wg_ref, bg_ref,
                        o_ref.at[rows, :], st, reverse)
    return st


def _gla_chunk(q, k, v, glr, wg_ref, bg_ref, o_ref, st, reverse):
    c = GLA_CHUNK
    kw = GLA_KEY_WIDTH
    col0 = kw if reverse else 0
    pre = _dot3(glr, wg_ref[:, col0:col0 + kw]) + bg_ref[:, col0:col0 + kw]
    g = (jnp.minimum(pre, 0.0) - jnp.log(1.0 + jnp.exp(-jnp.abs(pre)))) * (1.0 / GLA_GATE_NORMALIZER)
    row = lax.broadcasted_iota(jnp.int32, (c, c), 0)
    col = lax.broadcasted_iota(jnp.int32, (c, c), 1)
    seen = (col >= row) if reverse else (col <= row)
    tri = jnp.where(seen, 1.0, 0.0).astype(BF16)
    gh, gl = _split(g)
    b = _dot(tri, gh) + _dot(tri, gl)
    mid = c // 2 if reverse else c // 2 - 1
    last = 0 if reverse else c - 1
    b_mid = b[mid:mid + 1, :]
    b_last = b[last:last + 1, :]
    q = q.astype(F32)
    k = k.astype(F32)
    qe = q * jnp.exp(b - b_mid)
    ke = k * jnp.exp(b_mid - b)
    kd = k * jnp.exp(b_last - b)
    head_of_lane = lax.broadcasted_iota(jnp.int32, (1, kw), 1) // GLA_DK
    q_heads = jnp.concatenate(
        [jnp.where(head_of_lane == h, qe, 0.0) for h in range(GLA_HEADS)], axis=0).astype(BF16)
    rhs = jnp.concatenate([ke, st * jnp.exp(b_mid)], axis=0).astype(BF16)
    res = _dot_nt(q_heads, rhs)
    outs = []
    for h in range(GLA_HEADS):
        blk = res[h * c:(h + 1) * c, :]
        scores = jnp.where(seen, blk[:, :c], 0.0).astype(BF16)
        outs.append(_dot(scores, v[:, h * GLA_DV:(h + 1) * GLA_DV]) + blk[:, c:])
    o_ref[...] = jnp.concatenate(outs, axis=1).astype(o_ref.dtype)
    kv = _dot_tn(v, kd.astype(BF16))
    ds = jnp.zeros_like(st)
    for h in range(GLA_HEADS):
        ds = ds + jnp.where(head_of_lane == h, kv[h * GLA_DV:(h + 1) * GLA_DV, :], 0.0)
    return st * jnp.exp(b_last) + ds


def _gla_kernel(qf, kf, vf, gf, qb, kb, vb, gb, wg_ref, bg_ref, of_ref, ob_ref, sf_ref, sb_ref):
    @pl.when(pl.program_id(0) == 0)
    def _():
        sf_ref[...] = jnp.zeros_like(sf_ref)
        sb_ref[...] = jnp.zeros_like(sb_ref)

    bsz = qf.shape[0]
    states = [(sf_ref[b], sb_ref[b]) for b in range(bsz)]
    new = []
    for b, (s_f, s_b) in enumerate(states):
        new.append((
            _gla_direction(qf.at[b], kf.at[b], vf.at[b], gf.at[b], wg_ref, bg_ref, of_ref.at[b], s_f, False),
            _gla_direction(qb.at[b], kb.at[b], vb.at[b], gb.at[b], wg_ref, bg_ref, ob_ref.at[b], s_b, True)))
    for b, (s_f, s_b) in enumerate(new):
        sf_ref[b] = s_f
        sb_ref[b] = s_b


def _gla_call(q, k, v, gl, wg_pad, bg_cat, n_lat, n_ctx):
    bsz, lt, _ = q.shape
    c = GLA_STEP_CHUNKS * GLA_CHUNK
    assert n_lat % c == 0 and n_ctx % c == 0
    cl, cc = n_lat // c, n_ctx // c

    def fwd_blk(j):
        return jnp.where(j < cc, cl + j, j - cc)

    def bwd_blk(j):
        return jnp.where(j < cc, cl + cc - 1 - j, cl - 1 - (j - cc))

    def spec(w, blk):
        return pl.BlockSpec((bsz, c, w), lambda j: (0, blk(j), 0))

    widths = (GLA_KEY_WIDTH, GLA_KEY_WIDTH, GLA_VALUE_WIDTH, GLR_PAD)
    in_specs = [spec(w, fwd_blk) for w in widths] + [spec(w, bwd_blk) for w in widths] + [
        pl.BlockSpec(wg_pad.shape, lambda j: (0, 0)),
        pl.BlockSpec(bg_cat.shape, lambda j: (0, 0))]
    return pl.pallas_call(
        _gla_kernel,
        grid=(cl + cc,),
        in_specs=in_specs,
        out_specs=[spec(GLA_VALUE_WIDTH, fwd_blk), spec(GLA_VALUE_WIDTH, bwd_blk)],
        out_shape=[jax.ShapeDtypeStruct((bsz, lt, GLA_VALUE_WIDTH), F32)] * 2,
        scratch_shapes=[pltpu.VMEM((bsz, GLA_DV, GLA_KEY_WIDTH), F32)] * 2,
        compiler_params=_cparams("arbitrary"),
    )(q, k, v, gl, q, k, v, gl, wg_pad, bg_cat)


def _outproj_kernel(yf_ref, of_ref, ob_ref, r_ref, cv_ref, x_ref, gate_ref, sh_ref, sc_ref,
                    gn_ref, wo_ref, n2_ref, rwh_ref, rwl_ref, rb_ref, *rest):
    xo_ref, h2_ref, idx_ref, prob_ref = rest[-4:]
    o = of_ref[...] + ob_ref[...]
    heads = []
    for h in range(GLA_HEADS):
        oh = o[:, h * GLA_DV:(h + 1) * GLA_DV]
        heads.append(oh * lax.rsqrt(jnp.mean(oh * oh, axis=-1, keepdims=True) + NORM_EPS))
    r = r_ref[...].astype(F32)
    gla = jnp.concatenate(heads, axis=1) * gn_ref[...] * (r * _sigmoid(r))
    c0, c1 = FOURIER_WIDTH, FOURIER_WIDTH + GLA_VALUE_WIDTH
    y = (_dot(yf_ref[...], wo_ref[0:c0, :]) + _dot(gla.astype(BF16), wo_ref[c0:c1, :])
         + _dot(cv_ref[...], wo_ref[c1:, :]))
    xn = x_ref[...] + gate_ref[...] * y
    xo_ref[...] = xn
    h2 = _rms(xn, n2_ref[...]) * (1.0 + sc_ref[...]) + sh_ref[...]
    hh, hl = _split(h2)
    h2_ref[...] = _pack_bf16_pairs(h2)
    logits = (_dot_nt(rwh_ref[...], hh) + _dot_nt(rwh_ref[...], hl) + _dot_nt(rwl_ref[...], hh)
              + rb_ref[...])
    expert = lax.broadcasted_iota(jnp.int32, logits.shape, 0)
    vals, idxs = [], []
    cur = logits
    for _ in range(TOP_K):
        m = jnp.max(cur, axis=0, keepdims=True)
        ix = jnp.min(jnp.where(cur == m, expert, N_EXPERTS), axis=0, keepdims=True)
        vals.append(m)
        idxs.append(ix)
        cur = jnp.where(expert == ix, -jnp.inf, cur)
    es = [jnp.exp(vv - vals[0]) for vv in vals]
    inv = 1.0 / functools.reduce(lambda a, b: a + b, es)
    idx_ref[...] = jnp.concatenate(idxs, axis=0)
    prob_ref[...] = jnp.concatenate([e * inv for e in es], axis=0)


def _outproj_call(yf, o_f, o_b, o_row0, r, cv, x, mods, group_of_batch, gn_tiled, w_out, norm2_g,
                  rw_hi, rw_lo, rb, n_tok, tok0, carried=None):
    bsz, n, d = x.shape
    tm = min(TM_ROWS, n)
    assert n % tm == 0 and o_row0 % tm == 0 and tok0 % tm == 0
    nb = n // tm
    spare = (n_tok - bsz * n) if carried is None else 0
    assert spare in (0, tm)
    steps = nb + spare // tm
    last = lambda i: jnp.minimum(i, nb - 1)
    row = lambda w: pl.BlockSpec((None, tm, w), lambda b, i: (b, last(i), 0))
    orow = pl.BlockSpec((None, tm, GLA_VALUE_WIDTH), lambda b, i: (b, o_row0 // tm + last(i), 0))
    mod = lambda which: pl.BlockSpec((None, None, 1, d), lambda b, i: (group_of_batch(b), which, 0, 0))
    const = lambda a: pl.BlockSpec(a.shape, lambda b, i: (0,) * a.ndim)
    consts = [gn_tiled, w_out, norm2_g.reshape(1, d), rw_hi, rw_lo, rb]
    in_specs = [row(FOURIER_WIDTH), orow, orow, row(GLA_VALUE_WIDTH), row(CONV_WIDTH), row(d),
                mod(2), mod(3), mod(4)] + [const(a) for a in consts]
    args = [yf, o_f, o_b, r, cv, x, mods, mods, mods] + consts
    aliases = {}
    if carried is not None:
        for t, arr in enumerate(carried):
            in_specs.append(pl.BlockSpec(memory_space=pl.ANY))
            aliases[len(args)] = 1 + t
            args.append(arr)
    tokblk = lambda b, i: tok0 // tm + jnp.where(
        jnp.logical_and(i == nb, b == bsz - 1), bsz * nb, b * nb + last(i))
    out_specs = [row(d),
                 pl.BlockSpec((tm, d // 2), lambda b, i: (tokblk(b, i), 0)),
                 pl.BlockSpec((TOP_K, tm), lambda b, i: (0, tokblk(b, i))),
                 pl.BlockSpec((TOP_K, tm), lambda b, i: (0, tokblk(b, i)))]
    out_shape = [jax.ShapeDtypeStruct((bsz, n, d), F32),
                 jax.ShapeDtypeStruct((n_tok, d // 2), U32),
                 jax.ShapeDtypeStruct((TOP_K, n_tok), jnp.int32),
                 jax.ShapeDtypeStruct((TOP_K, n_tok), F32)]
    outs = pl.pallas_call(
        _outproj_kernel,
        grid=(bsz, steps),
        in_specs=in_specs,
        out_specs=out_specs,
        out_shape=out_shape,
        input_output_aliases=aliases,
        compiler_params=_cparams("arbitrary", "arbitrary"),
    )(*args)
    return outs[0], outs[1:]


def _expert_kernel(be_ref, nu_ref, x_ref, wgu_ref, bgu_ref, wdn_ref, bdn_ref, o_ref, wgu_s, wdn_s):
    i = pl.program_id(0)
    nu = nu_ref[0]
    n_chunks = wgu_s.shape[0]
    fc = D_FF // n_chunks

    @pl.when(i >= nu)
    def _():
        o_ref[...] = jnp.zeros_like(o_ref)

    @pl.when(i < nu)
    def _():
        changed = jnp.logical_or(i == 0, be_ref[i] != be_ref[jnp.maximum(i - 1, 0)])

        @pl.when(changed)
        def _():
            rows = 128
            for s in range(wgu_ref.shape[0] // rows):
                rs = slice(s * rows, (s + 1) * rows)
                for n in range(n_chunks):
                    wgu_s[n, rs, 0:fc] = wgu_ref[rs, n * fc:(n + 1) * fc].astype(BF16)
                    wgu_s[n, rs, fc:2 * fc] = wgu_ref[rs, D_FF + n * fc:D_FF + (n + 1) * fc].astype(BF16)
                wdn_s[rs, :] = wdn_ref[rs, :].astype(BF16)

        x = jnp.concatenate(_unpack_bf16_pairs(x_ref[...]), axis=1).astype(BF16)
        acts = []
        for n in range(n_chunks):
            gate = _dot(x, wgu_s[n, :, 0:fc]) + bgu_ref[:, n * fc:(n + 1) * fc]
            up = _dot(x, wgu_s[n, :, fc:2 * fc]) + bgu_ref[:, D_FF + n * fc:D_FF + (n + 1) * fc]
            gate = jnp.minimum(gate, SWIGLU_LIMIT)
            up = jnp.clip(up, -SWIGLU_LIMIT, SWIGLU_LIMIT)
            acts.append((gate * _sigmoid(SWIGLU_ALPHA * gate) * (up + 1.0)).astype(BF16))
        half = n_chunks // 2
        y = (_dot(jnp.concatenate(acts[:half], axis=1), wdn_s[0:half * fc, :])
             + _dot(jnp.concatenate(acts[half:], axis=1), wdn_s[half * fc:, :]) + bdn_ref[...])
        o_ref[...] = _pack_bf16_pairs(y)


def _expert_call(block_e, n_used, xs, layer, w_gu, b_gu, w_dn, b_dn):
    rows, dh = xs.shape
    d = 2 * dh
    tm = TM_EXPERT
    nblk = rows // tm
    depth, e, _, f2 = w_gu.shape
    n_chunks = EXPERT_CHUNKS
    live = lambda i, nu: jnp.minimum(i, nu[0] - 1)
    wmap = lambda i, be, nu: (layer, be[live(i, nu)], 0, 0)
    grid_spec = pltpu.PrefetchScalarGridSpec(
        num_scalar_prefetch=2,
        grid=(nblk,),
        in_specs=[
            pl.BlockSpec((tm, dh), lambda i, be, nu: (live(i, nu), 0)),
            pl.BlockSpec((None, None, d, f2), wmap),
            pl.BlockSpec((None, None, 1, f2), wmap),
            pl.BlockSpec((None, None, f2 // 2, d), wmap),
            pl.BlockSpec((None, None, 1, d), wmap),
        ],
        out_specs=pl.BlockSpec((tm, dh), lambda i, be, nu: (i, 0)),
        scratch_shapes=[pltpu.VMEM((n_chunks, d, f2 // n_chunks), BF16), pltpu.VMEM((f2 // 2, d), BF16)],
    )
    return pl.pallas_call(
        _expert_kernel,
        grid_spec=grid_spec,
        out_shape=jax.ShapeDtypeStruct((rows, dh), xs.dtype),
        compiler_params=_cparams("arbitrary"),
    )(block_e, n_used, xs, w_gu, b_gu.reshape(depth, e, 1, f2), w_dn, b_dn.reshape(depth, e, 1, d))


def _rank_kernel(idx_ref, rank_ref, cnt_ref, base_ref):
    @pl.when(pl.program_id(0) == 0)
    def _():
        base_ref[...] = jnp.zeros_like(base_ref)

    tb = idx_ref.shape[1]
    row = lax.broadcasted_iota(jnp.int32, (tb, tb), 0)
    col = lax.broadcasted_iota(jnp.int32, (tb, tb), 1)
    earlier = jnp.where(row < col, 1.0, 0.0).astype(BF16)
    expert = lax.broadcasted_iota(jnp.int32, (N_EXPERTS, tb), 0)
    base = base_ref[...]
    ranks = []
    for kk in range(TOP_K):
        hit = expert == idx_ref[kk:kk + 1, :]
        onehot = hit.astype(F32)
        before = _dot(onehot.astype(BF16), earlier) + base
        ranks.append(jnp.sum(before * onehot, axis=0, keepdims=True))
        base = base + jnp.sum(onehot, axis=1, keepdims=True)
    rank_ref[...] = jnp.concatenate(ranks, axis=0).astype(jnp.int32)
    base_ref[...] = base
    cnt_ref[...] = base.astype(jnp.int32)


def _slot_kernel(idx_ref, rank_ref, start_ref, slot_ref):
    expert = lax.broadcasted_iota(jnp.int32, (N_EXPERTS, idx_ref.shape[1]), 0)
    rows = []
    for kk in range(TOP_K):
        start = jnp.sum(jnp.where(expert == idx_ref[kk:kk + 1, :], start_ref[...], 0), axis=0,
                        keepdims=True)
        rows.append(rank_ref[kk:kk + 1, :] + start)
    slot_ref[...] = jnp.concatenate(rows, axis=1)


def _row_copy(src, s, dst, t, sem):
    return pltpu.make_async_copy(src.at[pl.ds(s, 1), :], dst.at[pl.ds(t, 1), :], sem)


def _dispatch_kernel(pe_ref, nu_ref, slot_hbm, h_ref, xs_ref, slot_s, zero_ref, sem, zsem, ssem, *,
                     n_blocks):
    i = pl.program_id(0)
    tb = h_ref.shape[0]
    tm = zero_ref.shape[0]
    fetch = pltpu.make_async_copy(slot_hbm.at[i, 0], slot_s, ssem)
    fetch.start()

    def zero_copy(blk):
        return pltpu.make_async_copy(zero_ref, xs_ref.at[pl.ds(pl.multiple_of(blk * tm, tm), tm), :], zsem)

    @pl.when(i == 0)
    def _():
        zero_ref[...] = jnp.zeros_like(zero_ref)

        def per_expert(fn):
            for e in range(N_EXPERTS):
                end = pe_ref[e]
                start = pe_ref[e - 1] if e else 0

                @pl.when(end > start)
                def _():
                    fn(end // tm - 1)

        def per_tail(fn):
            def body(blk, carry):
                fn(blk)
                return carry
            lax.fori_loop(nu_ref[0], n_blocks, body, 0)

        per_expert(lambda blk: zero_copy(blk).start())
        per_tail(lambda blk: zero_copy(blk).start())
        per_expert(lambda blk: zero_copy(blk).wait())
        per_tail(lambda blk: zero_copy(blk).wait())

    fetch.wait()

    def issue(g, carry):
        t0 = pl.multiple_of(g * 8, 8)
        for j in range(8):
            for kk in range(TOP_K):
                _row_copy(h_ref, t0 + j, xs_ref, slot_s[kk * tb + t0 + j], sem).start(priority=kk % 2)
        return carry

    lax.fori_loop(0, tb // 8, issue, 0)
    for kk in range(TOP_K):
        pltpu.make_async_copy(h_ref, xs_ref.at[pl.ds(0, tb), :], sem).wait()


def _dispatch_call(pad_end, n_used, slot_blocks, h2, n_blocks):
    n_tok, d = h2.shape
    nblk, _, ktb = slot_blocks.shape
    tb = ktb // TOP_K
    tm = TM_EXPERT
    grid_spec = pltpu.PrefetchScalarGridSpec(
        num_scalar_prefetch=2,
        grid=(nblk,),
        in_specs=[pl.BlockSpec(memory_space=pl.ANY),
                  pl.BlockSpec((tb, d), lambda i, pe, nu: (i, 0))],
        out_specs=pl.BlockSpec(memory_space=pl.ANY),
        scratch_shapes=[pltpu.SMEM((ktb,), jnp.int32), pltpu.VMEM((tm, d), h2.dtype),
                        pltpu.SemaphoreType.DMA, pltpu.SemaphoreType.DMA, pltpu.SemaphoreType.DMA],
    )
    return pl.pallas_call(
        functools.partial(_dispatch_kernel, n_blocks=n_blocks),
        grid_spec=grid_spec,
        out_shape=jax.ShapeDtypeStruct((n_blocks * tm, d), h2.dtype),
        compiler_params=_cparams("arbitrary"),
    )(pad_end, n_used, slot_blocks, h2)


def _routing_tables(idx_t, n_tok):
    tm = TM_EXPERT
    tb = TM_ROWS
    assert n_tok % tb == 0
    blk = pl.BlockSpec((TOP_K, tb), lambda i: (0, i))
    rank_t, counts = pl.pallas_call(
        _rank_kernel,
        grid=(n_tok // tb,),
        in_specs=[blk],
        out_specs=[blk, pl.BlockSpec((N_EXPERTS, 1), lambda i: (0, 0))],
        out_shape=[jax.ShapeDtypeStruct((TOP_K, n_tok), jnp.int32),
                   jax.ShapeDtypeStruct((N_EXPERTS, 1), jnp.int32)],
        scratch_shapes=[pltpu.VMEM((N_EXPERTS, 1), F32)],
        compiler_params=_cparams("arbitrary"),
    )(idx_t)
    counts = counts[:, 0]
    padded = (counts + tm - 1) // tm * tm
    pad_end = jnp.cumsum(padded)
    pad_start = pad_end - padded
    slot_blocks = pl.pallas_call(
        _slot_kernel,
        grid=(n_tok // tb,),
        in_specs=[blk, blk, pl.BlockSpec((N_EXPERTS, 1), lambda i: (0, 0))],
        out_specs=pl.BlockSpec((None, 1, TOP_K * tb), lambda i: (i, 0, 0)),
        out_shape=jax.ShapeDtypeStruct((n_tok // tb, 1, TOP_K * tb), jnp.int32),
        compiler_params=_cparams("arbitrary"),
    )(idx_t, rank_t, pad_start[:, None])
    n_blocks = -(-TOP_K * n_tok // tm) + N_EXPERTS
    first_row = jnp.arange(n_blocks, dtype=jnp.int32) * tm
    block_e = jnp.minimum(jnp.sum(pad_end[None, :] <= first_row[:, None], axis=1),
                          N_EXPERTS - 1).astype(jnp.int32)
    n_used = (pad_end[-1:] // tm).astype(jnp.int32)
    return pad_end.astype(jnp.int32), block_e, n_used, slot_blocks, n_blocks


def _combine_kernel(slot_hbm, yb_hbm, p_ref, x_ref, gate_ref, fg_ref, o_ref, slot_s, ybuf, sem, ssem, *,
                    final_norm, blk0):
    i = pl.program_id(0)
    n = pl.num_programs(0)
    tb = x_ref.shape[0]

    def request(blk, buf):
        fetch = pltpu.make_async_copy(slot_hbm.at[blk0 + blk, 0], slot_s, ssem)
        fetch.start()
        fetch.wait()

        def issue(g, carry):
            t0 = pl.multiple_of(g * 8, 8)
            for j in range(8):
                for kk in range(TOP_K):
                    _row_copy(yb_hbm, slot_s[kk * tb + t0 + j], ybuf.at[buf, kk], t0 + j,
                              sem.at[buf]).start(priority=kk % 2)
            return carry

        lax.fori_loop(0, tb // 8, issue, 0)

    @pl.when(i == 0)
    def _():
        request(0, 0)

    @pl.when(i + 1 < n)
    def _():
        request(i + 1, (i + 1) % 2)

    cur = i % 2
    for kk in range(TOP_K):
        pltpu.make_async_copy(yb_hbm.at[pl.ds(0, tb), :], ybuf.at[cur, kk], sem.at[cur]).wait()
    f_lo = jnp.zeros(ybuf.shape[2:], F32)
    f_hi = jnp.zeros(ybuf.shape[2:], F32)
    for kk in range(TOP_K):
        lo, hi = _unpack_bf16_pairs(ybuf[cur, kk])
        f_lo = f_lo + lo * p_ref[:, kk:kk + 1]
        f_hi = f_hi + hi * p_ref[:, kk:kk + 1]
    xn = x_ref[...] + gate_ref[...] * jnp.concatenate([f_lo, f_hi], axis=1)
    if final_norm:
        xn = _rms(xn, fg_ref[...])
    o_ref[...] = xn


def _combine_call(slot_blocks, yb, probs, x, mods, group_of_block, final_g, tok0, final_norm):
    bsz, n, d = x.shape
    ktb = slot_blocks.shape[2]
    tb = ktb // TOP_K
    rows = bsz * n
    assert rows % tb == 0 and tok0 % tb == 0 and (n % tb == 0 or tb % n == 0)
    blk0 = tok0 // tb
    out = pl.pallas_call(
        functools.partial(_combine_kernel, final_norm=final_norm, blk0=blk0),
        grid=(rows // tb,),
        in_specs=[pl.BlockSpec(memory_space=pl.ANY),
                  pl.BlockSpec(memory_space=pl.ANY),
                  pl.BlockSpec((tb, TOP_K), lambda i: (blk0 + i, 0)),
                  pl.BlockSpec((tb, d), lambda i: (i, 0)),
                  pl.BlockSpec((None, None, 1, d), lambda i: (group_of_block(i), 5, 0, 0)),
                  pl.BlockSpec((1, d), lambda i: (0, 0))],
        out_specs=pl.BlockSpec((tb, d), lambda i: (i, 0)),
        out_shape=jax.ShapeDtypeStruct((rows, d), F32),
        scratch_shapes=[pltpu.SMEM((ktb,), jnp.int32), pltpu.VMEM((2, TOP_K, tb, yb.shape[1]), yb.dtype),
                        pltpu.SemaphoreType.DMA((2,)), pltpu.SemaphoreType.DMA],
        compiler_params=_cparams("arbitrary"),
    )(slot_blocks, yb, probs, x.reshape(rows, d), mods, final_g.reshape(1, d))
    return out.reshape(bsz, n, d)


def _count_kernel(idx_ref, cnt_ref):
    expert = lax.broadcasted_iota(jnp.int32, (N_EXPERTS, idx_ref.shape[1]), 0)
    total = jnp.zeros((N_EXPERTS, 1), F32)
    for kk in range(TOP_K):
        total = total + jnp.sum((expert == idx_ref[kk:kk + 1, :]).astype(F32), axis=1, keepdims=True)
    cnt_ref[...] = total.astype(jnp.int32)


def _run_tables(idx_t, n_tok):
    tm, tb = TM_EXPERT, TM_ROWS
    nblk = n_tok // tb
    assert n_tok % tb == 0
    cnt = pl.pallas_call(
        _count_kernel,
        grid=(nblk,),
        in_specs=[pl.BlockSpec((TOP_K, tb), lambda i: (0, i))],
        out_specs=pl.BlockSpec((None, N_EXPERTS, 1), lambda i: (i, 0, 0)),
        out_shape=jax.ShapeDtypeStruct((nblk, N_EXPERTS, 1), jnp.int32),
        compiler_params=_cparams("arbitrary"),
    )(idx_t)[:, :, 0]
    run = (cnt + RUN_ALIGN - 1) // RUN_ALIGN * RUN_ALIGN
    total = jnp.sum(run, axis=0)
    padded = (total + tm - 1) // tm * tm
    pad_end = jnp.cumsum(padded)
    dst = (pad_end - padded)[None, :] + jnp.cumsum(run, axis=0) - run
    boff = jnp.cumsum(run, axis=1) - run
    n_blocks = -(-(TOP_K * n_tok + (RUN_ALIGN - 1) * N_EXPERTS * nblk) // tm) + N_EXPERTS
    first_row = jnp.arange(n_blocks, dtype=jnp.int32) * tm
    block_e = jnp.minimum(jnp.sum(pad_end[None, :] <= first_row[:, None], axis=1),
                          N_EXPERTS - 1).astype(jnp.int32)
    n_used = (pad_end[-1:] // tm).astype(jnp.int32)
    flat = lambda a: a.reshape(-1).astype(jnp.int32)
    return dict(run=flat(run), dst=flat(dst), boff=flat(boff), boff_f=boff.astype(F32),
                pad_end=pad_end.astype(jnp.int32), block_e=block_e, n_used=n_used, n_blocks=n_blocks)


def _run_pieces(run_len, fn):
    off = jnp.int32(0)
    size = TM_ROWS
    while size >= RUN_ALIGN:
        hit = (run_len & size) != 0

        @pl.when(hit)
        def _(off=off, size=size):
            fn(off, size)

        off = off + jnp.where(hit, size, 0)
        size //= 2


def _run_copies(blk, run_ref, grp_ref, boff_ref, grouped, sbuf, sem, to_grouped):
    copies = []
    for e in range(N_EXPERTS):
        n = run_ref[blk * N_EXPERTS + e]
        g0 = grp_ref[blk * N_EXPERTS + e]
        s0 = boff_ref[blk * N_EXPERTS + e]

        def piece(off, size, g0=g0, s0=s0):
            g = grouped.at[pl.ds(pl.multiple_of(g0 + off, RUN_ALIGN), size), :]
            s = sbuf.at[pl.ds(pl.multiple_of(s0 + off, RUN_ALIGN), size), :]
            return pltpu.make_async_copy(s, g, sem) if to_grouped else pltpu.make_async_copy(g, s, sem)

        copies.append((n, piece))
    for n, piece in copies:
        _run_pieces(n, lambda off, size, piece=piece: piece(off, size).start())
    for n, piece in copies:
        _run_pieces(n, lambda off, size, piece=piece: piece(off, size).wait())


def _sort_dispatch_kernel(run_ref, dst_ref, boffs_ref, pe_ref, nu_ref, idx_ref, boff_ref, h_ref, xs_ref,
                          sbuf, zero_ref, sem, zsem, *, n_blocks):
    i = pl.program_id(0)
    tb = h_ref.shape[0]
    tm = zero_ref.shape[0]

    def zero_copy(blk):
        return pltpu.make_async_copy(zero_ref, xs_ref.at[pl.ds(pl.multiple_of(blk * tm, tm), tm), :], zsem)

    @pl.when(i == 0)
    def _():
        zero_ref[...] = jnp.zeros_like(zero_ref)

        def per_expert(fn):
            for e in range(N_EXPERTS):
                end = pe_ref[e]
                start = pe_ref[e - 1] if e else 0

                @pl.when(end > start)
                def _():
                    fn(end // tm - 1)

        def per_tail(fn):
            def body(blk, carry):
                fn(blk)
                return carry
            lax.fori_loop(nu_ref[0], n_blocks, body, 0)

        per_expert(lambda blk: zero_copy(blk).start())
        per_tail(lambda blk: zero_copy(blk).start())
        per_expert(lambda blk: zero_copy(blk).wait())
        per_tail(lambda blk: zero_copy(blk).wait())

    row = lax.broadcasted_iota(jnp.int32, (tb, tb), 0)
    col = lax.broadcasted_iota(jnp.int32, (tb, tb), 1)
    earlier = (row < col).astype(F32).astype(BF16)
    expert = lax.broadcasted_iota(jnp.int32, (N_EXPERTS, tb), 0)
    base = boff_ref[...]
    pos = []
    for kk in range(TOP_K):
        onehot = (expert == idx_ref[kk:kk + 1, :]).astype(F32)
        before = _dot(onehot.astype(BF16), earlier) + base
        pos.append(jnp.sum(before * onehot, axis=0, keepdims=True).astype(jnp.int32))
        base = base + jnp.sum(onehot, axis=1, keepdims=True)

    lo, hi = _unpack_bf16_pairs(h_ref[...])
    h = jnp.concatenate([lo, hi], axis=1).astype(BF16)
    rc = SORT_CHUNK
    for c in range(sbuf.shape[0] // rc):
        r = lax.broadcasted_iota(jnp.int32, (rc, tb), 0) + c * rc
        sel = jnp.zeros((rc, tb), F32)
        for kk in range(TOP_K):
            sel = jnp.where(r == pos[kk], 1.0, sel).astype(F32)
        sbuf[c * rc:(c + 1) * rc, :] = _pack_bf16_pairs(_dot(sel.astype(BF16), h))
    _run_copies(i, run_ref, dst_ref, boffs_ref, xs_ref, sbuf, sem, True)


def _sort_dispatch_call(t, idx_t, h2):
    n_tok, dh = h2.shape
    tb, tm = TM_ROWS, TM_EXPERT
    nblk = n_tok // tb
    grid_spec = pltpu.PrefetchScalarGridSpec(
        num_scalar_prefetch=5,
        grid=(nblk,),
        in_specs=[pl.BlockSpec((TOP_K, tb), lambda i, *_: (0, i)),
                  pl.BlockSpec((None, N_EXPERTS, 1), lambda i, *_: (i, 0, 0)),
                  pl.BlockSpec((tb, dh), lambda i, *_: (i, 0))],
        out_specs=pl.BlockSpec(memory_space=pl.ANY),
        scratch_shapes=[pltpu.VMEM((SORT_ROWS, dh), h2.dtype), pltpu.VMEM((tm, dh), h2.dtype),
                        pltpu.SemaphoreType.DMA, pltpu.SemaphoreType.DMA],
    )
    return pl.pallas_call(
        functools.partial(_sort_dispatch_kernel, n_blocks=t["n_blocks"]),
        grid_spec=grid_spec,
        out_shape=jax.ShapeDtypeStruct((t["n_blocks"] * tm, dh), h2.dtype),
        compiler_params=_cparams("arbitrary"),
    )(t["run"], t["dst"], t["boff"], t["pad_end"], t["n_used"], idx_t, t["boff_f"][:, :, None], h2)


def _sort_combine_kernel(run_ref, dst_ref, boffs_ref, idx_ref, p_ref, boff_ref, yb_hbm, x_ref, gate_ref,
                         fg_ref, o_ref, sbuf, sem, *, final_norm, blk0):
    i = pl.program_id(0)
    tb = x_ref.shape[0]

    @pl.when(i == 0)
    def _():
        sbuf[...] = jnp.zeros_like(sbuf)

    _run_copies(blk0 + i, run_ref, dst_ref, boffs_ref, yb_hbm, sbuf, sem, False)

    row = lax.broadcasted_iota(jnp.int32, (tb, tb), 0)
    col = lax.broadcasted_iota(jnp.int32, (tb, tb), 1)
    earlier = (col < row).astype(F32).astype(BF16)
    expert = lax.broadcasted_iota(jnp.int32, (tb, N_EXPERTS), 1)
    base = boff_ref[...]
    pos = []
    for kk in range(TOP_K):
        onehot = (expert == idx_ref[:, kk:kk + 1]).astype(F32)
        before = _dot(earlier, onehot.astype(BF16)) + base
        pos.append(jnp.sum(before * onehot, axis=1, keepdims=True).astype(jnp.int32))
        base = base + jnp.sum(onehot, axis=0, keepdims=True)

    rc = SORT_CHUNK
    f = jnp.zeros(x_ref.shape, F32)
    for c in range(sbuf.shape[0] // rc):
        r = lax.broadcasted_iota(jnp.int32, (tb, rc), 1) + c * rc
        wgt = jnp.zeros((tb, rc), F32)
        for kk in range(TOP_K):
            wgt = jnp.where(r == pos[kk], p_ref[:, kk:kk + 1], wgt)
        lo, hi = _unpack_bf16_pairs(sbuf[c * rc:(c + 1) * rc, :])
        y = jnp.concatenate([lo, hi], axis=1).astype(BF16)
        f = f + _dot(wgt.astype(BF16), y)
    xn = x_ref[...] + gate_ref[...] * f
    if final_norm:
        xn = _rms(xn, fg_ref[...])
    o_ref[...] = xn


def _sort_combine_call(t, idx_tok, probs, yb, x, mods, group_of_block, final_g, tok0, final_norm):
    bsz, n, d = x.shape
    tb = TM_ROWS
    rows = bsz * n
    assert rows % tb == 0 and tok0 % tb == 0 and (n % tb == 0 or tb % n == 0)
    blk0 = tok0 // tb
    grid_spec = pltpu.PrefetchScalarGridSpec(
        num_scalar_prefetch=3,
        grid=(rows // tb,),
        in_specs=[pl.BlockSpec((tb, TOP_K), lambda i, *_: (blk0 + i, 0)),
                  pl.BlockSpec((tb, TOP_K), lambda i, *_: (blk0 + i, 0)),
                  pl.BlockSpec((None, 1, N_EXPERTS), lambda i, *_: (blk0 + i, 0, 0)),
                  pl.BlockSpec(memory_space=pl.ANY),
                  pl.BlockSpec((tb, d), lambda i, *_: (i, 0)),
                  pl.BlockSpec((None, None, 1, d), lambda i, *_: (group_of_block(i), 5, 0, 0)),
                  pl.BlockSpec((1, d), lambda i, *_: (0, 0))],
        out_specs=pl.BlockSpec((tb, d), lambda i, *_: (i, 0)),
        scratch_shapes=[pltpu.VMEM((SORT_ROWS, yb.shape[1]), yb.dtype), pltpu.SemaphoreType.DMA],
    )
    out = pl.pallas_call(
        functools.partial(_sort_combine_kernel, final_norm=final_norm, blk0=blk0),
        grid_spec=grid_spec,
        out_shape=jax.ShapeDtypeStruct((rows, d), F32),
        compiler_params=_cparams("arbitrary"),
    )(t["run"], t["dst"], t["boff"], idx_tok, probs, t["boff_f"][:, None, :], yb, x.reshape(rows, d), mods,
      final_g.reshape(1, d))
    return out.reshape(bsz, n, d)


def kernel(x, c, ctx, c_ctx, norm1_g, norm2_g, w_mod, b_mod, w_in, gla_wg2_f, gla_bg_f, gla_wg2_b,
           gla_bg_b, gla_norm_g, conv_w, conv_b, conv_ln_g, conv_ln_b, w_out, router_w, router_b,
           exp_w_gu, exp_b_gu, exp_w_dn, exp_b_dn, final_norm_g):
    bsz, seq, d = x.shape
    n_ctx = ctx.shape[1]
    depth = w_mod.shape[0]
    assert d == D_MODEL and seq % (FFT_N2 * FFT_KB) == 0 and seq % n_ctx == 0
    lt = seq + n_ctx
    ctx_group = bsz

    rows = 8
    cvec = jnp.concatenate([c, c_ctx[None, :], jnp.zeros((rows - bsz - 1, d), F32)], axis=0)
    mods_all = _mod_call(cvec, w_mod, b_mod).reshape(depth, rows, 6, 1, d)

    def pack_w_in(w):
        o = [0, 256, 512, 768, 1280, 1792, 1808, 1824, 2336]
        parts = [w[:, o[0]:o[1]], w[:, o[1]:o[2]], w[:, o[2]:o[3]], w[:, o[3]:o[4]], w[:, o[4]:o[5]],
                 w[:, o[7]:o[7] + CONV_WIDTH], w[:, o[7] + CONV_WIDTH:o[8]], w[:, o[5]:o[7]],
                 jnp.zeros((d, GLR_PAD - 2 * GLA_GATE_RANK), w.dtype)]
        return jnp.concatenate(parts, axis=1).astype(BF16)

    long_tables = _dft_tables(seq)
    chan = _channel_tables()
    x_lat, x_ctx = x, ctx
    lat_group = lambda b: b
    ctx_group_fn = lambda b: ctx_group

    for layer in range(depth):
        last = layer == depth - 1
        mods = mods_all[layer]
        w_in_p = pack_w_in(w_in[layer])
        wg_pad = jnp.zeros((GLR_PAD, 2 * GLA_KEY_WIDTH), F32)
        wg_pad = wg_pad.at[:GLA_GATE_RANK, :GLA_KEY_WIDTH].set(gla_wg2_f[layer])
        wg_pad = wg_pad.at[GLA_GATE_RANK:2 * GLA_GATE_RANK, GLA_KEY_WIDTH:].set(gla_wg2_b[layer])
        bg_cat = jnp.concatenate([gla_bg_f[layer], gla_bg_b[layer]])[None, :]
        w_out_b = w_out[layer].astype(BF16)
        gn_tiled = jnp.tile(gla_norm_g[layer], GLA_HEADS)[None, :]
        rw_t = router_w[layer].T
        rw_hi = rw_t.astype(BF16)
        rw_lo = (rw_t - rw_hi.astype(F32)).astype(BF16)
        rb = router_b[layer][:, None]

        (u_l, r_l, glu_l), comb = _inproj_call(x_lat, mods, lat_group, norm1_g[layer], w_in_p, lt, 0)
        (u_c, r_c, glu_c), comb = _inproj_call(x_ctx, mods, ctx_group_fn, norm1_g[layer], w_in_p, lt,
                                               seq, combined=comb)
        o_f, o_b = _gla_call(*comb, wg_pad, bg_cat, seq, n_ctx)
        yf_l = _fourier_long(u_l, long_tables, chan)
        cv_l = _conv_call(glu_l, seq // (seq // GRID_W), conv_w[layer], conv_b[layer],
                          conv_ln_g[layer], conv_ln_b[layer])
        n_tok = bsz * seq + (0 if last else bsz * n_ctx)
        x_lat, routed = _outproj_call(yf_l, o_f, o_b, 0, r_l, cv_l, x_lat, mods, lat_group, gn_tiled,
                                      w_out_b, norm2_g[layer], rw_hi, rw_lo, rb, n_tok, 0)
        if not last:
            yf_c = _fourier_short(u_c, chan)
            cv_c = _conv_call(glu_c, n_ctx, conv_w[layer], conv_b[layer], conv_ln_g[layer],
                              conv_ln_b[layer])
            x_ctx, routed = _outproj_call(yf_c, o_f, o_b, seq, r_c, cv_c, x_ctx, mods, ctx_group_fn,
                                          gn_tiled, w_out_b, norm2_g[layer], rw_hi, rw_lo, rb, n_tok,
                                          bsz * seq, carried=routed)

        h2, idx_t, prob_t = routed
        runs = _run_tables(idx_t, n_tok)
        xs = _sort_dispatch_call(runs, idx_t, h2)
        yb = _expert_call(runs["block_e"], runs["n_used"], xs, layer, exp_w_gu, exp_b_gu, exp_w_dn,
                          exp_b_dn)
        probs, idx_tok = prob_t.T, idx_t.T
        lat_blocks = seq // TM_ROWS
        x_lat = _sort_combine_call(runs, idx_tok, probs, yb, x_lat, mods, lambda i: i // lat_blocks,
                                   final_norm_g, 0, last)
        if not last:
            x_ctx = _sort_combine_call(runs, idx_tok, probs, yb, x_ctx, mods, lambda i: ctx_group,
                                       final_norm_g, bsz * seq, False)

    return x_lat
```

```python
import functools

import jax
import jax.numpy as jnp
from jax import lax
from jax.experimental import pallas as pl
from jax.experimental.pallas import tpu as pltpu

F32 = jnp.float32
BF16 = jnp.bfloat16
U32 = jnp.uint32
HIGH_HALF = 0xFFFF0000

D_MODEL = 1024
DEPTH = 2
GRID_W = 64
FOURIER_WIDTH = 256
FOURIER_HEADS = 4
FOURIER_HEAD_DIM = FOURIER_WIDTH // FOURIER_HEADS
GLA_HEADS = 4
GLA_KEY_WIDTH = 256
GLA_VALUE_WIDTH = 512
GLA_DK = GLA_KEY_WIDTH // GLA_HEADS
GLA_DV = GLA_VALUE_WIDTH // GLA_HEADS
GLA_GATE_RANK = 16
GLA_GATE_NORMALIZER = 16.0
CONV_WIDTH = 256
CONV_KERNEL = 31
N_EXPERTS = 32
TOP_K = 4
D_FF = D_MODEL
SWIGLU_LIMIT = 7.0
SWIGLU_ALPHA = 1.702
NORM_EPS = 1e-6

LANES = 128
VMEM_LIMIT = 56 * 1024 * 1024

COL_U = 0
COL_Q = COL_U + FOURIER_WIDTH
COL_K = COL_Q + GLA_KEY_WIDTH
COL_V = COL_K + GLA_KEY_WIDTH
COL_R = COL_V + GLA_VALUE_WIDTH
COL_CA = COL_R + GLA_VALUE_WIDTH
COL_CG = COL_CA + CONV_WIDTH
COL_GL = COL_CG + CONV_WIDTH
GLR_PAD = LANES
IN_PAD = COL_GL + GLR_PAD

GLA_CHUNK = 128
GLA_STEP_CHUNKS = 2
TM_ROWS = 512
TM_EXPERT = 512
EXPERT_CHUNKS = 4
RUN_ALIGN = 8
SORT_CHUNK = 256
SORT_ROWS = TOP_K * TM_ROWS + RUN_ALIGN * N_EXPERTS
FFT_N2 = 128
FFT_KB = 8
CONV_HALO = 16
CONV_GROUP = 8
CONV_PITCH = 100


def _cparams(*sem):
    return pltpu.CompilerParams(dimension_semantics=sem, vmem_limit_bytes=VMEM_LIMIT)


def _dot(a, b):
    return jnp.dot(a, b, preferred_element_type=F32)


def _dot_nt(a, b):
    return lax.dot_general(a, b, (((1,), (1,)), ((), ())), preferred_element_type=F32)


def _dot_tn(a, b):
    return lax.dot_general(a, b, (((0,), (0,)), ((), ())), preferred_element_type=F32)


def _split(a):
    hi = a.astype(BF16)
    lo = (a - hi.astype(F32)).astype(BF16)
    return hi, lo


def _dot3(a, b):
    ah, al = _split(a)
    bh, bl = _split(b)
    return _dot(ah, bh) + _dot(ah, bl) + _dot(al, bh)


def _sigmoid(x):
    return 1.0 / (1.0 + jnp.exp(-x))


def _pack_bf16_pairs(a):
    h = a.shape[1] // 2
    bits = lax.bitcast_convert_type(a.astype(BF16).astype(F32), U32)
    return (bits[:, :h] >> 16) | (bits[:, h:] & U32(HIGH_HALF))


def _unpack_bf16_pairs(w):
    return (lax.bitcast_convert_type(w << 16, F32), lax.bitcast_convert_type(w & U32(HIGH_HALF), F32))


def _rms(x, g):
    ms = jnp.mean(x * x, axis=-1, keepdims=True)
    return x * lax.rsqrt(ms + NORM_EPS) * g


def _mod_kernel(cv_ref, w_ref, b_ref, o_ref):
    cv = cv_ref[...]
    a = cv * _sigmoid(cv)
    o_ref[...] = _dot3(a, w_ref[...]) + b_ref[...]


def _mod_call(cvec, w_mod, b_mod):
    depth, d, n = w_mod.shape
    rows = cvec.shape[0]
    tn = 1536
    return pl.pallas_call(
        _mod_kernel,
        grid=(depth, n // tn),
        in_specs=[
            pl.BlockSpec((rows, d), lambda l, j: (0, 0)),
            pl.BlockSpec((None, d, tn), lambda l, j: (l, 0, j)),
            pl.BlockSpec((None, 1, tn), lambda l, j: (l, 0, j)),
        ],
        out_specs=pl.BlockSpec((None, rows, tn), lambda l, j: (l, 0, j)),
        out_shape=jax.ShapeDtypeStruct((depth, rows, n), F32),
        compiler_params=_cparams("arbitrary", "arbitrary"),
    )(cvec, w_mod, b_mod.reshape(depth, 1, n))


def _inproj_kernel(x_ref, g_ref, sh_ref, sc_ref, w_ref, *rest):
    u_ref, r_ref, glu_ref, q_ref, k_ref, v_ref, gl_ref = rest[-7:]
    x = x_ref[...]
    h = _rms(x, g_ref[...]) * (1.0 + sc_ref[...]) + sh_ref[...]
    p = _dot(h.astype(BF16), w_ref[...])
    u_ref[...] = p[:, COL_U:COL_Q].astype(u_ref.dtype)
    q_ref[...] = (p[:, COL_Q:COL_K] * (GLA_DK ** -0.5)).astype(q_ref.dtype)
    k_ref[...] = p[:, COL_K:COL_V].astype(k_ref.dtype)
    v_ref[...] = p[:, COL_V:COL_R].astype(v_ref.dtype)
    r_ref[...] = p[:, COL_R:COL_CA].astype(r_ref.dtype)
    glu_ref[...] = (p[:, COL_CA:COL_CG] * _sigmoid(p[:, COL_CG:COL_GL])).astype(glu_ref.dtype)
    gl_ref[...] = p[:, COL_GL:IN_PAD]


def _inproj_call(x, mods, group_of_batch, norm_g, w_in_p, lt, row0, combined=None):
    bsz, n, d = x.shape
    tm = min(TM_ROWS, n)
    assert n % tm == 0 and row0 % tm == 0
    blk0 = row0 // tm
    nb = n // tm
    steps = nb + (-(-(lt - n) // tm) if combined is None else 0)
    widths = (GLA_KEY_WIDTH, GLA_KEY_WIDTH, GLA_VALUE_WIDTH, GLR_PAD)
    dtypes = (BF16, BF16, BF16, F32)
    row_spec = lambda w: pl.BlockSpec((None, tm, w), lambda b, i: (b, jnp.minimum(i, nb - 1), 0))
    comb_spec = lambda w: pl.BlockSpec((None, tm, w), lambda b, i: (b, blk0 + i, 0))
    mod_spec = lambda which: pl.BlockSpec(
        (None, None, 1, d), lambda b, i: (group_of_batch(b), which, 0, 0))
    in_specs = [
        row_spec(d),
        pl.BlockSpec((1, d), lambda b, i: (0, 0)),
        mod_spec(0), mod_spec(1),
        pl.BlockSpec((d, IN_PAD), lambda b, i: (0, 0)),
    ]
    args = [x, norm_g.reshape(1, d), mods, mods, w_in_p]
    aliases = {}
    if combined is not None:
        for t, arr in enumerate(combined):
            in_specs.append(pl.BlockSpec(memory_space=pl.ANY))
            aliases[len(args)] = 3 + t
            args.append(arr)
    out_shape = [
        jax.ShapeDtypeStruct((bsz, n, FOURIER_WIDTH), BF16),
        jax.ShapeDtypeStruct((bsz, n, GLA_VALUE_WIDTH), BF16),
        jax.ShapeDtypeStruct((bsz, n, CONV_WIDTH), BF16),
    ] + [jax.ShapeDtypeStruct((bsz, lt, w), dt) for w, dt in zip(widths, dtypes)]
    out_specs = [row_spec(FOURIER_WIDTH), row_spec(GLA_VALUE_WIDTH), row_spec(CONV_WIDTH)] + [
        comb_spec(w) for w in widths]
    outs = pl.pallas_call(
        _inproj_kernel,
        grid=(bsz, steps),
        in_specs=in_specs,
        out_specs=out_specs,
        out_shape=out_shape,
        input_output_aliases=aliases,
        compiler_params=_cparams("arbitrary", "arbitrary"),
    )(*args)
    return outs[:3], outs[3:]


def _dft_tables(length):
    n2 = FFT_N2
    n1 = length // n2
    two_pi = 2.0 * jnp.pi

    def cs(num, den):
        ang = (num % den).astype(F32) * (two_pi / den)
        return jnp.cos(ang), jnp.sin(ang)

    k1 = jnp.arange(n1, dtype=jnp.int32)
    c1, s1 = cs(k1[:, None] * k1[None, :], n1)
    stage1 = (jnp.concatenate([c1, -s1], axis=0) * (n1 ** -0.5)).astype(BF16)
    k2 = jnp.arange(n2, dtype=jnp.int32)
    ct, st = cs(k1[:, None] * k2[None, :], length)
    cf, sf = cs(k2[:, None] * k2[None, :], n2)
    scale = n2 ** -0.5
    mr = (ct[:, None, :] * cf[None] - st[:, None, :] * sf[None]) * scale
    mi = -(st[:, None, :] * cf[None] + ct[:, None, :] * sf[None]) * scale
    stage2 = jnp.concatenate([jnp.concatenate([mr, -mi], axis=2),
                              jnp.concatenate([mi, mr], axis=2)], axis=1).astype(BF16)
    return stage1, stage2


def _channel_tables():
    hd = FOURIER_HEAD_DIM
    c = jnp.arange(FOURIER_WIDTH, dtype=jnp.int32)
    same_head = (c[:, None] // hd) == (c[None, :] // hd)
    ang = (((c[:, None] % hd) * (c[None, :] % hd)) % hd).astype(F32) * (2.0 * jnp.pi / hd)
    scale = hd ** -0.5
    bdc = jnp.where(same_head, jnp.cos(ang) * scale, 0.0).astype(BF16)
    bds = jnp.where(same_head, jnp.sin(ang) * scale, 0.0).astype(BF16)
    return bdc, bds


def _fft1_kernel(x_ref, cs_ref, zr_ref, zi_ref):
    n1 = x_ref.shape[0]
    z = _dot(cs_ref[...], x_ref[...])
    zr_ref[...] = z[:n1].astype(zr_ref.dtype)
    zi_ref[...] = z[n1:].astype(zi_ref.dtype)


def _fft2_kernel(zr_ref, zi_ref, m_ref, bdc_ref, bds_ref, o_ref):
    kb, n2, w = zr_ref.shape
    for j in range(kb):
        z = jnp.concatenate([zr_ref[j], zi_ref[j]], axis=0)
        a = _dot(m_ref[j], z)
        y = _dot(a[:n2].astype(BF16), bdc_ref[...]) + _dot(a[n2:].astype(BF16), bds_ref[...])
        o_ref[:, j * w:(j + 1) * w] = y.astype(o_ref.dtype)


def _fourier_long(u, tables, chan):
    bsz, length, w = u.shape
    stage1, stage2 = tables
    bdc, bds = chan
    n2 = FFT_N2
    n1 = length // n2
    tn = 4096
    cols = n2 * w
    zr, zi = pl.pallas_call(
        _fft1_kernel,
        grid=(bsz, cols // tn),
        in_specs=[pl.BlockSpec((None, n1, tn), lambda b, j: (b, 0, j)),
                  pl.BlockSpec((2 * n1, n1), lambda b, j: (0, 0))],
        out_specs=[pl.BlockSpec((None, n1, tn), lambda b, j: (b, 0, j))] * 2,
        out_shape=[jax.ShapeDtypeStruct((bsz, n1, cols), BF16)] * 2,
        compiler_params=_cparams("arbitrary", "arbitrary"),
    )(u.reshape(bsz, n1, cols), stage1)
    kb = FFT_KB
    z_spec = pl.BlockSpec((None, kb, n2, w), lambda b, j: (b, j, 0, 0))
    y = pl.pallas_call(
        _fft2_kernel,
        grid=(bsz, n1 // kb),
        in_specs=[z_spec, z_spec,
                  pl.BlockSpec((kb, 2 * n2, 2 * n2), lambda b, j: (j, 0, 0)),
                  pl.BlockSpec((w, w), lambda b, j: (0, 0)),
                  pl.BlockSpec((w, w), lambda b, j: (0, 0))],
        out_specs=pl.BlockSpec((None, n2, kb * w), lambda b, j: (b, 0, j)),
        out_shape=jax.ShapeDtypeStruct((bsz, n2, n1 * w), BF16),
        compiler_params=_cparams("arbitrary", "arbitrary"),
    )(zr.reshape(bsz, n1, n2, w), zi.reshape(bsz, n1, n2, w), stage2, bdc, bds)
    return y.reshape(bsz, length, w)


def _dft_short_kernel(u_ref, c_ref, s_ref, bdc_ref, bds_ref, o_ref):
    u = u_ref[...]
    p = _dot(u, bdc_ref[...]).astype(BF16)
    q = _dot(u, bds_ref[...]).astype(BF16)
    o_ref[...] = (_dot(c_ref[...], p) - _dot(s_ref[...], q)).astype(o_ref.dtype)


def _fourier_short(u, chan):
    bsz, length, w = u.shape
    bdc, bds = chan
    k = jnp.arange(length, dtype=jnp.int32)
    ang = ((k[:, None] * k[None, :]) % length).astype(F32) * (2.0 * jnp.pi / length)
    c = (jnp.cos(ang) * length ** -0.5).astype(BF16)
    s = (jnp.sin(ang) * length ** -0.5).astype(BF16)
    full = lambda n: pl.BlockSpec((n, n), lambda b: (0, 0))
    return pl.pallas_call(
        _dft_short_kernel,
        grid=(bsz,),
        in_specs=[pl.BlockSpec((None, length, w), lambda b: (b, 0, 0)),
                  full(length), full(length), full(w), full(w)],
        out_specs=pl.BlockSpec((None, length, w), lambda b: (b, 0, 0)),
        out_shape=jax.ShapeDtypeStruct((bsz, length, w), BF16),
        compiler_params=_cparams("arbitrary"),
    )(u, c, s, bdc, bds)


def _conv_kernel(x_ref, w_ref, cb_ref, lg_ref, lb_ref, o_ref, pad_ref, *, seg):
    nseg = x_ref.shape[0] // seg
    width = x_ref.shape[1]
    halo = jnp.zeros((CONV_HALO, width), F32)
    for s in range(nseg):
        pad_ref[s, 0:CONV_HALO, :] = halo
        pad_ref[s, CONV_HALO:CONV_HALO + seg, :] = x_ref[s * seg:(s + 1) * seg, :].astype(F32)
        pad_ref[s, CONV_HALO + seg:2 * CONV_HALO + seg, :] = halo
    first = CONV_HALO - CONV_KERNEL // 2
    sub = 8
    span = seg + 2 * CONV_HALO - sub
    for s in range(nseg):
        acc = jnp.zeros((seg, width), F32)
        for r in range(sub):
            shifted = pad_ref[s, r:r + span, :]
            for a in range((span - seg) // sub + 1):
                j = a * sub + r - first
                if 0 <= j < CONV_KERNEL:
                    acc = acc + shifted[a * sub:a * sub + seg, :] * w_ref[j:j + 1, :]
        y = acc + cb_ref[...]
        mu = jnp.mean(y, axis=-1, keepdims=True)
        yc = y - mu
        var = jnp.mean(yc * yc, axis=-1, keepdims=True)
        z = yc * lax.rsqrt(var + NORM_EPS) * lg_ref[...] + lb_ref[...]
        o_ref[s * seg:(s + 1) * seg, :] = (z * _sigmoid(z)).astype(o_ref.dtype)


def _conv_group_kernel(x_ref, w_ref, cb_ref, lg_ref, lb_ref, o_ref, pad_ref, acc_ref, *, seg):
    g, p = CONV_GROUP, CONV_PITCH
    halves = x_ref.shape[1] // LANES
    pad_ref[...] = jnp.zeros_like(pad_ref)
    for s in range(g):
        for h in range(halves):
            pad_ref[h, s * p + CONV_HALO:s * p + CONV_HALO + seg, :] = (
                x_ref[s * seg:(s + 1) * seg, h * LANES:(h + 1) * LANES].astype(F32))
    first = CONV_HALO - CONV_KERNEL // 2
    unroll = 4

    def body(i, carry):
        t0 = unroll * i
        for h in range(halves):
            accs = [None] * unroll
            for j in range(CONV_KERNEL):
                tap = w_ref[h, j]
                for u in range(unroll):
                    term = pad_ref[h, pl.ds(t0 + u + first + j, g, stride=p), :] * tap
                    accs[u] = term if accs[u] is None else accs[u] + term
            for u in range(unroll):
                acc_ref[h, pl.ds(t0 + u, g, stride=p), :] = accs[u]
        return carry

    lax.fori_loop(0, seg // unroll, body, 0)
    for s in range(g):
        y = jnp.concatenate([acc_ref[h, s * p:s * p + seg, :] for h in range(halves)], axis=1) + cb_ref[...]
        mu = jnp.mean(y, axis=-1, keepdims=True)
        yc = y - mu
        var = jnp.mean(yc * yc, axis=-1, keepdims=True)
        z = yc * lax.rsqrt(var + NORM_EPS) * lg_ref[...] + lb_ref[...]
        o_ref[s * seg:(s + 1) * seg, :] = (z * _sigmoid(z)).astype(o_ref.dtype)


def _conv_call(glu, seg, conv_w, conv_b, ln_g, ln_b):
    bsz, n, w = glu.shape
    t = max(seg, min(TM_ROWS, n))
    assert n % t == 0 and t % seg == 0
    vec = lambda: pl.BlockSpec((1, w), lambda b, i: (0, 0))
    if t // seg == CONV_GROUP and seg + 2 * CONV_HALO <= CONV_PITCH and w % LANES == 0:
        halves = w // LANES
        taps = jnp.broadcast_to(conv_w.reshape(CONV_KERNEL, halves, 1, LANES).transpose(1, 0, 2, 3),
                                (halves, CONV_KERNEL, 8, LANES))
        slab = pltpu.VMEM((halves, CONV_GROUP * CONV_PITCH, LANES), F32)
        return pl.pallas_call(
            functools.partial(_conv_group_kernel, seg=seg),
            grid=(bsz, n // t),
            in_specs=[pl.BlockSpec((None, t, w), lambda b, i: (b, i, 0)),
                      pl.BlockSpec(taps.shape, lambda b, i: (0, 0, 0, 0)),
                      vec(), vec(), vec()],
            out_specs=pl.BlockSpec((None, t, w), lambda b, i: (b, i, 0)),
            out_shape=jax.ShapeDtypeStruct((bsz, n, w), BF16),
            scratch_shapes=[slab, slab],
            compiler_params=_cparams("arbitrary", "arbitrary"),
        )(glu, taps, conv_b.reshape(1, w), ln_g.reshape(1, w), ln_b.reshape(1, w))
    return pl.pallas_call(
        functools.partial(_conv_kernel, seg=seg),
        grid=(bsz, n // t),
        in_specs=[pl.BlockSpec((None, t, w), lambda b, i: (b, i, 0)),
                  pl.BlockSpec((CONV_KERNEL, w), lambda b, i: (0, 0)),
                  vec(), vec(), vec()],
        out_specs=pl.BlockSpec((None, t, w), lambda b, i: (b, i, 0)),
        out_shape=jax.ShapeDtypeStruct((bsz, n, w), BF16),
        scratch_shapes=[pltpu.VMEM((t // seg, seg + 2 * CONV_HALO, w), F32)],
        compiler_params=_cparams("arbitrary", "arbitrary"),
    )(glu, conv_w, conv_b.reshape(1, w), ln_g.reshape(1, w), ln_b.reshape(1, w))


def _gla_direction(q_ref, k_ref, v_ref, gl_ref, wg_ref, bg_ref, o_ref, st, reverse):
    c = GLA_CHUNK
    n_sub = q_ref.shape[0] // c
    for s in (reversed(range(n_sub)) if reverse else range(n_sub)):
        rows = slice(s * c, (s + 1) * c)
        st = _gla_chunk(q_ref[rows, :], k_ref[rows, :], v_ref[rows, :], gl_ref[rows, :], wg_ref, bg_ref,
                        o_ref.at[rows, :], st, reverse)
    return st


def _gla_chunk(q, k, v, glr, wg_ref, bg_ref, o_ref, st, reverse):
    c = GLA_CHUNK
    kw = GLA_KEY_WIDTH
    col0 = kw if reverse else 0
    pre = _dot3(glr, wg_ref[:, col0:col0 + kw]) + bg_ref[:, col0:col0 + kw]
    g = (jnp.minimum(pre, 0.0) - jnp.log(1.0 + jnp.exp(-jnp.abs(pre)))) * (1.0 / GLA_GATE_NORMALIZER)
    row = lax.broadcasted_iota(jnp.int32, (c, c), 0)
    col = lax.broadcasted_iota(jnp.int32, (c, c), 1)
    seen = (col >= row) if reverse else (col <= row)
    tri = jnp.where(seen, 1.0, 0.0).astype(BF16)
    gh, gl = _split(g)
    b = _dot(tri, gh) + _dot(tri, gl)
    mid = c // 2 if reverse else c // 2 - 1
    last = 0 if reverse else c - 1
    b_mid = b[mid:mid + 1, :]
    b_last = b[last:last + 1, :]
    q = q.astype(F32)
    k = k.astype(F32)
    qe = q * jnp.exp(b - b_mid)
    ke = k * jnp.exp(b_mid - b)
    kd = k * jnp.exp(b_last - b)
    head_of_lane = lax.broadcasted_iota(jnp.int32, (1, kw), 1) // GLA_DK
    q_heads = jnp.concatenate(
        [jnp.where(head_of_lane == h, qe, 0.0) for h in range(GLA_HEADS)], axis=0).astype(BF16)
    rhs = jnp.concatenate([ke, st * jnp.exp(b_mid)], axis=0).astype(BF16)
    res = _dot_nt(q_heads, rhs)
    outs = []
    for h in range(GLA_HEADS):
        blk = res[h * c:(h + 1) * c, :]
        scores = jnp.where(seen, blk[:, :c], 0.0).astype(BF16)
        outs.append(_dot(scores, v[:, h * GLA_DV:(h + 1) * GLA_DV]) + blk[:, c:])
    o_ref[...] = jnp.concatenate(outs, axis=1).astype(o_ref.dtype)
    kv = _dot_tn(v, kd.astype(BF16))
    ds = jnp.zeros_like(st)
    for h in range(GLA_HEADS):
        ds = ds + jnp.where(head_of_lane == h, kv[h * GLA_DV:(h + 1) * GLA_DV, :], 0.0)
    return st * jnp.exp(b_last) + ds


def _gla_kernel(qf, kf, vf, gf, qb, kb, vb, gb, wg_ref, bg_ref, of_ref, ob_ref, sf_ref, sb_ref):
    @pl.when(pl.program_id(0) == 0)
    def _():
        sf_ref[...] = jnp.zeros_like(sf_ref)
        sb_ref[...] = jnp.zeros_like(sb_ref)

    bsz = qf.shape[0]
    states = [(sf_ref[b], sb_ref[b]) for b in range(bsz)]
    new = []
    for b, (s_f, s_b) in enumerate(states):
        new.append((
            _gla_direction(qf.at[b], kf.at[b], vf.at[b], gf.at[b], wg_ref, bg_ref, of_ref.at[b], s_f, False),
            _gla_direction(qb.at[b], kb.at[b], vb.at[b], gb.at[b], wg_ref, bg_ref, ob_ref.at[b], s_b, True)))
    for b, (s_f, s_b) in enumerate(new):
        sf_ref[b] = s_f
        sb_ref[b] = s_b


def _gla_call(q, k, v, gl, wg_pad, bg_cat, n_lat, n_ctx):
    bsz, lt, _ = q.shape
    c = GLA_STEP_CHUNKS * GLA_CHUNK
    assert n_lat % c == 0 and n_ctx % c == 0
    cl, cc = n_lat // c, n_ctx // c

    def fwd_blk(j):
        return jnp.where(j < cc, cl + j, j - cc)

    def bwd_blk(j):
        return jnp.where(j < cc, cl + cc - 1 - j, cl - 1 - (j - cc))

    def spec(w, blk):
        return pl.BlockSpec((bsz, c, w), lambda j: (0, blk(j), 0))

    widths = (GLA_KEY_WIDTH, GLA_KEY_WIDTH, GLA_VALUE_WIDTH, GLR_PAD)
    in_specs = [spec(w, fwd_blk) for w in widths] + [spec(w, bwd_blk) for w in widths] + [
        pl.BlockSpec(wg_pad.shape, lambda j: (0, 0)),
        pl.BlockSpec(bg_cat.shape, lambda j: (0, 0))]
    return pl.pallas_call(
        _gla_kernel,
        grid=(cl + cc,),
        in_specs=in_specs,
        out_specs=[spec(GLA_VALUE_WIDTH, fwd_blk), spec(GLA_VALUE_WIDTH, bwd_blk)],
        out_shape=[jax.ShapeDtypeStruct((bsz, lt, GLA_VALUE_WIDTH), F32)] * 2,
        scratch_shapes=[pltpu.VMEM((bsz, GLA_DV, GLA_KEY_WIDTH), F32)] * 2,
        compiler_params=_cparams("arbitrary"),
    )(q, k, v, gl, q, k, v, gl, wg_pad, bg_cat)


def _outproj_kernel(yf_ref, of_ref, ob_ref, r_ref, cv_ref, x_ref, gate_ref, sh_ref, sc_ref,
                    gn_ref, wo_ref, n2_ref, rwh_ref, rwl_ref, rb_ref, *rest):
    xo_ref, h2_ref, idx_ref, prob_ref = rest[-4:]
    o = of_ref[...] + ob_ref[...]
    heads = []
    for h in range(GLA_HEADS):
        oh = o[:, h * GLA_DV:(h + 1) * GLA_DV]
        heads.append(oh * lax.rsqrt(jnp.mean(oh * oh, axis=-1, keepdims=True) + NORM_EPS))
    r = r_ref[...].astype(F32)
    gla = jnp.concatenate(heads, axis=1) * gn_ref[...] * (r * _sigmoid(r))
    c0, c1 = FOURIER_WIDTH, FOURIER_WIDTH + GLA_VALUE_WIDTH
    y = (_dot(yf_ref[...], wo_ref[0:c0, :]) + _dot(gla.astype(BF16), wo_ref[c0:c1, :])
         + _dot(cv_ref[...], wo_ref[c1:, :]))
    xn = x_ref[...] + gate_ref[...] * y
    xo_ref[...] = xn
    h2 = _rms(xn, n2_ref[...]) * (1.0 + sc_ref[...]) + sh_ref[...]
    hh, hl = _split(h2)
    h2_ref[...] = _pack_bf16_pairs(h2)
    logits = (_dot_nt(rwh_ref[...], hh) + _dot_nt(rwh_ref[...], hl) + _dot_nt(rwl_ref[...], hh)
              + rb_ref[...])
    expert = lax.broadcasted_iota(jnp.int32, logits.shape, 0)
    vals, idxs = [], []
    cur = logits
    for _ in range(TOP_K):
        m = jnp.max(cur, axis=0, keepdims=True)
        ix = jnp.min(jnp.where(cur == m, expert, N_EXPERTS), axis=0, keepdims=True)
        vals.append(m)
        idxs.append(ix)
        cur = jnp.where(expert == ix, -jnp.inf, cur)
    es = [jnp.exp(vv - vals[0]) for vv in vals]
    inv = 1.0 / functools.reduce(lambda a, b: a + b, es)
    idx_ref[...] = jnp.concatenate(idxs, axis=0)
    prob_ref[...] = jnp.concatenate([e * inv for e in es], axis=0)


def _outproj_call(yf, o_f, o_b, o_row0, r, cv, x, mods, group_of_batch, gn_tiled, w_out, norm2_g,
                  rw_hi, rw_lo, rb, n_tok, tok0, carried=None):
    bsz, n, d = x.shape
    tm = min(TM_ROWS, n)
    assert n % tm == 0 and o_row0 % tm == 0 and tok0 % tm == 0
    nb = n // tm
    spare = (n_tok - bsz * n) if carried is None else 0
    assert spare in (0, tm)
    steps = nb + spare // tm
    last = lambda i: jnp.minimum(i, nb - 1)
    row = lambda w: pl.BlockSpec((None, tm, w), lambda b, i: (b, last(i), 0))
    orow = pl.BlockSpec((None, tm, GLA_VALUE_WIDTH), lambda b, i: (b, o_row0 // tm + last(i), 0))
    mod = lambda which: pl.BlockSpec((None, None, 1, d), lambda b, i: (group_of_batch(b), which, 0, 0))
    const = lambda a: pl.BlockSpec(a.shape, lambda b, i: (0,) * a.ndim)
    consts = [gn_tiled, w_out, norm2_g.reshape(1, d), rw_hi, rw_lo, rb]
    in_specs = [row(FOURIER_WIDTH), orow, orow, row(GLA_VALUE_WIDTH), row(CONV_WIDTH), row(d),
                mod(2), mod(3), mod(4)] + [const(a) for a in consts]
    args = [yf, o_f, o_b, r, cv, x, mods, mods, mods] + consts
    aliases = {}
    if carried is not None:
        for t, arr in enumerate(carried):
            in_specs.append(pl.BlockSpec(memory_space=pl.ANY))
            aliases[len(args)] = 1 + t
            args.append(arr)
    tokblk = lambda b, i: tok0 // tm + jnp.where(
        jnp.logical_and(i == nb, b == bsz - 1), bsz * nb, b * nb + last(i))
    out_specs = [row(d),
                 pl.BlockSpec((tm, d // 2), lambda b, i: (tokblk(b, i), 0)),
                 pl.BlockSpec((TOP_K, tm), lambda b, i: (0, tokblk(b, i))),
                 pl.BlockSpec((TOP_K, tm), lambda b, i: (0, tokblk(b, i)))]
    out_shape = [jax.ShapeDtypeStruct((bsz, n, d), F32),
                 jax.ShapeDtypeStruct((n_tok, d // 2), U32),
                 jax.ShapeDtypeStruct((TOP_K, n_tok), jnp.int32),
                 jax.ShapeDtypeStruct((TOP_K, n_tok), F32)]
    outs = pl.pallas_call(
        _outproj_kernel,
        grid=(bsz, steps),
        in_specs=in_specs,
        out_specs=out_specs,
        out_shape=out_shape,
        input_output_aliases=aliases,
        compiler_params=_cparams("arbitrary", "arbitrary"),
    )(*args)
    return outs[0], outs[1:]


def _expert_kernel(be_ref, nu_ref, x_ref, wgu_ref, bgu_ref, wdn_ref, bdn_ref, o_ref, wgu_s, wdn_s):
    i = pl.program_id(0)
    nu = nu_ref[0]
    n_chunks = wgu_s.shape[0]
    fc = D_FF // n_chunks

    @pl.when(i >= nu)
    def _():
        o_ref[...] = jnp.zeros_like(o_ref)

    @pl.when(i < nu)
    def _():
        changed = jnp.logical_or(i == 0, be_ref[i] != be_ref[jnp.maximum(i - 1, 0)])

        @pl.when(changed)
        def _():
            rows = 128
            for s in range(wgu_ref.shape[0] // rows):
                rs = slice(s * rows, (s + 1) * rows)
                for n in range(n_chunks):
                    wgu_s[n, rs, 0:fc] = wgu_ref[rs, n * fc:(n + 1) * fc].astype(BF16)
                    wgu_s[n, rs, fc:2 * fc] = wgu_ref[rs, D_FF + n * fc:D_FF + (n + 1) * fc].astype(BF16)
                wdn_s[rs, :] = wdn_ref[rs, :].astype(BF16)

        x = jnp.concatenate(_unpack_bf16_pairs(x_ref[...]), axis=1).astype(BF16)
        acts = []
        for n in range(n_chunks):
            gate = _dot(x, wgu_s[n, :, 0:fc]) + bgu_ref[:, n * fc:(n + 1) * fc]
            up = _dot(x, wgu_s[n, :, fc:2 * fc]) + bgu_ref[:, D_FF + n * fc:D_FF + (n + 1) * fc]
            gate = jnp.minimum(gate, SWIGLU_LIMIT)
            up = jnp.clip(up, -SWIGLU_LIMIT, SWIGLU_LIMIT)
            acts.append((gate * _sigmoid(SWIGLU_ALPHA * gate) * (up + 1.0)).astype(BF16))
        half = n_chunks // 2
        y = (_dot(jnp.concatenate(acts[:half], axis=1), wdn_s[0:half * fc, :])
             + _dot(jnp.concatenate(acts[half:], axis=1), wdn_s[half * fc:, :]) + bdn_ref[...])
        o_ref[...] = _pack_bf16_pairs(y)


def _expert_call(block_e, n_used, xs, layer, w_gu, b_gu, w_dn, b_dn):
    rows, dh = xs.shape
    d = 2 * dh
    tm = TM_EXPERT
    nblk = rows // tm
    depth, e, _, f2 = w_gu.shape
    n_chunks = EXPERT_CHUNKS
    live = lambda i, nu: jnp.minimum(i, nu[0] - 1)
    wmap = lambda i, be, nu: (layer, be[live(i, nu)], 0, 0)
    grid_spec = pltpu.PrefetchScalarGridSpec(
        num_scalar_prefetch=2,
        grid=(nblk,),
        in_specs=[
            pl.BlockSpec((tm, dh), lambda i, be, nu: (live(i, nu), 0)),
            pl.BlockSpec((None, None, d, f2), wmap),
            pl.BlockSpec((None, None, 1, f2), wmap),
            pl.BlockSpec((None, None, f2 // 2, d), wmap),
            pl.BlockSpec((None, None, 1, d), wmap),
        ],
        out_specs=pl.BlockSpec((tm, dh), lambda i, be, nu: (i, 0)),
        scratch_shapes=[pltpu.VMEM((n_chunks, d, f2 // n_chunks), BF16), pltpu.VMEM((f2 // 2, d), BF16)],
    )
    return pl.pallas_call(
        _expert_kernel,
        grid_spec=grid_spec,
        out_shape=jax.ShapeDtypeStruct((rows, dh), xs.dtype),
        compiler_params=_cparams("arbitrary"),
    )(block_e, n_used, xs, w_gu, b_gu.reshape(depth, e, 1, f2), w_dn, b_dn.reshape(depth, e, 1, d))


def _count_kernel(idx_ref, cnt_ref):
    expert = lax.broadcasted_iota(jnp.int32, (N_EXPERTS, idx_ref.shape[1]), 0)
    total = jnp.zeros((N_EXPERTS, 1), F32)
    for kk in range(TOP_K):
        total = total + jnp.sum((expert == idx_ref[kk:kk + 1, :]).astype(F32), axis=1, keepdims=True)
    cnt_ref[...] = total.astype(jnp.int32)


def _run_tables(idx_t, n_tok):
    tm, tb = TM_EXPERT, TM_ROWS
    nblk = n_tok // tb
    assert n_tok % tb == 0
    cnt = pl.pallas_call(
        _count_kernel,
        grid=(nblk,),
        in_specs=[pl.BlockSpec((TOP_K, tb), lambda i: (0, i))],
        out_specs=pl.BlockSpec((None, N_EXPERTS, 1), lambda i: (i, 0, 0)),
        out_shape=jax.ShapeDtypeStruct((nblk, N_EXPERTS, 1), jnp.int32),
        compiler_params=_cparams("arbitrary"),
    )(idx_t)[:, :, 0]
    run = (cnt + RUN_ALIGN - 1) // RUN_ALIGN * RUN_ALIGN
    total = jnp.sum(run, axis=0)
    padded = (total + tm - 1) // tm * tm
    pad_end = jnp.cumsum(padded)
    dst = (pad_end - padded)[None, :] + jnp.cumsum(run, axis=0) - run
    boff = jnp.cumsum(run, axis=1) - run
    n_blocks = -(-(TOP_K * n_tok + (RUN_ALIGN - 1) * N_EXPERTS * nblk) // tm) + N_EXPERTS
    first_row = jnp.arange(n_blocks, dtype=jnp.int32) * tm
    block_e = jnp.minimum(jnp.sum(pad_end[None, :] <= first_row[:, None], axis=1),
                          N_EXPERTS - 1).astype(jnp.int32)
    n_used = (pad_end[-1:] // tm).astype(jnp.int32)
    flat = lambda a: a.reshape(-1).astype(jnp.int32)
    return dict(run=flat(run), dst=flat(dst), boff=flat(boff), boff_f=boff.astype(F32),
                pad_end=pad_end.astype(jnp.int32), block_e=block_e, n_used=n_used, n_blocks=n_blocks)


def _run_pieces(run_len, fn):
    off = jnp.int32(0)
    size = TM_ROWS
    while size >= RUN_ALIGN:
        hit = (run_len & size) != 0

        @pl.when(hit)
        def _(off=off, size=size):
            fn(off, size)

        off = off + jnp.where(hit, size, 0)
        size //= 2


def _run_copies(blk, run_ref, grp_ref, boff_ref, grouped, sbuf, sem, to_grouped, wait):
    for e in range(N_EXPERTS):
        n = run_ref[blk * N_EXPERTS + e]
        g0 = grp_ref[blk * N_EXPERTS + e]
        s0 = boff_ref[blk * N_EXPERTS + e]

        def piece(off, size, g0=g0, s0=s0):
            g = grouped.at[pl.ds(pl.multiple_of(g0 + off, RUN_ALIGN), size), :]
            s = sbuf.at[pl.ds(pl.multiple_of(s0 + off, RUN_ALIGN), size), :]
            copy = pltpu.make_async_copy(s, g, sem) if to_grouped else pltpu.make_async_copy(g, s, sem)
            copy.wait() if wait else copy.start()

        _run_pieces(n, piece)


def _sort_dispatch_kernel(run_ref, dst_ref, boffs_ref, pe_ref, nu_ref, idx_ref, boff_ref, h_ref, xs_ref,
                          sbuf, zero_ref, sem, zsem, *, n_blocks):
    i = pl.program_id(0)
    tb = h_ref.shape[0]
    tm = zero_ref.shape[0]

    def zero_copy(blk):
        return pltpu.make_async_copy(zero_ref, xs_ref.at[pl.ds(pl.multiple_of(blk * tm, tm), tm), :], zsem)

    @pl.when(i == 0)
    def _():
        zero_ref[...] = jnp.zeros_like(zero_ref)

        def per_expert(fn):
            for e in range(N_EXPERTS):
                end = pe_ref[e]
                start = pe_ref[e - 1] if e else 0

                @pl.when(end > start)
                def _():
                    fn(end // tm - 1)

        def per_tail(fn):
            def body(blk, carry):
                fn(blk)
                return carry
            lax.fori_loop(nu_ref[0], n_blocks, body, 0)

        per_expert(lambda blk: zero_copy(blk).start())
        per_tail(lambda blk: zero_copy(blk).start())
        per_expert(lambda blk: zero_copy(blk).wait())
        per_tail(lambda blk: zero_copy(blk).wait())

    row = lax.broadcasted_iota(jnp.int32, (tb, tb), 0)
    col = lax.broadcasted_iota(jnp.int32, (tb, tb), 1)
    earlier = (row < col).astype(F32).astype(BF16)
    expert = lax.broadcasted_iota(jnp.int32, (N_EXPERTS, tb), 0)
    base = boff_ref[...]
    pos = []
    for kk in range(TOP_K):
        onehot = (expert == idx_ref[kk:kk + 1, :]).astype(F32)
        before = _dot(onehot.astype(BF16), earlier) + base
        pos.append(jnp.sum(before * onehot, axis=0, keepdims=True).astype(jnp.int32))
        base = base + jnp.sum(onehot, axis=1, keepdims=True)

    lo, hi = _unpack_bf16_pairs(h_ref[...])
    h = jnp.concatenate([lo, hi], axis=1).astype(BF16)
    rc = SORT_CHUNK
    cur = sbuf.at[i % 2]
    for c in range(sbuf.shape[1] // rc):
        r = lax.broadcasted_iota(jnp.int32, (rc, tb), 0) + c * rc
        sel = jnp.zeros((rc, tb), F32)
        for kk in range(TOP_K):
            sel = jnp.where(r == pos[kk], 1.0, sel).astype(F32)
        cur[c * rc:(c + 1) * rc, :] = _pack_bf16_pairs(_dot(sel.astype(BF16), h))
    _run_copies(i, run_ref, dst_ref, boffs_ref, xs_ref, cur, sem.at[i % 2], True, False)

    @pl.when(i > 0)
    def _():
        _run_copies(i - 1, run_ref, dst_ref, boffs_ref, xs_ref, sbuf.at[(i - 1) % 2], sem.at[(i - 1) % 2],
                    True, True)

    @pl.when(i == pl.num_programs(0) - 1)
    def _():
        _run_copies(i, run_ref, dst_ref, boffs_ref, xs_ref, cur, sem.at[i % 2], True, True)


def _sort_dispatch_call(t, idx_t, h2):
    n_tok, dh = h2.shape
    tb, tm = TM_ROWS, TM_EXPERT
    nblk = n_tok // tb
    grid_spec = pltpu.PrefetchScalarGridSpec(
        num_scalar_prefetch=5,
        grid=(nblk,),
        in_specs=[pl.BlockSpec((TOP_K, tb), lambda i, *_: (0, i)),
                  pl.BlockSpec((None, N_EXPERTS, 1), lambda i, *_: (i, 0, 0)),
                  pl.BlockSpec((tb, dh), lambda i, *_: (i, 0))],
        out_specs=pl.BlockSpec(memory_space=pl.ANY),
        scratch_shapes=[pltpu.VMEM((2, SORT_ROWS, dh), h2.dtype), pltpu.VMEM((tm, dh), h2.dtype),
                        pltpu.SemaphoreType.DMA((2,)), pltpu.SemaphoreType.DMA],
    )
    return pl.pallas_call(
        functools.partial(_sort_dispatch_kernel, n_blocks=t["n_blocks"]),
        grid_spec=grid_spec,
        out_shape=jax.ShapeDtypeStruct((t["n_blocks"] * tm, dh), h2.dtype),
        compiler_params=_cparams("arbitrary"),
    )(t["run"], t["dst"], t["boff"], t["pad_end"], t["n_used"], idx_t, t["boff_f"][:, :, None], h2)


def _sort_combine_kernel(run_ref, dst_ref, boffs_ref, idx_ref, p_ref, boff_ref, yb_hbm, x_ref, gate_ref,
                         fg_ref, o_ref, sbuf, sem, *, final_norm, blk0):
    i = pl.program_id(0)
    tb = x_ref.shape[0]

    def fetch(blk, wait):
        _run_copies(blk0 + blk, run_ref, dst_ref, boffs_ref, yb_hbm, sbuf.at[blk % 2], sem.at[blk % 2],
                    False, wait)

    @pl.when(i == 0)
    def _():
        sbuf[...] = jnp.zeros_like(sbuf)
        fetch(0, False)

    @pl.when(i + 1 < pl.num_programs(0))
    def _():
        fetch(i + 1, False)

    fetch(i, True)
    cur = sbuf.at[i % 2]

    row = lax.broadcasted_iota(jnp.int32, (tb, tb), 0)
    col = lax.broadcasted_iota(jnp.int32, (tb, tb), 1)
    earlier = (col < row).astype(F32).astype(BF16)
    expert = lax.broadcasted_iota(jnp.int32, (tb, N_EXPERTS), 1)
    base = boff_ref[...]
    pos = []
    for kk in range(TOP_K):
        onehot = (expert == idx_ref[:, kk:kk + 1]).astype(F32)
        before = _dot(earlier, onehot.astype(BF16)) + base
        pos.append(jnp.sum(before * onehot, axis=1, keepdims=True).astype(jnp.int32))
        base = base + jnp.sum(onehot, axis=0, keepdims=True)

    rc = SORT_CHUNK
    f = jnp.zeros(x_ref.shape, F32)
    for c in range(sbuf.shape[1] // rc):
        r = lax.broadcasted_iota(jnp.int32, (tb, rc), 1) + c * rc
        wgt = jnp.zeros((tb, rc), F32)
        for kk in range(TOP_K):
            wgt = jnp.where(r == pos[kk], p_ref[:, kk:kk + 1], wgt)
        lo, hi = _unpack_bf16_pairs(cur[c * rc:(c + 1) * rc, :])
        y = jnp.concatenate([lo, hi], axis=1).astype(BF16)
        f = f + _dot(wgt.astype(BF16), y)
    xn = x_ref[...] + gate_ref[...] * f
    if final_norm:
        xn = _rms(xn, fg_ref[...])
    o_ref[...] = xn


def _sort_combine_call(t, idx_tok, probs, yb, x, mods, group_of_block, final_g, tok0, final_norm):
    bsz, n, d = x.shape
    tb = TM_ROWS
    rows = bsz * n
    assert rows % tb == 0 and tok0 % tb == 0 and (n % tb == 0 or tb % n == 0)
    blk0 = tok0 // tb
    grid_spec = pltpu.PrefetchScalarGridSpec(
        num_scalar_prefetch=3,
        grid=(rows // tb,),
        in_specs=[pl.BlockSpec((tb, TOP_K), lambda i, *_: (blk0 + i, 0)),
                  pl.BlockSpec((tb, TOP_K), lambda i, *_: (blk0 + i, 0)),
                  pl.BlockSpec((None, 1, N_EXPERTS), lambda i, *_: (blk0 + i, 0, 0)),
                  pl.BlockSpec(memory_space=pl.ANY),
                  pl.BlockSpec((tb, d), lambda i, *_: (i, 0)),
                  pl.BlockSpec((None, None, 1, d), lambda i, *_: (group_of_block(i), 5, 0, 0)),
                  pl.BlockSpec((1, d), lambda i, *_: (0, 0))],
        out_specs=pl.BlockSpec((tb, d), lambda i, *_: (i, 0)),
        scratch_shapes=[pltpu.VMEM((2, SORT_ROWS, yb.shape[1]), yb.dtype), pltpu.SemaphoreType.DMA((2,))],
    )
    out = pl.pallas_call(
        functools.partial(_sort_combine_kernel, final_norm=final_norm, blk0=blk0),
        grid_spec=grid_spec,
        out_shape=jax.ShapeDtypeStruct((rows, d), F32),
        compiler_params=_cparams("arbitrary"),
    )(t["run"], t["dst"], t["boff"], idx_tok, probs, t["boff_f"][:, None, :], yb, x.reshape(rows, d), mods,
      final_g.reshape(1, d))
    return out.reshape(bsz, n, d)


def kernel(x, c, ctx, c_ctx, norm1_g, norm2_g, w_mod, b_mod, w_in, gla_wg2_f, gla_bg_f, gla_wg2_b,
           gla_bg_b, gla_norm_g, conv_w, conv_b, conv_ln_g, conv_ln_b, w_out, router_w, router_b,
           exp_w_gu, exp_b_gu, exp_w_dn, exp_b_dn, final_norm_g):
    bsz, seq, d = x.shape
    n_ctx = ctx.shape[1]
    depth = w_mod.shape[0]
    assert d == D_MODEL and seq % (FFT_N2 * FFT_KB) == 0 and seq % n_ctx == 0
    lt = seq + n_ctx
    ctx_group = bsz

    rows = 8
    cvec = jnp.concatenate([c, c_ctx[None, :], jnp.zeros((rows - bsz - 1, d), F32)], axis=0)
    mods_all = _mod_call(cvec, w_mod, b_mod).reshape(depth, rows, 6, 1, d)

    def pack_w_in(w):
        o = [0, 256, 512, 768, 1280, 1792, 1808, 1824, 2336]
        parts = [w[:, o[0]:o[1]], w[:, o[1]:o[2]], w[:, o[2]:o[3]], w[:, o[3]:o[4]], w[:, o[4]:o[5]],
                 w[:, o[7]:o[7] + CONV_WIDTH], w[:, o[7] + CONV_WIDTH:o[8]], w[:, o[5]:o[7]],
                 jnp.zeros((d, GLR_PAD - 2 * GLA_GATE_RANK), w.dtype)]
        return jnp.concatenate(parts, axis=1).astype(BF16)

    long_tables = _dft_tables(seq)
    chan = _channel_tables()
    x_lat, x_ctx = x, ctx
    lat_group = lambda b: b
    ctx_group_fn = lambda b: ctx_group

    for layer in range(depth):
        last = layer == depth - 1
        mods = mods_all[layer]
        w_in_p = pack_w_in(w_in[layer])
        wg_pad = jnp.zeros((GLR_PAD, 2 * GLA_KEY_WIDTH), F32)
        wg_pad = wg_pad.at[:GLA_GATE_RANK, :GLA_KEY_WIDTH].set(gla_wg2_f[layer])
        wg_pad = wg_pad.at[GLA_GATE_RANK:2 * GLA_GATE_RANK, GLA_KEY_WIDTH:].set(gla_wg2_b[layer])
        bg_cat = jnp.concatenate([gla_bg_f[layer], gla_bg_b[layer]])[None, :]
        w_out_b = w_out[layer].astype(BF16)
        gn_tiled = jnp.tile(gla_norm_g[layer], GLA_HEADS)[None, :]
        rw_t = router_w[layer].T
        rw_hi = rw_t.astype(BF16)
        rw_lo = (rw_t - rw_hi.astype(F32)).astype(BF16)
        rb = router_b[layer][:, None]

        (u_l, r_l, glu_l), comb = _inproj_call(x_lat, mods, lat_group, norm1_g[layer], w_in_p, lt, 0)
        (u_c, r_c, glu_c), comb = _inproj_call(x_ctx, mods, ctx_group_fn, norm1_g[layer], w_in_p, lt,
                                               seq, combined=comb)
        o_f, o_b = _gla_call(*comb, wg_pad, bg_cat, seq, n_ctx)
        yf_l = _fourier_long(u_l, long_tables, chan)
        cv_l = _conv_call(glu_l, seq // (seq // GRID_W), conv_w[layer], conv_b[layer],
                          conv_ln_g[layer], conv_ln_b[layer])
        n_tok = bsz * seq + (0 if last else bsz * n_ctx)
        x_lat, routed = _outproj_call(yf_l, o_f, o_b, 0, r_l, cv_l, x_lat, mods, lat_group, gn_tiled,
                                      w_out_b, norm2_g[layer], rw_hi, rw_lo, rb, n_tok, 0)
        if not last:
            yf_c = _fourier_short(u_c, chan)
            cv_c = _conv_call(glu_c, n_ctx, conv_w[layer], conv_b[layer], conv_ln_g[layer],
                              conv_ln_b[layer])
            x_ctx, routed = _outproj_call(yf_c, o_f, o_b, seq, r_c, cv_c, x_ctx, mods, ctx_group_fn,
                                          gn_tiled, w_out_b, norm2_g[layer], rw_hi, rw_lo, rb, n_tok,
                                          bsz * seq, carried=routed)

        h2, idx_t, prob_t = routed
        runs = _run_tables(idx_t, n_tok)
        xs = _sort_dispatch_call(runs, idx_t, h2)
        yb = _expert_call(runs["block_e"], runs["n_used"], xs, layer, exp_w_gu, exp_b_gu, exp_w_dn,
                          exp_b_dn)
        probs, idx_tok = prob_t.T, idx_t.T
        lat_blocks = seq // TM_ROWS
        x_lat = _sort_combine_call(runs, idx_tok, probs, yb, x_lat, mods, lambda i: i // lat_blocks,
                                   final_norm_g, 0, last)
        if not last:
            x_ctx = _sort_combine_call(runs, idx_tok, probs, yb, x_ctx, mods, lambda i: ctx_group,
                                       final_norm_g, bsz * seq, False)

    return x_lat
```

```python
import functools

import jax
import jax.numpy as jnp
from jax import lax
from jax.experimental import pallas as pl
from jax.experimental.pallas import tpu as pltpu

F32 = jnp.float32
BF16 = jnp.bfloat16
U32 = jnp.uint32
HIGH_HALF = 0xFFFF0000

D_MODEL = 1024
DEPTH = 2
GRID_W = 64
FOURIER_WIDTH = 256
FOURIER_HEADS = 4
FOURIER_HEAD_DIM = FOURIER_WIDTH // FOURIER_HEADS
GLA_HEADS = 4
GLA_KEY_WIDTH = 256
GLA_VALUE_WIDTH = 512
GLA_DK = GLA_KEY_WIDTH // GLA_HEADS
GLA_DV = GLA_VALUE_WIDTH // GLA_HEADS
GLA_GATE_RANK = 16
GLA_GATE_NORMALIZER = 16.0
CONV_WIDTH = 256
CONV_KERNEL = 31
N_EXPERTS = 32
TOP_K = 4
D_FF = D_MODEL
SWIGLU_LIMIT = 7.0
SWIGLU_ALPHA = 1.702
NORM_EPS = 1e-6

LANES = 128
VMEM_LIMIT = 56 * 1024 * 1024

COL_U = 0
COL_Q = COL_U + FOURIER_WIDTH
COL_K = COL_Q + GLA_KEY_WIDTH
COL_V = COL_K + GLA_KEY_WIDTH
COL_R = COL_V + GLA_VALUE_WIDTH
COL_CA = COL_R + GLA_VALUE_WIDTH
COL_CG = COL_CA + CONV_WIDTH
COL_GL = COL_CG + CONV_WIDTH
GLR_PAD = LANES
IN_PAD = COL_GL + GLR_PAD

GLA_CHUNK = 128
GLA_STEP_CHUNKS = 2
TM_ROWS = 512
TM_EXPERT = 512
EXPERT_CHUNKS = 4
RUN_ALIGN = 8
SORT_TB = 512
SORT_CHUNK = 256
SORT_ROWS = TOP_K * SORT_TB + RUN_ALIGN * N_EXPERTS
FFT_N2 = 128
FFT_KB = 8
CONV_HALO = 16
CONV_GROUP = 8
CONV_PITCH = 100


def _cparams(*sem):
    return pltpu.CompilerParams(dimension_semantics=sem, vmem_limit_bytes=VMEM_LIMIT)


def _dot(a, b):
    return jnp.dot(a, b, preferred_element_type=F32)


def _dot_nt(a, b):
    return lax.dot_general(a, b, (((1,), (1,)), ((), ())), preferred_element_type=F32)


def _dot_tn(a, b):
    return lax.dot_general(a, b, (((0,), (0,)), ((), ())), preferred_element_type=F32)


def _split(a):
    hi = a.astype(BF16)
    lo = (a - hi.astype(F32)).astype(BF16)
    return hi, lo


def _dot3(a, b):
    ah, al = _split(a)
    bh, bl = _split(b)
    return _dot(ah, bh) + _dot(ah, bl) + _dot(al, bh)


def _sigmoid(x):
    return 1.0 / (1.0 + jnp.exp(-x))


def _pack_bf16_pairs(a, holds_bf16=False):
    h = a.shape[1] // 2
    bits = lax.bitcast_convert_type(a if holds_bf16 else a.astype(BF16).astype(F32), U32)
    return (bits[:, :h] >> 16) | (bits[:, h:] & U32(HIGH_HALF))


def _unpack_bf16_pairs(w):
    return (lax.bitcast_convert_type(w << 16, F32), lax.bitcast_convert_type(w & U32(HIGH_HALF), F32))


def _rms(x, g):
    ms = jnp.mean(x * x, axis=-1, keepdims=True)
    return x * lax.rsqrt(ms + NORM_EPS) * g


def _mod_kernel(cv_ref, w_ref, b_ref, o_ref):
    cv = cv_ref[...]
    a = cv * _sigmoid(cv)
    o_ref[...] = _dot3(a, w_ref[...]) + b_ref[...]


def _mod_call(cvec, w_mod, b_mod):
    depth, d, n = w_mod.shape
    rows = cvec.shape[0]
    tn = 1536
    return pl.pallas_call(
        _mod_kernel,
        grid=(depth, n // tn),
        in_specs=[
            pl.BlockSpec((rows, d), lambda l, j: (0, 0)),
            pl.BlockSpec((None, d, tn), lambda l, j: (l, 0, j)),
            pl.BlockSpec((None, 1, tn), lambda l, j: (l, 0, j)),
        ],
        out_specs=pl.BlockSpec((None, rows, tn), lambda l, j: (l, 0, j)),
        out_shape=jax.ShapeDtypeStruct((depth, rows, n), F32),
        compiler_params=_cparams("arbitrary", "arbitrary"),
    )(cvec, w_mod, b_mod.reshape(depth, 1, n))


def _inproj_kernel(x_ref, g_ref, sh_ref, sc_ref, w_ref, *rest):
    u_ref, r_ref, glu_ref, q_ref, k_ref, v_ref, gl_ref = rest[-7:]
    x = x_ref[...]
    h = _rms(x, g_ref[...]) * (1.0 + sc_ref[...]) + sh_ref[...]
    p = _dot(h.astype(BF16), w_ref[...])
    u_ref[...] = p[:, COL_U:COL_Q].astype(u_ref.dtype)
    q_ref[...] = (p[:, COL_Q:COL_K] * (GLA_DK ** -0.5)).astype(q_ref.dtype)
    k_ref[...] = p[:, COL_K:COL_V].astype(k_ref.dtype)
    v_ref[...] = p[:, COL_V:COL_R].astype(v_ref.dtype)
    r_ref[...] = p[:, COL_R:COL_CA].astype(r_ref.dtype)
    glu_ref[...] = (p[:, COL_CA:COL_CG] * _sigmoid(p[:, COL_CG:COL_GL])).astype(glu_ref.dtype)
    gl_ref[...] = p[:, COL_GL:IN_PAD]


def _inproj_call(x, mods, group_of_batch, norm_g, w_in_p, lt, row0, combined=None):
    bsz, n, d = x.shape
    tm = min(TM_ROWS, n)
    assert n % tm == 0 and row0 % tm == 0
    blk0 = row0 // tm
    nb = n // tm
    steps = nb + (-(-(lt - n) // tm) if combined is None else 0)
    widths = (GLA_KEY_WIDTH, GLA_KEY_WIDTH, GLA_VALUE_WIDTH, GLR_PAD)
    dtypes = (BF16, BF16, BF16, F32)
    row_spec = lambda w: pl.BlockSpec((None, tm, w), lambda b, i: (b, jnp.minimum(i, nb - 1), 0))
    comb_spec = lambda w: pl.BlockSpec((None, tm, w), lambda b, i: (b, blk0 + i, 0))
    mod_spec = lambda which: pl.BlockSpec(
        (None, None, 1, d), lambda b, i: (group_of_batch(b), which, 0, 0))
    in_specs = [
        row_spec(d),
        pl.BlockSpec((1, d), lambda b, i: (0, 0)),
        mod_spec(0), mod_spec(1),
        pl.BlockSpec((d, IN_PAD), lambda b, i: (0, 0)),
    ]
    args = [x, norm_g.reshape(1, d), mods, mods, w_in_p]
    aliases = {}
    if combined is not None:
        for t, arr in enumerate(combined):
            in_specs.append(pl.BlockSpec(memory_space=pl.ANY))
            aliases[len(args)] = 3 + t
            args.append(arr)
    out_shape = [
        jax.ShapeDtypeStruct((bsz, n, FOURIER_WIDTH), BF16),
        jax.ShapeDtypeStruct((bsz, n, GLA_VALUE_WIDTH), BF16),
        jax.ShapeDtypeStruct((bsz, n, CONV_WIDTH), BF16),
    ] + [jax.ShapeDtypeStruct((bsz, lt, w), dt) for w, dt in zip(widths, dtypes)]
    out_specs = [row_spec(FOURIER_WIDTH), row_spec(GLA_VALUE_WIDTH), row_spec(CONV_WIDTH)] + [
        comb_spec(w) for w in widths]
    outs = pl.pallas_call(
        _inproj_kernel,
        grid=(bsz, steps),
        in_specs=in_specs,
        out_specs=out_specs,
        out_shape=out_shape,
        input_output_aliases=aliases,
        compiler_params=_cparams("arbitrary", "arbitrary"),
    )(*args)
    return outs[:3], outs[3:]


def _dft_tables(length):
    n2 = FFT_N2
    n1 = length // n2
    two_pi = 2.0 * jnp.pi

    def cs(num, den):
        ang = (num % den).astype(F32) * (two_pi / den)
        return jnp.cos(ang), jnp.sin(ang)

    k1 = jnp.arange(n1, dtype=jnp.int32)
    c1, s1 = cs(k1[:, None] * k1[None, :], n1)
    stage1 = (jnp.concatenate([c1, -s1], axis=0) * (n1 ** -0.5)).astype(BF16)
    k2 = jnp.arange(n2, dtype=jnp.int32)
    ct, st = cs(k1[:, None] * k2[None, :], length)
    cf, sf = cs(k2[:, None] * k2[None, :], n2)
    scale = n2 ** -0.5
    mr = (ct[:, None, :] * cf[None] - st[:, None, :] * sf[None]) * scale
    mi = -(st[:, None, :] * cf[None] + ct[:, None, :] * sf[None]) * scale
    stage2 = jnp.concatenate([jnp.concatenate([mr, -mi], axis=2),
                              jnp.concatenate([mi, mr], axis=2)], axis=1).astype(BF16)
    return stage1, stage2


def _channel_tables():
    hd = FOURIER_HEAD_DIM
    c = jnp.arange(FOURIER_WIDTH, dtype=jnp.int32)
    same_head = (c[:, None] // hd) == (c[None, :] // hd)
    ang = (((c[:, None] % hd) * (c[None, :] % hd)) % hd).astype(F32) * (2.0 * jnp.pi / hd)
    scale = hd ** -0.5
    bdc = jnp.where(same_head, jnp.cos(ang) * scale, 0.0).astype(BF16)
    bds = jnp.where(same_head, jnp.sin(ang) * scale, 0.0).astype(BF16)
    return bdc, bds


def _fft1_kernel(x_ref, cs_ref, zr_ref, zi_ref):
    n1 = x_ref.shape[0]
    z = _dot(cs_ref[...], x_ref[...])
    zr_ref[...] = z[:n1].astype(zr_ref.dtype)
    zi_ref[...] = z[n1:].astype(zi_ref.dtype)


def _fft2_kernel(zr_ref, zi_ref, m_ref, bdc_ref, bds_ref, o_ref):
    kb, n2, w = zr_ref.shape
    for j in range(kb):
        z = jnp.concatenate([zr_ref[j], zi_ref[j]], axis=0)
        a = _dot(m_ref[j], z)
        y = _dot(a[:n2].astype(BF16), bdc_ref[...]) + _dot(a[n2:].astype(BF16), bds_ref[...])
        o_ref[:, j * w:(j + 1) * w] = y.astype(o_ref.dtype)


def _fourier_long(u, tables, chan):
    bsz, length, w = u.shape
    stage1, stage2 = tables
    bdc, bds = chan
    n2 = FFT_N2
    n1 = length // n2
    tn = 4096
    cols = n2 * w
    zr, zi = pl.pallas_call(
        _fft1_kernel,
        grid=(bsz, cols // tn),
        in_specs=[pl.BlockSpec((None, n1, tn), lambda b, j: (b, 0, j)),
                  pl.BlockSpec((2 * n1, n1), lambda b, j: (0, 0))],
        out_specs=[pl.BlockSpec((None, n1, tn), lambda b, j: (b, 0, j))] * 2,
        out_shape=[jax.ShapeDtypeStruct((bsz, n1, cols), BF16)] * 2,
        compiler_params=_cparams("arbitrary", "arbitrary"),
    )(u.reshape(bsz, n1, cols), stage1)
    kb = FFT_KB
    z_spec = pl.BlockSpec((None, kb, n2, w), lambda b, j: (b, j, 0, 0))
    y = pl.pallas_call(
        _fft2_kernel,
        grid=(bsz, n1 // kb),
        in_specs=[z_spec, z_spec,
                  pl.BlockSpec((kb, 2 * n2, 2 * n2), lambda b, j: (j, 0, 0)),
                  pl.BlockSpec((w, w), lambda b, j: (0, 0)),
                  pl.BlockSpec((w, w), lambda b, j: (0, 0))],
        out_specs=pl.BlockSpec((None, n2, kb * w), lambda b, j: (b, 0, j)),
        out_shape=jax.ShapeDtypeStruct((bsz, n2, n1 * w), BF16),
        compiler_params=_cparams("arbitrary", "arbitrary"),
    )(zr.reshape(bsz, n1, n2, w), zi.reshape(bsz, n1, n2, w), stage2, bdc, bds)
    return y.reshape(bsz, length, w)


def _dft_short_kernel(u_ref, c_ref, s_ref, bdc_ref, bds_ref, o_ref):
    u = u_ref[...]
    p = _dot(u, bdc_ref[...]).astype(BF16)
    q = _dot(u, bds_ref[...]).astype(BF16)
    o_ref[...] = (_dot(c_ref[...], p) - _dot(s_ref[...], q)).astype(o_ref.dtype)


def _fourier_short(u, chan):
    bsz, length, w = u.shape
    bdc, bds = chan
    k = jnp.arange(length, dtype=jnp.int32)
    ang = ((k[:, None] * k[None, :]) % length).astype(F32) * (2.0 * jnp.pi / length)
    c = (jnp.cos(ang) * length ** -0.5).astype(BF16)
    s = (jnp.sin(ang) * length ** -0.5).astype(BF16)
    full = lambda n: pl.BlockSpec((n, n), lambda b: (0, 0))
    return pl.pallas_call(
        _dft_short_kernel,
        grid=(bsz,),
        in_specs=[pl.BlockSpec((None, length, w), lambda b: (b, 0, 0)),
                  full(length), full(length), full(w), full(w)],
        out_specs=pl.BlockSpec((None, length, w), lambda b: (b, 0, 0)),
        out_shape=jax.ShapeDtypeStruct((bsz, length, w), BF16),
        compiler_params=_cparams("arbitrary"),
    )(u, c, s, bdc, bds)


def _conv_kernel(x_ref, w_ref, cb_ref, lg_ref, lb_ref, o_ref, pad_ref, *, seg):
    nseg = x_ref.shape[0] // seg
    width = x_ref.shape[1]
    halo = jnp.zeros((CONV_HALO, width), F32)
    for s in range(nseg):
        pad_ref[s, 0:CONV_HALO, :] = halo
        pad_ref[s, CONV_HALO:CONV_HALO + seg, :] = x_ref[s * seg:(s + 1) * seg, :].astype(F32)
        pad_ref[s, CONV_HALO + seg:2 * CONV_HALO + seg, :] = halo
    first = CONV_HALO - CONV_KERNEL // 2
    sub = 8
    span = seg + 2 * CONV_HALO - sub
    for s in range(nseg):
        acc = jnp.zeros((seg, width), F32)
        for r in range(sub):
            shifted = pad_ref[s, r:r + span, :]
            for a in range((span - seg) // sub + 1):
                j = a * sub + r - first
                if 0 <= j < CONV_KERNEL:
                    acc = acc + shifted[a * sub:a * sub + seg, :] * w_ref[j:j + 1, :]
        y = acc + cb_ref[...]
        mu = jnp.mean(y, axis=-1, keepdims=True)
        yc = y - mu
        var = jnp.mean(yc * yc, axis=-1, keepdims=True)
        z = yc * lax.rsqrt(var + NORM_EPS) * lg_ref[...] + lb_ref[...]
        o_ref[s * seg:(s + 1) * seg, :] = (z * _sigmoid(z)).astype(o_ref.dtype)


def _conv_group_kernel(x_ref, w_ref, cb_ref, lg_ref, lb_ref, o_ref, pad_ref, acc_ref, *, seg):
    g, p = CONV_GROUP, CONV_PITCH
    halves = x_ref.shape[1] // LANES
    pad_ref[...] = jnp.zeros_like(pad_ref)
    for s in range(g):
        for h in range(halves):
            pad_ref[h, s * p + CONV_HALO:s * p + CONV_HALO + seg, :] = (
                x_ref[s * seg:(s + 1) * seg, h * LANES:(h + 1) * LANES].astype(F32))
    first = CONV_HALO - CONV_KERNEL // 2
    unroll = 4

    def body(i, carry):
        t0 = unroll * i
        for h in range(halves):
            accs = [None] * unroll
            for j in range(CONV_KERNEL):
                tap = w_ref[h, j]
                for u in range(unroll):
                    term = pad_ref[h, pl.ds(t0 + u + first + j, g, stride=p), :] * tap
                    accs[u] = term if accs[u] is None else accs[u] + term
            for u in range(unroll):
                acc_ref[h, pl.ds(t0 + u, g, stride=p), :] = accs[u]
        return carry

    lax.fori_loop(0, seg // unroll, body, 0)
    for s in range(g):
        y = jnp.concatenate([acc_ref[h, s * p:s * p + seg, :] for h in range(halves)], axis=1) + cb_ref[...]
        mu = jnp.mean(y, axis=-1, keepdims=True)
        yc = y - mu
        var = jnp.mean(yc * yc, axis=-1, keepdims=True)
        z = yc * lax.rsqrt(var + NORM_EPS) * lg_ref[...] + lb_ref[...]
        o_ref[s * seg:(s + 1) * seg, :] = (z * _sigmoid(z)).astype(o_ref.dtype)


def _conv_call(glu, seg, conv_w, conv_b, ln_g, ln_b):
    bsz, n, w = glu.shape
    t = max(seg, min(TM_ROWS, n))
    assert n % t == 0 and t % seg == 0
    vec = lambda: pl.BlockSpec((1, w), lambda b, i: (0, 0))
    if t // seg == CONV_GROUP and seg + 2 * CONV_HALO <= CONV_PITCH and w % LANES == 0:
        halves = w // LANES
        taps = jnp.broadcast_to(conv_w.reshape(CONV_KERNEL, halves, 1, LANES).transpose(1, 0, 2, 3),
                                (halves, CONV_KERNEL, 8, LANES))
        slab = pltpu.VMEM((halves, CONV_GROUP * CONV_PITCH, LANES), F32)
        return pl.pallas_call(
            functools.partial(_conv_group_kernel, seg=seg),
            grid=(bsz, n // t),
            in_specs=[pl.BlockSpec((None, t, w), lambda b, i: (b, i, 0)),
                      pl.BlockSpec(taps.shape, lambda b, i: (0, 0, 0, 0)),
                      vec(), vec(), vec()],
            out_specs=pl.BlockSpec((None, t, w), lambda b, i: (b, i, 0)),
            out_shape=jax.ShapeDtypeStruct((bsz, n, w), BF16),
            scratch_shapes=[slab, slab],
            compiler_params=_cparams("arbitrary", "arbitrary"),
        )(glu, taps, conv_b.reshape(1, w), ln_g.reshape(1, w), ln_b.reshape(1, w))
    return pl.pallas_call(
        functools.partial(_conv_kernel, seg=seg),
        grid=(bsz, n // t),
        in_specs=[pl.BlockSpec((None, t, w), lambda b, i: (b, i, 0)),
                  pl.BlockSpec((CONV_KERNEL, w), lambda b, i: (0, 0)),
                  vec(), vec(), vec()],
        out_specs=pl.BlockSpec((None, t, w), lambda b, i: (b, i, 0)),
        out_shape=jax.ShapeDtypeStruct((bsz, n, w), BF16),
        scratch_shapes=[pltpu.VMEM((t // seg, seg + 2 * CONV_HALO, w), F32)],
        compiler_params=_cparams("arbitrary", "arbitrary"),
    )(glu, conv_w, conv_b.reshape(1, w), ln_g.reshape(1, w), ln_b.reshape(1, w))


def _gla_direction(q_ref, k_ref, v_ref, gl_ref, wg_ref, bg_ref, o_ref, st, reverse):
    c = GLA_CHUNK
    n_sub = q_ref.shape[0] // c
    for s in (reversed(range(n_sub)) if reverse else range(n_sub)):
        rows = slice(s * c, (s + 1) * c)
        st = _gla_chunk(q_ref[rows, :], k_ref[rows, :], v_ref[rows, :], gl_ref[rows, :], wg_ref, bg_ref,
                        o_ref.at[rows, :], st, reverse)
    return st


def _gla_chunk(q, k, v, glr, wg_ref, bg_ref, o_ref, st, reverse):
    c = GLA_CHUNK
    kw = GLA_KEY_WIDTH
    col0 = kw if reverse else 0
    pre = _dot3(glr, wg_ref[:, col0:col0 + kw]) + bg_ref[:, col0:col0 + kw]
    g = (jnp.minimum(pre, 0.0) - jnp.log(1.0 + jnp.exp(-jnp.abs(pre)))) * (1.0 / GLA_GATE_NORMALIZER)
    row = lax.broadcasted_iota(jnp.int32, (c, c), 0)
    col = lax.broadcasted_iota(jnp.int32, (c, c), 1)
    seen = (col >= row) if reverse else (col <= row)
    tri = jnp.where(seen, 1.0, 0.0).astype(BF16)
    gh, gl = _split(g)
    b = _dot(tri, gh) + _dot(tri, gl)
    mid = c // 2 if reverse else c // 2 - 1
    last = 0 if reverse else c - 1
    b_mid = b[mid:mid + 1, :]
    b_last = b[last:last + 1, :]
    q = q.astype(F32)
    k = k.astype(F32)
    qe = q * jnp.exp(b - b_mid)
    ke = k * jnp.exp(b_mid - b)
    kd = k * jnp.exp(b_last - b)
    head_of_lane = lax.broadcasted_iota(jnp.int32, (1, kw), 1) // GLA_DK
    q_heads = jnp.concatenate(
        [jnp.where(head_of_lane == h, qe, 0.0) for h in range(GLA_HEADS)], axis=0).astype(BF16)
    rhs = jnp.concatenate([ke, st * jnp.exp(b_mid)], axis=0).astype(BF16)
    res = _dot_nt(q_heads, rhs)
    outs = []
    for h in range(GLA_HEADS):
        blk = res[h * c:(h + 1) * c, :]
        scores = jnp.where(seen, blk[:, :c], 0.0).astype(BF16)
        outs.append(_dot(scores, v[:, h * GLA_DV:(h + 1) * GLA_DV]) + blk[:, c:])
    o_ref[...] = jnp.concatenate(outs, axis=1).astype(o_ref.dtype)
    kv = _dot_tn(v, kd.astype(BF16))
    ds = jnp.zeros_like(st)
    for h in range(GLA_HEADS):
        ds = ds + jnp.where(head_of_lane == h, kv[h * GLA_DV:(h + 1) * GLA_DV, :], 0.0)
    return st * jnp.exp(b_last) + ds


def _gla_kernel(qf, kf, vf, gf, qb, kb, vb, gb, wg_ref, bg_ref, of_ref, ob_ref, sf_ref, sb_ref):
    @pl.when(pl.program_id(0) == 0)
    def _():
        sf_ref[...] = jnp.zeros_like(sf_ref)
        sb_ref[...] = jnp.zeros_like(sb_ref)

    bsz = qf.shape[0]
    states = [(sf_ref[b], sb_ref[b]) for b in range(bsz)]
    new = []
    for b, (s_f, s_b) in enumerate(states):
        new.append((
            _gla_direction(qf.at[b], kf.at[b], vf.at[b], gf.at[b], wg_ref, bg_ref, of_ref.at[b], s_f, False),
            _gla_direction(qb.at[b], kb.at[b], vb.at[b], gb.at[b], wg_ref, bg_ref, ob_ref.at[b], s_b, True)))
    for b, (s_f, s_b) in enumerate(new):
        sf_ref[b] = s_f
        sb_ref[b] = s_b


def _gla_call(q, k, v, gl, wg_pad, bg_cat, n_lat, n_ctx):
    bsz, lt, _ = q.shape
    c = GLA_STEP_CHUNKS * GLA_CHUNK
    assert n_lat % c == 0 and n_ctx % c == 0
    cl, cc = n_lat // c, n_ctx // c

    def fwd_blk(j):
        return jnp.where(j < cc, cl + j, j - cc)

    def bwd_blk(j):
        return jnp.where(j < cc, cl + cc - 1 - j, cl - 1 - (j - cc))

    def spec(w, blk):
        return pl.BlockSpec((bsz, c, w), lambda j: (0, blk(j), 0))

    widths = (GLA_KEY_WIDTH, GLA_KEY_WIDTH, GLA_VALUE_WIDTH, GLR_PAD)
    in_specs = [spec(w, fwd_blk) for w in widths] + [spec(w, bwd_blk) for w in widths] + [
        pl.BlockSpec(wg_pad.shape, lambda j: (0, 0)),
        pl.BlockSpec(bg_cat.shape, lambda j: (0, 0))]
    return pl.pallas_call(
        _gla_kernel,
        grid=(cl + cc,),
        in_specs=in_specs,
        out_specs=[spec(GLA_VALUE_WIDTH, fwd_blk), spec(GLA_VALUE_WIDTH, bwd_blk)],
        out_shape=[jax.ShapeDtypeStruct((bsz, lt, GLA_VALUE_WIDTH), F32)] * 2,
        scratch_shapes=[pltpu.VMEM((bsz, GLA_DV, GLA_KEY_WIDTH), F32)] * 2,
        compiler_params=_cparams("arbitrary"),
    )(q, k, v, gl, q, k, v, gl, wg_pad, bg_cat)


def _outproj_kernel(yf_ref, of_ref, ob_ref, r_ref, cv_ref, x_ref, gate_ref, sh_ref, sc_ref,
                    gn_ref, wo_ref, n2_ref, rwh_ref, rwl_ref, rb_ref, *rest):
    xo_ref, h2_ref, idx_ref, prob_ref = rest[-4:]
    o = of_ref[...] + ob_ref[...]
    heads = []
    for h in range(GLA_HEADS):
        oh = o[:, h * GLA_DV:(h + 1) * GLA_DV]
        heads.append(oh * lax.rsqrt(jnp.mean(oh * oh, axis=-1, keepdims=True) + NORM_EPS))
    r = r_ref[...].astype(F32)
    gla = jnp.concatenate(heads, axis=1) * gn_ref[...] * (r * _sigmoid(r))
    c0, c1 = FOURIER_WIDTH, FOURIER_WIDTH + GLA_VALUE_WIDTH
    y = (_dot(yf_ref[...], wo_ref[0:c0, :]) + _dot(gla.astype(BF16), wo_ref[c0:c1, :])
         + _dot(cv_ref[...], wo_ref[c1:, :]))
    xn = x_ref[...] + gate_ref[...] * y
    xo_ref[...] = xn
    h2 = _rms(xn, n2_ref[...]) * (1.0 + sc_ref[...]) + sh_ref[...]
    hh, hl = _split(h2)
    h2_ref[...] = _pack_bf16_pairs(h2)
    logits = (_dot_nt(rwh_ref[...], hh) + _dot_nt(rwh_ref[...], hl) + _dot_nt(rwl_ref[...], hh)
              + rb_ref[...])
    expert = lax.broadcasted_iota(jnp.int32, logits.shape, 0)
    vals, idxs = [], []
    cur = logits
    for _ in range(TOP_K):
        m = jnp.max(cur, axis=0, keepdims=True)
        ix = jnp.min(jnp.where(cur == m, expert, N_EXPERTS), axis=0, keepdims=True)
        vals.append(m)
        idxs.append(ix)
        cur = jnp.where(expert == ix, -jnp.inf, cur)
    es = [jnp.exp(vv - vals[0]) for vv in vals]
    inv = 1.0 / functools.reduce(lambda a, b: a + b, es)
    idx_ref[...] = jnp.concatenate(idxs, axis=0)
    prob_ref[...] = jnp.concatenate([e * inv for e in es], axis=0)


def _outproj_call(yf, o_f, o_b, o_row0, r, cv, x, mods, group_of_batch, gn_tiled, w_out, norm2_g,
                  rw_hi, rw_lo, rb, n_tok, tok0, carried=None):
    bsz, n, d = x.shape
    tm = min(TM_ROWS, n)
    assert n % tm == 0 and o_row0 % tm == 0 and tok0 % tm == 0
    nb = n // tm
    spare = (n_tok - bsz * n) if carried is None else 0
    assert spare in (0, tm)
    steps = nb + spare // tm
    last = lambda i: jnp.minimum(i, nb - 1)
    row = lambda w: pl.BlockSpec((None, tm, w), lambda b, i: (b, last(i), 0))
    orow = pl.BlockSpec((None, tm, GLA_VALUE_WIDTH), lambda b, i: (b, o_row0 // tm + last(i), 0))
    mod = lambda which: pl.BlockSpec((None, None, 1, d), lambda b, i: (group_of_batch(b), which, 0, 0))
    const = lambda a: pl.BlockSpec(a.shape, lambda b, i: (0,) * a.ndim)
    consts = [gn_tiled, w_out, norm2_g.reshape(1, d), rw_hi, rw_lo, rb]
    in_specs = [row(FOURIER_WIDTH), orow, orow, row(GLA_VALUE_WIDTH), row(CONV_WIDTH), row(d),
                mod(2), mod(3), mod(4)] + [const(a) for a in consts]
    args = [yf, o_f, o_b, r, cv, x, mods, mods, mods] + consts
    aliases = {}
    if carried is not None:
        for t, arr in enumerate(carried):
            in_specs.append(pl.BlockSpec(memory_space=pl.ANY))
            aliases[len(args)] = 1 + t
            args.append(arr)
    tokblk = lambda b, i: tok0 // tm + jnp.where(
        jnp.logical_and(i == nb, b == bsz - 1), bsz * nb, b * nb + last(i))
    out_specs = [row(d),
                 pl.BlockSpec((tm, d // 2), lambda b, i: (tokblk(b, i), 0)),
                 pl.BlockSpec((TOP_K, tm), lambda b, i: (0, tokblk(b, i))),
                 pl.BlockSpec((TOP_K, tm), lambda b, i: (0, tokblk(b, i)))]
    out_shape = [jax.ShapeDtypeStruct((bsz, n, d), F32),
                 jax.ShapeDtypeStruct((n_tok, d // 2), U32),
                 jax.ShapeDtypeStruct((TOP_K, n_tok), jnp.int32),
                 jax.ShapeDtypeStruct((TOP_K, n_tok), F32)]
    outs = pl.pallas_call(
        _outproj_kernel,
        grid=(bsz, steps),
        in_specs=in_specs,
        out_specs=out_specs,
        out_shape=out_shape,
        input_output_aliases=aliases,
        compiler_params=_cparams("arbitrary", "arbitrary"),
    )(*args)
    return outs[0], outs[1:]


def _expert_kernel(be_ref, nu_ref, x_ref, wgu_ref, bgu_ref, wdn_ref, bdn_ref, o_ref, wgu_s, wdn_s):
    i = pl.program_id(0)
    nu = nu_ref[0]
    n_chunks = wgu_s.shape[0]
    fc = D_FF // n_chunks

    @pl.when(i >= nu)
    def _():
        o_ref[...] = jnp.zeros_like(o_ref)

    @pl.when(i < nu)
    def _():
        changed = jnp.logical_or(i == 0, be_ref[i] != be_ref[jnp.maximum(i - 1, 0)])

        @pl.when(changed)
        def _():
            rows = 128
            for s in range(wgu_ref.shape[0] // rows):
                rs = slice(s * rows, (s + 1) * rows)
                for n in range(n_chunks):
                    wgu_s[n, rs, 0:fc] = wgu_ref[rs, n * fc:(n + 1) * fc].astype(BF16)
                    wgu_s[n, rs, fc:2 * fc] = wgu_ref[rs, D_FF + n * fc:D_FF + (n + 1) * fc].astype(BF16)
                wdn_s[rs, :] = wdn_ref[rs, :].astype(BF16)

        x = jnp.concatenate(_unpack_bf16_pairs(x_ref[...]), axis=1).astype(BF16)
        acts = []
        for n in range(n_chunks):
            gate = _dot(x, wgu_s[n, :, 0:fc]) + bgu_ref[:, n * fc:(n + 1) * fc]
            up = _dot(x, wgu_s[n, :, fc:2 * fc]) + bgu_ref[:, D_FF + n * fc:D_FF + (n + 1) * fc]
            gate = jnp.minimum(gate, SWIGLU_LIMIT)
            up = jnp.clip(up, -SWIGLU_LIMIT, SWIGLU_LIMIT)
            acts.append((gate * _sigmoid(SWIGLU_ALPHA * gate) * (up + 1.0)).astype(BF16))
        half = n_chunks // 2
        y = (_dot(jnp.concatenate(acts[:half], axis=1), wdn_s[0:half * fc, :])
             + _dot(jnp.concatenate(acts[half:], axis=1), wdn_s[half * fc:, :]) + bdn_ref[...])
        o_ref[...] = _pack_bf16_pairs(y)


def _expert_call(block_e, n_used, xs, layer, w_gu, b_gu, w_dn, b_dn):
    rows, dh = xs.shape
    d = 2 * dh
    tm = TM_EXPERT
    nblk = rows // tm
    depth, e, _, f2 = w_gu.shape
    n_chunks = EXPERT_CHUNKS
    live = lambda i, nu: jnp.minimum(i, nu[0] - 1)
    wmap = lambda i, be, nu: (layer, be[live(i, nu)], 0, 0)
    grid_spec = pltpu.PrefetchScalarGridSpec(
        num_scalar_prefetch=2,
        grid=(nblk,),
        in_specs=[
            pl.BlockSpec((tm, dh), lambda i, be, nu: (live(i, nu), 0)),
            pl.BlockSpec((None, None, d, f2), wmap),
            pl.BlockSpec((None, None, 1, f2), wmap),
            pl.BlockSpec((None, None, f2 // 2, d), wmap),
            pl.BlockSpec((None, None, 1, d), wmap),
        ],
        out_specs=pl.BlockSpec((tm, dh), lambda i, be, nu: (i, 0)),
        scratch_shapes=[pltpu.VMEM((n_chunks, d, f2 // n_chunks), BF16), pltpu.VMEM((f2 // 2, d), BF16)],
    )
    return pl.pallas_call(
        _expert_kernel,
        grid_spec=grid_spec,
        out_shape=jax.ShapeDtypeStruct((rows, dh), xs.dtype),
        compiler_params=_cparams("arbitrary"),
    )(block_e, n_used, xs, w_gu, b_gu.reshape(depth, e, 1, f2), w_dn, b_dn.reshape(depth, e, 1, d))


def _count_kernel(idx_ref, cnt_ref):
    expert = lax.broadcasted_iota(jnp.int32, (N_EXPERTS, idx_ref.shape[1]), 0)
    total = jnp.zeros((N_EXPERTS, 1), F32)
    for kk in range(TOP_K):
        total = total + jnp.sum((expert == idx_ref[kk:kk + 1, :]).astype(F32), axis=1, keepdims=True)
    cnt_ref[...] = total.astype(jnp.int32)


def _run_tables(idx_t, n_tok):
    tm, tb = TM_EXPERT, SORT_TB
    nblk = n_tok // tb
    assert n_tok % tb == 0
    cnt = pl.pallas_call(
        _count_kernel,
        grid=(nblk,),
        in_specs=[pl.BlockSpec((TOP_K, tb), lambda i: (0, i))],
        out_specs=pl.BlockSpec((None, N_EXPERTS, 1), lambda i: (i, 0, 0)),
        out_shape=jax.ShapeDtypeStruct((nblk, N_EXPERTS, 1), jnp.int32),
        compiler_params=_cparams("arbitrary"),
    )(idx_t)[:, :, 0]
    run = (cnt + RUN_ALIGN - 1) // RUN_ALIGN * RUN_ALIGN
    total = jnp.sum(run, axis=0)
    padded = (total + tm - 1) // tm * tm
    pad_end = jnp.cumsum(padded)
    dst = (pad_end - padded)[None, :] + jnp.cumsum(run, axis=0) - run
    boff = jnp.cumsum(run, axis=1) - run
    n_blocks = -(-(TOP_K * n_tok + (RUN_ALIGN - 1) * N_EXPERTS * nblk) // tm) + N_EXPERTS
    first_row = jnp.arange(n_blocks, dtype=jnp.int32) * tm
    block_e = jnp.minimum(jnp.sum(pad_end[None, :] <= first_row[:, None], axis=1),
                          N_EXPERTS - 1).astype(jnp.int32)
    n_used = (pad_end[-1:] // tm).astype(jnp.int32)
    flat = lambda a: a.reshape(-1).astype(jnp.int32)
    return dict(run=flat(run), dst=flat(dst), boff=flat(boff), boff_f=boff.astype(F32),
                pad_end=pad_end.astype(jnp.int32), block_e=block_e, n_used=n_used, n_blocks=n_blocks)


def _run_pieces(run_len, fn):
    size = SORT_TB
    while size >= RUN_ALIGN:
        @pl.when((run_len & size) != 0)
        def _(size=size):
            fn(run_len & -(2 * size), size)

        size //= 2


def _run_copies(blk, run_ref, grp_ref, boff_ref, grouped, sbuf, sem, to_grouped, wait):
    if wait:
        last = blk * N_EXPERTS + N_EXPERTS - 1
        rows = pl.multiple_of(boff_ref[last] + run_ref[last], RUN_ALIGN)

        @pl.when(rows > 0)
        def _():
            s, g = sbuf.at[pl.ds(0, rows), :], grouped.at[pl.ds(0, rows), :]
            (pltpu.make_async_copy(s, g, sem) if to_grouped else pltpu.make_async_copy(g, s, sem)).wait()
        return
    for e in range(N_EXPERTS):
        n = run_ref[blk * N_EXPERTS + e]
        g0 = grp_ref[blk * N_EXPERTS + e]
        s0 = boff_ref[blk * N_EXPERTS + e]

        def piece(off, size, g0=g0, s0=s0):
            g = grouped.at[pl.ds(pl.multiple_of(g0 + off, RUN_ALIGN), size), :]
            s = sbuf.at[pl.ds(pl.multiple_of(s0 + off, RUN_ALIGN), size), :]
            (pltpu.make_async_copy(s, g, sem) if to_grouped else pltpu.make_async_copy(g, s, sem)).start()

        _run_pieces(n, piece)


def _sort_dispatch_kernel(run_ref, dst_ref, boffs_ref, pe_ref, nu_ref, idx_ref, boff_ref, h_ref, xs_ref,
                          sbuf, zero_ref, sem, zsem, *, n_blocks):
    i = pl.program_id(0)
    tb = h_ref.shape[0]
    tm = zero_ref.shape[0]

    def zero_copy(blk):
        return pltpu.make_async_copy(zero_ref, xs_ref.at[pl.ds(pl.multiple_of(blk * tm, tm), tm), :], zsem)

    @pl.when(i == 0)
    def _():
        zero_ref[...] = jnp.zeros_like(zero_ref)

        def per_expert(fn):
            for e in range(N_EXPERTS):
                end = pe_ref[e]
                start = pe_ref[e - 1] if e else 0

                @pl.when(end > start)
                def _():
                    fn(end // tm - 1)

        def per_tail(fn):
            def body(blk, carry):
                fn(blk)
                return carry
            lax.fori_loop(nu_ref[0], n_blocks, body, 0)

        per_expert(lambda blk: zero_copy(blk).start())
        per_tail(lambda blk: zero_copy(blk).start())
        per_expert(lambda blk: zero_copy(blk).wait())
        per_tail(lambda blk: zero_copy(blk).wait())

    row = lax.broadcasted_iota(jnp.int32, (tb, tb), 0)
    col = lax.broadcasted_iota(jnp.int32, (tb, tb), 1)
    earlier = (row < col).astype(F32).astype(BF16)
    expert = lax.broadcasted_iota(jnp.int32, (N_EXPERTS, tb), 0)
    base = boff_ref[...]
    pos = []
    for kk in range(TOP_K):
        onehot = (expert == idx_ref[kk:kk + 1, :]).astype(F32)
        before = _dot(onehot.astype(BF16), earlier) + base
        pos.append(jnp.sum(before * onehot, axis=0, keepdims=True).astype(jnp.int32))
        base = base + jnp.sum(onehot, axis=1, keepdims=True)

    lo, hi = _unpack_bf16_pairs(h_ref[...])
    h = jnp.concatenate([lo, hi], axis=1).astype(BF16)
    rc = SORT_CHUNK
    cur = sbuf.at[i % 2]
    def selector(c):
        r = lax.broadcasted_iota(jnp.int32, (rc, tb), 0) + c * rc
        sel = jnp.zeros((rc, tb), F32)
        for kk in range(TOP_K):
            sel = jnp.where(r == pos[kk], 1.0, sel).astype(F32)
        return sel.astype(BF16)

    n_chunks = sbuf.shape[1] // rc
    sel = selector(0)
    for c in range(n_chunks):
        nxt = selector(c + 1) if c + 1 < n_chunks else None
        cur[c * rc:(c + 1) * rc, :] = _pack_bf16_pairs(_dot(sel, h), holds_bf16=True)
        sel = nxt
    _run_copies(i, run_ref, dst_ref, boffs_ref, xs_ref, cur, sem.at[i % 2], True, False)

    @pl.when(i > 0)
    def _():
        _run_copies(i - 1, run_ref, dst_ref, boffs_ref, xs_ref, sbuf.at[(i - 1) % 2], sem.at[(i - 1) % 2],
                    True, True)

    @pl.when(i == pl.num_programs(0) - 1)
    def _():
        _run_copies(i, run_ref, dst_ref, boffs_ref, xs_ref, cur, sem.at[i % 2], True, True)


def _sort_dispatch_call(t, idx_t, h2):
    n_tok, dh = h2.shape
    tb, tm = SORT_TB, TM_EXPERT
    nblk = n_tok // tb
    grid_spec = pltpu.PrefetchScalarGridSpec(
        num_scalar_prefetch=5,
        grid=(nblk,),
        in_specs=[pl.BlockSpec((TOP_K, tb), lambda i, *_: (0, i)),
                  pl.BlockSpec((None, N_EXPERTS, 1), lambda i, *_: (i, 0, 0)),
                  pl.BlockSpec((tb, dh), lambda i, *_: (i, 0))],
        out_specs=pl.BlockSpec(memory_space=pl.ANY),
        scratch_shapes=[pltpu.VMEM((2, SORT_ROWS, dh), h2.dtype), pltpu.VMEM((tm, dh), h2.dtype),
                        pltpu.SemaphoreType.DMA((2,)), pltpu.SemaphoreType.DMA],
    )
    return pl.pallas_call(
        functools.partial(_sort_dispatch_kernel, n_blocks=t["n_blocks"]),
        grid_spec=grid_spec,
        out_shape=jax.ShapeDtypeStruct((t["n_blocks"] * tm, dh), h2.dtype),
        compiler_params=_cparams("arbitrary"),
    )(t["run"], t["dst"], t["boff"], t["pad_end"], t["n_used"], idx_t, t["boff_f"][:, :, None], h2)


def _sort_combine_kernel(run_ref, dst_ref, boffs_ref, idx_ref, p_ref, boff_ref, yb_hbm, x_ref, gate_ref,
                         fg_ref, o_ref, sbuf, sem, *, final_norm, blk0):
    i = pl.program_id(0)
    tb = x_ref.shape[0]

    def fetch(blk, wait):
        _run_copies(blk0 + blk, run_ref, dst_ref, boffs_ref, yb_hbm, sbuf.at[blk % 2], sem.at[blk % 2],
                    False, wait)

    @pl.when(i == 0)
    def _():
        sbuf[...] = jnp.zeros_like(sbuf)
        fetch(0, False)

    @pl.when(i + 1 < pl.num_programs(0))
    def _():
        fetch(i + 1, False)

    fetch(i, True)
    cur = sbuf.at[i % 2]

    row = lax.broadcasted_iota(jnp.int32, (tb, tb), 0)
    col = lax.broadcasted_iota(jnp.int32, (tb, tb), 1)
    earlier = (col < row).astype(F32).astype(BF16)
    expert = lax.broadcasted_iota(jnp.int32, (tb, N_EXPERTS), 1)
    base = boff_ref[...]
    pos = []
    for kk in range(TOP_K):
        onehot = (expert == idx_ref[:, kk:kk + 1]).astype(F32)
        before = _dot(earlier, onehot.astype(BF16)) + base
        pos.append(jnp.sum(before * onehot, axis=1, keepdims=True).astype(jnp.int32))
        base = base + jnp.sum(onehot, axis=0, keepdims=True)

    rc = SORT_CHUNK
    f = jnp.zeros(x_ref.shape, F32)
    for c in range(sbuf.shape[1] // rc):
        r = lax.broadcasted_iota(jnp.int32, (tb, rc), 1) + c * rc
        wgt = jnp.zeros((tb, rc), F32)
        for kk in range(TOP_K):
            wgt = jnp.where(r == pos[kk], p_ref[:, kk:kk + 1], wgt)
        lo, hi = _unpack_bf16_pairs(cur[c * rc:(c + 1) * rc, :])
        y = jnp.concatenate([lo, hi], axis=1).astype(BF16)
        f = f + _dot(wgt.astype(BF16), y)
    xn = x_ref[...] + gate_ref[...] * f
    if final_norm:
        xn = _rms(xn, fg_ref[...])
    o_ref[...] = xn


def _sort_combine_call(t, idx_tok, probs, yb, x, mods, group_of_block, final_g, tok0, final_norm):
    bsz, n, d = x.shape
    tb = SORT_TB
    rows = bsz * n
    assert rows % tb == 0 and tok0 % tb == 0 and (n % tb == 0 or tb % n == 0)
    blk0 = tok0 // tb
    grid_spec = pltpu.PrefetchScalarGridSpec(
        num_scalar_prefetch=3,
        grid=(rows // tb,),
        in_specs=[pl.BlockSpec((tb, TOP_K), lambda i, *_: (blk0 + i, 0)),
                  pl.BlockSpec((tb, TOP_K), lambda i, *_: (blk0 + i, 0)),
                  pl.BlockSpec((None, 1, N_EXPERTS), lambda i, *_: (blk0 + i, 0, 0)),
                  pl.BlockSpec(memory_space=pl.ANY),
                  pl.BlockSpec((tb, d), lambda i, *_: (i, 0)),
                  pl.BlockSpec((None, None, 1, d), lambda i, *_: (group_of_block(i), 5, 0, 0)),
                  pl.BlockSpec((1, d), lambda i, *_: (0, 0))],
        out_specs=pl.BlockSpec((tb, d), lambda i, *_: (i, 0)),
        scratch_shapes=[pltpu.VMEM((2, SORT_ROWS, yb.shape[1]), yb.dtype), pltpu.SemaphoreType.DMA((2,))],
    )
    out = pl.pallas_call(
        functools.partial(_sort_combine_kernel, final_norm=final_norm, blk0=blk0),
        grid_spec=grid_spec,
        out_shape=jax.ShapeDtypeStruct((rows, d), F32),
        compiler_params=_cparams("arbitrary"),
    )(t["run"], t["dst"], t["boff"], idx_tok, probs, t["boff_f"][:, None, :], yb, x.reshape(rows, d), mods,
      final_g.reshape(1, d))
    return out.reshape(bsz, n, d)


def kernel(x, c, ctx, c_ctx, norm1_g, norm2_g, w_mod, b_mod, w_in, gla_wg2_f, gla_bg_f, gla_wg2_b,
           gla_bg_b, gla_norm_g, conv_w, conv_b, conv_ln_g, conv_ln_b, w_out, router_w, router_b,
           exp_w_gu, exp_b_gu, exp_w_dn, exp_b_dn, final_norm_g):
    bsz, seq, d = x.shape
    n_ctx = ctx.shape[1]
    depth = w_mod.shape[0]
    assert d == D_MODEL and seq % (FFT_N2 * FFT_KB) == 0 and seq % n_ctx == 0
    lt = seq + n_ctx
    ctx_group = bsz

    rows = 8
    cvec = jnp.concatenate([c, c_ctx[None, :], jnp.zeros((rows - bsz - 1, d), F32)], axis=0)
    mods_all = _mod_call(cvec, w_mod, b_mod).reshape(depth, rows, 6, 1, d)

    def pack_w_in(w):
        o = [0, 256, 512, 768, 1280, 1792, 1808, 1824, 2336]
        parts = [w[:, o[0]:o[1]], w[:, o[1]:o[2]], w[:, o[2]:o[3]], w[:, o[3]:o[4]], w[:, o[4]:o[5]],
                 w[:, o[7]:o[7] + CONV_WIDTH], w[:, o[7] + CONV_WIDTH:o[8]], w[:, o[5]:o[7]],
                 jnp.zeros((d, GLR_PAD - 2 * GLA_GATE_RANK), w.dtype)]
        return jnp.concatenate(parts, axis=1).astype(BF16)

    long_tables = _dft_tables(seq)
    chan = _channel_tables()
    x_lat, x_ctx = x, ctx
    lat_group = lambda b: b
    ctx_group_fn = lambda b: ctx_group

    for layer in range(depth):
        last = layer == depth - 1
        mods = mods_all[layer]
        w_in_p = pack_w_in(w_in[layer])
        wg_pad = jnp.zeros((GLR_PAD, 2 * GLA_KEY_WIDTH), F32)
        wg_pad = wg_pad.at[:GLA_GATE_RANK, :GLA_KEY_WIDTH].set(gla_wg2_f[layer])
        wg_pad = wg_pad.at[GLA_GATE_RANK:2 * GLA_GATE_RANK, GLA_KEY_WIDTH:].set(gla_wg2_b[layer])
        bg_cat = jnp.concatenate([gla_bg_f[layer], gla_bg_b[layer]])[None, :]
        w_out_b = w_out[layer].astype(BF16)
        gn_tiled = jnp.tile(gla_norm_g[layer], GLA_HEADS)[None, :]
        rw_t = router_w[layer].T
        rw_hi = rw_t.astype(BF16)
        rw_lo = (rw_t - rw_hi.astype(F32)).astype(BF16)
        rb = router_b[layer][:, None]

        (u_l, r_l, glu_l), comb = _inproj_call(x_lat, mods, lat_group, norm1_g[layer], w_in_p, lt, 0)
        (u_c, r_c, glu_c), comb = _inproj_call(x_ctx, mods, ctx_group_fn, norm1_g[layer], w_in_p, lt,
                                               seq, combined=comb)
        o_f, o_b = _gla_call(*comb, wg_pad, bg_cat, seq, n_ctx)
        yf_l = _fourier_long(u_l, long_tables, chan)
        cv_l = _conv_call(glu_l, seq // (seq // GRID_W), conv_w[layer], conv_b[layer],
                          conv_ln_g[layer], conv_ln_b[layer])
        n_tok = bsz * seq + (0 if last else bsz * n_ctx)
        x_lat, routed = _outproj_call(yf_l, o_f, o_b, 0, r_l, cv_l, x_lat, mods, lat_group, gn_tiled,
                                      w_out_b, norm2_g[layer], rw_hi, rw_lo, rb, n_tok, 0)
        if not last:
            yf_c = _fourier_short(u_c, chan)
            cv_c = _conv_call(glu_c, n_ctx, conv_w[layer], conv_b[layer], conv_ln_g[layer],
                              conv_ln_b[layer])
            x_ctx, routed = _outproj_call(yf_c, o_f, o_b, seq, r_c, cv_c, x_ctx, mods, ctx_group_fn,
                                          gn_tiled, w_out_b, norm2_g[layer], rw_hi, rw_lo, rb, n_tok,
                                          bsz * seq, carried=routed)

        h2, idx_t, prob_t = routed
        runs = _run_tables(idx_t, n_tok)
        xs = _sort_dispatch_call(runs, idx_t, h2)
        yb = _expert_call(runs["block_e"], runs["n_used"], xs, layer, exp_w_gu, exp_b_gu, exp_w_dn,
                          exp_b_dn)
        probs, idx_tok = prob_t.T, idx_t.T
        lat_blocks = seq // SORT_TB
        x_lat = _sort_combine_call(runs, idx_tok, probs, yb, x_lat, mods, lambda i: i // lat_blocks,
                                   final_norm_g, 0, last)
        if not last:
            x_ctx = _sort_combine_call(runs, idx_tok, probs, yb, x_ctx, mods, lambda i: ctx_group,
                                       final_norm_g, bsz * seq, False)

    return x_lat
```

```python
import functools

import jax
import jax.numpy as jnp
from jax import lax
from jax.experimental import pallas as pl
from jax.experimental.pallas import tpu as pltpu

F32 = jnp.float32
BF16 = jnp.bfloat16
U32 = jnp.uint32
HIGH_HALF = 0xFFFF0000

D_MODEL = 1024
DEPTH = 2
GRID_W = 64
FOURIER_WIDTH = 256
FOURIER_HEADS = 4
FOURIER_HEAD_DIM = FOURIER_WIDTH // FOURIER_HEADS
GLA_HEADS = 4
GLA_KEY_WIDTH = 256
GLA_VALUE_WIDTH = 512
GLA_DK = GLA_KEY_WIDTH // GLA_HEADS
GLA_DV = GLA_VALUE_WIDTH // GLA_HEADS
GLA_GATE_RANK = 16
GLA_GATE_NORMALIZER = 16.0
CONV_WIDTH = 256
CONV_KERNEL = 31
N_EXPERTS = 32
TOP_K = 4
D_FF = D_MODEL
SWIGLU_LIMIT = 7.0
SWIGLU_ALPHA = 1.702
NORM_EPS = 1e-6

LANES = 128
VMEM_LIMIT = 56 * 1024 * 1024

COL_U = 0
COL_Q = COL_U + FOURIER_WIDTH
COL_K = COL_Q + GLA_KEY_WIDTH
COL_V = COL_K + GLA_KEY_WIDTH
COL_R = COL_V + GLA_VALUE_WIDTH
COL_CA = COL_R + GLA_VALUE_WIDTH
COL_CG = COL_CA + CONV_WIDTH
COL_GL = COL_CG + CONV_WIDTH
GLR_PAD = LANES
IN_PAD = COL_GL + GLR_PAD

GLA_CHUNK = 128
GLA_STEP_CHUNKS = 2
TM_ROWS = 512
TM_EXPERT = 512
EXPERT_CHUNKS = 4
RUN_ALIGN = 8
SORT_TB = 512
SORT_CHUNK = 256
SORT_ROWS = TOP_K * SORT_TB + RUN_ALIGN * N_EXPERTS
FFT_N2 = 128
FFT_KB = 8
CONV_HALO = 16
CONV_GROUP = 8
CONV_PITCH = 100


def _cparams(*sem):
    return pltpu.CompilerParams(dimension_semantics=sem, vmem_limit_bytes=VMEM_LIMIT)


def _dot(a, b):
    return jnp.dot(a, b, preferred_element_type=F32)


def _dot_nt(a, b):
    return lax.dot_general(a, b, (((1,), (1,)), ((), ())), preferred_element_type=F32)


def _dot_tn(a, b):
    return lax.dot_general(a, b, (((0,), (0,)), ((), ())), preferred_element_type=F32)


def _split(a):
    hi = a.astype(BF16)
    lo = (a - hi.astype(F32)).astype(BF16)
    return hi, lo


def _dot3(a, b):
    ah, al = _split(a)
    bh, bl = _split(b)
    return _dot(ah, bh) + _dot(ah, bl) + _dot(al, bh)


def _sigmoid(x):
    return 1.0 / (1.0 + jnp.exp(-x))


def _pack_bf16_pairs(a, holds_bf16=False):
    h = a.shape[1] // 2
    bits = lax.bitcast_convert_type(a if holds_bf16 else a.astype(BF16).astype(F32), U32)
    return (bits[:, :h] >> 16) | (bits[:, h:] & U32(HIGH_HALF))


def _unpack_bf16_pairs(w):
    return (lax.bitcast_convert_type(w << 16, F32), lax.bitcast_convert_type(w & U32(HIGH_HALF), F32))


def _rms(x, g):
    ms = jnp.mean(x * x, axis=-1, keepdims=True)
    return x * lax.rsqrt(ms + NORM_EPS) * g


def _mod_kernel(cv_ref, w_ref, b_ref, o_ref):
    cv = cv_ref[...]
    a = cv * _sigmoid(cv)
    o_ref[...] = _dot3(a, w_ref[...]) + b_ref[...]


def _mod_call(cvec, w_mod, b_mod):
    depth, d, n = w_mod.shape
    rows = cvec.shape[0]
    tn = 1536
    return pl.pallas_call(
        _mod_kernel,
        grid=(depth, n // tn),
        in_specs=[
            pl.BlockSpec((rows, d), lambda l, j: (0, 0)),
            pl.BlockSpec((None, d, tn), lambda l, j: (l, 0, j)),
            pl.BlockSpec((None, 1, tn), lambda l, j: (l, 0, j)),
        ],
        out_specs=pl.BlockSpec((None, rows, tn), lambda l, j: (l, 0, j)),
        out_shape=jax.ShapeDtypeStruct((depth, rows, n), F32),
        compiler_params=_cparams("arbitrary", "arbitrary"),
    )(cvec, w_mod, b_mod.reshape(depth, 1, n))


def _inproj_kernel(x_ref, g_ref, sh_ref, sc_ref, w_ref, *rest):
    u_ref, r_ref, glu_ref, q_ref, k_ref, v_ref, gl_ref = rest[-7:]
    x = x_ref[...]
    h = _rms(x, g_ref[...]) * (1.0 + sc_ref[...]) + sh_ref[...]
    p = _dot(h.astype(BF16), w_ref[...])
    u_ref[...] = p[:, COL_U:COL_Q].astype(u_ref.dtype)
    q_ref[...] = (p[:, COL_Q:COL_K] * (GLA_DK ** -0.5)).astype(q_ref.dtype)
    k_ref[...] = p[:, COL_K:COL_V].astype(k_ref.dtype)
    v_ref[...] = p[:, COL_V:COL_R].astype(v_ref.dtype)
    r_ref[...] = p[:, COL_R:COL_CA].astype(r_ref.dtype)
    glu_ref[...] = (p[:, COL_CA:COL_CG] * _sigmoid(p[:, COL_CG:COL_GL])).astype(glu_ref.dtype)
    gl_ref[...] = p[:, COL_GL:IN_PAD]


def _inproj_call(x, mods, group_of_batch, norm_g, w_in_p, lt, row0, combined=None):
    bsz, n, d = x.shape
    tm = min(TM_ROWS, n)
    assert n % tm == 0 and row0 % tm == 0
    blk0 = row0 // tm
    nb = n // tm
    steps = nb + (-(-(lt - n) // tm) if combined is None else 0)
    widths = (GLA_KEY_WIDTH, GLA_KEY_WIDTH, GLA_VALUE_WIDTH, GLR_PAD)
    dtypes = (BF16, BF16, BF16, F32)
    row_spec = lambda w: pl.BlockSpec((None, tm, w), lambda b, i: (b, jnp.minimum(i, nb - 1), 0))
    comb_spec = lambda w: pl.BlockSpec((None, tm, w), lambda b, i: (b, blk0 + i, 0))
    mod_spec = lambda which: pl.BlockSpec(
        (None, None, 1, d), lambda b, i: (group_of_batch(b), which, 0, 0))
    in_specs = [
        row_spec(d),
        pl.BlockSpec((1, d), lambda b, i: (0, 0)),
        mod_spec(0), mod_spec(1),
        pl.BlockSpec((d, IN_PAD), lambda b, i: (0, 0)),
    ]
    args = [x, norm_g.reshape(1, d), mods, mods, w_in_p]
    aliases = {}
    if combined is not None:
        for t, arr in enumerate(combined):
            in_specs.append(pl.BlockSpec(memory_space=pl.ANY))
            aliases[len(args)] = 3 + t
            args.append(arr)
    out_shape = [
        jax.ShapeDtypeStruct((bsz, n, FOURIER_WIDTH), BF16),
        jax.ShapeDtypeStruct((bsz, n, GLA_VALUE_WIDTH), BF16),
        jax.ShapeDtypeStruct((bsz, n, CONV_WIDTH), BF16),
    ] + [jax.ShapeDtypeStruct((bsz, lt, w), dt) for w, dt in zip(widths, dtypes)]
    out_specs = [row_spec(FOURIER_WIDTH), row_spec(GLA_VALUE_WIDTH), row_spec(CONV_WIDTH)] + [
        comb_spec(w) for w in widths]
    outs = pl.pallas_call(
        _inproj_kernel,
        grid=(bsz, steps),
        in_specs=in_specs,
        out_specs=out_specs,
        out_shape=out_shape,
        input_output_aliases=aliases,
        compiler_params=_cparams("arbitrary", "arbitrary"),
    )(*args)
    return outs[:3], outs[3:]


def _dft_tables(length):
    n2 = FFT_N2
    n1 = length // n2
    two_pi = 2.0 * jnp.pi

    def cs(num, den):
        ang = (num % den).astype(F32) * (two_pi / den)
        return jnp.cos(ang), jnp.sin(ang)

    k1 = jnp.arange(n1, dtype=jnp.int32)
    c1, s1 = cs(k1[:, None] * k1[None, :], n1)
    stage1 = (jnp.concatenate([c1, -s1], axis=0) * (n1 ** -0.5)).astype(BF16)
    k2 = jnp.arange(n2, dtype=jnp.int32)
    ct, st = cs(k1[:, None] * k2[None, :], length)
    cf, sf = cs(k2[:, None] * k2[None, :], n2)
    scale = n2 ** -0.5
    mr = (ct[:, None, :] * cf[None] - st[:, None, :] * sf[None]) * scale
    mi = -(st[:, None, :] * cf[None] + ct[:, None, :] * sf[None]) * scale
    stage2 = jnp.concatenate([jnp.concatenate([mr, -mi], axis=2),
                              jnp.concatenate([mi, mr], axis=2)], axis=1).astype(BF16)
    return stage1, stage2


def _channel_tables():
    hd = FOURIER_HEAD_DIM
    c = jnp.arange(FOURIER_WIDTH, dtype=jnp.int32)
    same_head = (c[:, None] // hd) == (c[None, :] // hd)
    ang = (((c[:, None] % hd) * (c[None, :] % hd)) % hd).astype(F32) * (2.0 * jnp.pi / hd)
    scale = hd ** -0.5
    bdc = jnp.where(same_head, jnp.cos(ang) * scale, 0.0).astype(BF16)
    bds = jnp.where(same_head, jnp.sin(ang) * scale, 0.0).astype(BF16)
    return bdc, bds


def _fft1_kernel(x_ref, cs_ref, zr_ref, zi_ref):
    n1 = x_ref.shape[0]
    z = _dot(cs_ref[...], x_ref[...])
    zr_ref[...] = z[:n1].astype(zr_ref.dtype)
    zi_ref[...] = z[n1:].astype(zi_ref.dtype)


def _fft2_kernel(zr_ref, zi_ref, m_ref, bdc_ref, bds_ref, o_ref):
    kb, n2, w = zr_ref.shape
    for j in range(kb):
        z = jnp.concatenate([zr_ref[j], zi_ref[j]], axis=0)
        a = _dot(m_ref[j], z)
        y = _dot(a[:n2].astype(BF16), bdc_ref[...]) + _dot(a[n2:].astype(BF16), bds_ref[...])
        o_ref[:, j * w:(j + 1) * w] = y.astype(o_ref.dtype)


def _fourier_long(u, tables, chan):
    bsz, length, w = u.shape
    stage1, stage2 = tables
    bdc, bds = chan
    n2 = FFT_N2
    n1 = length // n2
    tn = 4096
    cols = n2 * w
    zr, zi = pl.pallas_call(
        _fft1_kernel,
        grid=(bsz, cols // tn),
        in_specs=[pl.BlockSpec((None, n1, tn), lambda b, j: (b, 0, j)),
                  pl.BlockSpec((2 * n1, n1), lambda b, j: (0, 0))],
        out_specs=[pl.BlockSpec((None, n1, tn), lambda b, j: (b, 0, j))] * 2,
        out_shape=[jax.ShapeDtypeStruct((bsz, n1, cols), BF16)] * 2,
        compiler_params=_cparams("arbitrary", "arbitrary"),
    )(u.reshape(bsz, n1, cols), stage1)
    kb = FFT_KB
    z_spec = pl.BlockSpec((None, kb, n2, w), lambda b, j: (b, j, 0, 0))
    y = pl.pallas_call(
        _fft2_kernel,
        grid=(bsz, n1 // kb),
        in_specs=[z_spec, z_spec,
                  pl.BlockSpec((kb, 2 * n2, 2 * n2), lambda b, j: (j, 0, 0)),
                  pl.BlockSpec((w, w), lambda b, j: (0, 0)),
                  pl.BlockSpec((w, w), lambda b, j: (0, 0))],
        out_specs=pl.BlockSpec((None, n2, kb * w), lambda b, j: (b, 0, j)),
        out_shape=jax.ShapeDtypeStruct((bsz, n2, n1 * w), BF16),
        compiler_params=_cparams("arbitrary", "arbitrary"),
    )(zr.reshape(bsz, n1, n2, w), zi.reshape(bsz, n1, n2, w), stage2, bdc, bds)
    return y.reshape(bsz, length, w)


def _dft_short_kernel(u_ref, c_ref, s_ref, bdc_ref, bds_ref, o_ref):
    u = u_ref[...]
    p = _dot(u, bdc_ref[...]).astype(BF16)
    q = _dot(u, bds_ref[...]).astype(BF16)
    o_ref[...] = (_dot(c_ref[...], p) - _dot(s_ref[...], q)).astype(o_ref.dtype)


def _fourier_short(u, chan):
    bsz, length, w = u.shape
    bdc, bds = chan
    k = jnp.arange(length, dtype=jnp.int32)
    ang = ((k[:, None] * k[None, :]) % length).astype(F32) * (2.0 * jnp.pi / length)
    c = (jnp.cos(ang) * length ** -0.5).astype(BF16)
    s = (jnp.sin(ang) * length ** -0.5).astype(BF16)
    full = lambda n: pl.BlockSpec((n, n), lambda b: (0, 0))
    return pl.pallas_call(
        _dft_short_kernel,
        grid=(bsz,),
        in_specs=[pl.BlockSpec((None, length, w), lambda b: (b, 0, 0)),
                  full(length), full(length), full(w), full(w)],
        out_specs=pl.BlockSpec((None, length, w), lambda b: (b, 0, 0)),
        out_shape=jax.ShapeDtypeStruct((bsz, length, w), BF16),
        compiler_params=_cparams("arbitrary"),
    )(u, c, s, bdc, bds)


def _conv_kernel(x_ref, w_ref, cb_ref, lg_ref, lb_ref, o_ref, pad_ref, *, seg):
    nseg = x_ref.shape[0] // seg
    width = x_ref.shape[1]
    halo = jnp.zeros((CONV_HALO, width), F32)
    for s in range(nseg):
        pad_ref[s, 0:CONV_HALO, :] = halo
        pad_ref[s, CONV_HALO:CONV_HALO + seg, :] = x_ref[s * seg:(s + 1) * seg, :].astype(F32)
        pad_ref[s, CONV_HALO + seg:2 * CONV_HALO + seg, :] = halo
    first = CONV_HALO - CONV_KERNEL // 2
    sub = 8
    span = seg + 2 * CONV_HALO - sub
    for s in range(nseg):
        acc = jnp.zeros((seg, width), F32)
        for r in range(sub):
            shifted = pad_ref[s, r:r + span, :]
            for a in range((span - seg) // sub + 1):
                j = a * sub + r - first
                if 0 <= j < CONV_KERNEL:
                    acc = acc + shifted[a * sub:a * sub + seg, :] * w_ref[j:j + 1, :]
        y = acc + cb_ref[...]
        mu = jnp.mean(y, axis=-1, keepdims=True)
        yc = y - mu
        var = jnp.mean(yc * yc, axis=-1, keepdims=True)
        z = yc * lax.rsqrt(var + NORM_EPS) * lg_ref[...] + lb_ref[...]
        o_ref[s * seg:(s + 1) * seg, :] = (z * _sigmoid(z)).astype(o_ref.dtype)


def _conv_group_kernel(x_ref, w_ref, cb_ref, lg_ref, lb_ref, o_ref, pad_ref, acc_ref, *, seg):
    g, p = CONV_GROUP, CONV_PITCH
    halves = x_ref.shape[1] // LANES
    pad_ref[...] = jnp.zeros_like(pad_ref)
    for s in range(g):
        for h in range(halves):
            pad_ref[h, s * p + CONV_HALO:s * p + CONV_HALO + seg, :] = (
                x_ref[s * seg:(s + 1) * seg, h * LANES:(h + 1) * LANES].astype(F32))
    first = CONV_HALO - CONV_KERNEL // 2
    unroll = 4

    def body(i, carry):
        t0 = unroll * i
        for h in range(halves):
            accs = [None] * unroll
            for j in range(CONV_KERNEL):
                tap = w_ref[h, j]
                for u in range(unroll):
                    term = pad_ref[h, pl.ds(t0 + u + first + j, g, stride=p), :] * tap
                    accs[u] = term if accs[u] is None else accs[u] + term
            for u in range(unroll):
                acc_ref[h, pl.ds(t0 + u, g, stride=p), :] = accs[u]
        return carry

    lax.fori_loop(0, seg // unroll, body, 0)
    for s in range(g):
        y = jnp.concatenate([acc_ref[h, s * p:s * p + seg, :] for h in range(halves)], axis=1) + cb_ref[...]
        mu = jnp.mean(y, axis=-1, keepdims=True)
        yc = y - mu
        var = jnp.mean(yc * yc, axis=-1, keepdims=True)
        z = yc * lax.rsqrt(var + NORM_EPS) * lg_ref[...] + lb_ref[...]
        o_ref[s * seg:(s + 1) * seg, :] = (z * _sigmoid(z)).astype(o_ref.dtype)


def _conv_call(glu, seg, conv_w, conv_b, ln_g, ln_b):
    bsz, n, w = glu.shape
    t = max(seg, min(TM_ROWS, n))
    assert n % t == 0 and t % seg == 0
    vec = lambda: pl.BlockSpec((1, w), lambda b, i: (0, 0))
    if t // seg == CONV_GROUP and seg + 2 * CONV_HALO <= CONV_PITCH and w % LANES == 0:
        halves = w // LANES
        taps = jnp.broadcast_to(conv_w.reshape(CONV_KERNEL, halves, 1, LANES).transpose(1, 0, 2, 3),
                                (halves, CONV_KERNEL, 8, LANES))
        slab = pltpu.VMEM((halves, CONV_GROUP * CONV_PITCH, LANES), F32)
        return pl.pallas_call(
            functools.partial(_conv_group_kernel, seg=seg),
            grid=(bsz, n // t),
            in_specs=[pl.BlockSpec((None, t, w), lambda b, i: (b, i, 0)),
                      pl.BlockSpec(taps.shape, lambda b, i: (0, 0, 0, 0)),
                      vec(), vec(), vec()],
            out_specs=pl.BlockSpec((None, t, w), lambda b, i: (b, i, 0)),
            out_shape=jax.ShapeDtypeStruct((bsz, n, w), BF16),
            scratch_shapes=[slab, slab],
            compiler_params=_cparams("arbitrary", "arbitrary"),
        )(glu, taps, conv_b.reshape(1, w), ln_g.reshape(1, w), ln_b.reshape(1, w))
    return pl.pallas_call(
        functools.partial(_conv_kernel, seg=seg),
        grid=(bsz, n // t),
        in_specs=[pl.BlockSpec((None, t, w), lambda b, i: (b, i, 0)),
                  pl.BlockSpec((CONV_KERNEL, w), lambda b, i: (0, 0)),
                  vec(), vec(), vec()],
        out_specs=pl.BlockSpec((None, t, w), lambda b, i: (b, i, 0)),
        out_shape=jax.ShapeDtypeStruct((bsz, n, w), BF16),
        scratch_shapes=[pltpu.VMEM((t // seg, seg + 2 * CONV_HALO, w), F32)],
        compiler_params=_cparams("arbitrary", "arbitrary"),
    )(glu, conv_w, conv_b.reshape(1, w), ln_g.reshape(1, w), ln_b.reshape(1, w))


def _gla_direction(q_ref, k_ref, v_ref, gl_ref, wg_ref, bg_ref, o_ref, st, reverse):
    c = GLA_CHUNK
    n_sub = q_ref.shape[0] // c
    for s in (reversed(range(n_sub)) if reverse else range(n_sub)):
        rows = slice(s * c, (s + 1) * c)
        st = _gla_chunk(q_ref[rows, :], k_ref[rows, :], v_ref[rows, :], gl_ref[rows, :], wg_ref, bg_ref,
                        o_ref.at[rows, :], st, reverse)
    return st


def _gla_chunk(q, k, v, glr, wg_ref, bg_ref, o_ref, st, reverse):
    c = GLA_CHUNK
    kw = GLA_KEY_WIDTH
    col0 = kw if reverse else 0
    pre = _dot3(glr, wg_ref[:, col0:col0 + kw]) + bg_ref[:, col0:col0 + kw]
    g = (jnp.minimum(pre, 0.0) - jnp.log(1.0 + jnp.exp(-jnp.abs(pre)))) * (1.0 / GLA_GATE_NORMALIZER)
    row = lax.broadcasted_iota(jnp.int32, (c, c), 0)
    col = lax.broadcasted_iota(jnp.int32, (c, c), 1)
    seen = (col >= row) if reverse else (col <= row)
    tri = jnp.where(seen, 1.0, 0.0).astype(BF16)
    gh, gl = _split(g)
    b = _dot(tri, gh) + _dot(tri, gl)
    mid = c // 2 if reverse else c // 2 - 1
    last = 0 if reverse else c - 1
    b_mid = b[mid:mid + 1, :]
    b_last = b[last:last + 1, :]
    q = q.astype(F32)
    k = k.astype(F32)
    qe = q * jnp.exp(b - b_mid)
    ke = k * jnp.exp(b_mid - b)
    kd = k * jnp.exp(b_last - b)
    head_of_lane = lax.broadcasted_iota(jnp.int32, (1, kw), 1) // GLA_DK
    q_heads = jnp.concatenate(
        [jnp.where(head_of_lane == h, qe, 0.0) for h in range(GLA_HEADS)], axis=0).astype(BF16)
    rhs = jnp.concatenate([ke, st * jnp.exp(b_mid)], axis=0).astype(BF16)
    res = _dot_nt(q_heads, rhs)
    outs = []
    for h in range(GLA_HEADS):
        blk = res[h * c:(h + 1) * c, :]
        scores = jnp.where(seen, blk[:, :c], 0.0).astype(BF16)
        outs.append(_dot(scores, v[:, h * GLA_DV:(h + 1) * GLA_DV]) + blk[:, c:])
    o_ref[...] = jnp.concatenate(outs, axis=1).astype(o_ref.dtype)
    kv = _dot_tn(v, kd.astype(BF16))
    ds = jnp.zeros_like(st)
    for h in range(GLA_HEADS):
        ds = ds + jnp.where(head_of_lane == h, kv[h * GLA_DV:(h + 1) * GLA_DV, :], 0.0)
    return st * jnp.exp(b_last) + ds


def _gla_kernel(qf, kf, vf, gf, qb, kb, vb, gb, wg_ref, bg_ref, of_ref, ob_ref, sf_ref, sb_ref):
    @pl.when(pl.program_id(0) == 0)
    def _():
        sf_ref[...] = jnp.zeros_like(sf_ref)
        sb_ref[...] = jnp.zeros_like(sb_ref)

    bsz = qf.shape[0]
    states = [(sf_ref[b], sb_ref[b]) for b in range(bsz)]
    new = []
    for b, (s_f, s_b) in enumerate(states):
        new.append((
            _gla_direction(qf.at[b], kf.at[b], vf.at[b], gf.at[b], wg_ref, bg_ref, of_ref.at[b], s_f, False),
            _gla_direction(qb.at[b], kb.at[b], vb.at[b], gb.at[b], wg_ref, bg_ref, ob_ref.at[b], s_b, True)))
    for b, (s_f, s_b) in enumerate(new):
        sf_ref[b] = s_f
        sb_ref[b] = s_b


def _gla_call(q, k, v, gl, wg_pad, bg_cat, n_lat, n_ctx):
    bsz, lt, _ = q.shape
    c = GLA_STEP_CHUNKS * GLA_CHUNK
    assert n_lat % c == 0 and n_ctx % c == 0
    cl, cc = n_lat // c, n_ctx // c

    def fwd_blk(j):
        return jnp.where(j < cc, cl + j, j - cc)

    def bwd_blk(j):
        return jnp.where(j < cc, cl + cc - 1 - j, cl - 1 - (j - cc))

    def spec(w, blk):
        return pl.BlockSpec((bsz, c, w), lambda j: (0, blk(j), 0))

    widths = (GLA_KEY_WIDTH, GLA_KEY_WIDTH, GLA_VALUE_WIDTH, GLR_PAD)
    in_specs = [spec(w, fwd_blk) for w in widths] + [spec(w, bwd_blk) for w in widths] + [
        pl.BlockSpec(wg_pad.shape, lambda j: (0, 0)),
        pl.BlockSpec(bg_cat.shape, lambda j: (0, 0))]
    return pl.pallas_call(
        _gla_kernel,
        grid=(cl + cc,),
        in_specs=in_specs,
        out_specs=[spec(GLA_VALUE_WIDTH, fwd_blk), spec(GLA_VALUE_WIDTH, bwd_blk)],
        out_shape=[jax.ShapeDtypeStruct((bsz, lt, GLA_VALUE_WIDTH), F32)] * 2,
        scratch_shapes=[pltpu.VMEM((bsz, GLA_DV, GLA_KEY_WIDTH), F32)] * 2,
        compiler_params=_cparams("arbitrary"),
    )(q, k, v, gl, q, k, v, gl, wg_pad, bg_cat)


def _outproj_kernel(yf_ref, of_ref, ob_ref, r_ref, cv_ref, x_ref, gate_ref, sh_ref, sc_ref,
                    gn_ref, wo_ref, n2_ref, rwh_ref, rwl_ref, rb_ref, *rest):
    xo_ref, h2_ref, idx_ref, prob_ref = rest[-4:]
    o = of_ref[...] + ob_ref[...]
    heads = []
    for h in range(GLA_HEADS):
        oh = o[:, h * GLA_DV:(h + 1) * GLA_DV]
        heads.append(oh * lax.rsqrt(jnp.mean(oh * oh, axis=-1, keepdims=True) + NORM_EPS))
    r = r_ref[...].astype(F32)
    gla = jnp.concatenate(heads, axis=1) * gn_ref[...] * (r * _sigmoid(r))
    c0, c1 = FOURIER_WIDTH, FOURIER_WIDTH + GLA_VALUE_WIDTH
    y = (_dot(yf_ref[...], wo_ref[0:c0, :]) + _dot(gla.astype(BF16), wo_ref[c0:c1, :])
         + _dot(cv_ref[...], wo_ref[c1:, :]))
    xn = x_ref[...] + gate_ref[...] * y
    xo_ref[...] = xn
    h2 = _rms(xn, n2_ref[...]) * (1.0 + sc_ref[...]) + sh_ref[...]
    hh, hl = _split(h2)
    h2_ref[...] = _pack_bf16_pairs(h2)
    logits = (_dot_nt(rwh_ref[...], hh) + _dot_nt(rwh_ref[...], hl) + _dot_nt(rwl_ref[...], hh)
              + rb_ref[...])
    expert = lax.broadcasted_iota(jnp.int32, logits.shape, 0)
    vals, idxs = [], []
    cur = logits
    for _ in range(TOP_K):
        m = jnp.max(cur, axis=0, keepdims=True)
        ix = jnp.min(jnp.where(cur == m, expert, N_EXPERTS), axis=0, keepdims=True)
        vals.append(m)
        idxs.append(ix)
        cur = jnp.where(expert == ix, -jnp.inf, cur)
    es = [jnp.exp(vv - vals[0]) for vv in vals]
    inv = 1.0 / functools.reduce(lambda a, b: a + b, es)
    idx_ref[...] = jnp.concatenate(idxs, axis=0)
    prob_ref[...] = jnp.concatenate([e * inv for e in es], axis=0)


def _outproj_call(yf, o_f, o_b, o_row0, r, cv, x, mods, group_of_batch, gn_tiled, w_out, norm2_g,
                  rw_hi, rw_lo, rb, n_tok, tok0, carried=None):
    bsz, n, d = x.shape
    tm = min(TM_ROWS, n)
    assert n % tm == 0 and o_row0 % tm == 0 and tok0 % tm == 0
    nb = n // tm
    spare = (n_tok - bsz * n) if carried is None else 0
    assert spare in (0, tm)
    steps = nb + spare // tm
    last = lambda i: jnp.minimum(i, nb - 1)
    row = lambda w: pl.BlockSpec((None, tm, w), lambda b, i: (b, last(i), 0))
    orow = pl.BlockSpec((None, tm, GLA_VALUE_WIDTH), lambda b, i: (b, o_row0 // tm + last(i), 0))
    mod = lambda which: pl.BlockSpec((None, None, 1, d), lambda b, i: (group_of_batch(b), which, 0, 0))
    const = lambda a: pl.BlockSpec(a.shape, lambda b, i: (0,) * a.ndim)
    consts = [gn_tiled, w_out, norm2_g.reshape(1, d), rw_hi, rw_lo, rb]
    in_specs = [row(FOURIER_WIDTH), orow, orow, row(GLA_VALUE_WIDTH), row(CONV_WIDTH), row(d),
                mod(2), mod(3), mod(4)] + [const(a) for a in consts]
    args = [yf, o_f, o_b, r, cv, x, mods, mods, mods] + consts
    aliases = {}
    if carried is not None:
        for t, arr in enumerate(carried):
            in_specs.append(pl.BlockSpec(memory_space=pl.ANY))
            aliases[len(args)] = 1 + t
            args.append(arr)
    tokblk = lambda b, i: tok0 // tm + jnp.where(
        jnp.logical_and(i == nb, b == bsz - 1), bsz * nb, b * nb + last(i))
    out_specs = [row(d),
                 pl.BlockSpec((tm, d // 2), lambda b, i: (tokblk(b, i), 0)),
                 pl.BlockSpec((TOP_K, tm), lambda b, i: (0, tokblk(b, i))),
                 pl.BlockSpec((TOP_K, tm), lambda b, i: (0, tokblk(b, i)))]
    out_shape = [jax.ShapeDtypeStruct((bsz, n, d), F32),
                 jax.ShapeDtypeStruct((n_tok, d // 2), U32),
                 jax.ShapeDtypeStruct((TOP_K, n_tok), jnp.int32),
                 jax.ShapeDtypeStruct((TOP_K, n_tok), F32)]
    outs = pl.pallas_call(
        _outproj_kernel,
        grid=(bsz, steps),
        in_specs=in_specs,
        out_specs=out_specs,
        out_shape=out_shape,
        input_output_aliases=aliases,
        compiler_params=_cparams("arbitrary", "arbitrary"),
    )(*args)
    return outs[0], outs[1:]


def _expert_kernel(be_ref, nu_ref, x_ref, wgu_ref, bgu_ref, wdn_ref, bdn_ref, o_ref, wgu_s, wdn_s):
    i = pl.program_id(0)
    nu = nu_ref[0]
    n_chunks = wgu_s.shape[0]
    fc = D_FF // n_chunks

    @pl.when(i >= nu)
    def _():
        o_ref[...] = jnp.zeros_like(o_ref)

    @pl.when(i < nu)
    def _():
        changed = jnp.logical_or(i == 0, be_ref[i] != be_ref[jnp.maximum(i - 1, 0)])

        @pl.when(changed)
        def _():
            rows = 128
            for s in range(wgu_ref.shape[0] // rows):
                rs = slice(s * rows, (s + 1) * rows)
                for n in range(n_chunks):
                    wgu_s[n, rs, 0:fc] = wgu_ref[rs, n * fc:(n + 1) * fc].astype(BF16)
                    wgu_s[n, rs, fc:2 * fc] = wgu_ref[rs, D_FF + n * fc:D_FF + (n + 1) * fc].astype(BF16)
                wdn_s[rs, :] = wdn_ref[rs, :].astype(BF16)

        x = jnp.concatenate(_unpack_bf16_pairs(x_ref[...]), axis=1).astype(BF16)
        acts = []
        for n in range(n_chunks):
            gate = _dot(x, wgu_s[n, :, 0:fc]) + bgu_ref[:, n * fc:(n + 1) * fc]
            up = _dot(x, wgu_s[n, :, fc:2 * fc]) + bgu_ref[:, D_FF + n * fc:D_FF + (n + 1) * fc]
            gate = jnp.minimum(gate, SWIGLU_LIMIT)
            up = jnp.clip(up, -SWIGLU_LIMIT, SWIGLU_LIMIT)
            acts.append((gate * _sigmoid(SWIGLU_ALPHA * gate) * (up + 1.0)).astype(BF16))
        half = n_chunks // 2
        y = (_dot(jnp.concatenate(acts[:half], axis=1), wdn_s[0:half * fc, :])
             + _dot(jnp.concatenate(acts[half:], axis=1), wdn_s[half * fc:, :]) + bdn_ref[...])
        o_ref[...] = _pack_bf16_pairs(y)


def _expert_call(block_e, n_used, xs, layer, w_gu, b_gu, w_dn, b_dn):
    rows, dh = xs.shape
    d = 2 * dh
    tm = TM_EXPERT
    nblk = rows // tm
    depth, e, _, f2 = w_gu.shape
    n_chunks = EXPERT_CHUNKS
    live = lambda i, nu: jnp.minimum(i, nu[0] - 1)
    wmap = lambda i, be, nu: (layer, be[live(i, nu)], 0, 0)
    grid_spec = pltpu.PrefetchScalarGridSpec(
        num_scalar_prefetch=2,
        grid=(nblk,),
        in_specs=[
            pl.BlockSpec((tm, dh), lambda i, be, nu: (live(i, nu), 0)),
            pl.BlockSpec((None, None, d, f2), wmap),
            pl.BlockSpec((None, None, 1, f2), wmap),
            pl.BlockSpec((None, None, f2 // 2, d), wmap),
            pl.BlockSpec((None, None, 1, d), wmap),
        ],
        out_specs=pl.BlockSpec((tm, dh), lambda i, be, nu: (i, 0)),
        scratch_shapes=[pltpu.VMEM((n_chunks, d, f2 // n_chunks), BF16), pltpu.VMEM((f2 // 2, d), BF16)],
    )
    return pl.pallas_call(
        _expert_kernel,
        grid_spec=grid_spec,
        out_shape=jax.ShapeDtypeStruct((rows, dh), xs.dtype),
        compiler_params=_cparams("arbitrary"),
    )(block_e, n_used, xs, w_gu, b_gu.reshape(depth, e, 1, f2), w_dn, b_dn.reshape(depth, e, 1, d))


def _count_kernel(idx_ref, cnt_ref):
    expert = lax.broadcasted_iota(jnp.int32, (N_EXPERTS, idx_ref.shape[1]), 0)
    total = jnp.zeros((N_EXPERTS, 1), F32)
    for kk in range(TOP_K):
        total = total + jnp.sum((expert == idx_ref[kk:kk + 1, :]).astype(F32), axis=1, keepdims=True)
    cnt_ref[...] = total.astype(jnp.int32)


def _run_tables(idx_t, n_tok):
    tm, tb = TM_EXPERT, SORT_TB
    nblk = n_tok // tb
    assert n_tok % tb == 0
    cnt = pl.pallas_call(
        _count_kernel,
        grid=(nblk,),
        in_specs=[pl.BlockSpec((TOP_K, tb), lambda i: (0, i))],
        out_specs=pl.BlockSpec((None, N_EXPERTS, 1), lambda i: (i, 0, 0)),
        out_shape=jax.ShapeDtypeStruct((nblk, N_EXPERTS, 1), jnp.int32),
        compiler_params=_cparams("arbitrary"),
    )(idx_t)[:, :, 0]
    run = (cnt + RUN_ALIGN - 1) // RUN_ALIGN * RUN_ALIGN
    total = jnp.sum(run, axis=0)
    padded = (total + tm - 1) // tm * tm
    pad_end = jnp.cumsum(padded)
    dst = (pad_end - padded)[None, :] + jnp.cumsum(run, axis=0) - run
    boff = jnp.cumsum(run, axis=1) - run
    n_blocks = -(-(TOP_K * n_tok + (RUN_ALIGN - 1) * N_EXPERTS * nblk) // tm) + N_EXPERTS
    first_row = jnp.arange(n_blocks, dtype=jnp.int32) * tm
    block_e = jnp.minimum(jnp.sum(pad_end[None, :] <= first_row[:, None], axis=1),
                          N_EXPERTS - 1).astype(jnp.int32)
    n_used = (pad_end[-1:] // tm).astype(jnp.int32)
    flat = lambda a: a.reshape(-1).astype(jnp.int32)
    return dict(run=flat(run), dst=flat(dst), boff=flat(boff), boff_f=boff.astype(F32),
                pad_end=pad_end.astype(jnp.int32), block_e=block_e, n_used=n_used, n_blocks=n_blocks)


def _run_copies(blk, run_ref, grp_ref, boff_ref, grouped, sbuf, sem, to_grouped, wait):
    if wait:
        last = blk * N_EXPERTS + N_EXPERTS - 1
        rows = pl.multiple_of(boff_ref[last] + run_ref[last], RUN_ALIGN)

        @pl.when(rows > 0)
        def _():
            s, g = sbuf.at[pl.ds(0, rows), :], grouped.at[pl.ds(0, rows), :]
            (pltpu.make_async_copy(s, g, sem) if to_grouped else pltpu.make_async_copy(g, s, sem)).wait()
        return
    for e in range(N_EXPERTS):
        n = pl.multiple_of(run_ref[blk * N_EXPERTS + e], RUN_ALIGN)
        g0 = pl.multiple_of(grp_ref[blk * N_EXPERTS + e], RUN_ALIGN)
        s0 = pl.multiple_of(boff_ref[blk * N_EXPERTS + e], RUN_ALIGN)

        @pl.when(n > 0)
        def _(n=n, g0=g0, s0=s0):
            g, s = grouped.at[pl.ds(g0, n), :], sbuf.at[pl.ds(s0, n), :]
            (pltpu.make_async_copy(s, g, sem) if to_grouped else pltpu.make_async_copy(g, s, sem)).start()


def _sort_dispatch_kernel(run_ref, dst_ref, boffs_ref, pe_ref, nu_ref, idx_ref, boff_ref, h_ref, xs_ref,
                          sbuf, zero_ref, sem, zsem, *, n_blocks):
    i = pl.program_id(0)
    tb = h_ref.shape[0]
    tm = zero_ref.shape[0]

    def zero_copy(blk):
        return pltpu.make_async_copy(zero_ref, xs_ref.at[pl.ds(pl.multiple_of(blk * tm, tm), tm), :], zsem)

    @pl.when(i == 0)
    def _():
        zero_ref[...] = jnp.zeros_like(zero_ref)

        def per_expert(fn):
            for e in range(N_EXPERTS):
                end = pe_ref[e]
                start = pe_ref[e - 1] if e else 0

                @pl.when(end > start)
                def _():
                    fn(end // tm - 1)

        def per_tail(fn):
            def body(blk, carry):
                fn(blk)
                return carry
            lax.fori_loop(nu_ref[0], n_blocks, body, 0)

        per_expert(lambda blk: zero_copy(blk).start())
        per_tail(lambda blk: zero_copy(blk).start())
        per_expert(lambda blk: zero_copy(blk).wait())
        per_tail(lambda blk: zero_copy(blk).wait())

    row = lax.broadcasted_iota(jnp.int32, (tb, tb), 0)
    col = lax.broadcasted_iota(jnp.int32, (tb, tb), 1)
    earlier = (row < col).astype(F32).astype(BF16)
    expert = lax.broadcasted_iota(jnp.int32, (N_EXPERTS, tb), 0)
    base = boff_ref[...]
    pos = []
    for kk in range(TOP_K):
        onehot = (expert == idx_ref[kk:kk + 1, :]).astype(F32)
        before = _dot(onehot.astype(BF16), earlier) + base
        pos.append(jnp.sum(before * onehot, axis=0, keepdims=True).astype(jnp.int32))
        base = base + jnp.sum(onehot, axis=1, keepdims=True)

    lo, hi = _unpack_bf16_pairs(h_ref[...])
    h = jnp.concatenate([lo, hi], axis=1).astype(BF16)
    rc = SORT_CHUNK
    cur = sbuf.at[i % 2]
    def selector(c):
        r = lax.broadcasted_iota(jnp.int32, (rc, tb), 0) + c * rc
        sel = jnp.zeros((rc, tb), F32)
        for kk in range(TOP_K):
            sel = jnp.where(r == pos[kk], 1.0, sel).astype(F32)
        return sel.astype(BF16)

    n_chunks = sbuf.shape[1] // rc
    sel = selector(0)
    for c in range(n_chunks):
        nxt = selector(c + 1) if c + 1 < n_chunks else None
        cur[c * rc:(c + 1) * rc, :] = _pack_bf16_pairs(_dot(sel, h), holds_bf16=True)
        sel = nxt
    _run_copies(i, run_ref, dst_ref, boffs_ref, xs_ref, cur, sem.at[i % 2], True, False)

    @pl.when(i > 0)
    def _():
        _run_copies(i - 1, run_ref, dst_ref, boffs_ref, xs_ref, sbuf.at[(i - 1) % 2], sem.at[(i - 1) % 2],
                    True, True)

    @pl.when(i == pl.num_programs(0) - 1)
    def _():
        _run_copies(i, run_ref, dst_ref, boffs_ref, xs_ref, cur, sem.at[i % 2], True, True)


def _sort_dispatch_call(t, idx_t, h2):
    n_tok, dh = h2.shape
    tb, tm = SORT_TB, TM_EXPERT
    nblk = n_tok // tb
    grid_spec = pltpu.PrefetchScalarGridSpec(
        num_scalar_prefetch=5,
        grid=(nblk,),
        in_specs=[pl.BlockSpec((TOP_K, tb), lambda i, *_: (0, i)),
                  pl.BlockSpec((None, N_EXPERTS, 1), lambda i, *_: (i, 0, 0)),
                  pl.BlockSpec((tb, dh), lambda i, *_: (i, 0))],
        out_specs=pl.BlockSpec(memory_space=pl.ANY),
        scratch_shapes=[pltpu.VMEM((2, SORT_ROWS, dh), h2.dtype), pltpu.VMEM((tm, dh), h2.dtype),
                        pltpu.SemaphoreType.DMA((2,)), pltpu.SemaphoreType.DMA],
    )
    return pl.pallas_call(
        functools.partial(_sort_dispatch_kernel, n_blocks=t["n_blocks"]),
        grid_spec=grid_spec,
        out_shape=jax.ShapeDtypeStruct((t["n_blocks"] * tm, dh), h2.dtype),
        compiler_params=_cparams("arbitrary"),
    )(t["run"], t["dst"], t["boff"], t["pad_end"], t["n_used"], idx_t, t["boff_f"][:, :, None], h2)


def _sort_combine_kernel(run_ref, dst_ref, boffs_ref, idx_ref, p_ref, boff_ref, yb_hbm, x_ref, gate_ref,
                         fg_ref, o_ref, sbuf, sem, *, final_norm, blk0):
    i = pl.program_id(0)
    tb = x_ref.shape[0]

    def fetch(blk, wait):
        _run_copies(blk0 + blk, run_ref, dst_ref, boffs_ref, yb_hbm, sbuf.at[blk % 2], sem.at[blk % 2],
                    False, wait)

    @pl.when(i == 0)
    def _():
        sbuf[...] = jnp.zeros_like(sbuf)
        fetch(0, False)

    @pl.when(i + 1 < pl.num_programs(0))
    def _():
        fetch(i + 1, False)

    fetch(i, True)
    cur = sbuf.at[i % 2]

    row = lax.broadcasted_iota(jnp.int32, (tb, tb), 0)
    col = lax.broadcasted_iota(jnp.int32, (tb, tb), 1)
    earlier = (col < row).astype(F32).astype(BF16)
    expert = lax.broadcasted_iota(jnp.int32, (tb, N_EXPERTS), 1)
    base = boff_ref[...]
    pos = []
    for kk in range(TOP_K):
        onehot = (expert == idx_ref[:, kk:kk + 1]).astype(F32)
        before = _dot(earlier, onehot.astype(BF16)) + base
        pos.append(jnp.sum(before * onehot, axis=1, keepdims=True).astype(jnp.int32))
        base = base + jnp.sum(onehot, axis=0, keepdims=True)

    rc = SORT_CHUNK
    f = jnp.zeros(x_ref.shape, F32)
    for c in range(sbuf.shape[1] // rc):
        r = lax.broadcasted_iota(jnp.int32, (tb, rc), 1) + c * rc
        wgt = jnp.zeros((tb, rc), F32)
        for kk in range(TOP_K):
            wgt = jnp.where(r == pos[kk], p_ref[:, kk:kk + 1], wgt)
        lo, hi = _unpack_bf16_pairs(cur[c * rc:(c + 1) * rc, :])
        y = jnp.concatenate([lo, hi], axis=1).astype(BF16)
        f = f + _dot(wgt.astype(BF16), y)
    xn = x_ref[...] + gate_ref[...] * f
    if final_norm:
        xn = _rms(xn, fg_ref[...])
    o_ref[...] = xn


def _sort_combine_call(t, idx_tok, probs, yb, x, mods, group_of_block, final_g, tok0, final_norm):
    bsz, n, d = x.shape
    tb = SORT_TB
    rows = bsz * n
    assert rows % tb == 0 and tok0 % tb == 0 and (n % tb == 0 or tb % n == 0)
    blk0 = tok0 // tb
    grid_spec = pltpu.PrefetchScalarGridSpec(
        num_scalar_prefetch=3,
        grid=(rows // tb,),
        in_specs=[pl.BlockSpec((tb, TOP_K), lambda i, *_: (blk0 + i, 0)),
                  pl.BlockSpec((tb, TOP_K), lambda i, *_: (blk0 + i, 0)),
                  pl.BlockSpec((None, 1, N_EXPERTS), lambda i, *_: (blk0 + i, 0, 0)),
                  pl.BlockSpec(memory_space=pl.ANY),
                  pl.BlockSpec((tb, d), lambda i, *_: (i, 0)),
                  pl.BlockSpec((None, None, 1, d), lambda i, *_: (group_of_block(i), 5, 0, 0)),
                  pl.BlockSpec((1, d), lambda i, *_: (0, 0))],
        out_specs=pl.BlockSpec((tb, d), lambda i, *_: (i, 0)),
        scratch_shapes=[pltpu.VMEM((2, SORT_ROWS, yb.shape[1]), yb.dtype), pltpu.SemaphoreType.DMA((2,))],
    )
    out = pl.pallas_call(
        functools.partial(_sort_combine_kernel, final_norm=final_norm, blk0=blk0),
        grid_spec=grid_spec,
        out_shape=jax.ShapeDtypeStruct((rows, d), F32),
        compiler_params=_cparams("arbitrary"),
    )(t["run"], t["dst"], t["boff"], idx_tok, probs, t["boff_f"][:, None, :], yb, x.reshape(rows, d), mods,
      final_g.reshape(1, d))
    return out.reshape(bsz, n, d)


def kernel(x, c, ctx, c_ctx, norm1_g, norm2_g, w_mod, b_mod, w_in, gla_wg2_f, gla_bg_f, gla_wg2_b,
           gla_bg_b, gla_norm_g, conv_w, conv_b, conv_ln_g, conv_ln_b, w_out, router_w, router_b,
           exp_w_gu, exp_b_gu, exp_w_dn, exp_b_dn, final_norm_g):
    bsz, seq, d = x.shape
    n_ctx = ctx.shape[1]
    depth = w_mod.shape[0]
    assert d == D_MODEL and seq % (FFT_N2 * FFT_KB) == 0 and seq % n_ctx == 0
    lt = seq + n_ctx
    ctx_group = bsz

    rows = 8
    cvec = jnp.concatenate([c, c_ctx[None, :], jnp.zeros((rows - bsz - 1, d), F32)], axis=0)
    mods_all = _mod_call(cvec, w_mod, b_mod).reshape(depth, rows, 6, 1, d)

    def pack_w_in(w):
        o = [0, 256, 512, 768, 1280, 1792, 1808, 1824, 2336]
        parts = [w[:, o[0]:o[1]], w[:, o[1]:o[2]], w[:, o[2]:o[3]], w[:, o[3]:o[4]], w[:, o[4]:o[5]],
                 w[:, o[7]:o[7] + CONV_WIDTH], w[:, o[7] + CONV_WIDTH:o[8]], w[:, o[5]:o[7]],
                 jnp.zeros((d, GLR_PAD - 2 * GLA_GATE_RANK), w.dtype)]
        return jnp.concatenate(parts, axis=1).astype(BF16)

    long_tables = _dft_tables(seq)
    chan = _channel_tables()
    x_lat, x_ctx = x, ctx
    lat_group = lambda b: b
    ctx_group_fn = lambda b: ctx_group

    for layer in range(depth):
        last = layer == depth - 1
        mods = mods_all[layer]
        w_in_p = pack_w_in(w_in[layer])
        wg_pad = jnp.zeros((GLR_PAD, 2 * GLA_KEY_WIDTH), F32)
        wg_pad = wg_pad.at[:GLA_GATE_RANK, :GLA_KEY_WIDTH].set(gla_wg2_f[layer])
        wg_pad = wg_pad.at[GLA_GATE_RANK:2 * GLA_GATE_RANK, GLA_KEY_WIDTH:].set(gla_wg2_b[layer])
        bg_cat = jnp.concatenate([gla_bg_f[layer], gla_bg_b[layer]])[None, :]
        w_out_b = w_out[layer].astype(BF16)
        gn_tiled = jnp.tile(gla_norm_g[layer], GLA_HEADS)[None, :]
        rw_t = router_w[layer].T
        rw_hi = rw_t.astype(BF16)
        rw_lo = (rw_t - rw_hi.astype(F32)).astype(BF16)
        rb = router_b[layer][:, None]

        (u_l, r_l, glu_l), comb = _inproj_call(x_lat, mods, lat_group, norm1_g[layer], w_in_p, lt, 0)
        (u_c, r_c, glu_c), comb = _inproj_call(x_ctx, mods, ctx_group_fn, norm1_g[layer], w_in_p, lt,
                                               seq, combined=comb)
        o_f, o_b = _gla_call(*comb, wg_pad, bg_cat, seq, n_ctx)
        yf_l = _fourier_long(u_l, long_tables, chan)
        cv_l = _conv_call(glu_l, seq // (seq // GRID_W), conv_w[layer], conv_b[layer],
                          conv_ln_g[layer], conv_ln_b[layer])
        n_tok = bsz * seq + (0 if last else bsz * n_ctx)
        x_lat, routed = _outproj_call(yf_l, o_f, o_b, 0, r_l, cv_l, x_lat, mods, lat_group, gn_tiled,
                                      w_out_b, norm2_g[layer], rw_hi, rw_lo, rb, n_tok, 0)
        if not last:
            yf_c = _fourier_short(u_c, chan)
            cv_c = _conv_call(glu_c, n_ctx, conv_w[layer], conv_b[layer], conv_ln_g[layer],
                              conv_ln_b[layer])
            x_ctx, routed = _outproj_call(yf_c, o_f, o_b, seq, r_c, cv_c, x_ctx, mods, ctx_group_fn,
                                          gn_tiled, w_out_b, norm2_g[layer], rw_hi, rw_lo, rb, n_tok,
                                          bsz * seq, carried=routed)

        h2, idx_t, prob_t = routed
        runs = _run_tables(idx_t, n_tok)
        xs = _sort_dispatch_call(runs, idx_t, h2)
        yb = _expert_call(runs["block_e"], runs["n_used"], xs, layer, exp_w_gu, exp_b_gu, exp_w_dn,
                          exp_b_dn)
        probs, idx_tok = prob_t.T, idx_t.T
        lat_blocks = seq // SORT_TB
        x_lat = _sort_combine_call(runs, idx_tok, probs, yb, x_lat, mods, lambda i: i // lat_blocks,
                                   final_norm_g, 0, last)
        if not last:
            x_ctx = _sort_combine_call(runs, idx_tok, probs, yb, x_ctx, mods, lambda i: ctx_group,
                                       final_norm_g, bsz * seq, False)

    return x_lat
```

```python
import functools

import jax
import jax.numpy as jnp
from jax import lax
from jax.experimental import pallas as pl
from jax.experimental.pallas import tpu as pltpu

F32 = jnp.float32
BF16 = jnp.bfloat16
U32 = jnp.uint32
HIGH_HALF = 0xFFFF0000

D_MODEL = 1024
DEPTH = 2
GRID_W = 64
FOURIER_WIDTH = 256
FOURIER_HEADS = 4
FOURIER_HEAD_DIM = FOURIER_WIDTH // FOURIER_HEADS
GLA_HEADS = 4
GLA_KEY_WIDTH = 256
GLA_VALUE_WIDTH = 512
GLA_DK = GLA_KEY_WIDTH // GLA_HEADS
GLA_DV = GLA_VALUE_WIDTH // GLA_HEADS
GLA_GATE_RANK = 16
GLA_GATE_NORMALIZER = 16.0
CONV_WIDTH = 256
CONV_KERNEL = 31
N_EXPERTS = 32
TOP_K = 4
D_FF = D_MODEL
SWIGLU_LIMIT = 7.0
SWIGLU_ALPHA = 1.702
NORM_EPS = 1e-6

LANES = 128
VMEM_LIMIT = 56 * 1024 * 1024

COL_U = 0
COL_Q = COL_U + FOURIER_WIDTH
COL_K = COL_Q + GLA_KEY_WIDTH
COL_V = COL_K + GLA_KEY_WIDTH
COL_R = COL_V + GLA_VALUE_WIDTH
COL_CA = COL_R + GLA_VALUE_WIDTH
COL_CG = COL_CA + CONV_WIDTH
COL_GL = COL_CG + CONV_WIDTH
GLR_PAD = LANES
IN_PAD = COL_GL + GLR_PAD

GLA_CHUNK = 128
GLA_STEP_CHUNKS = 2
TM_ROWS = 512
TM_EXPERT = 512
EXPERT_CHUNKS = 4
RUN_ALIGN = 8
SORT_TB = 512
SORT_CHUNK = 256
SORT_ROWS = -(-(TOP_K * SORT_TB + (RUN_ALIGN - 1) * N_EXPERTS) // SORT_CHUNK) * SORT_CHUNK
FFT_N2 = 128
FFT_KB = 8
CONV_HALO = 16
CONV_GROUP = 8
CONV_PITCH = 100


def _cparams(*sem):
    return pltpu.CompilerParams(dimension_semantics=sem, vmem_limit_bytes=VMEM_LIMIT)


def _dot(a, b):
    return jnp.dot(a, b, preferred_element_type=F32)


def _dot_nt(a, b):
    return lax.dot_general(a, b, (((1,), (1,)), ((), ())), preferred_element_type=F32)


def _dot_tn(a, b):
    return lax.dot_general(a, b, (((0,), (0,)), ((), ())), preferred_element_type=F32)


def _split(a):
    hi = a.astype(BF16)
    lo = (a - hi.astype(F32)).astype(BF16)
    return hi, lo


def _dot3(a, b):
    ah, al = _split(a)
    bh, bl = _split(b)
    return _dot(ah, bh) + _dot(ah, bl) + _dot(al, bh)


def _sigmoid(x):
    return 1.0 / (1.0 + jnp.exp(-x))


def _pack_bf16_pairs(a, holds_bf16=False):
    h = a.shape[1] // 2
    bits = lax.bitcast_convert_type(a if holds_bf16 else a.astype(BF16).astype(F32), U32)
    return (bits[:, :h] >> 16) | (bits[:, h:] & U32(HIGH_HALF))


def _unpack_bf16_pairs(w):
    return (lax.bitcast_convert_type(w << 16, F32), lax.bitcast_convert_type(w & U32(HIGH_HALF), F32))


def _rms(x, g):
    ms = jnp.mean(x * x, axis=-1, keepdims=True)
    return x * lax.rsqrt(ms + NORM_EPS) * g


def _mod_kernel(cv_ref, w_ref, b_ref, o_ref):
    cv = cv_ref[...]
    a = cv * _sigmoid(cv)
    o_ref[...] = _dot3(a, w_ref[...]) + b_ref[...]


def _mod_call(cvec, w_mod, b_mod):
    depth, d, n = w_mod.shape
    rows = cvec.shape[0]
    tn = 1536
    return pl.pallas_call(
        _mod_kernel,
        grid=(depth, n // tn),
        in_specs=[
            pl.BlockSpec((rows, d), lambda l, j: (0, 0)),
            pl.BlockSpec((None, d, tn), lambda l, j: (l, 0, j)),
            pl.BlockSpec((None, 1, tn), lambda l, j: (l, 0, j)),
        ],
        out_specs=pl.BlockSpec((None, rows, tn), lambda l, j: (l, 0, j)),
        out_shape=jax.ShapeDtypeStruct((depth, rows, n), F32),
        compiler_params=_cparams("arbitrary", "arbitrary"),
    )(cvec, w_mod, b_mod.reshape(depth, 1, n))


def _inproj_kernel(x_ref, g_ref, sh_ref, sc_ref, w_ref, *rest):
    u_ref, r_ref, glu_ref, q_ref, k_ref, v_ref, gl_ref = rest[-7:]
    x = x_ref[...]
    h = _rms(x, g_ref[...]) * (1.0 + sc_ref[...]) + sh_ref[...]
    p = _dot(h.astype(BF16), w_ref[...])
    u_ref[...] = p[:, COL_U:COL_Q].astype(u_ref.dtype)
    q_ref[...] = (p[:, COL_Q:COL_K] * (GLA_DK ** -0.5)).astype(q_ref.dtype)
    k_ref[...] = p[:, COL_K:COL_V].astype(k_ref.dtype)
    v_ref[...] = p[:, COL_V:COL_R].astype(v_ref.dtype)
    r_ref[...] = p[:, COL_R:COL_CA].astype(r_ref.dtype)
    glu_ref[...] = (p[:, COL_CA:COL_CG] * _sigmoid(p[:, COL_CG:COL_GL])).astype(glu_ref.dtype)
    gl_ref[...] = p[:, COL_GL:IN_PAD]


def _inproj_call(x, mods, group_of_batch, norm_g, w_in_p, lt, row0, combined=None):
    bsz, n, d = x.shape
    tm = min(TM_ROWS, n)
    assert n % tm == 0 and row0 % tm == 0
    blk0 = row0 // tm
    nb = n // tm
    steps = nb + (-(-(lt - n) // tm) if combined is None else 0)
    widths = (GLA_KEY_WIDTH, GLA_KEY_WIDTH, GLA_VALUE_WIDTH, GLR_PAD)
    dtypes = (BF16, BF16, BF16, F32)
    row_spec = lambda w: pl.BlockSpec((None, tm, w), lambda b, i: (b, jnp.minimum(i, nb - 1), 0))
    comb_spec = lambda w: pl.BlockSpec((None, tm, w), lambda b, i: (b, blk0 + i, 0))
    mod_spec = lambda which: pl.BlockSpec(
        (None, None, 1, d), lambda b, i: (group_of_batch(b), which, 0, 0))
    in_specs = [
        row_spec(d),
        pl.BlockSpec((1, d), lambda b, i: (0, 0)),
        mod_spec(0), mod_spec(1),
        pl.BlockSpec((d, IN_PAD), lambda b, i: (0, 0)),
    ]
    args = [x, norm_g.reshape(1, d), mods, mods, w_in_p]
    aliases = {}
    if combined is not None:
        for t, arr in enumerate(combined):
            in_specs.append(pl.BlockSpec(memory_space=pl.ANY))
            aliases[len(args)] = 3 + t
            args.append(arr)
    out_shape = [
        jax.ShapeDtypeStruct((bsz, n, FOURIER_WIDTH), BF16),
        jax.ShapeDtypeStruct((bsz, n, GLA_VALUE_WIDTH), BF16),
        jax.ShapeDtypeStruct((bsz, n, CONV_WIDTH), BF16),
    ] + [jax.ShapeDtypeStruct((bsz, lt, w), dt) for w, dt in zip(widths, dtypes)]
    out_specs = [row_spec(FOURIER_WIDTH), row_spec(GLA_VALUE_WIDTH), row_spec(CONV_WIDTH)] + [
        comb_spec(w) for w in widths]
    outs = pl.pallas_call(
        _inproj_kernel,
        grid=(bsz, steps),
        in_specs=in_specs,
        out_specs=out_specs,
        out_shape=out_shape,
        input_output_aliases=aliases,
        compiler_params=_cparams("arbitrary", "arbitrary"),
    )(*args)
    return outs[:3], outs[3:]


def _dft_tables(length):
    n2 = FFT_N2
    n1 = length // n2
    two_pi = 2.0 * jnp.pi

    def cs(num, den):
        ang = (num % den).astype(F32) * (two_pi / den)
        return jnp.cos(ang), jnp.sin(ang)

    k1 = jnp.arange(n1, dtype=jnp.int32)
    c1, s1 = cs(k1[:, None] * k1[None, :], n1)
    stage1 = (jnp.concatenate([c1, -s1], axis=0) * (n1 ** -0.5)).astype(BF16)
    k2 = jnp.arange(n2, dtype=jnp.int32)
    ct, st = cs(k1[:, None] * k2[None, :], length)
    cf, sf = cs(k2[:, None] * k2[None, :], n2)
    scale = n2 ** -0.5
    mr = (ct[:, None, :] * cf[None] - st[:, None, :] * sf[None]) * scale
    mi = -(st[:, None, :] * cf[None] + ct[:, None, :] * sf[None]) * scale
    stage2 = jnp.concatenate([jnp.concatenate([mr, -mi], axis=2),
                              jnp.concatenate([mi, mr], axis=2)], axis=1).astype(BF16)
    return stage1, stage2


def _channel_tables():
    hd = FOURIER_HEAD_DIM
    c = jnp.arange(FOURIER_WIDTH, dtype=jnp.int32)
    same_head = (c[:, None] // hd) == (c[None, :] // hd)
    ang = (((c[:, None] % hd) * (c[None, :] % hd)) % hd).astype(F32) * (2.0 * jnp.pi / hd)
    scale = hd ** -0.5
    bdc = jnp.where(same_head, jnp.cos(ang) * scale, 0.0).astype(BF16)
    bds = jnp.where(same_head, jnp.sin(ang) * scale, 0.0).astype(BF16)
    return bdc, bds


def _fft1_kernel(x_ref, cs_ref, zr_ref, zi_ref):
    n1 = x_ref.shape[0]
    z = _dot(cs_ref[...], x_ref[...])
    zr_ref[...] = z[:n1].astype(zr_ref.dtype)
    zi_ref[...] = z[n1:].astype(zi_ref.dtype)


def _fft2_kernel(zr_ref, zi_ref, m_ref, bdc_ref, bds_ref, o_ref):
    kb, n2, w = zr_ref.shape
    for j in range(kb):
        z = jnp.concatenate([zr_ref[j], zi_ref[j]], axis=0)
        a = _dot(m_ref[j], z)
        y = _dot(a[:n2].astype(BF16), bdc_ref[...]) + _dot(a[n2:].astype(BF16), bds_ref[...])
        o_ref[:, j * w:(j + 1) * w] = y.astype(o_ref.dtype)


def _fourier_long(u, tables, chan):
    bsz, length, w = u.shape
    stage1, stage2 = tables
    bdc, bds = chan
    n2 = FFT_N2
    n1 = length // n2
    tn = 4096
    cols = n2 * w
    zr, zi = pl.pallas_call(
        _fft1_kernel,
        grid=(bsz, cols // tn),
        in_specs=[pl.BlockSpec((None, n1, tn), lambda b, j: (b, 0, j)),
                  pl.BlockSpec((2 * n1, n1), lambda b, j: (0, 0))],
        out_specs=[pl.BlockSpec((None, n1, tn), lambda b, j: (b, 0, j))] * 2,
        out_shape=[jax.ShapeDtypeStruct((bsz, n1, cols), BF16)] * 2,
        compiler_params=_cparams("arbitrary", "arbitrary"),
    )(u.reshape(bsz, n1, cols), stage1)
    kb = FFT_KB
    z_spec = pl.BlockSpec((None, kb, n2, w), lambda b, j: (b, j, 0, 0))
    y = pl.pallas_call(
        _fft2_kernel,
        grid=(bsz, n1 // kb),
        in_specs=[z_spec, z_spec,
                  pl.BlockSpec((kb, 2 * n2, 2 * n2), lambda b, j: (j, 0, 0)),
                  pl.BlockSpec((w, w), lambda b, j: (0, 0)),
                  pl.BlockSpec((w, w), lambda b, j: (0, 0))],
        out_specs=pl.BlockSpec((None, n2, kb * w), lambda b, j: (b, 0, j)),
        out_shape=jax.ShapeDtypeStruct((bsz, n2, n1 * w), BF16),
        compiler_params=_cparams("arbitrary", "arbitrary"),
    )(zr.reshape(bsz, n1, n2, w), zi.reshape(bsz, n1, n2, w), stage2, bdc, bds)
    return y.reshape(bsz, length, w)


def _dft_short_kernel(u_ref, c_ref, s_ref, bdc_ref, bds_ref, o_ref):
    u = u_ref[...]
    p = _dot(u, bdc_ref[...]).astype(BF16)
    q = _dot(u, bds_ref[...]).astype(BF16)
    o_ref[...] = (_dot(c_ref[...], p) - _dot(s_ref[...], q)).astype(o_ref.dtype)


def _fourier_short(u, chan):
    bsz, length, w = u.shape
    bdc, bds = chan
    k = jnp.arange(length, dtype=jnp.int32)
    ang = ((k[:, None] * k[None, :]) % length).astype(F32) * (2.0 * jnp.pi / length)
    c = (jnp.cos(ang) * length ** -0.5).astype(BF16)
    s = (jnp.sin(ang) * length ** -0.5).astype(BF16)
    full = lambda n: pl.BlockSpec((n, n), lambda b: (0, 0))
    return pl.pallas_call(
        _dft_short_kernel,
        grid=(bsz,),
        in_specs=[pl.BlockSpec((None, length, w), lambda b: (b, 0, 0)),
                  full(length), full(length), full(w), full(w)],
        out_specs=pl.BlockSpec((None, length, w), lambda b: (b, 0, 0)),
        out_shape=jax.ShapeDtypeStruct((bsz, length, w), BF16),
        compiler_params=_cparams("arbitrary"),
    )(u, c, s, bdc, bds)


def _conv_kernel(x_ref, w_ref, cb_ref, lg_ref, lb_ref, o_ref, pad_ref, *, seg):
    nseg = x_ref.shape[0] // seg
    width = x_ref.shape[1]
    halo = jnp.zeros((CONV_HALO, width), F32)
    for s in range(nseg):
        pad_ref[s, 0:CONV_HALO, :] = halo
        pad_ref[s, CONV_HALO:CONV_HALO + seg, :] = x_ref[s * seg:(s + 1) * seg, :].astype(F32)
        pad_ref[s, CONV_HALO + seg:2 * CONV_HALO + seg, :] = halo
    first = CONV_HALO - CONV_KERNEL // 2
    sub = 8
    span = seg + 2 * CONV_HALO - sub
    for s in range(nseg):
        acc = jnp.zeros((seg, width), F32)
        for r in range(sub):
            shifted = pad_ref[s, r:r + span, :]
            for a in range((span - seg) // sub + 1):
                j = a * sub + r - first
                if 0 <= j < CONV_KERNEL:
                    acc = acc + shifted[a * sub:a * sub + seg, :] * w_ref[j:j + 1, :]
        y = acc + cb_ref[...]
        mu = jnp.mean(y, axis=-1, keepdims=True)
        yc = y - mu
        var = jnp.mean(yc * yc, axis=-1, keepdims=True)
        z = yc * lax.rsqrt(var + NORM_EPS) * lg_ref[...] + lb_ref[...]
        o_ref[s * seg:(s + 1) * seg, :] = (z * _sigmoid(z)).astype(o_ref.dtype)


def _conv_group_kernel(x_ref, w_ref, cb_ref, lg_ref, lb_ref, o_ref, pad_ref, acc_ref, *, seg):
    g, p = CONV_GROUP, CONV_PITCH
    halves = x_ref.shape[1] // LANES
    pad_ref[...] = jnp.zeros_like(pad_ref)
    for s in range(g):
        for h in range(halves):
            pad_ref[h, s * p + CONV_HALO:s * p + CONV_HALO + seg, :] = (
                x_ref[s * seg:(s + 1) * seg, h * LANES:(h + 1) * LANES].astype(F32))
    first = CONV_HALO - CONV_KERNEL // 2
    unroll = 4

    def body(i, carry):
        t0 = unroll * i
        for h in range(halves):
            accs = [None] * unroll
            for j in range(CONV_KERNEL):
                tap = w_ref[h, j]
                for u in range(unroll):
                    term = pad_ref[h, pl.ds(t0 + u + first + j, g, stride=p), :] * tap
                    accs[u] = term if accs[u] is None else accs[u] + term
            for u in range(unroll):
                acc_ref[h, pl.ds(t0 + u, g, stride=p), :] = accs[u]
        return carry

    lax.fori_loop(0, seg // unroll, body, 0)
    for s in range(g):
        y = jnp.concatenate([acc_ref[h, s * p:s * p + seg, :] for h in range(halves)], axis=1) + cb_ref[...]
        mu = jnp.mean(y, axis=-1, keepdims=True)
        yc = y - mu
        var = jnp.mean(yc * yc, axis=-1, keepdims=True)
        z = yc * lax.rsqrt(var + NORM_EPS) * lg_ref[...] + lb_ref[...]
        o_ref[s * seg:(s + 1) * seg, :] = (z * _sigmoid(z)).astype(o_ref.dtype)


def _conv_call(glu, seg, conv_w, conv_b, ln_g, ln_b):
    bsz, n, w = glu.shape
    t = max(seg, min(TM_ROWS, n))
    assert n % t == 0 and t % seg == 0
    vec = lambda: pl.BlockSpec((1, w), lambda b, i: (0, 0))
    if t // seg == CONV_GROUP and seg + 2 * CONV_HALO <= CONV_PITCH and w % LANES == 0:
        halves = w // LANES
        taps = jnp.broadcast_to(conv_w.reshape(CONV_KERNEL, halves, 1, LANES).transpose(1, 0, 2, 3),
                                (halves, CONV_KERNEL, 8, LANES))
        slab = pltpu.VMEM((halves, CONV_GROUP * CONV_PITCH, LANES), F32)
        return pl.pallas_call(
            functools.partial(_conv_group_kernel, seg=seg),
            grid=(bsz, n // t),
            in_specs=[pl.BlockSpec((None, t, w), lambda b, i: (b, i, 0)),
                      pl.BlockSpec(taps.shape, lambda b, i: (0, 0, 0, 0)),
                      vec(), vec(), vec()],
            out_specs=pl.BlockSpec((None, t, w), lambda b, i: (b, i, 0)),
            out_shape=jax.ShapeDtypeStruct((bsz, n, w), BF16),
            scratch_shapes=[slab, slab],
            compiler_params=_cparams("arbitrary", "arbitrary"),
        )(glu, taps, conv_b.reshape(1, w), ln_g.reshape(1, w), ln_b.reshape(1, w))
    return pl.pallas_call(
        functools.partial(_conv_kernel, seg=seg),
        grid=(bsz, n // t),
        in_specs=[pl.BlockSpec((None, t, w), lambda b, i: (b, i, 0)),
                  pl.BlockSpec((CONV_KERNEL, w), lambda b, i: (0, 0)),
                  vec(), vec(), vec()],
        out_specs=pl.BlockSpec((None, t, w), lambda b, i: (b, i, 0)),
        out_shape=jax.ShapeDtypeStruct((bsz, n, w), BF16),
        scratch_shapes=[pltpu.VMEM((t // seg, seg + 2 * CONV_HALO, w), F32)],
        compiler_params=_cparams("arbitrary", "arbitrary"),
    )(glu, conv_w, conv_b.reshape(1, w), ln_g.reshape(1, w), ln_b.reshape(1, w))


def _gla_direction(q_ref, k_ref, v_ref, gl_ref, wg_ref, bg_ref, o_ref, st, reverse):
    c = GLA_CHUNK
    n_sub = q_ref.shape[0] // c
    for s in (reversed(range(n_sub)) if reverse else range(n_sub)):
        rows = slice(s * c, (s + 1) * c)
        st = _gla_chunk(q_ref[rows, :], k_ref[rows, :], v_ref[rows, :], gl_ref[rows, :], wg_ref, bg_ref,
                        o_ref.at[rows, :], st, reverse)
    return st


def _gla_chunk(q, k, v, glr, wg_ref, bg_ref, o_ref, st, reverse):
    c = GLA_CHUNK
    kw = GLA_KEY_WIDTH
    col0 = kw if reverse else 0
    pre = _dot3(glr, wg_ref[:, col0:col0 + kw]) + bg_ref[:, col0:col0 + kw]
    g = (jnp.minimum(pre, 0.0) - jnp.log(1.0 + jnp.exp(-jnp.abs(pre)))) * (1.0 / GLA_GATE_NORMALIZER)
    row = lax.broadcasted_iota(jnp.int32, (c, c), 0)
    col = lax.broadcasted_iota(jnp.int32, (c, c), 1)
    seen = (col >= row) if reverse else (col <= row)
    tri = jnp.where(seen, 1.0, 0.0).astype(BF16)
    gh, gl = _split(g)
    b = _dot(tri, gh) + _dot(tri, gl)
    mid = c // 2 if reverse else c // 2 - 1
    last = 0 if reverse else c - 1
    b_mid = b[mid:mid + 1, :]
    b_last = b[last:last + 1, :]
    q = q.astype(F32)
    k = k.astype(F32)
    qe = q * jnp.exp(b - b_mid)
    ke = k * jnp.exp(b_mid - b)
    kd = k * jnp.exp(b_last - b)
    head_of_lane = lax.broadcasted_iota(jnp.int32, (1, kw), 1) // GLA_DK
    q_heads = jnp.concatenate(
        [jnp.where(head_of_lane == h, qe, 0.0) for h in range(GLA_HEADS)], axis=0).astype(BF16)
    rhs = jnp.concatenate([ke, st * jnp.exp(b_mid)], axis=0).astype(BF16)
    res = _dot_nt(q_heads, rhs)
    outs = []
    for h in range(GLA_HEADS):
        blk = res[h * c:(h + 1) * c, :]
        scores = jnp.where(seen, blk[:, :c], 0.0).astype(BF16)
        outs.append(_dot(scores, v[:, h * GLA_DV:(h + 1) * GLA_DV]) + blk[:, c:])
    o_ref[...] = jnp.concatenate(outs, axis=1).astype(o_ref.dtype)
    kv = _dot_tn(v, kd.astype(BF16))
    ds = jnp.zeros_like(st)
    for h in range(GLA_HEADS):
        ds = ds + jnp.where(head_of_lane == h, kv[h * GLA_DV:(h + 1) * GLA_DV, :], 0.0)
    return st * jnp.exp(b_last) + ds


def _gla_kernel(qf, kf, vf, gf, qb, kb, vb, gb, wg_ref, bg_ref, of_ref, ob_ref, sf_ref, sb_ref):
    @pl.when(pl.program_id(0) == 0)
    def _():
        sf_ref[...] = jnp.zeros_like(sf_ref)
        sb_ref[...] = jnp.zeros_like(sb_ref)

    bsz = qf.shape[0]
    states = [(sf_ref[b], sb_ref[b]) for b in range(bsz)]
    new = []
    for b, (s_f, s_b) in enumerate(states):
        new.append((
            _gla_direction(qf.at[b], kf.at[b], vf.at[b], gf.at[b], wg_ref, bg_ref, of_ref.at[b], s_f, False),
            _gla_direction(qb.at[b], kb.at[b], vb.at[b], gb.at[b], wg_ref, bg_ref, ob_ref.at[b], s_b, True)))
    for b, (s_f, s_b) in enumerate(new):
        sf_ref[b] = s_f
        sb_ref[b] = s_b


def _gla_call(q, k, v, gl, wg_pad, bg_cat, n_lat, n_ctx):
    bsz, lt, _ = q.shape
    c = GLA_STEP_CHUNKS * GLA_CHUNK
    assert n_lat % c == 0 and n_ctx % c == 0
    cl, cc = n_lat // c, n_ctx // c

    def fwd_blk(j):
        return jnp.where(j < cc, cl + j, j - cc)

    def bwd_blk(j):
        return jnp.where(j < cc, cl + cc - 1 - j, cl - 1 - (j - cc))

    def spec(w, blk):
        return pl.BlockSpec((bsz, c, w), lambda j: (0, blk(j), 0))

    widths = (GLA_KEY_WIDTH, GLA_KEY_WIDTH, GLA_VALUE_WIDTH, GLR_PAD)
    in_specs = [spec(w, fwd_blk) for w in widths] + [spec(w, bwd_blk) for w in widths] + [
        pl.BlockSpec(wg_pad.shape, lambda j: (0, 0)),
        pl.BlockSpec(bg_cat.shape, lambda j: (0, 0))]
    return pl.pallas_call(
        _gla_kernel,
        grid=(cl + cc,),
        in_specs=in_specs,
        out_specs=[spec(GLA_VALUE_WIDTH, fwd_blk), spec(GLA_VALUE_WIDTH, bwd_blk)],
        out_shape=[jax.ShapeDtypeStruct((bsz, lt, GLA_VALUE_WIDTH), F32)] * 2,
        scratch_shapes=[pltpu.VMEM((bsz, GLA_DV, GLA_KEY_WIDTH), F32)] * 2,
        compiler_params=_cparams("arbitrary"),
    )(q, k, v, gl, q, k, v, gl, wg_pad, bg_cat)


def _outproj_kernel(yf_ref, of_ref, ob_ref, r_ref, cv_ref, x_ref, gate_ref, sh_ref, sc_ref,
                    gn_ref, wo_ref, n2_ref, rwh_ref, rwl_ref, rb_ref, *rest):
    xo_ref, h2_ref, idx_ref, prob_ref = rest[-4:]
    o = of_ref[...] + ob_ref[...]
    heads = []
    for h in range(GLA_HEADS):
        oh = o[:, h * GLA_DV:(h + 1) * GLA_DV]
        heads.append(oh * lax.rsqrt(jnp.mean(oh * oh, axis=-1, keepdims=True) + NORM_EPS))
    r = r_ref[...].astype(F32)
    gla = jnp.concatenate(heads, axis=1) * gn_ref[...] * (r * _sigmoid(r))
    c0, c1 = FOURIER_WIDTH, FOURIER_WIDTH + GLA_VALUE_WIDTH
    y = (_dot(yf_ref[...], wo_ref[0:c0, :]) + _dot(gla.astype(BF16), wo_ref[c0:c1, :])
         + _dot(cv_ref[...], wo_ref[c1:, :]))
    xn = x_ref[...] + gate_ref[...] * y
    xo_ref[...] = xn
    h2 = _rms(xn, n2_ref[...]) * (1.0 + sc_ref[...]) + sh_ref[...]
    hh, hl = _split(h2)
    h2_ref[...] = _pack_bf16_pairs(h2)
    logits = (_dot_nt(rwh_ref[...], hh) + _dot_nt(rwh_ref[...], hl) + _dot_nt(rwl_ref[...], hh)
              + rb_ref[...])
    expert = lax.broadcasted_iota(jnp.int32, logits.shape, 0)
    vals, idxs = [], []
    cur = logits
    for _ in range(TOP_K):
        m = jnp.max(cur, axis=0, keepdims=True)
        ix = jnp.min(jnp.where(cur == m, expert, N_EXPERTS), axis=0, keepdims=True)
        vals.append(m)
        idxs.append(ix)
        cur = jnp.where(expert == ix, -jnp.inf, cur)
    es = [jnp.exp(vv - vals[0]) for vv in vals]
    inv = 1.0 / functools.reduce(lambda a, b: a + b, es)
    idx_ref[...] = jnp.concatenate(idxs, axis=0)
    prob_ref[...] = jnp.concatenate([e * inv for e in es], axis=0)


def _outproj_call(yf, o_f, o_b, o_row0, r, cv, x, mods, group_of_batch, gn_tiled, w_out, norm2_g,
                  rw_hi, rw_lo, rb, n_tok, tok0, carried=None):
    bsz, n, d = x.shape
    tm = min(TM_ROWS, n)
    assert n % tm == 0 and o_row0 % tm == 0 and tok0 % tm == 0
    nb = n // tm
    spare = (n_tok - bsz * n) if carried is None else 0
    assert spare in (0, tm)
    steps = nb + spare // tm
    last = lambda i: jnp.minimum(i, nb - 1)
    row = lambda w: pl.BlockSpec((None, tm, w), lambda b, i: (b, last(i), 0))
    orow = pl.BlockSpec((None, tm, GLA_VALUE_WIDTH), lambda b, i: (b, o_row0 // tm + last(i), 0))
    mod = lambda which: pl.BlockSpec((None, None, 1, d), lambda b, i: (group_of_batch(b), which, 0, 0))
    const = lambda a: pl.BlockSpec(a.shape, lambda b, i: (0,) * a.ndim)
    consts = [gn_tiled, w_out, norm2_g.reshape(1, d), rw_hi, rw_lo, rb]
    in_specs = [row(FOURIER_WIDTH), orow, orow, row(GLA_VALUE_WIDTH), row(CONV_WIDTH), row(d),
                mod(2), mod(3), mod(4)] + [const(a) for a in consts]
    args = [yf, o_f, o_b, r, cv, x, mods, mods, mods] + consts
    aliases = {}
    if carried is not None:
        for t, arr in enumerate(carried):
            in_specs.append(pl.BlockSpec(memory_space=pl.ANY))
            aliases[len(args)] = 1 + t
            args.append(arr)
    tokblk = lambda b, i: tok0 // tm + jnp.where(
        jnp.logical_and(i == nb, b == bsz - 1), bsz * nb, b * nb + last(i))
    out_specs = [row(d),
                 pl.BlockSpec((tm, d // 2), lambda b, i: (tokblk(b, i), 0)),
                 pl.BlockSpec((TOP_K, tm), lambda b, i: (0, tokblk(b, i))),
                 pl.BlockSpec((TOP_K, tm), lambda b, i: (0, tokblk(b, i)))]
    out_shape = [jax.ShapeDtypeStruct((bsz, n, d), F32),
                 jax.ShapeDtypeStruct((n_tok, d // 2), U32),
                 jax.ShapeDtypeStruct((TOP_K, n_tok), jnp.int32),
                 jax.ShapeDtypeStruct((TOP_K, n_tok), F32)]
    outs = pl.pallas_call(
        _outproj_kernel,
        grid=(bsz, steps),
        in_specs=in_specs,
        out_specs=out_specs,
        out_shape=out_shape,
        input_output_aliases=aliases,
        compiler_params=_cparams("arbitrary", "arbitrary"),
    )(*args)
    return outs[0], outs[1:]


def _expert_kernel(be_ref, nu_ref, x_ref, wgu_ref, bgu_ref, wdn_ref, bdn_ref, o_ref, wgu_s, wdn_s):
    i = pl.program_id(0)
    nu = nu_ref[0]
    n_chunks = wgu_s.shape[0]
    fc = D_FF // n_chunks

    @pl.when(i >= nu)
    def _():
        o_ref[...] = jnp.zeros_like(o_ref)

    @pl.when(i < nu)
    def _():
        changed = jnp.logical_or(i == 0, be_ref[i] != be_ref[jnp.maximum(i - 1, 0)])

        @pl.when(changed)
        def _():
            rows = 128
            for s in range(wgu_ref.shape[0] // rows):
                rs = slice(s * rows, (s + 1) * rows)
                for n in range(n_chunks):
                    wgu_s[n, rs, 0:fc] = wgu_ref[rs, n * fc:(n + 1) * fc].astype(BF16)
                    wgu_s[n, rs, fc:2 * fc] = wgu_ref[rs, D_FF + n * fc:D_FF + (n + 1) * fc].astype(BF16)
                wdn_s[rs, :] = wdn_ref[rs, :].astype(BF16)

        x = jnp.concatenate(_unpack_bf16_pairs(x_ref[...]), axis=1).astype(BF16)
        acts = []
        for n in range(n_chunks):
            gate = _dot(x, wgu_s[n, :, 0:fc]) + bgu_ref[:, n * fc:(n + 1) * fc]
            up = _dot(x, wgu_s[n, :, fc:2 * fc]) + bgu_ref[:, D_FF + n * fc:D_FF + (n + 1) * fc]
            gate = jnp.minimum(gate, SWIGLU_LIMIT)
            up = jnp.clip(up, -SWIGLU_LIMIT, SWIGLU_LIMIT)
            acts.append((gate * _sigmoid(SWIGLU_ALPHA * gate) * (up + 1.0)).astype(BF16))
        half = n_chunks // 2
        y = (_dot(jnp.concatenate(acts[:half], axis=1), wdn_s[0:half * fc, :])
             + _dot(jnp.concatenate(acts[half:], axis=1), wdn_s[half * fc:, :]) + bdn_ref[...])
        o_ref[...] = _pack_bf16_pairs(y)


def _expert_call(block_e, n_used, xs, layer, w_gu, b_gu, w_dn, b_dn):
    rows, dh = xs.shape
    d = 2 * dh
    tm = TM_EXPERT
    nblk = rows // tm
    depth, e, _, f2 = w_gu.shape
    n_chunks = EXPERT_CHUNKS
    live = lambda i, nu: jnp.minimum(i, nu[0] - 1)
    wmap = lambda i, be, nu: (layer, be[live(i, nu)], 0, 0)
    grid_spec = pltpu.PrefetchScalarGridSpec(
        num_scalar_prefetch=2,
        grid=(nblk,),
        in_specs=[
            pl.BlockSpec((tm, dh), lambda i, be, nu: (live(i, nu), 0)),
            pl.BlockSpec((None, None, d, f2), wmap),
            pl.BlockSpec((None, None, 1, f2), wmap),
            pl.BlockSpec((None, None, f2 // 2, d), wmap),
            pl.BlockSpec((None, None, 1, d), wmap),
        ],
        out_specs=pl.BlockSpec((tm, dh), lambda i, be, nu: (i, 0)),
        scratch_shapes=[pltpu.VMEM((n_chunks, d, f2 // n_chunks), BF16), pltpu.VMEM((f2 // 2, d), BF16)],
    )
    return pl.pallas_call(
        _expert_kernel,
        grid_spec=grid_spec,
        out_shape=jax.ShapeDtypeStruct((rows, dh), xs.dtype),
        compiler_params=_cparams("arbitrary"),
    )(block_e, n_used, xs, w_gu, b_gu.reshape(depth, e, 1, f2), w_dn, b_dn.reshape(depth, e, 1, d))


def _count_kernel(idx_ref, cnt_ref):
    expert = lax.broadcasted_iota(jnp.int32, (N_EXPERTS, idx_ref.shape[1]), 0)
    total = jnp.zeros((N_EXPERTS, 1), F32)
    for kk in range(TOP_K):
        total = total + jnp.sum((expert == idx_ref[kk:kk + 1, :]).astype(F32), axis=1, keepdims=True)
    cnt_ref[...] = total.astype(jnp.int32)


def _run_tables(idx_t, n_tok):
    tm, tb = TM_EXPERT, SORT_TB
    nblk = n_tok // tb
    assert n_tok % tb == 0
    cnt = pl.pallas_call(
        _count_kernel,
        grid=(nblk,),
        in_specs=[pl.BlockSpec((TOP_K, tb), lambda i: (0, i))],
        out_specs=pl.BlockSpec((None, N_EXPERTS, 1), lambda i: (i, 0, 0)),
        out_shape=jax.ShapeDtypeStruct((nblk, N_EXPERTS, 1), jnp.int32),
        compiler_params=_cparams("arbitrary"),
    )(idx_t)[:, :, 0]
    run = (cnt + RUN_ALIGN - 1) // RUN_ALIGN * RUN_ALIGN
    total = jnp.sum(run, axis=0)
    padded = (total + tm - 1) // tm * tm
    pad_end = jnp.cumsum(padded)
    dst = (pad_end - padded)[None, :] + jnp.cumsum(run, axis=0) - run
    boff = jnp.cumsum(run, axis=1) - run
    n_blocks = -(-(TOP_K * n_tok + (RUN_ALIGN - 1) * N_EXPERTS * nblk) // tm) + N_EXPERTS
    first_row = jnp.arange(n_blocks, dtype=jnp.int32) * tm
    block_e = jnp.minimum(jnp.sum(pad_end[None, :] <= first_row[:, None], axis=1),
                          N_EXPERTS - 1).astype(jnp.int32)
    n_used = (pad_end[-1:] // tm).astype(jnp.int32)
    flat = lambda a: a.reshape(-1).astype(jnp.int32)
    return dict(run=flat(run), dst=flat(dst), boff=flat(boff), boff_f=boff.astype(F32),
                pad_end=pad_end.astype(jnp.int32), block_e=block_e, n_used=n_used, n_blocks=n_blocks)


def _run_copies(blk, run_ref, grp_ref, boff_ref, grouped, sbuf, sem, to_grouped, wait):
    if wait:
        last = blk * N_EXPERTS + N_EXPERTS - 1
        rows = pl.multiple_of(boff_ref[last] + run_ref[last], RUN_ALIGN)

        @pl.when(rows > 0)
        def _():
            s, g = sbuf.at[pl.ds(0, rows), :], grouped.at[pl.ds(0, rows), :]
            (pltpu.make_async_copy(s, g, sem) if to_grouped else pltpu.make_async_copy(g, s, sem)).wait()
        return
    for e in range(N_EXPERTS):
        n = pl.multiple_of(run_ref[blk * N_EXPERTS + e], RUN_ALIGN)
        g0 = pl.multiple_of(grp_ref[blk * N_EXPERTS + e], RUN_ALIGN)
        s0 = pl.multiple_of(boff_ref[blk * N_EXPERTS + e], RUN_ALIGN)

        @pl.when(n > 0)
        def _(n=n, g0=g0, s0=s0):
            g, s = grouped.at[pl.ds(g0, n), :], sbuf.at[pl.ds(s0, n), :]
            (pltpu.make_async_copy(s, g, sem) if to_grouped else pltpu.make_async_copy(g, s, sem)).start()


def _sort_dispatch_kernel(run_ref, dst_ref, boffs_ref, pe_ref, nu_ref, idx_ref, boff_ref, h_ref, xs_ref,
                          pos_ref, sbuf, zero_ref, sem, zsem, *, n_blocks):
    i = pl.program_id(0)
    tb = h_ref.shape[0]
    tm = zero_ref.shape[0]

    def zero_copy(blk):
        return pltpu.make_async_copy(zero_ref, xs_ref.at[pl.ds(pl.multiple_of(blk * tm, tm), tm), :], zsem)

    @pl.when(i == 0)
    def _():
        zero_ref[...] = jnp.zeros_like(zero_ref)

        def per_expert(fn):
            for e in range(N_EXPERTS):
                end = pe_ref[e]
                start = pe_ref[e - 1] if e else 0

                @pl.when(end > start)
                def _():
                    fn(end // tm - 1)

        def per_tail(fn):
            def body(blk, carry):
                fn(blk)
                return carry
            lax.fori_loop(nu_ref[0], n_blocks, body, 0)

        per_expert(lambda blk: zero_copy(blk).start())
        per_tail(lambda blk: zero_copy(blk).start())
        per_expert(lambda blk: zero_copy(blk).wait())
        per_tail(lambda blk: zero_copy(blk).wait())

    row = lax.broadcasted_iota(jnp.int32, (tb, tb), 0)
    col = lax.broadcasted_iota(jnp.int32, (tb, tb), 1)
    earlier = (row < col).astype(F32).astype(BF16)
    expert = lax.broadcasted_iota(jnp.int32, (N_EXPERTS, tb), 0)
    base = boff_ref[...]
    pos = []
    for kk in range(TOP_K):
        onehot = (expert == idx_ref[kk:kk + 1, :]).astype(F32)
        before = _dot(onehot.astype(BF16), earlier) + base
        pos.append(jnp.sum(before * onehot, axis=0, keepdims=True).astype(jnp.int32))
        base = base + jnp.sum(onehot, axis=1, keepdims=True)
    pos_ref[...] = jnp.concatenate(pos, axis=0)

    lo, hi = _unpack_bf16_pairs(h_ref[...])
    h = jnp.concatenate([lo, hi], axis=1).astype(BF16)
    rc = SORT_CHUNK
    cur = sbuf.at[i % 2]
    def selector(c):
        r = lax.broadcasted_iota(jnp.int32, (rc, tb), 0) + c * rc
        sel = jnp.zeros((rc, tb), F32)
        for kk in range(TOP_K):
            sel = jnp.where(r == pos[kk], 1.0, sel).astype(F32)
        return sel.astype(BF16)

    n_chunks = sbuf.shape[1] // rc
    sel = selector(0)
    for c in range(n_chunks):
        nxt = selector(c + 1) if c + 1 < n_chunks else None
        cur[c * rc:(c + 1) * rc, :] = _pack_bf16_pairs(_dot(sel, h), holds_bf16=True)
        sel = nxt
    _run_copies(i, run_ref, dst_ref, boffs_ref, xs_ref, cur, sem.at[i % 2], True, False)

    @pl.when(i > 0)
    def _():
        _run_copies(i - 1, run_ref, dst_ref, boffs_ref, xs_ref, sbuf.at[(i - 1) % 2], sem.at[(i - 1) % 2],
                    True, True)

    @pl.when(i == pl.num_programs(0) - 1)
    def _():
        _run_copies(i, run_ref, dst_ref, boffs_ref, xs_ref, cur, sem.at[i % 2], True, True)


def _sort_dispatch_call(t, idx_t, h2):
    n_tok, dh = h2.shape
    tb, tm = SORT_TB, TM_EXPERT
    nblk = n_tok // tb
    grid_spec = pltpu.PrefetchScalarGridSpec(
        num_scalar_prefetch=5,
        grid=(nblk,),
        in_specs=[pl.BlockSpec((TOP_K, tb), lambda i, *_: (0, i)),
                  pl.BlockSpec((None, N_EXPERTS, 1), lambda i, *_: (i, 0, 0)),
                  pl.BlockSpec((tb, dh), lambda i, *_: (i, 0))],
        out_specs=[pl.BlockSpec(memory_space=pl.ANY), pl.BlockSpec((TOP_K, tb), lambda i, *_: (0, i))],
        scratch_shapes=[pltpu.VMEM((2, SORT_ROWS, dh), h2.dtype), pltpu.VMEM((tm, dh), h2.dtype),
                        pltpu.SemaphoreType.DMA((2,)), pltpu.SemaphoreType.DMA],
    )
    return pl.pallas_call(
        functools.partial(_sort_dispatch_kernel, n_blocks=t["n_blocks"]),
        grid_spec=grid_spec,
        out_shape=[jax.ShapeDtypeStruct((t["n_blocks"] * tm, dh), h2.dtype),
                   jax.ShapeDtypeStruct((TOP_K, n_tok), jnp.int32)],
        compiler_params=_cparams("arbitrary"),
    )(t["run"], t["dst"], t["boff"], t["pad_end"], t["n_used"], idx_t, t["boff_f"][:, :, None], h2)


def _sort_combine_kernel(run_ref, dst_ref, boffs_ref, pos_ref, p_ref, yb_hbm, x_ref, gate_ref, fg_ref, o_ref,
                         sbuf, sem, *, final_norm, blk0):
    i = pl.program_id(0)
    tb = x_ref.shape[0]

    def fetch(blk, wait):
        _run_copies(blk0 + blk, run_ref, dst_ref, boffs_ref, yb_hbm, sbuf.at[blk % 2], sem.at[blk % 2],
                    False, wait)

    @pl.when(i == 0)
    def _():
        sbuf[...] = jnp.zeros_like(sbuf)
        fetch(0, False)

    @pl.when(i + 1 < pl.num_programs(0))
    def _():
        fetch(i + 1, False)

    fetch(i, True)
    cur = sbuf.at[i % 2]

    pos = [pos_ref[:, kk:kk + 1] for kk in range(TOP_K)]
    rc = SORT_CHUNK
    f = jnp.zeros(x_ref.shape, F32)
    for c in range(sbuf.shape[1] // rc):
        r = lax.broadcasted_iota(jnp.int32, (tb, rc), 1) + c * rc
        wgt = jnp.zeros((tb, rc), F32)
        for kk in range(TOP_K):
            wgt = jnp.where(r == pos[kk], p_ref[:, kk:kk + 1], wgt)
        lo, hi = _unpack_bf16_pairs(cur[c * rc:(c + 1) * rc, :])
        y = jnp.concatenate([lo, hi], axis=1).astype(BF16)
        f = f + _dot(wgt.astype(BF16), y)
    xn = x_ref[...] + gate_ref[...] * f
    if final_norm:
        xn = _rms(xn, fg_ref[...])
    o_ref[...] = xn


def _sort_combine_call(t, pos_tok, probs, yb, x, mods, group_of_block, final_g, tok0, final_norm):
    bsz, n, d = x.shape
    tb = SORT_TB
    rows = bsz * n
    assert rows % tb == 0 and tok0 % tb == 0 and (n % tb == 0 or tb % n == 0)
    blk0 = tok0 // tb
    grid_spec = pltpu.PrefetchScalarGridSpec(
        num_scalar_prefetch=3,
        grid=(rows // tb,),
        in_specs=[pl.BlockSpec((tb, TOP_K), lambda i, *_: (blk0 + i, 0)),
                  pl.BlockSpec((tb, TOP_K), lambda i, *_: (blk0 + i, 0)),
                  pl.BlockSpec(memory_space=pl.ANY),
                  pl.BlockSpec((tb, d), lambda i, *_: (i, 0)),
                  pl.BlockSpec((None, None, 1, d), lambda i, *_: (group_of_block(i), 5, 0, 0)),
                  pl.BlockSpec((1, d), lambda i, *_: (0, 0))],
        out_specs=pl.BlockSpec((tb, d), lambda i, *_: (i, 0)),
        scratch_shapes=[pltpu.VMEM((2, SORT_ROWS, yb.shape[1]), yb.dtype), pltpu.SemaphoreType.DMA((2,))],
    )
    out = pl.pallas_call(
        functools.partial(_sort_combine_kernel, final_norm=final_norm, blk0=blk0),
        grid_spec=grid_spec,
        out_shape=jax.ShapeDtypeStruct((rows, d), F32),
        compiler_params=_cparams("arbitrary"),
    )(t["run"], t["dst"], t["boff"], pos_tok, probs, yb, x.reshape(rows, d), mods, final_g.reshape(1, d))
    return out.reshape(bsz, n, d)


def kernel(x, c, ctx, c_ctx, norm1_g, norm2_g, w_mod, b_mod, w_in, gla_wg2_f, gla_bg_f, gla_wg2_b,
           gla_bg_b, gla_norm_g, conv_w, conv_b, conv_ln_g, conv_ln_b, w_out, router_w, router_b,
           exp_w_gu, exp_b_gu, exp_w_dn, exp_b_dn, final_norm_g):
    bsz, seq, d = x.shape
    n_ctx = ctx.shape[1]
    depth = w_mod.shape[0]
    assert d == D_MODEL and seq % (FFT_N2 * FFT_KB) == 0 and seq % n_ctx == 0
    lt = seq + n_ctx
    ctx_group = bsz

    rows = 8
    cvec = jnp.concatenate([c, c_ctx[None, :], jnp.zeros((rows - bsz - 1, d), F32)], axis=0)
    mods_all = _mod_call(cvec, w_mod, b_mod).reshape(depth, rows, 6, 1, d)

    def pack_w_in(w):
        o = [0, 256, 512, 768, 1280, 1792, 1808, 1824, 2336]
        parts = [w[:, o[0]:o[1]], w[:, o[1]:o[2]], w[:, o[2]:o[3]], w[:, o[3]:o[4]], w[:, o[4]:o[5]],
                 w[:, o[7]:o[7] + CONV_WIDTH], w[:, o[7] + CONV_WIDTH:o[8]], w[:, o[5]:o[7]],
                 jnp.zeros((d, GLR_PAD - 2 * GLA_GATE_RANK), w.dtype)]
        return jnp.concatenate(parts, axis=1).astype(BF16)

    long_tables = _dft_tables(seq)
    chan = _channel_tables()
    x_lat, x_ctx = x, ctx
    lat_group = lambda b: b
    ctx_group_fn = lambda b: ctx_group

    for layer in range(depth):
        last = layer == depth - 1
        mods = mods_all[layer]
        w_in_p = pack_w_in(w_in[layer])
        wg_pad = jnp.zeros((GLR_PAD, 2 * GLA_KEY_WIDTH), F32)
        wg_pad = wg_pad.at[:GLA_GATE_RANK, :GLA_KEY_WIDTH].set(gla_wg2_f[layer])
        wg_pad = wg_pad.at[GLA_GATE_RANK:2 * GLA_GATE_RANK, GLA_KEY_WIDTH:].set(gla_wg2_b[layer])
        bg_cat = jnp.concatenate([gla_bg_f[layer], gla_bg_b[layer]])[None, :]
        w_out_b = w_out[layer].astype(BF16)
        gn_tiled = jnp.tile(gla_norm_g[layer], GLA_HEADS)[None, :]
        rw_t = router_w[layer].T
        rw_hi = rw_t.astype(BF16)
        rw_lo = (rw_t - rw_hi.astype(F32)).astype(BF16)
        rb = router_b[layer][:, None]

        (u_l, r_l, glu_l), comb = _inproj_call(x_lat, mods, lat_group, norm1_g[layer], w_in_p, lt, 0)
        (u_c, r_c, glu_c), comb = _inproj_call(x_ctx, mods, ctx_group_fn, norm1_g[layer], w_in_p, lt,
                                               seq, combined=comb)
        o_f, o_b = _gla_call(*comb, wg_pad, bg_cat, seq, n_ctx)
        yf_l = _fourier_long(u_l, long_tables, chan)
        cv_l = _conv_call(glu_l, seq // (seq // GRID_W), conv_w[layer], conv_b[layer],
                          conv_ln_g[layer], conv_ln_b[layer])
        n_tok = bsz * seq + (0 if last else bsz * n_ctx)
        x_lat, routed = _outproj_call(yf_l, o_f, o_b, 0, r_l, cv_l, x_lat, mods, lat_group, gn_tiled,
                                      w_out_b, norm2_g[layer], rw_hi, rw_lo, rb, n_tok, 0)
        if not last:
            yf_c = _fourier_short(u_c, chan)
            cv_c = _conv_call(glu_c, n_ctx, conv_w[layer], conv_b[layer], conv_ln_g[layer],
                              conv_ln_b[layer])
            x_ctx, routed = _outproj_call(yf_c, o_f, o_b, seq, r_c, cv_c, x_ctx, mods, ctx_group_fn,
                                          gn_tiled, w_out_b, norm2_g[layer], rw_hi, rw_lo, rb, n_tok,
                                          bsz * seq, carried=routed)

        h2, idx_t, prob_t = routed
        runs = _run_tables(idx_t, n_tok)
        xs, pos_t = _sort_dispatch_call(runs, idx_t, h2)
        yb = _expert_call(runs["block_e"], runs["n_used"], xs, layer, exp_w_gu, exp_b_gu, exp_w_dn,
                          exp_b_dn)
        probs, pos_tok = prob_t.T, pos_t.T
        lat_blocks = seq // SORT_TB
        x_lat = _sort_combine_call(runs, pos_tok, probs, yb, x_lat, mods, lambda i: i // lat_blocks,
                                   final_norm_g, 0, last)
        if not last:
            x_ctx = _sort_combine_call(runs, pos_tok, probs, yb, x_ctx, mods, lambda i: ctx_group,
                                       final_norm_g, bsz * seq, False)

    return x_lat
```

```python
import functools

import jax
import jax.numpy as jnp
from jax import lax
from jax.experimental import pallas as pl
from jax.experimental.pallas import tpu as pltpu

F32 = jnp.float32
BF16 = jnp.bfloat16
U32 = jnp.uint32
HIGH_HALF = 0xFFFF0000

D_MODEL = 1024
DEPTH = 2
GRID_W = 64
FOURIER_WIDTH = 256
FOURIER_HEADS = 4
FOURIER_HEAD_DIM = FOURIER_WIDTH // FOURIER_HEADS
GLA_HEADS = 4
GLA_KEY_WIDTH = 256
GLA_VALUE_WIDTH = 512
GLA_DK = GLA_KEY_WIDTH // GLA_HEADS
GLA_DV = GLA_VALUE_WIDTH // GLA_HEADS
GLA_GATE_RANK = 16
GLA_GATE_NORMALIZER = 16.0
CONV_WIDTH = 256
CONV_KERNEL = 31
N_EXPERTS = 32
TOP_K = 4
D_FF = D_MODEL
SWIGLU_LIMIT = 7.0
SWIGLU_ALPHA = 1.702
NORM_EPS = 1e-6

LANES = 128
VMEM_LIMIT = 56 * 1024 * 1024

COL_U = 0
COL_Q = COL_U + FOURIER_WIDTH
COL_K = COL_Q + GLA_KEY_WIDTH
COL_V = COL_K + GLA_KEY_WIDTH
COL_R = COL_V + GLA_VALUE_WIDTH
COL_CA = COL_R + GLA_VALUE_WIDTH
COL_CG = COL_CA + CONV_WIDTH
COL_GL = COL_CG + CONV_WIDTH
GLR_PAD = LANES
IN_PAD = COL_GL + GLR_PAD

GLA_CHUNK = 128
GLA_STEP_CHUNKS = 2
TM_ROWS = 512
TM_EXPERT = 512
EXPERT_CHUNKS = 4
RUN_ALIGN = 8
SORT_TB = 512
SORT_CHUNK = 256
SORT_ROWS = -(-(TOP_K * SORT_TB + (RUN_ALIGN - 1) * N_EXPERTS) // SORT_CHUNK) * SORT_CHUNK
FFT_N2 = 128
FFT_KB = 8
CONV_HALO = 16
CONV_GROUP = 8
CONV_PITCH = 100


def _cparams(*sem):
    return pltpu.CompilerParams(dimension_semantics=sem, vmem_limit_bytes=VMEM_LIMIT)


def _dot(a, b):
    return jnp.dot(a, b, preferred_element_type=F32)


def _dot_nt(a, b):
    return lax.dot_general(a, b, (((1,), (1,)), ((), ())), preferred_element_type=F32)


def _dot_tn(a, b):
    return lax.dot_general(a, b, (((0,), (0,)), ((), ())), preferred_element_type=F32)


def _split(a):
    hi = a.astype(BF16)
    lo = (a - hi.astype(F32)).astype(BF16)
    return hi, lo


def _dot3(a, b):
    ah, al = _split(a)
    bh, bl = _split(b)
    return _dot(ah, bh) + _dot(ah, bl) + _dot(al, bh)


def _sigmoid(x):
    return 1.0 / (1.0 + jnp.exp(-x))


def _pack_bf16_pairs(a, holds_bf16=False):
    h = a.shape[1] // 2
    bits = lax.bitcast_convert_type(a if holds_bf16 else a.astype(BF16).astype(F32), U32)
    return (bits[:, :h] >> 16) | (bits[:, h:] & U32(HIGH_HALF))


def _unpack_bf16_pairs(w):
    return (lax.bitcast_convert_type(w << 16, F32), lax.bitcast_convert_type(w & U32(HIGH_HALF), F32))


def _rms(x, g):
    ms = jnp.mean(x * x, axis=-1, keepdims=True)
    return x * lax.rsqrt(ms + NORM_EPS) * g


def _mod_kernel(cv_ref, w_ref, b_ref, o_ref):
    cv = cv_ref[...]
    a = cv * _sigmoid(cv)
    o_ref[...] = _dot3(a, w_ref[...]) + b_ref[...]


def _mod_call(cvec, w_mod, b_mod):
    depth, d, n = w_mod.shape
    rows = cvec.shape[0]
    tn = 1536
    return pl.pallas_call(
        _mod_kernel,
        grid=(depth, n // tn),
        in_specs=[
            pl.BlockSpec((rows, d), lambda l, j: (0, 0)),
            pl.BlockSpec((None, d, tn), lambda l, j: (l, 0, j)),
            pl.BlockSpec((None, 1, tn), lambda l, j: (l, 0, j)),
        ],
        out_specs=pl.BlockSpec((None, rows, tn), lambda l, j: (l, 0, j)),
        out_shape=jax.ShapeDtypeStruct((depth, rows, n), F32),
        compiler_params=_cparams("arbitrary", "arbitrary"),
    )(cvec, w_mod, b_mod.reshape(depth, 1, n))


def _inproj_kernel(x_ref, g_ref, sh_ref, sc_ref, w_ref, *rest):
    u_ref, r_ref, glu_ref, q_ref, k_ref, v_ref, gl_ref = rest[-7:]
    x = x_ref[...]
    h = _rms(x, g_ref[...]) * (1.0 + sc_ref[...]) + sh_ref[...]
    p = _dot(h.astype(BF16), w_ref[...])
    u_ref[...] = p[:, COL_U:COL_Q].astype(u_ref.dtype)
    q_ref[...] = (p[:, COL_Q:COL_K] * (GLA_DK ** -0.5)).astype(q_ref.dtype)
    k_ref[...] = p[:, COL_K:COL_V].astype(k_ref.dtype)
    v_ref[...] = p[:, COL_V:COL_R].astype(v_ref.dtype)
    r_ref[...] = p[:, COL_R:COL_CA].astype(r_ref.dtype)
    glu_ref[...] = (p[:, COL_CA:COL_CG] * _sigmoid(p[:, COL_CG:COL_GL])).astype(glu_ref.dtype)
    gl_ref[...] = p[:, COL_GL:IN_PAD]


def _inproj_call(x, mods, group_of_batch, norm_g, w_in_p, lt, row0, combined=None):
    bsz, n, d = x.shape
    tm = min(TM_ROWS, n)
    assert n % tm == 0 and row0 % tm == 0
    blk0 = row0 // tm
    nb = n // tm
    steps = nb + (-(-(lt - n) // tm) if combined is None else 0)
    widths = (GLA_KEY_WIDTH, GLA_KEY_WIDTH, GLA_VALUE_WIDTH, GLR_PAD)
    dtypes = (BF16, BF16, BF16, F32)
    row_spec = lambda w: pl.BlockSpec((None, tm, w), lambda b, i: (b, jnp.minimum(i, nb - 1), 0))
    comb_spec = lambda w: pl.BlockSpec((None, tm, w), lambda b, i: (b, blk0 + i, 0))
    mod_spec = lambda which: pl.BlockSpec(
        (None, None, 1, d), lambda b, i: (group_of_batch(b), which, 0, 0))
    in_specs = [
        row_spec(d),
        pl.BlockSpec((1, d), lambda b, i: (0, 0)),
        mod_spec(0), mod_spec(1),
        pl.BlockSpec((d, IN_PAD), lambda b, i: (0, 0)),
    ]
    args = [x, norm_g.reshape(1, d), mods, mods, w_in_p]
    aliases = {}
    if combined is not None:
        for t, arr in enumerate(combined):
            in_specs.append(pl.BlockSpec(memory_space=pl.ANY))
            aliases[len(args)] = 3 + t
            args.append(arr)
    out_shape = [
        jax.ShapeDtypeStruct((bsz, n, FOURIER_WIDTH), BF16),
        jax.ShapeDtypeStruct((bsz, n, GLA_VALUE_WIDTH), BF16),
        jax.ShapeDtypeStruct((bsz, n, CONV_WIDTH), BF16),
    ] + [jax.ShapeDtypeStruct((bsz, lt, w), dt) for w, dt in zip(widths, dtypes)]
    out_specs = [row_spec(FOURIER_WIDTH), row_spec(GLA_VALUE_WIDTH), row_spec(CONV_WIDTH)] + [
        comb_spec(w) for w in widths]
    outs = pl.pallas_call(
        _inproj_kernel,
        grid=(bsz, steps),
        in_specs=in_specs,
        out_specs=out_specs,
        out_shape=out_shape,
        input_output_aliases=aliases,
        compiler_params=_cparams("arbitrary", "arbitrary"),
    )(*args)
    return outs[:3], outs[3:]


def _dft_tables(length):
    n2 = FFT_N2
    n1 = length // n2
    two_pi = 2.0 * jnp.pi

    def cs(num, den):
        ang = (num % den).astype(F32) * (two_pi / den)
        return jnp.cos(ang), jnp.sin(ang)

    k1 = jnp.arange(n1, dtype=jnp.int32)
    c1, s1 = cs(k1[:, None] * k1[None, :], n1)
    stage1 = (jnp.concatenate([c1, -s1], axis=0) * (n1 ** -0.5)).astype(BF16)
    k2 = jnp.arange(n2, dtype=jnp.int32)
    ct, st = cs(k1[:, None] * k2[None, :], length)
    cf, sf = cs(k2[:, None] * k2[None, :], n2)
    scale = n2 ** -0.5
    mr = (ct[:, None, :] * cf[None] - st[:, None, :] * sf[None]) * scale
    mi = -(st[:, None, :] * cf[None] + ct[:, None, :] * sf[None]) * scale
    stage2 = jnp.concatenate([jnp.concatenate([mr, -mi], axis=2),
                              jnp.concatenate([mi, mr], axis=2)], axis=1).astype(BF16)
    return stage1, stage2


def _channel_tables():
    hd = FOURIER_HEAD_DIM
    c = jnp.arange(FOURIER_WIDTH, dtype=jnp.int32)
    same_head = (c[:, None] // hd) == (c[None, :] // hd)
    ang = (((c[:, None] % hd) * (c[None, :] % hd)) % hd).astype(F32) * (2.0 * jnp.pi / hd)
    scale = hd ** -0.5
    bdc = jnp.where(same_head, jnp.cos(ang) * scale, 0.0).astype(BF16)
    bds = jnp.where(same_head, jnp.sin(ang) * scale, 0.0).astype(BF16)
    return bdc, bds


def _fft1_kernel(x_ref, cs_ref, zr_ref, zi_ref):
    n1 = x_ref.shape[0]
    z = _dot(cs_ref[...], x_ref[...])
    zr_ref[...] = z[:n1].astype(zr_ref.dtype)
    zi_ref[...] = z[n1:].astype(zi_ref.dtype)


def _fft2_kernel(zr_ref, zi_ref, m_ref, bdc_ref, bds_ref, o_ref):
    kb, n2, w = zr_ref.shape
    for j in range(kb):
        z = jnp.concatenate([zr_ref[j], zi_ref[j]], axis=0)
        a = _dot(m_ref[j], z)
        y = _dot(a[:n2].astype(BF16), bdc_ref[...]) + _dot(a[n2:].astype(BF16), bds_ref[...])
        o_ref[:, j * w:(j + 1) * w] = y.astype(o_ref.dtype)


def _fourier_long(u, tables, chan):
    bsz, length, w = u.shape
    stage1, stage2 = tables
    bdc, bds = chan
    n2 = FFT_N2
    n1 = length // n2
    tn = 4096
    cols = n2 * w
    zr, zi = pl.pallas_call(
        _fft1_kernel,
        grid=(bsz, cols // tn),
        in_specs=[pl.BlockSpec((None, n1, tn), lambda b, j: (b, 0, j)),
                  pl.BlockSpec((2 * n1, n1), lambda b, j: (0, 0))],
        out_specs=[pl.BlockSpec((None, n1, tn), lambda b, j: (b, 0, j))] * 2,
        out_shape=[jax.ShapeDtypeStruct((bsz, n1, cols), BF16)] * 2,
        compiler_params=_cparams("arbitrary", "arbitrary"),
    )(u.reshape(bsz, n1, cols), stage1)
    kb = FFT_KB
    z_spec = pl.BlockSpec((None, kb, n2, w), lambda b, j: (b, j, 0, 0))
    y = pl.pallas_call(
        _fft2_kernel,
        grid=(bsz, n1 // kb),
        in_specs=[z_spec, z_spec,
                  pl.BlockSpec((kb, 2 * n2, 2 * n2), lambda b, j: (j, 0, 0)),
                  pl.BlockSpec((w, w), lambda b, j: (0, 0)),
                  pl.BlockSpec((w, w), lambda b, j: (0, 0))],
        out_specs=pl.BlockSpec((None, n2, kb * w), lambda b, j: (b, 0, j)),
        out_shape=jax.ShapeDtypeStruct((bsz, n2, n1 * w), BF16),
        compiler_params=_cparams("arbitrary", "arbitrary"),
    )(zr.reshape(bsz, n1, n2, w), zi.reshape(bsz, n1, n2, w), stage2, bdc, bds)
    return y.reshape(bsz, length, w)


def _dft_short_kernel(u_ref, c_ref, s_ref, bdc_ref, bds_ref, o_ref):
    u = u_ref[...]
    p = _dot(u, bdc_ref[...]).astype(BF16)
    q = _dot(u, bds_ref[...]).astype(BF16)
    o_ref[...] = (_dot(c_ref[...], p) - _dot(s_ref[...], q)).astype(o_ref.dtype)


def _fourier_short(u, chan):
    bsz, length, w = u.shape
    bdc, bds = chan
    k = jnp.arange(length, dtype=jnp.int32)
    ang = ((k[:, None] * k[None, :]) % length).astype(F32) * (2.0 * jnp.pi / length)
    c = (jnp.cos(ang) * length ** -0.5).astype(BF16)
    s = (jnp.sin(ang) * length ** -0.5).astype(BF16)
    full = lambda n: pl.BlockSpec((n, n), lambda b: (0, 0))
    return pl.pallas_call(
        _dft_short_kernel,
        grid=(bsz,),
        in_specs=[pl.BlockSpec((None, length, w), lambda b: (b, 0, 0)),
                  full(length), full(length), full(w), full(w)],
        out_specs=pl.BlockSpec((None, length, w), lambda b: (b, 0, 0)),
        out_shape=jax.ShapeDtypeStruct((bsz, length, w), BF16),
        compiler_params=_cparams("arbitrary"),
    )(u, c, s, bdc, bds)


def _conv_kernel(x_ref, w_ref, cb_ref, lg_ref, lb_ref, o_ref, pad_ref, *, seg):
    nseg = x_ref.shape[0] // seg
    width = x_ref.shape[1]
    halo = jnp.zeros((CONV_HALO, width), F32)
    for s in range(nseg):
        pad_ref[s, 0:CONV_HALO, :] = halo
        pad_ref[s, CONV_HALO:CONV_HALO + seg, :] = x_ref[s * seg:(s + 1) * seg, :].astype(F32)
        pad_ref[s, CONV_HALO + seg:2 * CONV_HALO + seg, :] = halo
    first = CONV_HALO - CONV_KERNEL // 2
    sub = 8
    span = seg + 2 * CONV_HALO - sub
    for s in range(nseg):
        acc = jnp.zeros((seg, width), F32)
        for r in range(sub):
            shifted = pad_ref[s, r:r + span, :]
            for a in range((span - seg) // sub + 1):
                j = a * sub + r - first
                if 0 <= j < CONV_KERNEL:
                    acc = acc + shifted[a * sub:a * sub + seg, :] * w_ref[j:j + 1, :]
        y = acc + cb_ref[...]
        mu = jnp.mean(y, axis=-1, keepdims=True)
        yc = y - mu
        var = jnp.mean(yc * yc, axis=-1, keepdims=True)
        z = yc * lax.rsqrt(var + NORM_EPS) * lg_ref[...] + lb_ref[...]
        o_ref[s * seg:(s + 1) * seg, :] = (z * _sigmoid(z)).astype(o_ref.dtype)


def _conv_group_kernel(x_ref, w_ref, cb_ref, lg_ref, lb_ref, o_ref, pad_ref, acc_ref, *, seg):
    g, p = CONV_GROUP, CONV_PITCH
    halves = x_ref.shape[1] // LANES
    pad_ref[...] = jnp.zeros_like(pad_ref)
    for s in range(g):
        for h in range(halves):
            pad_ref[h, s * p + CONV_HALO:s * p + CONV_HALO + seg, :] = (
                x_ref[s * seg:(s + 1) * seg, h * LANES:(h + 1) * LANES].astype(F32))
    first = CONV_HALO - CONV_KERNEL // 2
    unroll = 4

    def body(i, carry):
        t0 = unroll * i
        for h in range(halves):
            accs = [None] * unroll
            for j in range(CONV_KERNEL):
                tap = w_ref[h, j]
                for u in range(unroll):
                    term = pad_ref[h, pl.ds(t0 + u + first + j, g, stride=p), :] * tap
                    accs[u] = term if accs[u] is None else accs[u] + term
            for u in range(unroll):
                acc_ref[h, pl.ds(t0 + u, g, stride=p), :] = accs[u]
        return carry

    lax.fori_loop(0, seg // unroll, body, 0)
    for s in range(g):
        y = jnp.concatenate([acc_ref[h, s * p:s * p + seg, :] for h in range(halves)], axis=1) + cb_ref[...]
        mu = jnp.mean(y, axis=-1, keepdims=True)
        yc = y - mu
        var = jnp.mean(yc * yc, axis=-1, keepdims=True)
        z = yc * lax.rsqrt(var + NORM_EPS) * lg_ref[...] + lb_ref[...]
        o_ref[s * seg:(s + 1) * seg, :] = (z * _sigmoid(z)).astype(o_ref.dtype)


def _conv_call(glu, seg, conv_w, conv_b, ln_g, ln_b):
    bsz, n, w = glu.shape
    t = max(seg, min(TM_ROWS, n))
    assert n % t == 0 and t % seg == 0
    vec = lambda: pl.BlockSpec((1, w), lambda b, i: (0, 0))
    if t // seg == CONV_GROUP and seg + 2 * CONV_HALO <= CONV_PITCH and w % LANES == 0:
        halves = w // LANES
        taps = jnp.broadcast_to(conv_w.reshape(CONV_KERNEL, halves, 1, LANES).transpose(1, 0, 2, 3),
                                (halves, CONV_KERNEL, 8, LANES))
        slab = pltpu.VMEM((halves, CONV_GROUP * CONV_PITCH, LANES), F32)
        return pl.pallas_call(
            functools.partial(_conv_group_kernel, seg=seg),
            grid=(bsz, n // t),
            in_specs=[pl.BlockSpec((None, t, w), lambda b, i: (b, i, 0)),
                      pl.BlockSpec(taps.shape, lambda b, i: (0, 0, 0, 0)),
                      vec(), vec(), vec()],
            out_specs=pl.BlockSpec((None, t, w), lambda b, i: (b, i, 0)),
            out_shape=jax.ShapeDtypeStruct((bsz, n, w), BF16),
            scratch_shapes=[slab, slab],
            compiler_params=_cparams("arbitrary", "arbitrary"),
        )(glu, taps, conv_b.reshape(1, w), ln_g.reshape(1, w), ln_b.reshape(1, w))
    return pl.pallas_call(
        functools.partial(_conv_kernel, seg=seg),
        grid=(bsz, n // t),
        in_specs=[pl.BlockSpec((None, t, w), lambda b, i: (b, i, 0)),
                  pl.BlockSpec((CONV_KERNEL, w), lambda b, i: (0, 0)),
                  vec(), vec(), vec()],
        out_specs=pl.BlockSpec((None, t, w), lambda b, i: (b, i, 0)),
        out_shape=jax.ShapeDtypeStruct((bsz, n, w), BF16),
        scratch_shapes=[pltpu.VMEM((t // seg, seg + 2 * CONV_HALO, w), F32)],
        compiler_params=_cparams("arbitrary", "arbitrary"),
    )(glu, conv_w, conv_b.reshape(1, w), ln_g.reshape(1, w), ln_b.reshape(1, w))


def _gla_direction(q_ref, k_ref, v_ref, gl_ref, wg_ref, bg_ref, o_ref, st, reverse):
    c = GLA_CHUNK
    n_sub = q_ref.shape[0] // c
    for s in (reversed(range(n_sub)) if reverse else range(n_sub)):
        rows = slice(s * c, (s + 1) * c)
        st = _gla_chunk(q_ref[rows, :], k_ref[rows, :], v_ref[rows, :], gl_ref[rows, :], wg_ref, bg_ref,
                        o_ref.at[rows, :], st, reverse)
    return st


def _gla_chunk(q, k, v, glr, wg_ref, bg_ref, o_ref, st, reverse):
    c = GLA_CHUNK
    kw = GLA_KEY_WIDTH
    col0 = kw if reverse else 0
    pre = _dot3(glr, wg_ref[:, col0:col0 + kw]) + bg_ref[:, col0:col0 + kw]
    g = (jnp.minimum(pre, 0.0) - jnp.log(1.0 + jnp.exp(-jnp.abs(pre)))) * (1.0 / GLA_GATE_NORMALIZER)
    row = lax.broadcasted_iota(jnp.int32, (c, c), 0)
    col = lax.broadcasted_iota(jnp.int32, (c, c), 1)
    seen = (col >= row) if reverse else (col <= row)
    tri = jnp.where(seen, 1.0, 0.0).astype(BF16)
    gh, gl = _split(g)
    b = _dot(tri, gh) + _dot(tri, gl)
    mid = c // 2 if reverse else c // 2 - 1
    last = 0 if reverse else c - 1
    b_mid = b[mid:mid + 1, :]
    b_last = b[last:last + 1, :]
    q = q.astype(F32)
    k = k.astype(F32)
    qe = q * jnp.exp(b - b_mid)
    ke = k * jnp.exp(b_mid - b)
    kd = k * jnp.exp(b_last - b)
    head_of_lane = lax.broadcasted_iota(jnp.int32, (1, kw), 1) // GLA_DK
    q_heads = jnp.concatenate(
        [jnp.where(head_of_lane == h, qe, 0.0) for h in range(GLA_HEADS)], axis=0).astype(BF16)
    rhs = jnp.concatenate([ke, st * jnp.exp(b_mid)], axis=0).astype(BF16)
    res = _dot_nt(q_heads, rhs)
    outs = []
    for h in range(GLA_HEADS):
        blk = res[h * c:(h + 1) * c, :]
        scores = jnp.where(seen, blk[:, :c], 0.0).astype(BF16)
        outs.append(_dot(scores, v[:, h * GLA_DV:(h + 1) * GLA_DV]) + blk[:, c:])
    o_ref[...] = jnp.concatenate(outs, axis=1).astype(o_ref.dtype)
    kv = _dot_tn(v, kd.astype(BF16))
    ds = jnp.zeros_like(st)
    for h in range(GLA_HEADS):
        ds = ds + jnp.where(head_of_lane == h, kv[h * GLA_DV:(h + 1) * GLA_DV, :], 0.0)
    return st * jnp.exp(b_last) + ds


def _gla_kernel(qf, kf, vf, gf, qb, kb, vb, gb, wg_ref, bg_ref, of_ref, ob_ref, sf_ref, sb_ref):
    @pl.when(pl.program_id(0) == 0)
    def _():
        sf_ref[...] = jnp.zeros_like(sf_ref)
        sb_ref[...] = jnp.zeros_like(sb_ref)

    bsz = qf.shape[0]
    states = [(sf_ref[b], sb_ref[b]) for b in range(bsz)]
    new = []
    for b, (s_f, s_b) in enumerate(states):
        new.append((
            _gla_direction(qf.at[b], kf.at[b], vf.at[b], gf.at[b], wg_ref, bg_ref, of_ref.at[b], s_f, False),
            _gla_direction(qb.at[b], kb.at[b], vb.at[b], gb.at[b], wg_ref, bg_ref, ob_ref.at[b], s_b, True)))
    for b, (s_f, s_b) in enumerate(new):
        sf_ref[b] = s_f
        sb_ref[b] = s_b


def _gla_call(q, k, v, gl, wg_pad, bg_cat, n_lat, n_ctx):
    bsz, lt, _ = q.shape
    c = GLA_STEP_CHUNKS * GLA_CHUNK
    assert n_lat % c == 0 and n_ctx % c == 0
    cl, cc = n_lat // c, n_ctx // c

    def fwd_blk(j):
        return jnp.where(j < cc, cl + j, j - cc)

    def bwd_blk(j):
        return jnp.where(j < cc, cl + cc - 1 - j, cl - 1 - (j - cc))

    def spec(w, blk):
        return pl.BlockSpec((bsz, c, w), lambda j: (0, blk(j), 0))

    widths = (GLA_KEY_WIDTH, GLA_KEY_WIDTH, GLA_VALUE_WIDTH, GLR_PAD)
    in_specs = [spec(w, fwd_blk) for w in widths] + [spec(w, bwd_blk) for w in widths] + [
        pl.BlockSpec(wg_pad.shape, lambda j: (0, 0)),
        pl.BlockSpec(bg_cat.shape, lambda j: (0, 0))]
    return pl.pallas_call(
        _gla_kernel,
        grid=(cl + cc,),
        in_specs=in_specs,
        out_specs=[spec(GLA_VALUE_WIDTH, fwd_blk), spec(GLA_VALUE_WIDTH, bwd_blk)],
        out_shape=[jax.ShapeDtypeStruct((bsz, lt, GLA_VALUE_WIDTH), F32)] * 2,
        scratch_shapes=[pltpu.VMEM((bsz, GLA_DV, GLA_KEY_WIDTH), F32)] * 2,
        compiler_params=_cparams("arbitrary"),
    )(q, k, v, gl, q, k, v, gl, wg_pad, bg_cat)


def _outproj_kernel(yf_ref, of_ref, ob_ref, r_ref, cv_ref, x_ref, gate_ref, sh_ref, sc_ref,
                    gn_ref, wo_ref, n2_ref, rwh_ref, rwl_ref, rb_ref, *rest):
    xo_ref, h2_ref, idx_ref, prob_ref = rest[-4:]
    o = of_ref[...] + ob_ref[...]
    heads = []
    for h in range(GLA_HEADS):
        oh = o[:, h * GLA_DV:(h + 1) * GLA_DV]
        heads.append(oh * lax.rsqrt(jnp.mean(oh * oh, axis=-1, keepdims=True) + NORM_EPS))
    r = r_ref[...].astype(F32)
    gla = jnp.concatenate(heads, axis=1) * gn_ref[...] * (r * _sigmoid(r))
    c0, c1 = FOURIER_WIDTH, FOURIER_WIDTH + GLA_VALUE_WIDTH
    y = (_dot(yf_ref[...], wo_ref[0:c0, :]) + _dot(gla.astype(BF16), wo_ref[c0:c1, :])
         + _dot(cv_ref[...], wo_ref[c1:, :]))
    xn = x_ref[...] + gate_ref[...] * y
    xo_ref[...] = xn
    h2 = _rms(xn, n2_ref[...]) * (1.0 + sc_ref[...]) + sh_ref[...]
    hh, hl = _split(h2)
    h2_ref[...] = _pack_bf16_pairs(h2)
    logits = (_dot_nt(rwh_ref[...], hh) + _dot_nt(rwh_ref[...], hl) + _dot_nt(rwl_ref[...], hh)
              + rb_ref[...])
    expert = lax.broadcasted_iota(jnp.int32, logits.shape, 0)
    vals, idxs = [], []
    cur = logits
    for _ in range(TOP_K):
        m = jnp.max(cur, axis=0, keepdims=True)
        ix = jnp.min(jnp.where(cur == m, expert, N_EXPERTS), axis=0, keepdims=True)
        vals.append(m)
        idxs.append(ix)
        cur = jnp.where(expert == ix, -jnp.inf, cur)
    es = [jnp.exp(vv - vals[0]) for vv in vals]
    inv = 1.0 / functools.reduce(lambda a, b: a + b, es)
    idx_ref[...] = jnp.concatenate(idxs, axis=0)
    prob_ref[...] = jnp.concatenate([e * inv for e in es], axis=0)


def _outproj_call(yf, o_f, o_b, o_row0, r, cv, x, mods, group_of_batch, gn_tiled, w_out, norm2_g,
                  rw_hi, rw_lo, rb, n_tok, tok0, carried=None):
    bsz, n, d = x.shape
    tm = min(TM_ROWS, n)
    assert n % tm == 0 and o_row0 % tm == 0 and tok0 % tm == 0
    nb = n // tm
    spare = (n_tok - bsz * n) if carried is None else 0
    assert spare in (0, tm)
    steps = nb + spare // tm
    last = lambda i: jnp.minimum(i, nb - 1)
    row = lambda w: pl.BlockSpec((None, tm, w), lambda b, i: (b, last(i), 0))
    orow = pl.BlockSpec((None, tm, GLA_VALUE_WIDTH), lambda b, i: (b, o_row0 // tm + last(i), 0))
    mod = lambda which: pl.BlockSpec((None, None, 1, d), lambda b, i: (group_of_batch(b), which, 0, 0))
    const = lambda a: pl.BlockSpec(a.shape, lambda b, i: (0,) * a.ndim)
    consts = [gn_tiled, w_out, norm2_g.reshape(1, d), rw_hi, rw_lo, rb]
    in_specs = [row(FOURIER_WIDTH), orow, orow, row(GLA_VALUE_WIDTH), row(CONV_WIDTH), row(d),
                mod(2), mod(3), mod(4)] + [const(a) for a in consts]
    args = [yf, o_f, o_b, r, cv, x, mods, mods, mods] + consts
    aliases = {}
    if carried is not None:
        for t, arr in enumerate(carried):
            in_specs.append(pl.BlockSpec(memory_space=pl.ANY))
            aliases[len(args)] = 1 + t
            args.append(arr)
    tokblk = lambda b, i: tok0 // tm + jnp.where(
        jnp.logical_and(i == nb, b == bsz - 1), bsz * nb, b * nb + last(i))
    out_specs = [row(d),
                 pl.BlockSpec((tm, d // 2), lambda b, i: (tokblk(b, i), 0)),
                 pl.BlockSpec((TOP_K, tm), lambda b, i: (0, tokblk(b, i))),
                 pl.BlockSpec((TOP_K, tm), lambda b, i: (0, tokblk(b, i)))]
    out_shape = [jax.ShapeDtypeStruct((bsz, n, d), F32),
                 jax.ShapeDtypeStruct((n_tok, d // 2), U32),
                 jax.ShapeDtypeStruct((TOP_K, n_tok), jnp.int32),
                 jax.ShapeDtypeStruct((TOP_K, n_tok), F32)]
    outs = pl.pallas_call(
        _outproj_kernel,
        grid=(bsz, steps),
        in_specs=in_specs,
        out_specs=out_specs,
        out_shape=out_shape,
        input_output_aliases=aliases,
        compiler_params=_cparams("arbitrary", "arbitrary"),
    )(*args)
    return outs[0], outs[1:]


def _expert_kernel(be_ref, nu_ref, x_ref, wgu_ref, bgu_ref, wdn_ref, bdn_ref, o_ref, wgu_s, wdn_s):
    i = pl.program_id(0)
    nu = nu_ref[0]
    n_chunks = wgu_s.shape[0]
    fc = D_FF // n_chunks

    @pl.when(i >= nu)
    def _():
        o_ref[...] = jnp.zeros_like(o_ref)

    @pl.when(i < nu)
    def _():
        changed = jnp.logical_or(i == 0, be_ref[i] != be_ref[jnp.maximum(i - 1, 0)])

        @pl.when(changed)
        def _():
            rows = 128
            for s in range(wgu_ref.shape[0] // rows):
                rs = slice(s * rows, (s + 1) * rows)
                for n in range(n_chunks):
                    wgu_s[n, rs, 0:fc] = wgu_ref[rs, n * fc:(n + 1) * fc].astype(BF16)
                    wgu_s[n, rs, fc:2 * fc] = wgu_ref[rs, D_FF + n * fc:D_FF + (n + 1) * fc].astype(BF16)
                wdn_s[rs, :] = wdn_ref[rs, :].astype(BF16)

        x = jnp.concatenate(_unpack_bf16_pairs(x_ref[...]), axis=1).astype(BF16)
        acts = []
        for n in range(n_chunks):
            gate = _dot(x, wgu_s[n, :, 0:fc]) + bgu_ref[:, n * fc:(n + 1) * fc]
            up = _dot(x, wgu_s[n, :, fc:2 * fc]) + bgu_ref[:, D_FF + n * fc:D_FF + (n + 1) * fc]
            gate = jnp.minimum(gate, SWIGLU_LIMIT)
            up = jnp.clip(up, -SWIGLU_LIMIT, SWIGLU_LIMIT)
            acts.append((gate * _sigmoid(SWIGLU_ALPHA * gate) * (up + 1.0)).astype(BF16))
        half = n_chunks // 2
        y = (_dot(jnp.concatenate(acts[:half], axis=1), wdn_s[0:half * fc, :])
             + _dot(jnp.concatenate(acts[half:], axis=1), wdn_s[half * fc:, :]) + bdn_ref[...])
        o_ref[...] = _pack_bf16_pairs(y)


def _expert_call(block_e, n_used, xs, layer, w_gu, b_gu, w_dn, b_dn):
    rows, dh = xs.shape
    d = 2 * dh
    tm = TM_EXPERT
    nblk = rows // tm
    depth, e, _, f2 = w_gu.shape
    n_chunks = EXPERT_CHUNKS
    live = lambda i, nu: jnp.minimum(i, nu[0] - 1)
    wmap = lambda i, be, nu: (layer, be[live(i, nu)], 0, 0)
    grid_spec = pltpu.PrefetchScalarGridSpec(
        num_scalar_prefetch=2,
        grid=(nblk,),
        in_specs=[
            pl.BlockSpec((tm, dh), lambda i, be, nu: (live(i, nu), 0)),
            pl.BlockSpec((None, None, d, f2), wmap),
            pl.BlockSpec((None, None, 1, f2), wmap),
            pl.BlockSpec((None, None, f2 // 2, d), wmap),
            pl.BlockSpec((None, None, 1, d), wmap),
        ],
        out_specs=pl.BlockSpec((tm, dh), lambda i, be, nu: (i, 0)),
        scratch_shapes=[pltpu.VMEM((n_chunks, d, f2 // n_chunks), BF16), pltpu.VMEM((f2 // 2, d), BF16)],
    )
    return pl.pallas_call(
        _expert_kernel,
        grid_spec=grid_spec,
        out_shape=jax.ShapeDtypeStruct((rows, dh), xs.dtype),
        compiler_params=_cparams("arbitrary"),
    )(block_e, n_used, xs, w_gu, b_gu.reshape(depth, e, 1, f2), w_dn, b_dn.reshape(depth, e, 1, d))


def _count_kernel(idx_ref, cnt_ref):
    per_step = cnt_ref.shape[0]
    tb = idx_ref.shape[1] // per_step
    expert = lax.broadcasted_iota(jnp.int32, (N_EXPERTS, tb), 0)
    for j in range(per_step):
        total = jnp.zeros((N_EXPERTS, 1), F32)
        for kk in range(TOP_K):
            hit = expert == idx_ref[kk:kk + 1, j * tb:(j + 1) * tb]
            total = total + jnp.sum(hit.astype(F32), axis=1, keepdims=True)
        cnt_ref[j] = total.astype(jnp.int32)


def _run_tables(idx_t, n_tok):
    tm, tb = TM_EXPERT, SORT_TB
    nblk = n_tok // tb
    assert n_tok % tb == 0
    per_step = max(g for g in range(1, 9) if nblk % g == 0)
    cnt = pl.pallas_call(
        _count_kernel,
        grid=(nblk // per_step,),
        in_specs=[pl.BlockSpec((TOP_K, per_step * tb), lambda i: (0, i))],
        out_specs=pl.BlockSpec((per_step, N_EXPERTS, 1), lambda i: (i, 0, 0)),
        out_shape=jax.ShapeDtypeStruct((nblk, N_EXPERTS, 1), jnp.int32),
        compiler_params=_cparams("arbitrary"),
    )(idx_t)[:, :, 0]
    run = (cnt + RUN_ALIGN - 1) // RUN_ALIGN * RUN_ALIGN
    total = jnp.sum(run, axis=0)
    padded = (total + tm - 1) // tm * tm
    pad_end = jnp.cumsum(padded)
    dst = (pad_end - padded)[None, :] + jnp.cumsum(run, axis=0) - run
    boff = jnp.cumsum(run, axis=1) - run
    n_blocks = -(-(TOP_K * n_tok + (RUN_ALIGN - 1) * N_EXPERTS * nblk) // tm) + N_EXPERTS
    first_row = jnp.arange(n_blocks, dtype=jnp.int32) * tm
    block_e = jnp.minimum(jnp.sum(pad_end[None, :] <= first_row[:, None], axis=1),
                          N_EXPERTS - 1).astype(jnp.int32)
    n_used = (pad_end[-1:] // tm).astype(jnp.int32)
    flat = lambda a: a.reshape(-1).astype(jnp.int32)
    return dict(run=flat(run), dst=flat(dst), boff=flat(boff), boff_f=boff.astype(F32),
                pad_end=pad_end.astype(jnp.int32), block_e=block_e, n_used=n_used, n_blocks=n_blocks)


def _run_copies(blk, run_ref, grp_ref, boff_ref, grouped, sbuf, sem, to_grouped, wait):
    if wait:
        last = blk * N_EXPERTS + N_EXPERTS - 1
        rows = pl.multiple_of(boff_ref[last] + run_ref[last], RUN_ALIGN)

        @pl.when(rows > 0)
        def _():
            s, g = sbuf.at[pl.ds(0, rows), :], grouped.at[pl.ds(0, rows), :]
            (pltpu.make_async_copy(s, g, sem) if to_grouped else pltpu.make_async_copy(g, s, sem)).wait()
        return
    for e in range(N_EXPERTS):
        n = pl.multiple_of(run_ref[blk * N_EXPERTS + e], RUN_ALIGN)
        g0 = pl.multiple_of(grp_ref[blk * N_EXPERTS + e], RUN_ALIGN)
        s0 = pl.multiple_of(boff_ref[blk * N_EXPERTS + e], RUN_ALIGN)

        @pl.when(n > 0)
        def _(n=n, g0=g0, s0=s0):
            g, s = grouped.at[pl.ds(g0, n), :], sbuf.at[pl.ds(s0, n), :]
            (pltpu.make_async_copy(s, g, sem) if to_grouped else pltpu.make_async_copy(g, s, sem)).start()


def _sort_dispatch_kernel(run_ref, dst_ref, boffs_ref, pe_ref, nu_ref, idx_ref, boff_ref, h_ref, xs_ref,
                          pos_ref, sbuf, zero_ref, sem, zsem, *, n_blocks):
    i = pl.program_id(0)
    tb = h_ref.shape[0]
    tm = zero_ref.shape[0]

    def zero_copy(blk):
        return pltpu.make_async_copy(zero_ref, xs_ref.at[pl.ds(pl.multiple_of(blk * tm, tm), tm), :], zsem)

    @pl.when(i == 0)
    def _():
        zero_ref[...] = jnp.zeros_like(zero_ref)

        def per_expert(fn):
            for e in range(N_EXPERTS):
                end = pe_ref[e]
                start = pe_ref[e - 1] if e else 0

                @pl.when(end > start)
                def _():
                    fn(end // tm - 1)

        def per_tail(fn):
            def body(blk, carry):
                fn(blk)
                return carry
            lax.fori_loop(nu_ref[0], n_blocks, body, 0)

        per_expert(lambda blk: zero_copy(blk).start())
        per_tail(lambda blk: zero_copy(blk).start())
        per_expert(lambda blk: zero_copy(blk).wait())
        per_tail(lambda blk: zero_copy(blk).wait())

    row = lax.broadcasted_iota(jnp.int32, (tb, tb), 0)
    col = lax.broadcasted_iota(jnp.int32, (tb, tb), 1)
    earlier = (row < col).astype(F32).astype(BF16)
    expert = lax.broadcasted_iota(jnp.int32, (N_EXPERTS, tb), 0)
    base = boff_ref[...]
    pos = []
    for kk in range(TOP_K):
        onehot = (expert == idx_ref[kk:kk + 1, :]).astype(F32)
        before = _dot(onehot.astype(BF16), earlier) + base
        pos.append(jnp.sum(before * onehot, axis=0, keepdims=True).astype(jnp.int32))
        base = base + jnp.sum(onehot, axis=1, keepdims=True)
    pos_ref[...] = jnp.concatenate(pos, axis=0)

    lo, hi = _unpack_bf16_pairs(h_ref[...])
    h = jnp.concatenate([lo, hi], axis=1).astype(BF16)
    rc = SORT_CHUNK
    cur = sbuf.at[i % 2]
    def selector(c):
        r = lax.broadcasted_iota(jnp.int32, (rc, tb), 0) + c * rc
        sel = jnp.zeros((rc, tb), F32)
        for kk in range(TOP_K):
            sel = jnp.where(r == pos[kk], 1.0, sel).astype(F32)
        return sel.astype(BF16)

    n_chunks = sbuf.shape[1] // rc
    sel = selector(0)
    for c in range(n_chunks):
        nxt = selector(c + 1) if c + 1 < n_chunks else None
        cur[c * rc:(c + 1) * rc, :] = _pack_bf16_pairs(_dot(sel, h), holds_bf16=True)
        sel = nxt
    _run_copies(i, run_ref, dst_ref, boffs_ref, xs_ref, cur, sem.at[i % 2], True, False)

    @pl.when(i > 0)
    def _():
        _run_copies(i - 1, run_ref, dst_ref, boffs_ref, xs_ref, sbuf.at[(i - 1) % 2], sem.at[(i - 1) % 2],
                    True, True)

    @pl.when(i == pl.num_programs(0) - 1)
    def _():
        _run_copies(i, run_ref, dst_ref, boffs_ref, xs_ref, cur, sem.at[i % 2], True, True)


def _sort_dispatch_call(t, idx_t, h2):
    n_tok, dh = h2.shape
    tb, tm = SORT_TB, TM_EXPERT
    nblk = n_tok // tb
    grid_spec = pltpu.PrefetchScalarGridSpec(
        num_scalar_prefetch=5,
        grid=(nblk,),
        in_specs=[pl.BlockSpec((TOP_K, tb), lambda i, *_: (0, i)),
                  pl.BlockSpec((None, N_EXPERTS, 1), lambda i, *_: (i, 0, 0)),
                  pl.BlockSpec((tb, dh), lambda i, *_: (i, 0))],
        out_specs=[pl.BlockSpec(memory_space=pl.ANY), pl.BlockSpec((TOP_K, tb), lambda i, *_: (0, i))],
        scratch_shapes=[pltpu.VMEM((2, SORT_ROWS, dh), h2.dtype), pltpu.VMEM((tm, dh), h2.dtype),
                        pltpu.SemaphoreType.DMA((2,)), pltpu.SemaphoreType.DMA],
    )
    return pl.pallas_call(
        functools.partial(_sort_dispatch_kernel, n_blocks=t["n_blocks"]),
        grid_spec=grid_spec,
        out_shape=[jax.ShapeDtypeStruct((t["n_blocks"] * tm, dh), h2.dtype),
                   jax.ShapeDtypeStruct((TOP_K, n_tok), jnp.int32)],
        compiler_params=_cparams("arbitrary"),
    )(t["run"], t["dst"], t["boff"], t["pad_end"], t["n_used"], idx_t, t["boff_f"][:, :, None], h2)


def _sort_combine_kernel(run_ref, dst_ref, boffs_ref, pos_ref, p_ref, yb_hbm, x_ref, gate_ref, fg_ref, o_ref,
                         sbuf, sem, *, final_norm, blk0):
    i = pl.program_id(0)
    tb = x_ref.shape[0]

    def fetch(blk, wait):
        _run_copies(blk0 + blk, run_ref, dst_ref, boffs_ref, yb_hbm, sbuf.at[blk % 2], sem.at[blk % 2],
                    False, wait)

    @pl.when(i == 0)
    def _():
        sbuf[...] = jnp.zeros_like(sbuf)
        fetch(0, False)

    @pl.when(i + 1 < pl.num_programs(0))
    def _():
        fetch(i + 1, False)

    fetch(i, True)
    cur = sbuf.at[i % 2]

    pos = [pos_ref[:, kk:kk + 1] for kk in range(TOP_K)]
    rc = SORT_CHUNK
    f = jnp.zeros(x_ref.shape, F32)
    for c in range(sbuf.shape[1] // rc):
        r = lax.broadcasted_iota(jnp.int32, (tb, rc), 1) + c * rc
        wgt = jnp.zeros((tb, rc), F32)
        for kk in range(TOP_K):
            wgt = jnp.where(r == pos[kk], p_ref[:, kk:kk + 1], wgt)
        lo, hi = _unpack_bf16_pairs(cur[c * rc:(c + 1) * rc, :])
        y = jnp.concatenate([lo, hi], axis=1).astype(BF16)
        f = f + _dot(wgt.astype(BF16), y)
    xn = x_ref[...] + gate_ref[...] * f
    if final_norm:
        xn = _rms(xn, fg_ref[...])
    o_ref[...] = xn


def _sort_combine_call(t, pos_tok, probs, yb, x, mods, group_of_block, final_g, tok0, final_norm):
    bsz, n, d = x.shape
    tb = SORT_TB
    rows = bsz * n
    assert rows % tb == 0 and tok0 % tb == 0 and (n % tb == 0 or tb % n == 0)
    blk0 = tok0 // tb
    grid_spec = pltpu.PrefetchScalarGridSpec(
        num_scalar_prefetch=3,
        grid=(rows // tb,),
        in_specs=[pl.BlockSpec((tb, TOP_K), lambda i, *_: (blk0 + i, 0)),
                  pl.BlockSpec((tb, TOP_K), lambda i, *_: (blk0 + i, 0)),
                  pl.BlockSpec(memory_space=pl.ANY),
                  pl.BlockSpec((tb, d), lambda i, *_: (i, 0)),
                  pl.BlockSpec((None, None, 1, d), lambda i, *_: (group_of_block(i), 5, 0, 0)),
                  pl.BlockSpec((1, d), lambda i, *_: (0, 0))],
        out_specs=pl.BlockSpec((tb, d), lambda i, *_: (i, 0)),
        scratch_shapes=[pltpu.VMEM((2, SORT_ROWS, yb.shape[1]), yb.dtype), pltpu.SemaphoreType.DMA((2,))],
    )
    out = pl.pallas_call(
        functools.partial(_sort_combine_kernel, final_norm=final_norm, blk0=blk0),
        grid_spec=grid_spec,
        out_shape=jax.ShapeDtypeStruct((rows, d), F32),
        compiler_params=_cparams("arbitrary"),
    )(t["run"], t["dst"], t["boff"], pos_tok, probs, yb, x.reshape(rows, d), mods, final_g.reshape(1, d))
    return out.reshape(bsz, n, d)


def kernel(x, c, ctx, c_ctx, norm1_g, norm2_g, w_mod, b_mod, w_in, gla_wg2_f, gla_bg_f, gla_wg2_b,
           gla_bg_b, gla_norm_g, conv_w, conv_b, conv_ln_g, conv_ln_b, w_out, router_w, router_b,
           exp_w_gu, exp_b_gu, exp_w_dn, exp_b_dn, final_norm_g):
    bsz, seq, d = x.shape
    n_ctx = ctx.shape[1]
    depth = w_mod.shape[0]
    assert d == D_MODEL and seq % (FFT_N2 * FFT_KB) == 0 and seq % n_ctx == 0
    lt = seq + n_ctx
    ctx_group = bsz

    rows = 8
    cvec = jnp.concatenate([c, c_ctx[None, :], jnp.zeros((rows - bsz - 1, d), F32)], axis=0)
    mods_all = _mod_call(cvec, w_mod, b_mod).reshape(depth, rows, 6, 1, d)

    def pack_w_in(w):
        o = [0, 256, 512, 768, 1280, 1792, 1808, 1824, 2336]
        parts = [w[:, o[0]:o[1]], w[:, o[1]:o[2]], w[:, o[2]:o[3]], w[:, o[3]:o[4]], w[:, o[4]:o[5]],
                 w[:, o[7]:o[7] + CONV_WIDTH], w[:, o[7] + CONV_WIDTH:o[8]], w[:, o[5]:o[7]],
                 jnp.zeros((d, GLR_PAD - 2 * GLA_GATE_RANK), w.dtype)]
        return jnp.concatenate(parts, axis=1).astype(BF16)

    long_tables = _dft_tables(seq)
    chan = _channel_tables()
    x_lat, x_ctx = x, ctx
    lat_group = lambda b: b
    ctx_group_fn = lambda b: ctx_group

    for layer in range(depth):
        last = layer == depth - 1
        mods = mods_all[layer]
        w_in_p = pack_w_in(w_in[layer])
        wg_pad = jnp.zeros((GLR_PAD, 2 * GLA_KEY_WIDTH), F32)
        wg_pad = wg_pad.at[:GLA_GATE_RANK, :GLA_KEY_WIDTH].set(gla_wg2_f[layer])
        wg_pad = wg_pad.at[GLA_GATE_RANK:2 * GLA_GATE_RANK, GLA_KEY_WIDTH:].set(gla_wg2_b[layer])
        bg_cat = jnp.concatenate([gla_bg_f[layer], gla_bg_b[layer]])[None, :]
        w_out_b = w_out[layer].astype(BF16)
        gn_tiled = jnp.tile(gla_norm_g[layer], GLA_HEADS)[None, :]
        rw_t = router_w[layer].T
        rw_hi = rw_t.astype(BF16)
        rw_lo = (rw_t - rw_hi.astype(F32)).astype(BF16)
        rb = router_b[layer][:, None]

        (u_l, r_l, glu_l), comb = _inproj_call(x_lat, mods, lat_group, norm1_g[layer], w_in_p, lt, 0)
        (u_c, r_c, glu_c), comb = _inproj_call(x_ctx, mods, ctx_group_fn, norm1_g[layer], w_in_p, lt,
                                               seq, combined=comb)
        o_f, o_b = _gla_call(*comb, wg_pad, bg_cat, seq, n_ctx)
        yf_l = _fourier_long(u_l, long_tables, chan)
        cv_l = _conv_call(glu_l, seq // (seq // GRID_W), conv_w[layer], conv_b[layer],
                          conv_ln_g[layer], conv_ln_b[layer])
        n_tok = bsz * seq + (0 if last else bsz * n_ctx)
        x_lat, routed = _outproj_call(yf_l, o_f, o_b, 0, r_l, cv_l, x_lat, mods, lat_group, gn_tiled,
                                      w_out_b, norm2_g[layer], rw_hi, rw_lo, rb, n_tok, 0)
        if not last:
            yf_c = _fourier_short(u_c, chan)
            cv_c = _conv_call(glu_c, n_ctx, conv_w[layer], conv_b[layer], conv_ln_g[layer],
                              conv_ln_b[layer])
            x_ctx, routed = _outproj_call(yf_c, o_f, o_b, seq, r_c, cv_c, x_ctx, mods, ctx_group_fn,
                                          gn_tiled, w_out_b, norm2_g[layer], rw_hi, rw_lo, rb, n_tok,
                                          bsz * seq, carried=routed)

        h2, idx_t, prob_t = routed
        runs = _run_tables(idx_t, n_tok)
        xs, pos_t = _sort_dispatch_call(runs, idx_t, h2)
        yb = _expert_call(runs["block_e"], runs["n_used"], xs, layer, exp_w_gu, exp_b_gu, exp_w_dn,
                          exp_b_dn)
        probs, pos_tok = prob_t.T, pos_t.T
        lat_blocks = seq // SORT_TB
        x_lat = _sort_combine_call(runs, pos_tok, probs, yb, x_lat, mods, lambda i: i // lat_blocks,
                                   final_norm_g, 0, last)
        if not last:
            x_ctx = _sort_combine_call(runs, pos_tok, probs, yb, x_ctx, mods, lambda i: ctx_group,
                                       final_norm_g, bsz * seq, False)

    return x_lat
```

```python
import functools

import jax
import jax.numpy as jnp
from jax import lax
from jax.experimental import pallas as pl
from jax.experimental.pallas import tpu as pltpu

F32 = jnp.float32
BF16 = jnp.bfloat16
U32 = jnp.uint32
HIGH_HALF = 0xFFFF0000

D_MODEL = 1024
DEPTH = 2
GRID_W = 64
FOURIER_WIDTH = 256
FOURIER_HEADS = 4
FOURIER_HEAD_DIM = FOURIER_WIDTH // FOURIER_HEADS
GLA_HEADS = 4
GLA_KEY_WIDTH = 256
GLA_VALUE_WIDTH = 512
GLA_DK = GLA_KEY_WIDTH // GLA_HEADS
GLA_DV = GLA_VALUE_WIDTH // GLA_HEADS
GLA_GATE_RANK = 16
GLA_GATE_NORMALIZER = 16.0
CONV_WIDTH = 256
CONV_KERNEL = 31
N_EXPERTS = 32
TOP_K = 4
D_FF = D_MODEL
SWIGLU_LIMIT = 7.0
SWIGLU_ALPHA = 1.702
NORM_EPS = 1e-6

LANES = 128
VMEM_LIMIT = 56 * 1024 * 1024

COL_U = 0
COL_Q = COL_U + FOURIER_WIDTH
COL_K = COL_Q + GLA_KEY_WIDTH
COL_V = COL_K + GLA_KEY_WIDTH
COL_R = COL_V + GLA_VALUE_WIDTH
COL_CA = COL_R + GLA_VALUE_WIDTH
COL_CG = COL_CA + CONV_WIDTH
COL_GL = COL_CG + CONV_WIDTH
GLR_PAD = LANES
IN_PAD = COL_GL + GLR_PAD

GLA_CHUNK = 128
GLA_STEP_CHUNKS = 2
TM_ROWS = 512
TM_EXPERT = 512
EXPERT_CHUNKS = 4
RUN_ALIGN = 8
SORT_TB = 512
SORT_CHUNK = 256
SORT_ROWS = -(-(TOP_K * SORT_TB + (RUN_ALIGN - 1) * N_EXPERTS) // SORT_CHUNK) * SORT_CHUNK
FFT_N2 = 128
FFT_KB = 8
CONV_HALO = 16
CONV_GROUP = 8
CONV_PITCH = 100


def _cparams(*sem):
    return pltpu.CompilerParams(dimension_semantics=sem, vmem_limit_bytes=VMEM_LIMIT)


def _dot(a, b):
    return jnp.dot(a, b, preferred_element_type=F32)


def _dot_nt(a, b):
    return lax.dot_general(a, b, (((1,), (1,)), ((), ())), preferred_element_type=F32)


def _dot_tn(a, b):
    return lax.dot_general(a, b, (((0,), (0,)), ((), ())), preferred_element_type=F32)


def _split(a):
    hi = a.astype(BF16)
    lo = (a - hi.astype(F32)).astype(BF16)
    return hi, lo


def _dot3(a, b):
    ah, al = _split(a)
    bh, bl = _split(b)
    return _dot(ah, bh) + _dot(ah, bl) + _dot(al, bh)


def _sigmoid(x):
    return 1.0 / (1.0 + jnp.exp(-x))


def _pack_bf16_pairs(a, holds_bf16=False):
    h = a.shape[1] // 2
    bits = lax.bitcast_convert_type(a if holds_bf16 else a.astype(BF16).astype(F32), U32)
    return (bits[:, :h] >> 16) | (bits[:, h:] & U32(HIGH_HALF))


def _unpack_bf16_pairs(w):
    return (lax.bitcast_convert_type(w << 16, F32), lax.bitcast_convert_type(w & U32(HIGH_HALF), F32))


def _rms(x, g):
    ms = jnp.mean(x * x, axis=-1, keepdims=True)
    return x * lax.rsqrt(ms + NORM_EPS) * g


def _mod_kernel(cv_ref, w_ref, b_ref, o_ref):
    cv = cv_ref[...]
    a = cv * _sigmoid(cv)
    o_ref[...] = _dot3(a, w_ref[...]) + b_ref[...]


def _mod_call(cvec, w_mod, b_mod):
    depth, d, n = w_mod.shape
    rows = cvec.shape[0]
    tn = 1536
    return pl.pallas_call(
        _mod_kernel,
        grid=(depth, n // tn),
        in_specs=[
            pl.BlockSpec((rows, d), lambda l, j: (0, 0)),
            pl.BlockSpec((None, d, tn), lambda l, j: (l, 0, j)),
            pl.BlockSpec((None, 1, tn), lambda l, j: (l, 0, j)),
        ],
        out_specs=pl.BlockSpec((None, rows, tn), lambda l, j: (l, 0, j)),
        out_shape=jax.ShapeDtypeStruct((depth, rows, n), F32),
        compiler_params=_cparams("arbitrary", "arbitrary"),
    )(cvec, w_mod, b_mod.reshape(depth, 1, n))


def _inproj_kernel(x_ref, g_ref, sh_ref, sc_ref, w_ref, *rest):
    u_ref, r_ref, glu_ref, q_ref, k_ref, v_ref, gl_ref = rest[-7:]
    x = x_ref[...]
    h = _rms(x, g_ref[...]) * (1.0 + sc_ref[...]) + sh_ref[...]
    p = _dot(h.astype(BF16), w_ref[...])
    u_ref[...] = p[:, COL_U:COL_Q].astype(u_ref.dtype)
    q_ref[...] = (p[:, COL_Q:COL_K] * (GLA_DK ** -0.5)).astype(q_ref.dtype)
    k_ref[...] = p[:, COL_K:COL_V].astype(k_ref.dtype)
    v_ref[...] = p[:, COL_V:COL_R].astype(v_ref.dtype)
    r_ref[...] = p[:, COL_R:COL_CA].astype(r_ref.dtype)
    glu_ref[...] = (p[:, COL_CA:COL_CG] * _sigmoid(p[:, COL_CG:COL_GL])).astype(glu_ref.dtype)
    gl_ref[...] = p[:, COL_GL:IN_PAD]


def _inproj_call(x, mods, group_of_batch, norm_g, w_in_p, lt, row0, combined=None):
    bsz, n, d = x.shape
    tm = min(TM_ROWS, n)
    assert n % tm == 0 and row0 % tm == 0
    blk0 = row0 // tm
    nb = n // tm
    steps = nb + (-(-(lt - n) // tm) if combined is None else 0)
    widths = (GLA_KEY_WIDTH, GLA_KEY_WIDTH, GLA_VALUE_WIDTH, GLR_PAD)
    dtypes = (BF16, BF16, BF16, F32)
    row_spec = lambda w: pl.BlockSpec((None, tm, w), lambda b, i: (b, jnp.minimum(i, nb - 1), 0))
    comb_spec = lambda w: pl.BlockSpec((None, tm, w), lambda b, i: (b, blk0 + i, 0))
    mod_spec = lambda which: pl.BlockSpec(
        (None, None, 1, d), lambda b, i: (group_of_batch(b), which, 0, 0))
    in_specs = [
        row_spec(d),
        pl.BlockSpec((1, d), lambda b, i: (0, 0)),
        mod_spec(0), mod_spec(1),
        pl.BlockSpec((d, IN_PAD), lambda b, i: (0, 0)),
    ]
    args = [x, norm_g.reshape(1, d), mods, mods, w_in_p]
    aliases = {}
    if combined is not None:
        for t, arr in enumerate(combined):
            in_specs.append(pl.BlockSpec(memory_space=pl.ANY))
            aliases[len(args)] = 3 + t
            args.append(arr)
    out_shape = [
        jax.ShapeDtypeStruct((bsz, n, FOURIER_WIDTH), BF16),
        jax.ShapeDtypeStruct((bsz, n, GLA_VALUE_WIDTH), BF16),
        jax.ShapeDtypeStruct((bsz, n, CONV_WIDTH), BF16),
    ] + [jax.ShapeDtypeStruct((bsz, lt, w), dt) for w, dt in zip(widths, dtypes)]
    out_specs = [row_spec(FOURIER_WIDTH), row_spec(GLA_VALUE_WIDTH), row_spec(CONV_WIDTH)] + [
        comb_spec(w) for w in widths]
    outs = pl.pallas_call(
        _inproj_kernel,
        grid=(bsz, steps),
        in_specs=in_specs,
        out_specs=out_specs,
        out_shape=out_shape,
        input_output_aliases=aliases,
        compiler_params=_cparams("arbitrary", "arbitrary"),
    )(*args)
    return outs[:3], outs[3:]


def _dft_tables(length):
    n2 = FFT_N2
    n1 = length // n2
    two_pi = 2.0 * jnp.pi

    def cs(num, den):
        ang = (num % den).astype(F32) * (two_pi / den)
        return jnp.cos(ang), jnp.sin(ang)

    k1 = jnp.arange(n1, dtype=jnp.int32)
    c1, s1 = cs(k1[:, None] * k1[None, :], n1)
    stage1 = (jnp.concatenate([c1, -s1], axis=0) * (n1 ** -0.5)).astype(BF16)
    k2 = jnp.arange(n2, dtype=jnp.int32)
    ct, st = cs(k1[:, None] * k2[None, :], length)
    cf, sf = cs(k2[:, None] * k2[None, :], n2)
    scale = n2 ** -0.5
    mr = (ct[:, None, :] * cf[None] - st[:, None, :] * sf[None]) * scale
    mi = -(st[:, None, :] * cf[None] + ct[:, None, :] * sf[None]) * scale
    stage2 = jnp.concatenate([jnp.concatenate([mr, -mi], axis=2),
                              jnp.concatenate([mi, mr], axis=2)], axis=1).astype(BF16)
    return stage1, stage2


def _channel_tables():
    hd = FOURIER_HEAD_DIM
    c = jnp.arange(FOURIER_WIDTH, dtype=jnp.int32)
    same_head = (c[:, None] // hd) == (c[None, :] // hd)
    ang = (((c[:, None] % hd) * (c[None, :] % hd)) % hd).astype(F32) * (2.0 * jnp.pi / hd)
    scale = hd ** -0.5
    bdc = jnp.where(same_head, jnp.cos(ang) * scale, 0.0).astype(BF16)
    bds = jnp.where(same_head, jnp.sin(ang) * scale, 0.0).astype(BF16)
    return bdc, bds


def _fft1_kernel(x_ref, cs_ref, zr_ref, zi_ref):
    n1 = x_ref.shape[0]
    z = _dot(cs_ref[...], x_ref[...])
    zr_ref[...] = z[:n1].astype(zr_ref.dtype)
    zi_ref[...] = z[n1:].astype(zi_ref.dtype)


def _fft2_kernel(zr_ref, zi_ref, m_ref, bdc_ref, bds_ref, o_ref):
    kb, n2, w = zr_ref.shape
    for j in range(kb):
        z = jnp.concatenate([zr_ref[j], zi_ref[j]], axis=0)
        a = _dot(m_ref[j], z)
        y = _dot(a[:n2].astype(BF16), bdc_ref[...]) + _dot(a[n2:].astype(BF16), bds_ref[...])
        o_ref[:, j * w:(j + 1) * w] = y.astype(o_ref.dtype)


def _fourier_long(u, tables, chan):
    bsz, length, w = u.shape
    stage1, stage2 = tables
    bdc, bds = chan
    n2 = FFT_N2
    n1 = length // n2
    tn = 4096
    cols = n2 * w
    zr, zi = pl.pallas_call(
        _fft1_kernel,
        grid=(bsz, cols // tn),
        in_specs=[pl.BlockSpec((None, n1, tn), lambda b, j: (b, 0, j)),
                  pl.BlockSpec((2 * n1, n1), lambda b, j: (0, 0))],
        out_specs=[pl.BlockSpec((None, n1, tn), lambda b, j: (b, 0, j))] * 2,
        out_shape=[jax.ShapeDtypeStruct((bsz, n1, cols), BF16)] * 2,
        compiler_params=_cparams("arbitrary", "arbitrary"),
    )(u.reshape(bsz, n1, cols), stage1)
    kb = FFT_KB
    z_spec = pl.BlockSpec((None, kb, n2, w), lambda b, j: (b, j, 0, 0))
    y = pl.pallas_call(
        _fft2_kernel,
        grid=(bsz, n1 // kb),
        in_specs=[z_spec, z_spec,
                  pl.BlockSpec((kb, 2 * n2, 2 * n2), lambda b, j: (j, 0, 0)),
                  pl.BlockSpec((w, w), lambda b, j: (0, 0)),
                  pl.BlockSpec((w, w), lambda b, j: (0, 0))],
        out_specs=pl.BlockSpec((None, n2, kb * w), lambda b, j: (b, 0, j)),
        out_shape=jax.ShapeDtypeStruct((bsz, n2, n1 * w), BF16),
        compiler_params=_cparams("arbitrary", "arbitrary"),
    )(zr.reshape(bsz, n1, n2, w), zi.reshape(bsz, n1, n2, w), stage2, bdc, bds)
    return y.reshape(bsz, length, w)


def _dft_short_kernel(u_ref, c_ref, s_ref, bdc_ref, bds_ref, o_ref):
    u = u_ref[...]
    p = _dot(u, bdc_ref[...]).astype(BF16)
    q = _dot(u, bds_ref[...]).astype(BF16)
    o_ref[...] = (_dot(c_ref[...], p) - _dot(s_ref[...], q)).astype(o_ref.dtype)


def _fourier_short(u, chan):
    bsz, length, w = u.shape
    bdc, bds = chan
    k = jnp.arange(length, dtype=jnp.int32)
    ang = ((k[:, None] * k[None, :]) % length).astype(F32) * (2.0 * jnp.pi / length)
    c = (jnp.cos(ang) * length ** -0.5).astype(BF16)
    s = (jnp.sin(ang) * length ** -0.5).astype(BF16)
    full = lambda n: pl.BlockSpec((n, n), lambda b: (0, 0))
    return pl.pallas_call(
        _dft_short_kernel,
        grid=(bsz,),
        in_specs=[pl.BlockSpec((None, length, w), lambda b: (b, 0, 0)),
                  full(length), full(length), full(w), full(w)],
        out_specs=pl.BlockSpec((None, length, w), lambda b: (b, 0, 0)),
        out_shape=jax.ShapeDtypeStruct((bsz, length, w), BF16),
        compiler_params=_cparams("arbitrary"),
    )(u, c, s, bdc, bds)


def _conv_kernel(x_ref, w_ref, cb_ref, lg_ref, lb_ref, o_ref, pad_ref, *, seg):
    nseg = x_ref.shape[0] // seg
    width = x_ref.shape[1]
    halo = jnp.zeros((CONV_HALO, width), F32)
    for s in range(nseg):
        pad_ref[s, 0:CONV_HALO, :] = halo
        pad_ref[s, CONV_HALO:CONV_HALO + seg, :] = x_ref[s * seg:(s + 1) * seg, :].astype(F32)
        pad_ref[s, CONV_HALO + seg:2 * CONV_HALO + seg, :] = halo
    first = CONV_HALO - CONV_KERNEL // 2
    sub = 8
    span = seg + 2 * CONV_HALO - sub
    for s in range(nseg):
        acc = jnp.zeros((seg, width), F32)
        for r in range(sub):
            shifted = pad_ref[s, r:r + span, :]
            for a in range((span - seg) // sub + 1):
                j = a * sub + r - first
                if 0 <= j < CONV_KERNEL:
                    acc = acc + shifted[a * sub:a * sub + seg, :] * w_ref[j:j + 1, :]
        y = acc + cb_ref[...]
        mu = jnp.mean(y, axis=-1, keepdims=True)
        yc = y - mu
        var = jnp.mean(yc * yc, axis=-1, keepdims=True)
        z = yc * lax.rsqrt(var + NORM_EPS) * lg_ref[...] + lb_ref[...]
        o_ref[s * seg:(s + 1) * seg, :] = (z * _sigmoid(z)).astype(o_ref.dtype)


def _conv_group_kernel(x_ref, w_ref, cb_ref, lg_ref, lb_ref, o_ref, pad_ref, acc_ref, *, seg):
    g, p = CONV_GROUP, CONV_PITCH
    halves = x_ref.shape[1] // LANES
    pad_ref[...] = jnp.zeros_like(pad_ref)
    for s in range(g):
        for h in range(halves):
            pad_ref[h, s * p + CONV_HALO:s * p + CONV_HALO + seg, :] = (
                x_ref[s * seg:(s + 1) * seg, h * LANES:(h + 1) * LANES].astype(F32))
    first = CONV_HALO - CONV_KERNEL // 2
    unroll = 4

    def body(i, carry):
        t0 = unroll * i
        for h in range(halves):
            accs = [None] * unroll
            for j in range(CONV_KERNEL):
                tap = w_ref[h, j]
                for u in range(unroll):
                    term = pad_ref[h, pl.ds(t0 + u + first + j, g, stride=p), :] * tap
                    accs[u] = term if accs[u] is None else accs[u] + term
            for u in range(unroll):
                acc_ref[h, pl.ds(t0 + u, g, stride=p), :] = accs[u]
        return carry

    lax.fori_loop(0, seg // unroll, body, 0)
    for s in range(g):
        y = jnp.concatenate([acc_ref[h, s * p:s * p + seg, :] for h in range(halves)], axis=1) + cb_ref[...]
        mu = jnp.mean(y, axis=-1, keepdims=True)
        yc = y - mu
        var = jnp.mean(yc * yc, axis=-1, keepdims=True)
        z = yc * lax.rsqrt(var + NORM_EPS) * lg_ref[...] + lb_ref[...]
        o_ref[s * seg:(s + 1) * seg, :] = (z * _sigmoid(z)).astype(o_ref.dtype)


def _conv_call(glu, seg, conv_w, conv_b, ln_g, ln_b):
    bsz, n, w = glu.shape
    t = max(seg, min(TM_ROWS, n))
    assert n % t == 0 and t % seg == 0
    vec = lambda: pl.BlockSpec((1, w), lambda b, i: (0, 0))
    if t // seg == CONV_GROUP and seg + 2 * CONV_HALO <= CONV_PITCH and w % LANES == 0:
        halves = w // LANES
        taps = jnp.broadcast_to(conv_w.reshape(CONV_KERNEL, halves, 1, LANES).transpose(1, 0, 2, 3),
                                (halves, CONV_KERNEL, 8, LANES))
        slab = pltpu.VMEM((halves, CONV_GROUP * CONV_PITCH, LANES), F32)
        return pl.pallas_call(
            functools.partial(_conv_group_kernel, seg=seg),
            grid=(bsz, n // t),
            in_specs=[pl.BlockSpec((None, t, w), lambda b, i: (b, i, 0)),
                      pl.BlockSpec(taps.shape, lambda b, i: (0, 0, 0, 0)),
                      vec(), vec(), vec()],
            out_specs=pl.BlockSpec((None, t, w), lambda b, i: (b, i, 0)),
            out_shape=jax.ShapeDtypeStruct((bsz, n, w), BF16),
            scratch_shapes=[slab, slab],
            compiler_params=_cparams("arbitrary", "arbitrary"),
        )(glu, taps, conv_b.reshape(1, w), ln_g.reshape(1, w), ln_b.reshape(1, w))
    return pl.pallas_call(
        functools.partial(_conv_kernel, seg=seg),
        grid=(bsz, n // t),
        in_specs=[pl.BlockSpec((None, t, w), lambda b, i: (b, i, 0)),
                  pl.BlockSpec((CONV_KERNEL, w), lambda b, i: (0, 0)),
                  vec(), vec(), vec()],
        out_specs=pl.BlockSpec((None, t, w), lambda b, i: (b, i, 0)),
        out_shape=jax.ShapeDtypeStruct((bsz, n, w), BF16),
        scratch_shapes=[pltpu.VMEM((t // seg, seg + 2 * CONV_HALO, w), F32)],
        compiler_params=_cparams("arbitrary", "arbitrary"),
    )(glu, conv_w, conv_b.reshape(1, w), ln_g.reshape(1, w), ln_b.reshape(1, w))


def _gla_direction(q_ref, k_ref, v_ref, gl_ref, wg_ref, bg_ref, o_ref, st, reverse):
    c = GLA_CHUNK
    n_sub = q_ref.shape[0] // c
    for s in (reversed(range(n_sub)) if reverse else range(n_sub)):
        rows = slice(s * c, (s + 1) * c)
        st = _gla_chunk(q_ref[rows, :], k_ref[rows, :], v_ref[rows, :], gl_ref[rows, :], wg_ref, bg_ref,
                        o_ref.at[rows, :], st, reverse)
    return st


def _gla_chunk(q, k, v, glr, wg_ref, bg_ref, o_ref, st, reverse):
    c = GLA_CHUNK
    kw = GLA_KEY_WIDTH
    col0 = kw if reverse else 0
    pre = _dot3(glr, wg_ref[:, col0:col0 + kw]) + bg_ref[:, col0:col0 + kw]
    g = (jnp.minimum(pre, 0.0) - jnp.log(1.0 + jnp.exp(-jnp.abs(pre)))) * (1.0 / GLA_GATE_NORMALIZER)
    row = lax.broadcasted_iota(jnp.int32, (c, c), 0)
    col = lax.broadcasted_iota(jnp.int32, (c, c), 1)
    seen = (col >= row) if reverse else (col <= row)
    tri = jnp.where(seen, 1.0, 0.0).astype(BF16)
    gh, gl = _split(g)
    b = _dot(tri, gh) + _dot(tri, gl)
    mid = c // 2 if reverse else c // 2 - 1
    last = 0 if reverse else c - 1
    b_mid = b[mid:mid + 1, :]
    b_last = b[last:last + 1, :]
    q = q.astype(F32)
    k = k.astype(F32)
    qe = q * jnp.exp(b - b_mid)
    ke = k * jnp.exp(b_mid - b)
    kd = k * jnp.exp(b_last - b)
    head_of_lane = lax.broadcasted_iota(jnp.int32, (1, kw), 1) // GLA_DK
    q_heads = jnp.concatenate(
        [jnp.where(head_of_lane == h, qe, 0.0) for h in range(GLA_HEADS)], axis=0).astype(BF16)
    rhs = jnp.concatenate([ke, st * jnp.exp(b_mid)], axis=0).astype(BF16)
    res = _dot_nt(q_heads, rhs)
    outs = []
    for h in range(GLA_HEADS):
        blk = res[h * c:(h + 1) * c, :]
        scores = jnp.where(seen, blk[:, :c], 0.0).astype(BF16)
        outs.append(_dot(scores, v[:, h * GLA_DV:(h + 1) * GLA_DV]) + blk[:, c:])
    o_ref[...] = jnp.concatenate(outs, axis=1).astype(o_ref.dtype)
    kv = _dot_tn(v, kd.astype(BF16))
    ds = jnp.zeros_like(st)
    for h in range(GLA_HEADS):
        ds = ds + jnp.where(head_of_lane == h, kv[h * GLA_DV:(h + 1) * GLA_DV, :], 0.0)
    return st * jnp.exp(b_last) + ds


def _gla_kernel(qf, kf, vf, gf, qb, kb, vb, gb, wg_ref, bg_ref, of_ref, ob_ref, sf_ref, sb_ref):
    @pl.when(pl.program_id(0) == 0)
    def _():
        sf_ref[...] = jnp.zeros_like(sf_ref)
        sb_ref[...] = jnp.zeros_like(sb_ref)

    bsz = qf.shape[0]
    states = [(sf_ref[b], sb_ref[b]) for b in range(bsz)]
    new = []
    for b, (s_f, s_b) in enumerate(states):
        new.append((
            _gla_direction(qf.at[b], kf.at[b], vf.at[b], gf.at[b], wg_ref, bg_ref, of_ref.at[b], s_f, False),
            _gla_direction(qb.at[b], kb.at[b], vb.at[b], gb.at[b], wg_ref, bg_ref, ob_ref.at[b], s_b, True)))
    for b, (s_f, s_b) in enumerate(new):
        sf_ref[b] = s_f
        sb_ref[b] = s_b


def _gla_call(q, k, v, gl, wg_pad, bg_cat, n_lat, n_ctx):
    bsz, lt, _ = q.shape
    c = GLA_STEP_CHUNKS * GLA_CHUNK
    assert n_lat % c == 0 and n_ctx % c == 0
    cl, cc = n_lat // c, n_ctx // c

    def fwd_blk(j):
        return jnp.where(j < cc, cl + j, j - cc)

    def bwd_blk(j):
        return jnp.where(j < cc, cl + cc - 1 - j, cl - 1 - (j - cc))

    def spec(w, blk):
        return pl.BlockSpec((bsz, c, w), lambda j: (0, blk(j), 0))

    widths = (GLA_KEY_WIDTH, GLA_KEY_WIDTH, GLA_VALUE_WIDTH, GLR_PAD)
    in_specs = [spec(w, fwd_blk) for w in widths] + [spec(w, bwd_blk) for w in widths] + [
        pl.BlockSpec(wg_pad.shape, lambda j: (0, 0)),
        pl.BlockSpec(bg_cat.shape, lambda j: (0, 0))]
    return pl.pallas_call(
        _gla_kernel,
        grid=(cl + cc,),
        in_specs=in_specs,
        out_specs=[spec(GLA_VALUE_WIDTH, fwd_blk), spec(GLA_VALUE_WIDTH, bwd_blk)],
        out_shape=[jax.ShapeDtypeStruct((bsz, lt, GLA_VALUE_WIDTH), BF16)] * 2,
        scratch_shapes=[pltpu.VMEM((bsz, GLA_DV, GLA_KEY_WIDTH), F32)] * 2,
        compiler_params=_cparams("arbitrary"),
    )(q, k, v, gl, q, k, v, gl, wg_pad, bg_cat)


def _outproj_kernel(yf_ref, of_ref, ob_ref, r_ref, cv_ref, x_ref, gate_ref, sh_ref, sc_ref,
                    gn_ref, wo_ref, n2_ref, rwh_ref, rwl_ref, rb_ref, *rest):
    xo_ref, h2_ref, idx_ref, prob_ref = rest[-4:]
    o = of_ref[...].astype(F32) + ob_ref[...].astype(F32)
    heads = []
    for h in range(GLA_HEADS):
        oh = o[:, h * GLA_DV:(h + 1) * GLA_DV]
        heads.append(oh * lax.rsqrt(jnp.mean(oh * oh, axis=-1, keepdims=True) + NORM_EPS))
    r = r_ref[...].astype(F32)
    gla = jnp.concatenate(heads, axis=1) * gn_ref[...] * (r * _sigmoid(r))
    c0, c1 = FOURIER_WIDTH, FOURIER_WIDTH + GLA_VALUE_WIDTH
    y = (_dot(yf_ref[...], wo_ref[0:c0, :]) + _dot(gla.astype(BF16), wo_ref[c0:c1, :])
         + _dot(cv_ref[...], wo_ref[c1:, :]))
    xn = x_ref[...] + gate_ref[...] * y
    xo_ref[...] = xn
    h2 = _rms(xn, n2_ref[...]) * (1.0 + sc_ref[...]) + sh_ref[...]
    hh, hl = _split(h2)
    h2_ref[...] = _pack_bf16_pairs(h2)
    logits = (_dot_nt(rwh_ref[...], hh) + _dot_nt(rwh_ref[...], hl) + _dot_nt(rwl_ref[...], hh)
              + rb_ref[...])
    expert = lax.broadcasted_iota(jnp.int32, logits.shape, 0)
    vals, idxs = [], []
    cur = logits
    for _ in range(TOP_K):
        m = jnp.max(cur, axis=0, keepdims=True)
        ix = jnp.min(jnp.where(cur == m, expert, N_EXPERTS), axis=0, keepdims=True)
        vals.append(m)
        idxs.append(ix)
        cur = jnp.where(expert == ix, -jnp.inf, cur)
    es = [jnp.exp(vv - vals[0]) for vv in vals]
    inv = 1.0 / functools.reduce(lambda a, b: a + b, es)
    idx_ref[...] = jnp.concatenate(idxs, axis=0)
    prob_ref[...] = jnp.concatenate([e * inv for e in es], axis=0)


def _outproj_call(yf, o_f, o_b, o_row0, r, cv, x, mods, group_of_batch, gn_tiled, w_out, norm2_g,
                  rw_hi, rw_lo, rb, n_tok, tok0, carried=None):
    bsz, n, d = x.shape
    tm = min(TM_ROWS, n)
    assert n % tm == 0 and o_row0 % tm == 0 and tok0 % tm == 0
    nb = n // tm
    spare = (n_tok - bsz * n) if carried is None else 0
    assert spare in (0, tm)
    steps = nb + spare // tm
    last = lambda i: jnp.minimum(i, nb - 1)
    row = lambda w: pl.BlockSpec((None, tm, w), lambda b, i: (b, last(i), 0))
    orow = pl.BlockSpec((None, tm, GLA_VALUE_WIDTH), lambda b, i: (b, o_row0 // tm + last(i), 0))
    mod = lambda which: pl.BlockSpec((None, None, 1, d), lambda b, i: (group_of_batch(b), which, 0, 0))
    const = lambda a: pl.BlockSpec(a.shape, lambda b, i: (0,) * a.ndim)
    consts = [gn_tiled, w_out, norm2_g.reshape(1, d), rw_hi, rw_lo, rb]
    in_specs = [row(FOURIER_WIDTH), orow, orow, row(GLA_VALUE_WIDTH), row(CONV_WIDTH), row(d),
                mod(2), mod(3), mod(4)] + [const(a) for a in consts]
    args = [yf, o_f, o_b, r, cv, x, mods, mods, mods] + consts
    aliases = {}
    if carried is not None:
        for t, arr in enumerate(carried):
            in_specs.append(pl.BlockSpec(memory_space=pl.ANY))
            aliases[len(args)] = 1 + t
            args.append(arr)
    tokblk = lambda b, i: tok0 // tm + jnp.where(
        jnp.logical_and(i == nb, b == bsz - 1), bsz * nb, b * nb + last(i))
    out_specs = [row(d),
                 pl.BlockSpec((tm, d // 2), lambda b, i: (tokblk(b, i), 0)),
                 pl.BlockSpec((TOP_K, tm), lambda b, i: (0, tokblk(b, i))),
                 pl.BlockSpec((TOP_K, tm), lambda b, i: (0, tokblk(b, i)))]
    out_shape = [jax.ShapeDtypeStruct((bsz, n, d), F32),
                 jax.ShapeDtypeStruct((n_tok, d // 2), U32),
                 jax.ShapeDtypeStruct((TOP_K, n_tok), jnp.int32),
                 jax.ShapeDtypeStruct((TOP_K, n_tok), F32)]
    outs = pl.pallas_call(
        _outproj_kernel,
        grid=(bsz, steps),
        in_specs=in_specs,
        out_specs=out_specs,
        out_shape=out_shape,
        input_output_aliases=aliases,
        compiler_params=_cparams("arbitrary", "arbitrary"),
    )(*args)
    return outs[0], outs[1:]


def _expert_kernel(be_ref, nu_ref, x_ref, wgu_ref, bgu_ref, wdn_ref, bdn_ref, o_ref, wgu_s, wdn_s):
    i = pl.program_id(0)
    nu = nu_ref[0]
    n_chunks = wgu_s.shape[0]
    fc = D_FF // n_chunks

    @pl.when(i >= nu)
    def _():
        o_ref[...] = jnp.zeros_like(o_ref)

    @pl.when(i < nu)
    def _():
        changed = jnp.logical_or(i == 0, be_ref[i] != be_ref[jnp.maximum(i - 1, 0)])

        @pl.when(changed)
        def _():
            rows = 128
            for s in range(wgu_ref.shape[0] // rows):
                rs = slice(s * rows, (s + 1) * rows)
                for n in range(n_chunks):
                    wgu_s[n, rs, 0:fc] = wgu_ref[rs, n * fc:(n + 1) * fc].astype(BF16)
                    wgu_s[n, rs, fc:2 * fc] = wgu_ref[rs, D_FF + n * fc:D_FF + (n + 1) * fc].astype(BF16)
                wdn_s[rs, :] = wdn_ref[rs, :].astype(BF16)

        x = jnp.concatenate(_unpack_bf16_pairs(x_ref[...]), axis=1).astype(BF16)
        acts = []
        for n in range(n_chunks):
            gate = _dot(x, wgu_s[n, :, 0:fc]) + bgu_ref[:, n * fc:(n + 1) * fc]
            up = _dot(x, wgu_s[n, :, fc:2 * fc]) + bgu_ref[:, D_FF + n * fc:D_FF + (n + 1) * fc]
            gate = jnp.minimum(gate, SWIGLU_LIMIT)
            up = jnp.clip(up, -SWIGLU_LIMIT, SWIGLU_LIMIT)
            acts.append((gate * _sigmoid(SWIGLU_ALPHA * gate) * (up + 1.0)).astype(BF16))
        half = n_chunks // 2
        y = (_dot(jnp.concatenate(acts[:half], axis=1), wdn_s[0:half * fc, :])
             + _dot(jnp.concatenate(acts[half:], axis=1), wdn_s[half * fc:, :]) + bdn_ref[...])
        o_ref[...] = _pack_bf16_pairs(y)


def _expert_call(block_e, n_used, xs, layer, w_gu, b_gu, w_dn, b_dn):
    rows, dh = xs.shape
    d = 2 * dh
    tm = TM_EXPERT
    nblk = rows // tm
    depth, e, _, f2 = w_gu.shape
    n_chunks = EXPERT_CHUNKS
    live = lambda i, nu: jnp.minimum(i, nu[0] - 1)
    wmap = lambda i, be, nu: (layer, be[live(i, nu)], 0, 0)
    grid_spec = pltpu.PrefetchScalarGridSpec(
        num_scalar_prefetch=2,
        grid=(nblk,),
        in_specs=[
            pl.BlockSpec((tm, dh), lambda i, be, nu: (live(i, nu), 0)),
            pl.BlockSpec((None, None, d, f2), wmap),
            pl.BlockSpec((None, None, 1, f2), wmap),
            pl.BlockSpec((None, None, f2 // 2, d), wmap),
            pl.BlockSpec((None, None, 1, d), wmap),
        ],
        out_specs=pl.BlockSpec((tm, dh), lambda i, be, nu: (i, 0)),
        scratch_shapes=[pltpu.VMEM((n_chunks, d, f2 // n_chunks), BF16), pltpu.VMEM((f2 // 2, d), BF16)],
    )
    return pl.pallas_call(
        _expert_kernel,
        grid_spec=grid_spec,
        out_shape=jax.ShapeDtypeStruct((rows, dh), xs.dtype),
        compiler_params=_cparams("arbitrary"),
    )(block_e, n_used, xs, w_gu, b_gu.reshape(depth, e, 1, f2), w_dn, b_dn.reshape(depth, e, 1, d))


def _count_kernel(idx_ref, cnt_ref):
    per_step = cnt_ref.shape[0]
    tb = idx_ref.shape[1] // per_step
    expert = lax.broadcasted_iota(jnp.int32, (N_EXPERTS, tb), 0)
    for j in range(per_step):
        total = jnp.zeros((N_EXPERTS, 1), F32)
        for kk in range(TOP_K):
            hit = expert == idx_ref[kk:kk + 1, j * tb:(j + 1) * tb]
            total = total + jnp.sum(hit.astype(F32), axis=1, keepdims=True)
        cnt_ref[j] = total.astype(jnp.int32)


def _run_tables(idx_t, n_tok):
    tm, tb = TM_EXPERT, SORT_TB
    nblk = n_tok // tb
    assert n_tok % tb == 0
    per_step = max(g for g in range(1, 9) if nblk % g == 0)
    cnt = pl.pallas_call(
        _count_kernel,
        grid=(nblk // per_step,),
        in_specs=[pl.BlockSpec((TOP_K, per_step * tb), lambda i: (0, i))],
        out_specs=pl.BlockSpec((per_step, N_EXPERTS, 1), lambda i: (i, 0, 0)),
        out_shape=jax.ShapeDtypeStruct((nblk, N_EXPERTS, 1), jnp.int32),
        compiler_params=_cparams("arbitrary"),
    )(idx_t)[:, :, 0]
    run = (cnt + RUN_ALIGN - 1) // RUN_ALIGN * RUN_ALIGN
    total = jnp.sum(run, axis=0)
    padded = (total + tm - 1) // tm * tm
    pad_end = jnp.cumsum(padded)
    dst = (pad_end - padded)[None, :] + jnp.cumsum(run, axis=0) - run
    boff = jnp.cumsum(run, axis=1) - run
    n_blocks = -(-(TOP_K * n_tok + (RUN_ALIGN - 1) * N_EXPERTS * nblk) // tm) + N_EXPERTS
    first_row = jnp.arange(n_blocks, dtype=jnp.int32) * tm
    block_e = jnp.minimum(jnp.sum(pad_end[None, :] <= first_row[:, None], axis=1),
                          N_EXPERTS - 1).astype(jnp.int32)
    n_used = (pad_end[-1:] // tm).astype(jnp.int32)
    flat = lambda a: a.reshape(-1).astype(jnp.int32)
    return dict(run=flat(run), dst=flat(dst), boff=flat(boff), boff_f=boff.astype(F32),
                pad_end=pad_end.astype(jnp.int32), block_e=block_e, n_used=n_used, n_blocks=n_blocks)


def _run_copies(blk, run_ref, grp_ref, boff_ref, grouped, sbuf, sem, to_grouped, wait):
    if wait:
        last = blk * N_EXPERTS + N_EXPERTS - 1
        rows = pl.multiple_of(boff_ref[last] + run_ref[last], RUN_ALIGN)

        @pl.when(rows > 0)
        def _():
            s, g = sbuf.at[pl.ds(0, rows), :], grouped.at[pl.ds(0, rows), :]
            (pltpu.make_async_copy(s, g, sem) if to_grouped else pltpu.make_async_copy(g, s, sem)).wait()
        return
    for e in range(N_EXPERTS):
        n = pl.multiple_of(run_ref[blk * N_EXPERTS + e], RUN_ALIGN)
        g0 = pl.multiple_of(grp_ref[blk * N_EXPERTS + e], RUN_ALIGN)
        s0 = pl.multiple_of(boff_ref[blk * N_EXPERTS + e], RUN_ALIGN)

        @pl.when(n > 0)
        def _(n=n, g0=g0, s0=s0):
            g, s = grouped.at[pl.ds(g0, n), :], sbuf.at[pl.ds(s0, n), :]
            (pltpu.make_async_copy(s, g, sem) if to_grouped else pltpu.make_async_copy(g, s, sem)).start()


def _sort_dispatch_kernel(run_ref, dst_ref, boffs_ref, pe_ref, nu_ref, idx_ref, boff_ref, h_ref, xs_ref,
                          pos_ref, sbuf, zero_ref, sem, zsem, *, n_blocks):
    i = pl.program_id(0)
    tb = h_ref.shape[0]
    tm = zero_ref.shape[0]

    def zero_copy(blk):
        return pltpu.make_async_copy(zero_ref, xs_ref.at[pl.ds(pl.multiple_of(blk * tm, tm), tm), :], zsem)

    @pl.when(i == 0)
    def _():
        zero_ref[...] = jnp.zeros_like(zero_ref)

        def per_expert(fn):
            for e in range(N_EXPERTS):
                end = pe_ref[e]
                start = pe_ref[e - 1] if e else 0

                @pl.when(end > start)
                def _():
                    fn(end // tm - 1)

        def per_tail(fn):
            def body(blk, carry):
                fn(blk)
                return carry
            lax.fori_loop(nu_ref[0], n_blocks, body, 0)

        per_expert(lambda blk: zero_copy(blk).start())
        per_tail(lambda blk: zero_copy(blk).start())
        per_expert(lambda blk: zero_copy(blk).wait())
        per_tail(lambda blk: zero_copy(blk).wait())

    row = lax.broadcasted_iota(jnp.int32, (tb, tb), 0)
    col = lax.broadcasted_iota(jnp.int32, (tb, tb), 1)
    earlier = (row < col).astype(F32).astype(BF16)
    expert = lax.broadcasted_iota(jnp.int32, (N_EXPERTS, tb), 0)
    base = boff_ref[...]
    pos = []
    for kk in range(TOP_K):
        onehot = (expert == idx_ref[kk:kk + 1, :]).astype(F32)
        before = _dot(onehot.astype(BF16), earlier) + base
        pos.append(jnp.sum(before * onehot, axis=0, keepdims=True).astype(jnp.int32))
        base = base + jnp.sum(onehot, axis=1, keepdims=True)
    pos_ref[...] = jnp.concatenate(pos, axis=0)

    lo, hi = _unpack_bf16_pairs(h_ref[...])
    h = jnp.concatenate([lo, hi], axis=1).astype(BF16)
    rc = SORT_CHUNK
    cur = sbuf.at[i % 2]
    def selector(c):
        r = lax.broadcasted_iota(jnp.int32, (rc, tb), 0) + c * rc
        sel = jnp.zeros((rc, tb), F32)
        for kk in range(TOP_K):
            sel = jnp.where(r == pos[kk], 1.0, sel).astype(F32)
        return sel.astype(BF16)

    n_chunks = sbuf.shape[1] // rc
    sel = selector(0)
    for c in range(n_chunks):
        nxt = selector(c + 1) if c + 1 < n_chunks else None
        cur[c * rc:(c + 1) * rc, :] = _pack_bf16_pairs(_dot(sel, h), holds_bf16=True)
        sel = nxt
    _run_copies(i, run_ref, dst_ref, boffs_ref, xs_ref, cur, sem.at[i % 2], True, False)

    @pl.when(i > 0)
    def _():
        _run_copies(i - 1, run_ref, dst_ref, boffs_ref, xs_ref, sbuf.at[(i - 1) % 2], sem.at[(i - 1) % 2],
                    True, True)

    @pl.when(i == pl.num_programs(0) - 1)
    def _():
        _run_copies(i, run_ref, dst_ref, boffs_ref, xs_ref, cur, sem.at[i % 2], True, True)


def _sort_dispatch_call(t, idx_t, h2):
    n_tok, dh = h2.shape
    tb, tm = SORT_TB, TM_EXPERT
    nblk = n_tok // tb
    grid_spec = pltpu.PrefetchScalarGridSpec(
        num_scalar_prefetch=5,
        grid=(nblk,),
        in_specs=[pl.BlockSpec((TOP_K, tb), lambda i, *_: (0, i)),
                  pl.BlockSpec((None, N_EXPERTS, 1), lambda i, *_: (i, 0, 0)),
                  pl.BlockSpec((tb, dh), lambda i, *_: (i, 0))],
        out_specs=[pl.BlockSpec(memory_space=pl.ANY), pl.BlockSpec((TOP_K, tb), lambda i, *_: (0, i))],
        scratch_shapes=[pltpu.VMEM((2, SORT_ROWS, dh), h2.dtype), pltpu.VMEM((tm, dh), h2.dtype),
                        pltpu.SemaphoreType.DMA((2,)), pltpu.SemaphoreType.DMA],
    )
    return pl.pallas_call(
        functools.partial(_sort_dispatch_kernel, n_blocks=t["n_blocks"]),
        grid_spec=grid_spec,
        out_shape=[jax.ShapeDtypeStruct((t["n_blocks"] * tm, dh), h2.dtype),
                   jax.ShapeDtypeStruct((TOP_K, n_tok), jnp.int32)],
        compiler_params=_cparams("arbitrary"),
    )(t["run"], t["dst"], t["boff"], t["pad_end"], t["n_used"], idx_t, t["boff_f"][:, :, None], h2)


def _sort_combine_kernel(run_ref, dst_ref, boffs_ref, pos_ref, p_ref, yb_hbm, x_ref, gate_ref, fg_ref, o_ref,
                         sbuf, sem, *, final_norm, blk0):
    i = pl.program_id(0)
    tb = x_ref.shape[0]

    def fetch(blk, wait):
        _run_copies(blk0 + blk, run_ref, dst_ref, boffs_ref, yb_hbm, sbuf.at[blk % 2], sem.at[blk % 2],
                    False, wait)

    @pl.when(i == 0)
    def _():
        sbuf[...] = jnp.zeros_like(sbuf)
        fetch(0, False)

    @pl.when(i + 1 < pl.num_programs(0))
    def _():
        fetch(i + 1, False)

    fetch(i, True)
    cur = sbuf.at[i % 2]

    pos = [pos_ref[:, kk:kk + 1] for kk in range(TOP_K)]
    rc = SORT_CHUNK
    f = jnp.zeros(x_ref.shape, F32)
    for c in range(sbuf.shape[1] // rc):
        r = lax.broadcasted_iota(jnp.int32, (tb, rc), 1) + c * rc
        wgt = jnp.zeros((tb, rc), F32)
        for kk in range(TOP_K):
            wgt = jnp.where(r == pos[kk], p_ref[:, kk:kk + 1], wgt)
        lo, hi = _unpack_bf16_pairs(cur[c * rc:(c + 1) * rc, :])
        y = jnp.concatenate([lo, hi], axis=1).astype(BF16)
        f = f + _dot(wgt.astype(BF16), y)
    xn = x_ref[...] + gate_ref[...] * f
    if final_norm:
        xn = _rms(xn, fg_ref[...])
    o_ref[...] = xn


def _sort_combine_call(t, pos_tok, probs, yb, x, mods, group_of_block, final_g, tok0, final_norm):
    bsz, n, d = x.shape
    tb = SORT_TB
    rows = bsz * n
    assert rows % tb == 0 and tok0 % tb == 0 and (n % tb == 0 or tb % n == 0)
    blk0 = tok0 // tb
    grid_spec = pltpu.PrefetchScalarGridSpec(
        num_scalar_prefetch=3,
        grid=(rows // tb,),
        in_specs=[pl.BlockSpec((tb, TOP_K), lambda i, *_: (blk0 + i, 0)),
                  pl.BlockSpec((tb, TOP_K), lambda i, *_: (blk0 + i, 0)),
                  pl.BlockSpec(memory_space=pl.ANY),
                  pl.BlockSpec((tb, d), lambda i, *_: (i, 0)),
                  pl.BlockSpec((None, None, 1, d), lambda i, *_: (group_of_block(i), 5, 0, 0)),
                  pl.BlockSpec((1, d), lambda i, *_: (0, 0))],
        out_specs=pl.BlockSpec((tb, d), lambda i, *_: (i, 0)),
        scratch_shapes=[pltpu.VMEM((2, SORT_ROWS, yb.shape[1]), yb.dtype), pltpu.SemaphoreType.DMA((2,))],
    )
    out = pl.pallas_call(
        functools.partial(_sort_combine_kernel, final_norm=final_norm, blk0=blk0),
        grid_spec=grid_spec,
        out_shape=jax.ShapeDtypeStruct((rows, d), F32),
        compiler_params=_cparams("arbitrary"),
    )(t["run"], t["dst"], t["boff"], pos_tok, probs, yb, x.reshape(rows, d), mods, final_g.reshape(1, d))
    return out.reshape(bsz, n, d)


def kernel(x, c, ctx, c_ctx, norm1_g, norm2_g, w_mod, b_mod, w_in, gla_wg2_f, gla_bg_f, gla_wg2_b,
           gla_bg_b, gla_norm_g, conv_w, conv_b, conv_ln_g, conv_ln_b, w_out, router_w, router_b,
           exp_w_gu, exp_b_gu, exp_w_dn, exp_b_dn, final_norm_g):
    bsz, seq, d = x.shape
    n_ctx = ctx.shape[1]
    depth = w_mod.shape[0]
    assert d == D_MODEL and seq % (FFT_N2 * FFT_KB) == 0 and seq % n_ctx == 0
    lt = seq + n_ctx
    ctx_group = bsz

    rows = 8
    cvec = jnp.concatenate([c, c_ctx[None, :], jnp.zeros((rows - bsz - 1, d), F32)], axis=0)
    mods_all = _mod_call(cvec, w_mod, b_mod).reshape(depth, rows, 6, 1, d)

    def pack_w_in(w):
        o = [0, 256, 512, 768, 1280, 1792, 1808, 1824, 2336]
        parts = [w[:, o[0]:o[1]], w[:, o[1]:o[2]], w[:, o[2]:o[3]], w[:, o[3]:o[4]], w[:, o[4]:o[5]],
                 w[:, o[7]:o[7] + CONV_WIDTH], w[:, o[7] + CONV_WIDTH:o[8]], w[:, o[5]:o[7]],
                 jnp.zeros((d, GLR_PAD - 2 * GLA_GATE_RANK), w.dtype)]
        return jnp.concatenate(parts, axis=1).astype(BF16)

    long_tables = _dft_tables(seq)
    chan = _channel_tables()
    x_lat, x_ctx = x, ctx
    lat_group = lambda b: b
    ctx_group_fn = lambda b: ctx_group

    for layer in range(depth):
        last = layer == depth - 1
        mods = mods_all[layer]
        w_in_p = pack_w_in(w_in[layer])
        wg_pad = jnp.zeros((GLR_PAD, 2 * GLA_KEY_WIDTH), F32)
        wg_pad = wg_pad.at[:GLA_GATE_RANK, :GLA_KEY_WIDTH].set(gla_wg2_f[layer])
        wg_pad = wg_pad.at[GLA_GATE_RANK:2 * GLA_GATE_RANK, GLA_KEY_WIDTH:].set(gla_wg2_b[layer])
        bg_cat = jnp.concatenate([gla_bg_f[layer], gla_bg_b[layer]])[None, :]
        w_out_b = w_out[layer].astype(BF16)
        gn_tiled = jnp.tile(gla_norm_g[layer], GLA_HEADS)[None, :]
        rw_t = router_w[layer].T
        rw_hi = rw_t.astype(BF16)
        rw_lo = (rw_t - rw_hi.astype(F32)).astype(BF16)
        rb = router_b[layer][:, None]

        (u_l, r_l, glu_l), comb = _inproj_call(x_lat, mods, lat_group, norm1_g[layer], w_in_p, lt, 0)
        (u_c, r_c, glu_c), comb = _inproj_call(x_ctx, mods, ctx_group_fn, norm1_g[layer], w_in_p, lt,
                                               seq, combined=comb)
        o_f, o_b = _gla_call(*comb, wg_pad, bg_cat, seq, n_ctx)
        yf_l = _fourier_long(u_l, long_tables, chan)
        cv_l = _conv_call(glu_l, seq // (seq // GRID_W), conv_w[layer], conv_b[layer],
                          conv_ln_g[layer], conv_ln_b[layer])
        n_tok = bsz * seq + (0 if last else bsz * n_ctx)
        x_lat, routed = _outproj_call(yf_l, o_f, o_b, 0, r_l, cv_l, x_lat, mods, lat_group, gn_tiled,
                                      w_out_b, norm2_g[layer], rw_hi, rw_lo, rb, n_tok, 0)
        if not last:
            yf_c = _fourier_short(u_c, chan)
            cv_c = _conv_call(glu_c, n_ctx, conv_w[layer], conv_b[layer], conv_ln_g[layer],
                              conv_ln_b[layer])
            x_ctx, routed = _outproj_call(yf_c, o_f, o_b, seq, r_c, cv_c, x_ctx, mods, ctx_group_fn,
                                          gn_tiled, w_out_b, norm2_g[layer], rw_hi, rw_lo, rb, n_tok,
                                          bsz * seq, carried=routed)

        h2, idx_t, prob_t = routed
        runs = _run_tables(idx_t, n_tok)
        xs, pos_t = _sort_dispatch_call(runs, idx_t, h2)
        yb = _expert_call(runs["block_e"], runs["n_used"], xs, layer, exp_w_gu, exp_b_gu, exp_w_dn,
                          exp_b_dn)
        probs, pos_tok = prob_t.T, pos_t.T
        lat_blocks = seq // SORT_TB
        x_lat = _sort_combine_call(runs, pos_tok, probs, yb, x_lat, mods, lambda i: i // lat_blocks,
                                   final_norm_g, 0, last)
        if not last:
            x_ctx = _sort_combine_call(runs, pos_tok, probs, yb, x_ctx, mods, lambda i: ctx_group,
                                       final_norm_g, bsz * seq, False)

    return x_lat
```
